```python
import math
import jax, jax.numpy as jnp
from jax import lax
import numpy as np

D_MODEL = 1024
BATCH = 8
SEQ = 4096
DEPTH = 2

N_MEM = 256
MEM_HEADS = 4
MEM_WIDTH = D_MODEL // 4
MEM_HEAD_DIM = MEM_WIDTH // MEM_HEADS
MIX_WIDTH = D_MODEL - MEM_WIDTH
BRANCH_WIDTH = D_MODEL
EPS = 1e-6

QK_NOPE_DIM = 128
QK_ROPE_DIM = 64
V_HEAD_DIM = 128
MLA_HEADS = MIX_WIDTH // V_HEAD_DIM
Q_LORA_RANK = 384
KV_LORA_RANK = 256
ROPE_THETA = 10000.0
Q_BLOCK = 128

MLSTM_HEADS = 4
MLSTM_V_DIM = MIX_WIDTH // MLSTM_HEADS
MLSTM_QK_DIM = MLSTM_V_DIM // 2
CONV_WIDTH = 4
CHUNK = 64

N_LAYERS_A = (DEPTH + 1) // 2
N_LAYERS_B = DEPTH // 2
A_IN_SIZES = [Q_LORA_RANK, KV_LORA_RANK, QK_ROPE_DIM, MEM_WIDTH, BRANCH_WIDTH]
B_IN_SIZES = [MIX_WIDTH, 2 * MLSTM_HEADS, MIX_WIDTH, MEM_WIDTH, BRANCH_WIDTH]
A_IN_COLS = sum(A_IN_SIZES)
B_IN_COLS = sum(B_IN_SIZES)

kernel_name = 'hybrid_mla_mlstm_memory_trunk'


def rms_norm(x, g):
    xf = x.astype(jnp.float32)
    y = xf * lax.rsqrt(jnp.mean(xf * xf, axis=-1, keepdims=True) + EPS)
    return (y * g.astype(jnp.float32)).astype(x.dtype)


def split_cols(t, sizes):
    offs = np.cumsum(sizes)[:-1].tolist()
    return jnp.split(t, offs, axis=-1)


def rope_tables(positions):
    inv_freq = ROPE_THETA ** (-jnp.arange(0, QK_ROPE_DIM, 2, dtype=jnp.float32) / QK_ROPE_DIM)
    ang = positions.astype(jnp.float32)[..., None] * inv_freq
    return jnp.cos(ang), jnp.sin(ang)


def apply_rope(x, cos, sin):
    x1, x2 = jnp.split(x, 2, axis=-1)
    c = cos.astype(x.dtype)
    s = sin.astype(x.dtype)
    return jnp.concatenate([x1 * c - x2 * s, x1 * s + x2 * c], axis=-1)


def causal_block_attention(q, k, v, scale):
    B, S, H, Dqk = q.shape
    nb = S // Q_BLOCK
    qb = jnp.moveaxis(q.reshape(B, nb, Q_BLOCK, H, Dqk), 1, 0)
    kpos = jnp.arange(S)

    def one_block(args):
        qi, blk = args
        s = jnp.einsum('bqhd,bkhd->bhqk', qi, k).astype(jnp.float32) * scale
        qpos = blk * Q_BLOCK + jnp.arange(Q_BLOCK)
        s = jnp.where(kpos[None, :] <= qpos[:, None], s, -jnp.inf)
        p = jax.nn.softmax(s, axis=-1).astype(v.dtype)
        return jnp.einsum('bhqk,bkhd->bqhd', p, v)

    out = lax.map(one_block, (qb, jnp.arange(nb)))
    return jnp.moveaxis(out, 0, 1).reshape(B, S, H, v.shape[-1])


def memory_attention(q_mem, mem, mem_g, w_mem_kv):
    B, S, _ = q_mem.shape
    kv = rms_norm(mem, mem_g) @ w_mem_kv
    k, v = jnp.split(kv, 2, axis=-1)
    q = q_mem.reshape(B, S, MEM_HEADS, MEM_HEAD_DIM)
    k = k.reshape(B, N_MEM, MEM_HEADS, MEM_HEAD_DIM)
    v = v.reshape(B, N_MEM, MEM_HEADS, MEM_HEAD_DIM)
    s = jnp.einsum('bshd,bmhd->bhsm', q, k).astype(jnp.float32) * (MEM_HEAD_DIM ** -0.5)
    p = jax.nn.softmax(s, axis=-1).astype(v.dtype)
    o = jnp.einsum('bhsm,bmhd->bshd', p, v)
    return o.reshape(B, S, MEM_WIDTH)


def mla_mixer(c_q, c_kv, k_rope, cos, sin, q_a_g, w_uq, kv_a_g, w_ukv):
    B, S, _ = c_q.shape
    q = (rms_norm(c_q, q_a_g) @ w_uq).reshape(B, S, MLA_HEADS, QK_NOPE_DIM + QK_ROPE_DIM)
    q_nope, q_rope = split_cols(q, [QK_NOPE_DIM, QK_ROPE_DIM])
    q_rope = apply_rope(q_rope, cos[:, :, None, :], sin[:, :, None, :])
    kv = (rms_norm(c_kv, kv_a_g) @ w_ukv).reshape(B, S, MLA_HEADS, QK_NOPE_DIM + V_HEAD_DIM)
    k_nope, v = split_cols(kv, [QK_NOPE_DIM, V_HEAD_DIM])
    k_rope = apply_rope(k_rope, cos, sin)
    k = jnp.concatenate(
        [k_nope, jnp.broadcast_to(k_rope[:, :, None, :], (B, S, MLA_HEADS, QK_ROPE_DIM))], axis=-1)
    q = jnp.concatenate([q_nope, q_rope], axis=-1)
    o = causal_block_attention(q, k, v, (QK_NOPE_DIM + QK_ROPE_DIM) ** -0.5)
    return o.reshape(B, S, MIX_WIDTH)


def mlstm_chunkwise(q, k, v, i_pre, f_pre):
    B, S, H, Dk = q.shape
    Dv = v.shape[-1]
    nc = S // CHUNK

    def to_chunks(t):
        t = t.astype(jnp.float32).reshape((B, nc, CHUNK, H) + t.shape[3:])
        return jnp.moveaxis(t, (1, 3), (0, 2))

    log_i = i_pre.astype(jnp.float32)
    log_f = jax.nn.log_sigmoid(f_pre.astype(jnp.float32))
    xs = (to_chunks(q), to_chunks(k), to_chunks(v), to_chunks(log_i), to_chunks(log_f))
    causal = jnp.tril(jnp.ones((CHUNK, CHUNK), dtype=bool))

    def step(carry, xs_c):
        C, n, m = carry
        qc, kc, vc, li, lf = xs_c
        b = jnp.cumsum(lf, axis=-1)
        d = jnp.where(causal, b[..., :, None] - b[..., None, :] + li[..., None, :], -jnp.inf)
        inter = b + m[..., None]
        m_t = jnp.maximum(inter, jnp.max(d, axis=-1))
        w_intra = jnp.exp(d - m_t[..., None])
        w_inter = jnp.exp(inter - m_t)
        sqk = jnp.einsum('bhtd,bhsd->bhts', qc, kc) * w_intra
        num = (w_inter[..., None] * jnp.einsum('bhtd,bhde->bhte', qc, C)
               + jnp.einsum('bhts,bhse->bhte', sqk, vc))
        den = w_inter * jnp.einsum('bhtd,bhd->bht', qc, n) + jnp.sum(sqk, axis=-1)
        den = jnp.maximum(jnp.abs(den), jnp.exp(-m_t))
        h = num / den[..., None]
        bL = b[..., -1]
        dec = bL[..., None] - b + li
        m_new = jnp.maximum(bL + m, jnp.max(dec, axis=-1))
        a = jnp.exp(bL + m - m_new)
        ws = jnp.exp(dec - m_new[..., None])
        C_new = a[..., None, None] * C + jnp.einsum('bhs,bhsd,bhse->bhde', ws, kc, vc)
        n_new = a[..., None] * n + jnp.einsum('bhs,bhsd->bhd', ws, kc)
        return (C_new, n_new, m_new), h

    init = (jnp.zeros((B, H, Dk, Dv), jnp.float32), jnp.zeros((B, H, Dk), jnp.float32),
            jnp.zeros((B, H), jnp.float32))
    _, h = lax.scan(step, init, xs)
    return jnp.moveaxis(h, (0, 2), (1, 3)).reshape(B, S, H, Dv).astype(v.dtype)


def mlstm_mixer(u, if_pre, o_pre, gate_bias, conv_w, conv_b, w_q, w_k, w_v, head_g, skip):
    B, S, _ = u.shape
    u_pad = jnp.pad(u, ((0, 0), (CONV_WIDTH - 1, 0), (0, 0)))
    conv = conv_b + sum(u_pad[:, t:t + S, :] * conv_w[t] for t in range(CONV_WIDTH))
    uc = jax.nn.silu(conv)
    uch = uc.reshape(B, S, MLSTM_HEADS, MLSTM_V_DIM)
    uh = u.reshape(B, S, MLSTM_HEADS, MLSTM_V_DIM)
    q = jnp.einsum('bshd,hde->bshe', uch, w_q)
    k = jnp.einsum('bshd,hde->bshe', uch, w_k) * (MLSTM_QK_DIM ** -0.5)
    v = jnp.einsum('bshd,hde->bshe', uh, w_v)
    gates = if_pre + gate_bias
    i_pre, f_pre = jnp.split(gates, 2, axis=-1)
    h = mlstm_chunkwise(q, k, v, i_pre, f_pre)
    h = rms_norm(h, head_g).reshape(B, S, MIX_WIDTH)
    return jax.nn.sigmoid(o_pre) * h + skip * uc


def mla_layer(x, mem, cos, sin, pre_g, w_in, q_a_g, w_uq, kv_a_g, w_ukv, mem_g, w_mem_kv, w_out, post_g):
    h = rms_norm(x, pre_g)
    c_q, c_kv, k_rope, q_mem, gate = split_cols(h @ w_in, A_IN_SIZES)
    mix = mla_mixer(c_q, c_kv, k_rope, cos, sin, q_a_g, w_uq, kv_a_g, w_ukv)
    mo = memory_attention(q_mem, mem, mem_g, w_mem_kv)
    y = (jnp.concatenate([mix, mo], axis=-1) * jax.nn.silu(gate)) @ w_out
    return x + rms_norm(y, post_g)


def mlstm_layer(x, mem, pre_g, w_in, gate_bias, conv_w, conv_b, w_q, w_k, w_v, head_g, skip,
                mem_g, w_mem_kv, w_out, post_g):
    h = rms_norm(x, pre_g)
    u, if_pre, o_pre, q_mem, gate = split_cols(h @ w_in, B_IN_SIZES)
    mix = mlstm_mixer(u, if_pre, o_pre, gate_bias, conv_w, conv_b, w_q, w_k, w_v, head_g, skip)
    mo = memory_attention(q_mem, mem, mem_g, w_mem_kv)
    y = (jnp.concatenate([mix, mo], axis=-1) * jax.nn.silu(gate)) @ w_out
    return x + rms_norm(y, post_g)


def setup_inputs(seed: int = 0) -> dict:
    key = jax.random.key(seed)
    ks = jax.random.split(key, 32)
    f32 = jnp.float32
    NA, NB, H = N_LAYERS_A, N_LAYERS_B, MLSTM_HEADS

    def nrm(k, shape, fan_in):
        return jax.random.normal(k, shape, f32) * (fan_in ** -0.5)

    def gain(k, shape):
        return 1.0 + 0.02 * jax.random.normal(k, shape, f32)

    x = jax.random.normal(ks[0], (BATCH, SEQ, D_MODEL), f32)
    mem = jax.random.normal(ks[1], (BATCH, N_MEM, D_MODEL), f32)
    offs = jax.random.randint(ks[2], (BATCH, 1), 0, 1024, dtype=jnp.int32)
    positions = (jnp.arange(SEQ, dtype=jnp.int32)[None, :] + offs).astype(jnp.int32)
    gb_i = 0.1 * jax.random.normal(ks[23], (NB, H), f32)
    gb_f = 3.0 + 0.1 * jax.random.normal(ks[24], (NB, H), f32)
    return {
        'x': x, 'mem': mem, 'positions': positions,
        'a_pre_g': gain(ks[3], (NA, D_MODEL)),
        'a_w_in': nrm(ks[4], (NA, D_MODEL, A_IN_COLS), D_MODEL),
        'a_q_a_g': gain(ks[5], (NA, Q_LORA_RANK)),
        'a_w_uq': nrm(ks[6], (NA, Q_LORA_RANK, MLA_HEADS * (QK_NOPE_DIM + QK_ROPE_DIM)), Q_LORA_RANK),
        'a_kv_a_g': gain(ks[7], (NA, KV_LORA_RANK)),
        'a_w_ukv': nrm(ks[8], (NA, KV_LORA_RANK, MLA_HEADS * (QK_NOPE_DIM + V_HEAD_DIM)), KV_LORA_RANK),
        'a_mem_g': gain(ks[9], (NA, D_MODEL)),
        'a_w_mem_kv': nrm(ks[10], (NA, D_MODEL, 2 * MEM_WIDTH), D_MODEL),
        'a_w_out': nrm(ks[11], (NA, BRANCH_WIDTH, D_MODEL), BRANCH_WIDTH),
        'a_post_g': gain(ks[12], (NA, D_MODEL)),
        'b_pre_g': gain(ks[13], (NB, D_MODEL)),
        'b_w_in': nrm(ks[14], (NB, D_MODEL, B_IN_COLS), D_MODEL),
        'b_gate_bias': jnp.concatenate([gb_i, gb_f], axis=-1),
        'b_conv_w': nrm(ks[15], (NB, CONV_WIDTH, MIX_WIDTH), CONV_WIDTH),
        'b_conv_b': 0.01 * jax.random.normal(ks[16], (NB, MIX_WIDTH), f32),
        'b_w_q': nrm(ks[17], (NB, H, MLSTM_V_DIM, MLSTM_QK_DIM), MLSTM_V_DIM),
        'b_w_k': nrm(ks[18], (NB, H, MLSTM_V_DIM, MLSTM_QK_DIM), MLSTM_V_DIM),
        'b_w_v': nrm(ks[19], (NB, H, MLSTM_V_DIM, MLSTM_V_DIM), MLSTM_V_DIM),
        'b_head_g': gain(ks[20], (NB, H, MLSTM_V_DIM)),
        'b_skip': gain(ks[21], (NB, MIX_WIDTH)),
        'b_mem_g': gain(ks[22], (NB, D_MODEL)),
        'b_w_mem_kv': nrm(ks[25], (NB, D_MODEL, 2 * MEM_WIDTH), D_MODEL),
        'b_w_out': nrm(ks[26], (NB, BRANCH_WIDTH, D_MODEL), BRANCH_WIDTH),
        'b_post_g': gain(ks[27], (NB, D_MODEL)),
    }


def reference(x, mem, positions, a_pre_g, a_w_in, a_q_a_g, a_w_uq, a_kv_a_g, a_w_ukv, a_mem_g,
              a_w_mem_kv, a_w_out, a_post_g, b_pre_g, b_w_in, b_gate_bias, b_conv_w, b_conv_b,
              b_w_q, b_w_k, b_w_v, b_head_g, b_skip, b_mem_g, b_w_mem_kv, b_w_out, b_post_g):
    cos, sin = rope_tables(positions)
    for i in range(DEPTH):
        j = i // 2
        if i % 2 == 0:
            x = mla_layer(x, mem, cos, sin, a_pre_g[j], a_w_in[j], a_q_a_g[j], a_w_uq[j],
                          a_kv_a_g[j], a_w_ukv[j], a_mem_g[j], a_w_mem_kv[j], a_w_out[j], a_post_g[j])
        else:
            x = mlstm_layer(x, mem, b_pre_g[j], b_w_in[j], b_gate_bias[j], b_conv_w[j], b_conv_b[j],
                            b_w_q[j], b_w_k[j], b_w_v[j], b_head_g[j], b_skip[j], b_mem_g[j],
                            b_w_mem_kv[j], b_w_out[j], b_post_g[j])
    return x
```

```python
import functools
import math

import jax
import jax.numpy as jnp
import numpy as np
from jax import lax
from jax.experimental import pallas as pl
from jax.experimental.pallas import tpu as pltpu

EPS = 1e-6
ROPE_THETA = 10000.0
MEM_HEADS = 4
MEM_HEAD_DIM = 64
MEM_WIDTH = MEM_HEADS * MEM_HEAD_DIM
QK_NOPE_DIM = 128
QK_ROPE_DIM = 64
V_HEAD_DIM = 128
MLA_HEADS = 6
Q_LORA_RANK = 384
KV_LORA_RANK = 256
MLSTM_HEADS = 4
MLSTM_V_DIM = 192
MLSTM_QK_DIM = 96
CONV_WIDTH = 4
MIX_WIDTH = 768

LANES = 128
MXU_DIM = 256
BF16_SUBLANES = 16
VMEM_LIMIT_BYTES = 56 * 1024 * 1024

ROW_TILE = 512
ATTN_TQ = 512
ATTN_TK = 512
MLSTM_CHUNK = 256
ROPE_ROWS = 1024

LOG2E = 1.4426950408889634
MLSTM_PAD_QK = LANES
MLSTM_PAD_V = MXU_DIM
MLSTM_HEAD_START = tuple((h * MLSTM_V_DIM // LANES) * LANES for h in range(MLSTM_HEADS))

F32 = jnp.float32
BF16 = jnp.bfloat16


def _dot(a, b):
    return jnp.dot(a, b, preferred_element_type=F32)


def _dot_nt(a, b):
    return lax.dot_general(a, b, (((1,), (1,)), ((), ())), preferred_element_type=F32)


def _rms(x, g, width=None):
    width = x.shape[-1] if width is None else width
    ms = jnp.sum(x * x, axis=-1, keepdims=True) * (1.0 / width)
    return x * lax.rsqrt(ms + EPS) * g


def _params(*semantics):
    return pltpu.CompilerParams(dimension_semantics=semantics, vmem_limit_bytes=VMEM_LIMIT_BYTES)


def _const_spec(shape):
    zeros = (0,) * len(shape)
    return pl.BlockSpec(shape, lambda *_: zeros)


def _rope_kernel(pos_ref, invf_ref, cos_ref, sin_ref):
    ang = pos_ref[...].astype(F32) * invf_ref[...]
    cos_ref[...] = jnp.cos(ang)
    sin_ref[...] = jnp.sin(ang)


def _rope_tables(positions):
    B, S = positions.shape
    T = B * S
    half = QK_ROPE_DIM // 2
    per_row = LANES // half
    rows = T // per_row
    inv_freq = ROPE_THETA ** (-jnp.arange(0, QK_ROPE_DIM, 2, dtype=F32) / QK_ROPE_DIM)
    pos4 = jnp.repeat(positions.reshape(rows, per_row), half, axis=1)
    invf = jnp.tile(inv_freq, per_row).reshape(1, LANES)
    rb = min(ROPE_ROWS, rows)
    cos4, sin4 = pl.pallas_call(
        _rope_kernel,
        grid=(rows // rb,),
        in_specs=[pl.BlockSpec((rb, LANES), lambda i: (i, 0)), _const_spec((1, LANES))],
        out_specs=[pl.BlockSpec((rb, LANES), lambda i: (i, 0))] * 2,
        out_shape=[jax.ShapeDtypeStruct((rows, LANES), F32)] * 2,
        compiler_params=_params("parallel"),
        name="rope_tables",
    )(pos4, invf)
    cos = cos4.reshape(B, S, half)
    sin = sin4.reshape(B, S, half)
    return jnp.concatenate([cos, cos, -sin, sin], axis=-1)


def _mem_kv_kernel(mem_ref, g_ref, w_ref, k_ref, v_ref):
    n_mem = mem_ref.shape[1]
    hn = _rms(mem_ref[0], g_ref[...]).astype(BF16)
    kv = _dot(hn, w_ref[...])
    k = kv[:, :MEM_WIDTH] * (MEM_HEAD_DIM ** -0.5 * LOG2E)
    v = kv[:, MEM_WIDTH:]
    col_head = lax.broadcasted_iota(jnp.int32, (n_mem, MEM_WIDTH), 1) // MEM_HEAD_DIM
    for h in range(MEM_HEADS):
        rows = pl.ds(h * n_mem, n_mem)
        k_ref[0, rows, :] = jnp.where(col_head == h, k, 0.0).astype(BF16)
        v_ref[0, rows, :] = jnp.where(col_head == h, v, 0.0).astype(BF16)


def _mem_kv(mem, mem_g, w_mem_kv):
    B, n_mem, D = mem.shape
    out = jax.ShapeDtypeStruct((B, MEM_HEADS * n_mem, MEM_WIDTH), BF16)
    spec = pl.BlockSpec((1, MEM_HEADS * n_mem, MEM_WIDTH), lambda b: (b, 0, 0))
    return pl.pallas_call(
        _mem_kv_kernel,
        grid=(B,),
        in_specs=[pl.BlockSpec((1, n_mem, D), lambda b: (b, 0, 0)), _const_spec((1, D)),
                  _const_spec((D, 2 * MEM_WIDTH))],
        out_specs=[spec, spec],
        out_shape=[out, out],
        compiler_params=_params("parallel"),
        name="mem_kv",
    )(mem, mem_g.reshape(1, D), w_mem_kv.astype(BF16))


A_CQ = (0, Q_LORA_RANK)
A_CKV = (A_CQ[1], A_CQ[1] + KV_LORA_RANK)
A_QMEM = (A_CKV[1], A_CKV[1] + MEM_WIDTH)
A_GATE = (A_QMEM[1], A_QMEM[1] + 1024)
A_KROPE = (A_GATE[1], A_GATE[1] + 2 * QK_ROPE_DIM)
A_COLS = A_KROPE[1]
Q_ROPE_OFF = MLA_HEADS * QK_NOPE_DIM


def _mla_in_kernel(x_ref, cs_ref, pre_g_ref, w_in_ref, qa_g_ref, w_uq_ref, kva_g_ref, w_ukv_ref,
                   q_ref, k_ref, v_ref, qmem_ref, gate_ref):
    q_scale = (QK_NOPE_DIM + QK_ROPE_DIM) ** -0.5 * LOG2E
    h = _rms(x_ref[0], pre_g_ref[...]).astype(BF16)
    p = _dot(h, w_in_ref[...])
    cs = cs_ref[0]
    qmem_ref[0] = p[:, A_QMEM[0]:A_QMEM[1]].astype(BF16)
    gate = p[:, A_GATE[0]:A_GATE[1]]
    gate_ref[0] = (gate * jax.nn.sigmoid(gate)).astype(BF16)

    c_q = _rms(p[:, A_CQ[0]:A_CQ[1]], qa_g_ref[...]).astype(BF16)
    q = _dot(c_q, w_uq_ref[...])
    c_kv = _rms(p[:, A_CKV[0]:A_CKV[1]], kva_g_ref[...]).astype(BF16)
    kv = _dot(c_kv, w_ukv_ref[...])

    kr = p[:, A_KROPE[0]:A_KROPE[1]] * cs
    k_rot = (kr + pltpu.roll(kr, QK_ROPE_DIM, 1)).astype(BF16)
    for hd in range(MLA_HEADS):
        nope = slice(hd * QK_NOPE_DIM, (hd + 1) * QK_NOPE_DIM)
        rope = slice(Q_ROPE_OFF + hd * LANES, Q_ROPE_OFF + (hd + 1) * LANES)
        q_ref[0, hd, :, 0:LANES] = (q[:, nope] * q_scale).astype(BF16)
        q_ref[0, hd, :, LANES:2 * LANES] = (q[:, rope] * cs * q_scale).astype(BF16)
        k_ref[0, hd, :, 0:LANES] = kv[:, 2 * hd * LANES:(2 * hd + 1) * LANES].astype(BF16)
        k_ref[0, hd, :, LANES:2 * LANES] = k_rot
        v_ref[0, hd] = kv[:, (2 * hd + 1) * LANES:(2 * hd + 2) * LANES].astype(BF16)


def _swap_halves(idx):
    half = len(idx) // 2
    return np.concatenate([idx[half:], idx[:half]])


def _mla_in(x, cs, pre_g, w_in, q_a_g, w_uq, kv_a_g, w_ukv):
    B, S, D = x.shape
    tm = min(ROW_TILE, S)
    o_cq, o_ckv, o_kr, o_qm, o_gate = np.cumsum([0, Q_LORA_RANK, KV_LORA_RANK, QK_ROPE_DIM, MEM_WIDTH])
    kr_idx = np.arange(o_kr, o_kr + QK_ROPE_DIM)
    in_idx = np.concatenate([np.arange(o_cq, o_kr), np.arange(o_qm, o_gate + 1024), kr_idx,
                             _swap_halves(kr_idx)])
    w_in_p = w_in[:, in_idx].astype(BF16)
    head_w = QK_NOPE_DIM + QK_ROPE_DIM
    nope_idx = np.concatenate([np.arange(h * head_w, h * head_w + QK_NOPE_DIM) for h in range(MLA_HEADS)])
    rope_idx = []
    for h in range(MLA_HEADS):
        r = np.arange(h * head_w + QK_NOPE_DIM, (h + 1) * head_w)
        rope_idx += [r, _swap_halves(r)]
    uq_idx = np.concatenate([nope_idx] + rope_idx)
    w_uq_p = w_uq[:, uq_idx].astype(BF16)
    w_ukv_p = w_ukv.astype(BF16)

    row = lambda b, i: (b, i, 0)
    head_row = lambda b, i: (b, 0, i, 0)
    out_shape = [
        jax.ShapeDtypeStruct((B, MLA_HEADS, S, 2 * LANES), BF16),
        jax.ShapeDtypeStruct((B, MLA_HEADS, S, 2 * LANES), BF16),
        jax.ShapeDtypeStruct((B, MLA_HEADS, S, V_HEAD_DIM), BF16),
        jax.ShapeDtypeStruct((B, S, MEM_WIDTH), BF16),
        jax.ShapeDtypeStruct((B, S, 1024), BF16),
    ]
    out_specs = [
        pl.BlockSpec((1, MLA_HEADS, tm, 2 * LANES), head_row),
        pl.BlockSpec((1, MLA_HEADS, tm, 2 * LANES), head_row),
        pl.BlockSpec((1, MLA_HEADS, tm, V_HEAD_DIM), head_row),
        pl.BlockSpec((1, tm, MEM_WIDTH), row),
        pl.BlockSpec((1, tm, 1024), row),
    ]
    return pl.pallas_call(
        _mla_in_kernel,
        grid=(B, S // tm),
        in_specs=[pl.BlockSpec((1, tm, D), row), pl.BlockSpec((1, tm, LANES), row),
                  _const_spec((1, D)), _const_spec(w_in_p.shape),
                  _const_spec((1, Q_LORA_RANK)), _const_spec(w_uq_p.shape),
                  _const_spec((1, KV_LORA_RANK)), _const_spec(w_ukv_p.shape)],
        out_specs=out_specs,
        out_shape=out_shape,
        compiler_params=_params("parallel", "parallel"),
        name="mla_in",
    )(x, cs, pre_g.reshape(1, D), w_in_p, q_a_g.reshape(1, -1), w_uq_p, kv_a_g.reshape(1, -1), w_ukv_p)


def _attn_kernel(q_ref, k_ref, v_ref, o_ref, *, tq, tk):
    S = q_ref.shape[2]
    dv = v_ref.shape[3]
    per_q = tq // tk
    row = lax.broadcasted_iota(jnp.int32, (tq, tk), 0)
    col = lax.broadcasted_iota(jnp.int32, (tq, tk), 1)

    def q_tile(qi, _):
        q = q_ref[0, 0, pl.ds(pl.multiple_of(qi * tq, tq), tq), :]

        def update(ki, carry, diag_off=None):
            m, l, acc = carry
            rows = pl.ds(pl.multiple_of(ki * tk, tk), tk)
            s = _dot_nt(q, k_ref[0, 0, rows, :])
            if diag_off is not None:
                s = jnp.where(col + diag_off * tk <= row, s, -jnp.inf)
            m_new = jnp.maximum(m, jnp.max(s, axis=-1, keepdims=True))
            alpha = jnp.exp2(m - m_new)
            p = jnp.exp2(s - m_new)
            l = alpha * l + jnp.sum(p, axis=-1, keepdims=True)
            acc = alpha * acc + _dot(p.astype(BF16), v_ref[0, 0, rows, :])
            return m_new, l, acc

        init = (jnp.full((tq, 1), -jnp.inf, F32), jnp.zeros((tq, 1), F32), jnp.zeros((tq, dv), F32))
        carry = lax.fori_loop(0, qi * per_q, update, init)
        for j in range(per_q):
            carry = update(qi * per_q + j, carry, diag_off=j)
        _, l, acc = carry
        o_ref[0, pl.ds(pl.multiple_of(qi * tq, tq), tq), :] = (acc / l).astype(o_ref.dtype)
        return 0

    lax.fori_loop(0, S // tq, q_tile, 0)


def _attention(q, k, v):
    B, H, S, dqk = q.shape
    dv = v.shape[-1]
    tq = min(ATTN_TQ, S)
    tk = min(ATTN_TK, tq)
    head = lambda b, h: (b, h, 0, 0)
    return pl.pallas_call(
        functools.partial(_attn_kernel, tq=tq, tk=tk),
        grid=(B, H),
        in_specs=[pl.BlockSpec((1, 1, S, dqk), head), pl.BlockSpec((1, 1, S, dqk), head),
                  pl.BlockSpec((1, 1, S, dv), head)],
        out_specs=pl.BlockSpec((1, S, dv), lambda b, h: (b, 0, h)),
        out_shape=jax.ShapeDtypeStruct((B, S, H * dv), BF16),
        compiler_params=_params("parallel", "parallel"),
        name="mla_attention",
    )(q, k, v)


def _memory_attention(qmem, kexp, vexp):
    n_mem = kexp.shape[0] // MEM_HEADS
    s = _dot_nt(qmem, kexp)
    probs = []
    for h in range(MEM_HEADS):
        sh = s[:, h * n_mem:(h + 1) * n_mem]
        e = jnp.exp2(sh - jnp.max(sh, axis=-1, keepdims=True))
        probs.append((e / jnp.sum(e, axis=-1, keepdims=True)).astype(BF16))
    return _dot(jnp.concatenate(probs, axis=-1), vexp)


def _tail(mix, x_ref, qmem_ref, gate_ref, kexp_ref, vexp_ref, w_out_ref, post_g_ref, o_ref):
    mo = _memory_attention(qmem_ref[0], kexp_ref[0], vexp_ref[0])
    gate = gate_ref[0].astype(F32)
    y_mix = (mix * gate[:, :MIX_WIDTH]).astype(BF16)
    y_mem = (mo * gate[:, MIX_WIDTH:]).astype(BF16)
    y = _dot(y_mix, w_out_ref[0:MIX_WIDTH, :]) + _dot(y_mem, w_out_ref[MIX_WIDTH:, :])
    o_ref[0] = x_ref[0] + _rms(y, post_g_ref[...])


def _mla_out_kernel(x_ref, mix_ref, qmem_ref, gate_ref, kexp_ref, vexp_ref, w_out_ref, post_g_ref, o_ref):
    _tail(mix_ref[0].astype(F32), x_ref, qmem_ref, gate_ref, kexp_ref, vexp_ref, w_out_ref, post_g_ref,
          o_ref)


def _mlstm_out_kernel(x_ref, hn_ref, og_ref, uc_ref, skip_ref, qmem_ref, gate_ref, kexp_ref, vexp_ref,
                      w_out_ref, post_g_ref, o_ref):
    hn = jnp.concatenate([hn_ref[0, h].astype(F32)[:, 0:MLSTM_V_DIM] for h in range(MLSTM_HEADS)], axis=-1)
    mix = og_ref[0].astype(F32) * hn + skip_ref[...] * uc_ref[0].astype(F32)
    _tail(mix, x_ref, qmem_ref, gate_ref, kexp_ref, vexp_ref, w_out_ref, post_g_ref, o_ref)


def _layer_out(kernel_fn, name, x, mixer_inputs, mixer_specs, qmem, gate, kexp, vexp, w_out, post_g):
    B, S, D = x.shape
    tm = min(ROW_TILE, S)
    row = lambda b, i: (b, i, 0)
    per_batch = lambda b, i: (b, 0, 0)
    in_specs = ([pl.BlockSpec((1, tm, D), row)] + mixer_specs(tm) +
                [pl.BlockSpec((1, tm, MEM_WIDTH), row), pl.BlockSpec((1, tm, gate.shape[-1]), row),
                 pl.BlockSpec((1,) + kexp.shape[1:], per_batch), pl.BlockSpec((1,) + vexp.shape[1:], per_batch),
                 _const_spec(w_out.shape), _const_spec((1, D))])
    return pl.pallas_call(
        kernel_fn,
        grid=(B, S // tm),
        in_specs=in_specs,
        out_specs=pl.BlockSpec((1, tm, D), row),
        out_shape=jax.ShapeDtypeStruct((B, S, D), F32),
        compiler_params=_params("parallel", "parallel"),
        name=name,
    )(x, *mixer_inputs, qmem, gate, kexp, vexp, w_out.astype(BF16), post_g.reshape(1, D))


B_U = (0, MIX_WIDTH)
B_O = (B_U[1], B_U[1] + MIX_WIDTH)
B_QMEM = (B_O[1], B_O[1] + MEM_WIDTH)
B_GATE = (B_QMEM[1], B_QMEM[1] + 1024)
B_IF = (B_GATE[1], B_GATE[1] + LANES)
B_COLS = B_IF[1]


def _mlstm_in_kernel(x_ref, pre_g_ref, w_in_ref, u_ref, og_ref, qmem_ref, gate_ref, if_ref):
    h = _rms(x_ref[0], pre_g_ref[...]).astype(BF16)
    p = _dot(h, w_in_ref[...])
    u_ref[0] = p[:, B_U[0]:B_U[1]].astype(BF16)
    og_ref[0] = jax.nn.sigmoid(p[:, B_O[0]:B_O[1]]).astype(BF16)
    qmem_ref[0] = p[:, B_QMEM[0]:B_QMEM[1]].astype(BF16)
    gate = p[:, B_GATE[0]:B_GATE[1]]
    gate_ref[0] = (gate * jax.nn.sigmoid(gate)).astype(BF16)
    if_ref[0] = p[:, B_IF[0]:B_IF[1]]


def _mlstm_in(x, pre_g, w_in):
    B, S, D = x.shape
    tm = min(ROW_TILE, S)
    n_if = 2 * MLSTM_HEADS
    o_u, o_if, o_o, o_qm, o_gate = np.cumsum([0, MIX_WIDTH, n_if, MIX_WIDTH, MEM_WIDTH])
    w_in_p = jnp.concatenate(
        [w_in[:, o_u:o_if], w_in[:, o_o:o_gate + 1024], w_in[:, o_if:o_o],
         jnp.zeros((D, LANES - n_if), w_in.dtype)], axis=1).astype(BF16)
    row = lambda b, i: (b, i, 0)
    widths = [(MIX_WIDTH, BF16), (MIX_WIDTH, BF16), (MEM_WIDTH, BF16), (1024, BF16), (LANES, F32)]
    return pl.pallas_call(
        _mlstm_in_kernel,
        grid=(B, S // tm),
        in_specs=[pl.BlockSpec((1, tm, D), row), _const_spec((1, D)), _const_spec(w_in_p.shape)],
        out_specs=[pl.BlockSpec((1, tm, w), row) for w, _ in widths],
        out_shape=[jax.ShapeDtypeStruct((B, S, w), dt) for w, dt in widths],
        compiler_params=_params("parallel", "parallel"),
        name="mlstm_in",
    )(x, pre_g.reshape(1, D), w_in_p)


def _split3_dot(a, b):
    b1 = b.astype(BF16)
    r1 = b - b1.astype(F32)
    b2 = r1.astype(BF16)
    b3 = (r1 - b2.astype(F32)).astype(BF16)
    return _dot(a, b1) + _dot(a, b2) + _dot(a, b3)


def _mlstm_prep_kernel(u_ref, halo_ref, if_ref, conv_w_ref, conv_b_ref, bias_ref, wqk_ref, wv_ref,
                       uc_ref, q_ref, kt_ref, v_ref, gc_ref, gr_ref, *, chunk):
    ts = u_ref.shape[1]
    halo_rows = halo_ref.shape[1]
    u_bf = u_ref[0]
    u = u_bf.astype(F32)
    halo = halo_ref[0].astype(F32) * (pl.program_id(1) > 0).astype(F32)
    ext = jnp.concatenate([halo, u], axis=0)
    conv = conv_b_ref[...] + u * conv_w_ref[CONV_WIDTH - 1:CONV_WIDTH, :]
    for back in range(1, CONV_WIDTH):
        tap = CONV_WIDTH - 1 - back
        conv = conv + ext[halo_rows - back:halo_rows - back + ts, :] * conv_w_ref[tap:tap + 1, :]
    uc = conv * jax.nn.sigmoid(conv)
    uc_ref[0] = uc.astype(BF16)
    uc_bf = uc.astype(BF16)

    k_scale = MLSTM_QK_DIM ** -0.5
    for h in range(MLSTM_HEADS):
        cols = slice(MLSTM_HEAD_START[h], MLSTM_HEAD_START[h] + MXU_DIM)
        qk = _dot(uc_bf[:, cols], wqk_ref[h])
        q_ref[0, h] = qk[:, :MLSTM_PAD_QK].astype(BF16)
        kt_ref[0, h] = (qk[:, MLSTM_PAD_QK:] * k_scale).T.astype(BF16)
        v_ref[0, h] = _dot(u_bf[:, cols], wv_ref[h]).astype(BF16)

    g = if_ref[0] + bias_ref[...]
    lane = lax.broadcasted_iota(jnp.int32, g.shape, 1)
    log_f = jnp.minimum(g, 0.0) - jnp.log1p(jnp.exp(-jnp.abs(g)))
    is_f = (lane >= MLSTM_HEADS) & (lane < 2 * MLSTM_HEADS)
    gates = jnp.where(lane < MLSTM_HEADS, g, jnp.where(is_f, log_f, 0.0))
    r = lax.broadcasted_iota(jnp.int32, (chunk, chunk), 0)
    c = lax.broadcasted_iota(jnp.int32, (chunk, chunk), 1)
    tril = (c <= r).astype(BF16)
    chunk_lane = lax.broadcasted_iota(jnp.int32, (chunk, LANES), 1)
    chunk_is_f = (chunk_lane >= MLSTM_HEADS) & (chunk_lane < 2 * MLSTM_HEADS)
    parts = []
    for j in range(ts // chunk):
        gj = gates[j * chunk:(j + 1) * chunk, :]
        parts.append(jnp.where(chunk_is_f, _split3_dot(tril, gj), gj))
    gc = jnp.concatenate(parts, axis=0) if len(parts) > 1 else parts[0]
    gc_ref[0] = gc
    gr_ref[0] = gc.T[0:2 * MLSTM_HEADS, :]


def _mlstm_prep(u, if_pre, gate_bias, conv_w, conv_b, w_q, w_k, w_v):
    B, S, _ = u.shape
    ts = min(ROW_TILE, S)
    chunk = min(MLSTM_CHUNK, S)
    H = MLSTM_HEADS
    wqk = jnp.zeros((H, MXU_DIM, 2 * MLSTM_PAD_QK), F32)
    wv = jnp.zeros((H, MXU_DIM, MLSTM_PAD_V), F32)
    for h in range(H):
        off = h * MLSTM_V_DIM - MLSTM_HEAD_START[h]
        wqk = wqk.at[h, off:off + MLSTM_V_DIM, 0:MLSTM_QK_DIM].set(w_q[h])
        wqk = wqk.at[h, off:off + MLSTM_V_DIM, MLSTM_PAD_QK:MLSTM_PAD_QK + MLSTM_QK_DIM].set(w_k[h])
        wv = wv.at[h, off:off + MLSTM_V_DIM, 0:MLSTM_V_DIM].set(w_v[h])
    bias = jnp.pad(gate_bias, (0, LANES - 2 * H)).reshape(1, LANES)
    halo_rows = BF16_SUBLANES
    blocks_per_tile = ts // halo_rows
    row = lambda b, i: (b, i, 0)
    head_row = lambda b, i: (b, 0, i, 0)
    out_shape = [
        jax.ShapeDtypeStruct((B, S, MIX_WIDTH), BF16),
        jax.ShapeDtypeStruct((B, H, S, MLSTM_PAD_QK), BF16),
        jax.ShapeDtypeStruct((B, H, MLSTM_PAD_QK, S), BF16),
        jax.ShapeDtypeStruct((B, H, S, MLSTM_PAD_V), BF16),
        jax.ShapeDtypeStruct((B, S, LANES), F32),
        jax.ShapeDtypeStruct((B, 2 * H, S), F32),
    ]
    out_specs = [
        pl.BlockSpec((1, ts, MIX_WIDTH), row),
        pl.BlockSpec((1, H, ts, MLSTM_PAD_QK), head_row),
        pl.BlockSpec((1, H, MLSTM_PAD_QK, ts), lambda b, i: (b, 0, 0, i)),
        pl.BlockSpec((1, H, ts, MLSTM_PAD_V), head_row),
        pl.BlockSpec((1, ts, LANES), row),
        pl.BlockSpec((1, 2 * H, ts), lambda b, i: (b, 0, i)),
    ]
    return pl.pallas_call(
        functools.partial(_mlstm_prep_kernel, chunk=chunk),
        grid=(B, S // ts),
        in_specs=[pl.BlockSpec((1, ts, MIX_WIDTH), row),
                  pl.BlockSpec((1, halo_rows, MIX_WIDTH),
                               lambda b, i: (b, jnp.maximum(i * blocks_per_tile - 1, 0), 0)),
                  pl.BlockSpec((1, ts, LANES), row),
                  _const_spec((CONV_WIDTH, MIX_WIDTH)), _const_spec((1, MIX_WIDTH)), _const_spec((1, LANES)),
                  _const_spec(wqk.shape), _const_spec(wv.shape)],
        out_specs=out_specs,
        out_shape=out_shape,
        compiler_params=_params("parallel", "parallel"),
        name="mlstm_prep",
    )(u, u, if_pre, conv_w, conv_b.reshape(1, -1), bias, wqk.astype(BF16), wv.astype(BF16))


def _mlstm_kernel(q_ref, kt_ref, v_ref, gc_ref, gr_ref, hg_ref, o_ref, *, chunk):
    S = q_ref.shape[2]
    L = chunk
    head = pl.program_id(1)
    tri = lax.broadcasted_iota(jnp.int32, (L, L), 1) <= lax.broadcasted_iota(jnp.int32, (L, L), 0)
    gate_lane = lax.broadcasted_iota(jnp.int32, (L, LANES), 1)
    vcol = lax.broadcasted_iota(jnp.int32, (L, MLSTM_PAD_V), 1)
    head_g = hg_ref[0]

    def step(c, carry):
        C, m = carry
        rows = pl.ds(pl.multiple_of(c * L, L), L)
        qc = q_ref[0, 0, rows, :]
        ktc = kt_ref[0, 0, :, rows]
        vc = jnp.where(vcol == MLSTM_V_DIM, 1.0, v_ref[0, 0, rows, :]).astype(BF16)
        b_col = jnp.sum(jnp.where(gate_lane == MLSTM_HEADS + head, gc_ref[0, rows, :], 0.0),
                        axis=1, keepdims=True)
        li_row = gr_ref[0, pl.ds(head, 1), rows]
        b_row = gr_ref[0, pl.ds(MLSTM_HEADS + head, 1), rows]

        d = jnp.where(tri, b_col - b_row + li_row, -jnp.inf)
        inter = b_col + m
        m_t = jnp.maximum(inter, jnp.max(d, axis=1, keepdims=True))
        w_intra = jnp.exp(d - m_t)
        w_inter = jnp.exp(inter - m_t)
        sqk = (_dot(qc, ktc) * w_intra).astype(BF16)
        num = w_inter * _dot(qc, C.astype(BF16)) + _dot(sqk, vc)
        den = jnp.sum(jnp.where(vcol == MLSTM_V_DIM, num, 0.0), axis=1, keepdims=True)
        den = jnp.maximum(jnp.abs(den), jnp.exp(-m_t))
        hv = jnp.where(vcol < MLSTM_V_DIM, num / den, 0.0)
        o_ref[0, 0, rows, :] = _rms(hv, head_g, MLSTM_V_DIM).astype(o_ref.dtype)

        b_last = b_row[:, L - 1:L]
        dec = b_last - b_row + li_row
        m_new = jnp.maximum(b_last + m, jnp.max(dec, axis=1, keepdims=True))
        decay = jnp.exp(b_last + m - m_new)
        ws = jnp.exp(dec - m_new)
        kw = (ktc.astype(F32) * ws).astype(BF16)
        return decay * C + _dot(kw, vc), m_new

    init = (jnp.zeros((MLSTM_PAD_QK, MLSTM_PAD_V), F32), jnp.zeros((1, 1), F32))
    lax.fori_loop(0, S // L, step, init)


def _mlstm(q, kt, v, gc, gr, head_g):
    B, H, S, _ = q.shape
    chunk = min(MLSTM_CHUNK, S)
    hg = jnp.pad(head_g, ((0, 0), (0, MLSTM_PAD_V - MLSTM_V_DIM))).reshape(H, 1, MLSTM_PAD_V)
    head = lambda b, h: (b, h, 0, 0)
    return pl.pallas_call(
        functools.partial(_mlstm_kernel, chunk=chunk),
        grid=(B, H),
        in_specs=[pl.BlockSpec((1, 1, S, MLSTM_PAD_QK), head),
                  pl.BlockSpec((1, 1, MLSTM_PAD_QK, S), head),
                  pl.BlockSpec((1, 1, S, MLSTM_PAD_V), head),
                  pl.BlockSpec((1, S, LANES), lambda b, h: (b, 0, 0)),
                  pl.BlockSpec((1, 2 * H, S), lambda b, h: (b, 0, 0)),
                  pl.BlockSpec((1, 1, MLSTM_PAD_V), lambda b, h: (h, 0, 0))],
        out_specs=pl.BlockSpec((1, 1, S, MLSTM_PAD_V), head),
        out_shape=jax.ShapeDtypeStruct((B, H, S, MLSTM_PAD_V), BF16),
        compiler_params=_params("parallel", "arbitrary"),
        name="mlstm_scan",
    )(q, kt, v, gc, gr, hg)


def _mla_layer(x, mem, cs, pre_g, w_in, q_a_g, w_uq, kv_a_g, w_ukv, mem_g, w_mem_kv, w_out, post_g):
    q, k, v, qmem, gate = _mla_in(x, cs, pre_g, w_in, q_a_g, w_uq, kv_a_g, w_ukv)
    mix = _attention(q, k, v)
    kexp, vexp = _mem_kv(mem, mem_g, w_mem_kv)
    specs = lambda tm: [pl.BlockSpec((1, tm, MIX_WIDTH), lambda b, i: (b, i, 0))]
    return _layer_out(_mla_out_kernel, "mla_out", x, [mix], specs, qmem, gate, kexp, vexp, w_out, post_g)


def _mlstm_layer(x, mem, pre_g, w_in, gate_bias, conv_w, conv_b, w_q, w_k, w_v, head_g, skip,
                 mem_g, w_mem_kv, w_out, post_g):
    u, og, qmem, gate, if_pre = _mlstm_in(x, pre_g, w_in)
    uc, q, kt, v, gc, gr = _mlstm_prep(u, if_pre, gate_bias, conv_w, conv_b, w_q, w_k, w_v)
    hn = _mlstm(q, kt, v, gc, gr, head_g)
    kexp, vexp = _mem_kv(mem, mem_g, w_mem_kv)
    row = lambda b, i: (b, i, 0)
    specs = lambda tm: [pl.BlockSpec((1, MLSTM_HEADS, tm, MLSTM_PAD_V), lambda b, i: (b, 0, i, 0)),
                        pl.BlockSpec((1, tm, MIX_WIDTH), row), pl.BlockSpec((1, tm, MIX_WIDTH), row),
                        _const_spec((1, MIX_WIDTH))]
    return _layer_out(_mlstm_out_kernel, "mlstm_out", x, [hn, og, uc, skip.reshape(1, -1)], specs,
                      qmem, gate, kexp, vexp, w_out, post_g)


def kernel(x, mem, positions, a_pre_g, a_w_in, a_q_a_g, a_w_uq, a_kv_a_g, a_w_ukv, a_mem_g, a_w_mem_kv, a_w_out, a_post_g, b_pre_g, b_w_in, b_gate_bias, b_conv_w, b_conv_b, b_w_q, b_w_k, b_w_v, b_head_g, b_skip, b_mem_g, b_w_mem_kv, b_w_out, b_post_g):
    depth = a_pre_g.shape[0] + b_pre_g.shape[0]
    cs = _rope_tables(positions)
    for i in range(depth):
        j = i // 2
        if i % 2 == 0:
            x = _mla_layer(x, mem, cs, a_pre_g[j], a_w_in[j], a_q_a_g[j], a_w_uq[j], a_kv_a_g[j],
                           a_w_ukv[j], a_mem_g[j], a_w_mem_kv[j], a_w_out[j], a_post_g[j])
        else:
            x = _mlstm_layer(x, mem, b_pre_g[j], b_w_in[j], b_gate_bias[j], b_conv_w[j], b_conv_b[j],
                             b_w_q[j], b_w_k[j], b_w_v[j], b_head_g[j], b_skip[j], b_mem_g[j],
                             b_w_mem_kv[j], b_w_out[j], b_post_g[j])
    return x
```

```python
import functools
import math

import jax
import jax.numpy as jnp
import numpy as np
from jax import lax
from jax.experimental import pallas as pl
from jax.experimental.pallas import tpu as pltpu

EPS = 1e-6
ROPE_THETA = 10000.0
MEM_HEADS = 4
MEM_HEAD_DIM = 64
MEM_WIDTH = MEM_HEADS * MEM_HEAD_DIM
QK_NOPE_DIM = 128
QK_ROPE_DIM = 64
V_HEAD_DIM = 128
MLA_HEADS = 6
V_EXT_DIM = V_HEAD_DIM + 16
Q_LORA_RANK = 384
KV_LORA_RANK = 256
MLSTM_HEADS = 4
MLSTM_V_DIM = 192
MLSTM_QK_DIM = 96
CONV_WIDTH = 4
MIX_WIDTH = 768

LANES = 128
MXU_DIM = 256
BF16_SUBLANES = 16
VMEM_LIMIT_BYTES = 56 * 1024 * 1024

ROW_TILE = 512
ATTN_TQ = 1024
ATTN_CW = 256
ATTN_TK = 512
ATTN_HEADS_PER_STEP = 2
MLSTM_CHUNK = 256
ROPE_ROWS = 1024

LOG2E = 1.4426950408889634
MLSTM_PAD_QK = LANES
MLSTM_PAD_V = MXU_DIM
MLSTM_HEAD_START = tuple((h * MLSTM_V_DIM // LANES) * LANES for h in range(MLSTM_HEADS))

F32 = jnp.float32
BF16 = jnp.bfloat16


def _dot(a, b):
    return jnp.dot(a, b, preferred_element_type=F32)


def _dot_nt(a, b):
    return lax.dot_general(a, b, (((1,), (1,)), ((), ())), preferred_element_type=F32)


def _rms(x, g, width=None):
    width = x.shape[-1] if width is None else width
    ms = jnp.sum(x * x, axis=-1, keepdims=True) * (1.0 / width)
    return x * lax.rsqrt(ms + EPS) * g


def _params(*semantics):
    return pltpu.CompilerParams(dimension_semantics=semantics, vmem_limit_bytes=VMEM_LIMIT_BYTES)


def _const_spec(shape):
    zeros = (0,) * len(shape)
    return pl.BlockSpec(shape, lambda *_: zeros)


def _rope_kernel(pos_ref, invf_ref, cos_ref, sin_ref):
    ang = pos_ref[...].astype(F32) * invf_ref[...]
    cos_ref[...] = jnp.cos(ang)
    sin_ref[...] = jnp.sin(ang)


def _rope_tables(positions):
    B, S = positions.shape
    T = B * S
    half = QK_ROPE_DIM // 2
    per_row = LANES // half
    rows = T // per_row
    inv_freq = ROPE_THETA ** (-jnp.arange(0, QK_ROPE_DIM, 2, dtype=F32) / QK_ROPE_DIM)
    pos4 = jnp.repeat(positions.reshape(rows, per_row), half, axis=1)
    invf = jnp.tile(inv_freq, per_row).reshape(1, LANES)
    rb = min(ROPE_ROWS, rows)
    cos4, sin4 = pl.pallas_call(
        _rope_kernel,
        grid=(rows // rb,),
        in_specs=[pl.BlockSpec((rb, LANES), lambda i: (i, 0)), _const_spec((1, LANES))],
        out_specs=[pl.BlockSpec((rb, LANES), lambda i: (i, 0))] * 2,
        out_shape=[jax.ShapeDtypeStruct((rows, LANES), F32)] * 2,
        compiler_params=_params("parallel"),
        name="rope_tables",
    )(pos4, invf)
    cos = cos4.reshape(B, S, half)
    sin = sin4.reshape(B, S, half)
    return jnp.concatenate([cos, cos, -sin, sin], axis=-1)


def _mem_kv_kernel(mem_ref, g_ref, w_ref, k_ref, v_ref):
    n_mem = mem_ref.shape[1]
    hn = _rms(mem_ref[0], g_ref[...]).astype(BF16)
    kv = _dot(hn, w_ref[...])
    k = kv[:, :MEM_WIDTH] * (MEM_HEAD_DIM ** -0.5 * LOG2E)
    v = kv[:, MEM_WIDTH:]
    col_head = lax.broadcasted_iota(jnp.int32, (n_mem, MEM_WIDTH), 1) // MEM_HEAD_DIM
    for h in range(MEM_HEADS):
        rows = pl.ds(h * n_mem, n_mem)
        k_ref[0, rows, :] = jnp.where(col_head == h, k, 0.0).astype(BF16)
        v_ref[0, rows, :] = jnp.where(col_head == h, v, 0.0).astype(BF16)


def _mem_kv(mem, mem_g, w_mem_kv):
    B, n_mem, D = mem.shape
    out = jax.ShapeDtypeStruct((B, MEM_HEADS * n_mem, MEM_WIDTH), BF16)
    spec = pl.BlockSpec((1, MEM_HEADS * n_mem, MEM_WIDTH), lambda b: (b, 0, 0))
    return pl.pallas_call(
        _mem_kv_kernel,
        grid=(B,),
        in_specs=[pl.BlockSpec((1, n_mem, D), lambda b: (b, 0, 0)), _const_spec((1, D)),
                  _const_spec((D, 2 * MEM_WIDTH))],
        out_specs=[spec, spec],
        out_shape=[out, out],
        compiler_params=_params("parallel"),
        name="mem_kv",
    )(mem, mem_g.reshape(1, D), w_mem_kv.astype(BF16))


A_CQ = (0, Q_LORA_RANK)
A_CKV = (A_CQ[1], A_CQ[1] + KV_LORA_RANK)
A_QMEM = (A_CKV[1], A_CKV[1] + MEM_WIDTH)
A_GATE = (A_QMEM[1], A_QMEM[1] + 1024)
A_KROPE = (A_GATE[1], A_GATE[1] + 2 * QK_ROPE_DIM)
A_COLS = A_KROPE[1]
Q_ROPE_OFF = MLA_HEADS * QK_NOPE_DIM


def _mla_in_kernel(x_ref, cs_ref, pre_g_ref, w_in_ref, qa_g_ref, w_uq_ref, kva_g_ref, w_ukv_ref,
                   q_ref, k_ref, v_ref, qmem_ref, gate_ref):
    q_scale = (QK_NOPE_DIM + QK_ROPE_DIM) ** -0.5 * LOG2E
    h = _rms(x_ref[0], pre_g_ref[...]).astype(BF16)
    p = _dot(h, w_in_ref[...])
    cs = cs_ref[0]
    qmem_ref[0] = p[:, A_QMEM[0]:A_QMEM[1]].astype(BF16)
    gate = p[:, A_GATE[0]:A_GATE[1]]
    gate_ref[0] = (gate * jax.nn.sigmoid(gate)).astype(BF16)

    c_q = _rms(p[:, A_CQ[0]:A_CQ[1]], qa_g_ref[...]).astype(BF16)
    q = _dot(c_q, w_uq_ref[...])
    c_kv = _rms(p[:, A_CKV[0]:A_CKV[1]], kva_g_ref[...]).astype(BF16)
    kv = _dot(c_kv, w_ukv_ref[...])

    kr = p[:, A_KROPE[0]:A_KROPE[1]] * cs
    k_rot = (kr + pltpu.roll(kr, QK_ROPE_DIM, 1)).astype(BF16)
    tm = cs.shape[0]
    pad_row = lax.broadcasted_iota(jnp.int32, (V_EXT_DIM - V_HEAD_DIM, tm), 0)
    ones_row = jnp.where(pad_row == 0, 1.0, 0.0).astype(BF16)
    for hd in range(MLA_HEADS):
        nope = slice(hd * QK_NOPE_DIM, (hd + 1) * QK_NOPE_DIM)
        rope = slice(Q_ROPE_OFF + hd * LANES, Q_ROPE_OFF + (hd + 1) * LANES)
        q_ref[0, hd, 0:LANES, :] = (q[:, nope] * q_scale).T.astype(BF16)
        q_ref[0, hd, LANES:2 * LANES, :] = (q[:, rope] * cs * q_scale).T.astype(BF16)
        k_ref[0, hd, :, 0:LANES] = kv[:, 2 * hd * LANES:(2 * hd + 1) * LANES].astype(BF16)
        k_ref[0, hd, :, LANES:2 * LANES] = k_rot
        v_ref[0, hd, 0:V_HEAD_DIM, :] = kv[:, (2 * hd + 1) * LANES:(2 * hd + 2) * LANES].T.astype(BF16)
        v_ref[0, hd, V_HEAD_DIM:V_EXT_DIM, :] = ones_row


def _swap_halves(idx):
    half = len(idx) // 2
    return np.concatenate([idx[half:], idx[:half]])


def _mla_in(x, cs, pre_g, w_in, q_a_g, w_uq, kv_a_g, w_ukv):
    B, S, D = x.shape
    tm = min(ROW_TILE, S)
    o_cq, o_ckv, o_kr, o_qm, o_gate = np.cumsum([0, Q_LORA_RANK, KV_LORA_RANK, QK_ROPE_DIM, MEM_WIDTH])
    kr_idx = np.arange(o_kr, o_kr + QK_ROPE_DIM)
    in_idx = np.concatenate([np.arange(o_cq, o_kr), np.arange(o_qm, o_gate + 1024), kr_idx,
                             _swap_halves(kr_idx)])
    w_in_p = w_in[:, in_idx].astype(BF16)
    head_w = QK_NOPE_DIM + QK_ROPE_DIM
    nope_idx = np.concatenate([np.arange(h * head_w, h * head_w + QK_NOPE_DIM) for h in range(MLA_HEADS)])
    rope_idx = []
    for h in range(MLA_HEADS):
        r = np.arange(h * head_w + QK_NOPE_DIM, (h + 1) * head_w)
        rope_idx += [r, _swap_halves(r)]
    uq_idx = np.concatenate([nope_idx] + rope_idx)
    w_uq_p = w_uq[:, uq_idx].astype(BF16)
    w_ukv_p = w_ukv.astype(BF16)

    row = lambda b, i: (b, i, 0)
    head_row = lambda b, i: (b, 0, i, 0)
    head_col = lambda b, i: (b, 0, 0, i)
    out_shape = [
        jax.ShapeDtypeStruct((B, MLA_HEADS, 2 * LANES, S), BF16),
        jax.ShapeDtypeStruct((B, MLA_HEADS, S, 2 * LANES), BF16),
        jax.ShapeDtypeStruct((B, MLA_HEADS, V_EXT_DIM, S), BF16),
        jax.ShapeDtypeStruct((B, S, MEM_WIDTH), BF16),
        jax.ShapeDtypeStruct((B, S, 1024), BF16),
    ]
    out_specs = [
        pl.BlockSpec((1, MLA_HEADS, 2 * LANES, tm), head_col),
        pl.BlockSpec((1, MLA_HEADS, tm, 2 * LANES), head_row),
        pl.BlockSpec((1, MLA_HEADS, V_EXT_DIM, tm), head_col),
        pl.BlockSpec((1, tm, MEM_WIDTH), row),
        pl.BlockSpec((1, tm, 1024), row),
    ]
    return pl.pallas_call(
        _mla_in_kernel,
        grid=(B, S // tm),
        in_specs=[pl.BlockSpec((1, tm, D), row), pl.BlockSpec((1, tm, LANES), row),
                  _const_spec((1, D)), _const_spec(w_in_p.shape),
                  _const_spec((1, Q_LORA_RANK)), _const_spec(w_uq_p.shape),
                  _const_spec((1, KV_LORA_RANK)), _const_spec(w_ukv_p.shape)],
        out_specs=out_specs,
        out_shape=out_shape,
        compiler_params=_params("parallel", "parallel"),
        name="mla_in",
    )(x, cs, pre_g.reshape(1, D), w_in_p, q_a_g.reshape(1, -1), w_uq_p, kv_a_g.reshape(1, -1), w_ukv_p)


def _attn_kernel(qt_ref, k_ref, vt_ref, o_ref, *, tq, tk, cw):
    heads = k_ref.shape[1]
    S = k_ref.shape[2]
    dve = vt_ref.shape[2]
    dv = V_HEAD_DIM
    nc = tq // cw
    causal = (lax.broadcasted_iota(jnp.int32, (cw, cw), 0) <= lax.broadcasted_iota(jnp.int32, (cw, cw), 1))

    def q_tile(j, _):
        q0 = pl.multiple_of(j * tq, tq)
        qts = [[qt_ref[0, h, :, pl.ds(q0 + c * cw, cw)] for c in range(nc)] for h in range(heads)]

        def softmax_pv(carry, s, vt_tile):
            m, acc = carry
            m_new = jnp.maximum(m, jnp.max(s, axis=0, keepdims=True))
            p = jnp.exp2(s - m_new).astype(BF16)
            return m_new, jnp.exp2(m - m_new) * acc + _dot(vt_tile, p)

        def body(ki, carries):
            rows = pl.ds(pl.multiple_of(ki * tk, tk), tk)
            k_tiles = [k_ref[0, h, rows, :] for h in range(heads)]
            scores = [[_dot(k_tiles[h], qts[h][c]) for c in range(nc)] for h in range(heads)]
            vt_tiles = [vt_ref[0, h, :, rows] for h in range(heads)]
            return tuple(tuple(softmax_pv(carries[h][c], scores[h][c], vt_tiles[h]) for c in range(nc))
                         for h in range(heads))

        init = tuple(tuple((jnp.full((1, cw), -jnp.inf, F32), jnp.zeros((dve, cw), F32)) for _ in range(nc))
                     for _ in range(heads))
        carries = lax.fori_loop(0, j * (tq // tk), body, init)

        chains = [(h, c) for h in range(heads) for c in range(nc)]
        scores = {}
        for h, c in chains:
            s = _dot(k_ref[0, h, pl.ds(q0, (c + 1) * cw), :], qts[h][c])
            s_diag = jnp.where(causal, s[c * cw:, :], -jnp.inf)
            scores[h, c] = jnp.concatenate([s[:c * cw, :], s_diag], axis=0) if c > 0 else s_diag
        for h, c in chains:
            _, acc = softmax_pv(carries[h][c], scores[h, c], vt_ref[0, h, :, pl.ds(q0, (c + 1) * cw)])
            out = acc[0:dv, :] / acc[dv:dv + 1, :]
            o_ref[0, pl.ds(q0 + c * cw, cw), h * dv:(h + 1) * dv] = out.T.astype(o_ref.dtype)
        return 0

    lax.fori_loop(0, S // tq, q_tile, 0)


def _attention(qt, k, vt):
    B, H, S, dqk = k.shape
    dve = vt.shape[2]
    tq = min(ATTN_TQ, S)
    tk = min(ATTN_TK, tq)
    cw = min(ATTN_CW, tq)
    hp = ATTN_HEADS_PER_STEP
    head = lambda b, h: (b, h, 0, 0)
    return pl.pallas_call(
        functools.partial(_attn_kernel, tq=tq, tk=tk, cw=cw),
        grid=(B, H // hp),
        in_specs=[pl.BlockSpec((1, hp, dqk, S), head), pl.BlockSpec((1, hp, S, dqk), head),
                  pl.BlockSpec((1, hp, dve, S), head)],
        out_specs=pl.BlockSpec((1, S, hp * V_HEAD_DIM), lambda b, h: (b, 0, h)),
        out_shape=jax.ShapeDtypeStruct((B, S, H * V_HEAD_DIM), BF16),
        compiler_params=_params("parallel", "parallel"),
        name="mla_attention",
    )(qt, k, vt)


def _memory_attention(qmem, kexp, vexp):
    n_mem = kexp.shape[0] // MEM_HEADS
    s = _dot_nt(qmem, kexp)
    probs = []
    for h in range(MEM_HEADS):
        sh = s[:, h * n_mem:(h + 1) * n_mem]
        e = jnp.exp2(sh - jnp.max(sh, axis=-1, keepdims=True))
        probs.append((e / jnp.sum(e, axis=-1, keepdims=True)).astype(BF16))
    return _dot(jnp.concatenate(probs, axis=-1), vexp)


def _tail(mix, x_ref, qmem_ref, gate_ref, kexp_ref, vexp_ref, w_out_ref, post_g_ref, o_ref):
    mo = _memory_attention(qmem_ref[0], kexp_ref[0], vexp_ref[0])
    gate = gate_ref[0].astype(F32)
    y_mix = (mix * gate[:, :MIX_WIDTH]).astype(BF16)
    y_mem = (mo * gate[:, MIX_WIDTH:]).astype(BF16)
    y = _dot(y_mix, w_out_ref[0:MIX_WIDTH, :]) + _dot(y_mem, w_out_ref[MIX_WIDTH:, :])
    o_ref[0] = x_ref[0] + _rms(y, post_g_ref[...])


def _mla_out_kernel(x_ref, mix_ref, qmem_ref, gate_ref, kexp_ref, vexp_ref, w_out_ref, post_g_ref, o_ref):
    _tail(mix_ref[0].astype(F32), x_ref, qmem_ref, gate_ref, kexp_ref, vexp_ref, w_out_ref, post_g_ref,
          o_ref)


def _mlstm_out_kernel(x_ref, hn_ref, og_ref, uc_ref, skip_ref, qmem_ref, gate_ref, kexp_ref, vexp_ref,
                      w_out_ref, post_g_ref, o_ref):
    hn = jnp.concatenate([hn_ref[0, h].astype(F32)[:, 0:MLSTM_V_DIM] for h in range(MLSTM_HEADS)], axis=-1)
    mix = og_ref[0].astype(F32) * hn + skip_ref[...] * uc_ref[0].astype(F32)
    _tail(mix, x_ref, qmem_ref, gate_ref, kexp_ref, vexp_ref, w_out_ref, post_g_ref, o_ref)


def _layer_out(kernel_fn, name, x, mixer_inputs, mixer_specs, qmem, gate, kexp, vexp, w_out, post_g):
    B, S, D = x.shape
    tm = min(ROW_TILE, S)
    row = lambda b, i: (b, i, 0)
    per_batch = lambda b, i: (b, 0, 0)
    in_specs = ([pl.BlockSpec((1, tm, D), row)] + mixer_specs(tm) +
                [pl.BlockSpec((1, tm, MEM_WIDTH), row), pl.BlockSpec((1, tm, gate.shape[-1]), row),
                 pl.BlockSpec((1,) + kexp.shape[1:], per_batch), pl.BlockSpec((1,) + vexp.shape[1:], per_batch),
                 _const_spec(w_out.shape), _const_spec((1, D))])
    return pl.pallas_call(
        kernel_fn,
        grid=(B, S // tm),
        in_specs=in_specs,
        out_specs=pl.BlockSpec((1, tm, D), row),
        out_shape=jax.ShapeDtypeStruct((B, S, D), F32),
        compiler_params=_params("parallel", "parallel"),
        name=name,
    )(x, *mixer_inputs, qmem, gate, kexp, vexp, w_out.astype(BF16), post_g.reshape(1, D))


B_U = (0, MIX_WIDTH)
B_O = (B_U[1], B_U[1] + MIX_WIDTH)
B_QMEM = (B_O[1], B_O[1] + MEM_WIDTH)
B_GATE = (B_QMEM[1], B_QMEM[1] + 1024)
B_IF = (B_GATE[1], B_GATE[1] + LANES)
B_COLS = B_IF[1]


def _mlstm_in_kernel(x_ref, pre_g_ref, w_in_ref, u_ref, og_ref, qmem_ref, gate_ref, if_ref):
    h = _rms(x_ref[0], pre_g_ref[...]).astype(BF16)
    p = _dot(h, w_in_ref[...])
    u_ref[0] = p[:, B_U[0]:B_U[1]].astype(BF16)
    og_ref[0] = jax.nn.sigmoid(p[:, B_O[0]:B_O[1]]).astype(BF16)
    qmem_ref[0] = p[:, B_QMEM[0]:B_QMEM[1]].astype(BF16)
    gate = p[:, B_GATE[0]:B_GATE[1]]
    gate_ref[0] = (gate * jax.nn.sigmoid(gate)).astype(BF16)
    if_ref[0] = p[:, B_IF[0]:B_IF[1]]


def _mlstm_in(x, pre_g, w_in):
    B, S, D = x.shape
    tm = min(ROW_TILE, S)
    n_if = 2 * MLSTM_HEADS
    o_u, o_if, o_o, o_qm, o_gate = np.cumsum([0, MIX_WIDTH, n_if, MIX_WIDTH, MEM_WIDTH])
    w_in_p = jnp.concatenate(
        [w_in[:, o_u:o_if], w_in[:, o_o:o_gate + 1024], w_in[:, o_if:o_o],
         jnp.zeros((D, LANES - n_if), w_in.dtype)], axis=1).astype(BF16)
    row = lambda b, i: (b, i, 0)
    widths = [(MIX_WIDTH, BF16), (MIX_WIDTH, BF16), (MEM_WIDTH, BF16), (1024, BF16), (LANES, F32)]
    return pl.pallas_call(
        _mlstm_in_kernel,
        grid=(B, S // tm),
        in_specs=[pl.BlockSpec((1, tm, D), row), _const_spec((1, D)), _const_spec(w_in_p.shape)],
        out_specs=[pl.BlockSpec((1, tm, w), row) for w, _ in widths],
        out_shape=[jax.ShapeDtypeStruct((B, S, w), dt) for w, dt in widths],
        compiler_params=_params("parallel", "parallel"),
        name="mlstm_in",
    )(x, pre_g.reshape(1, D), w_in_p)


def _split3_dot(a, b):
    b1 = b.astype(BF16)
    r1 = b - b1.astype(F32)
    b2 = r1.astype(BF16)
    b3 = (r1 - b2.astype(F32)).astype(BF16)
    return _dot(a, b1) + _dot(a, b2) + _dot(a, b3)


def _mlstm_prep_kernel(u_ref, halo_ref, if_ref, conv_w_ref, conv_b_ref, bias_ref, wqk_ref, wv_ref,
                       uc_ref, q_ref, kt_ref, v_ref, gc_ref, gr_ref, *, chunk):
    ts = u_ref.shape[1]
    halo_rows = halo_ref.shape[1]
    u_bf = u_ref[0]
    u = u_bf.astype(F32)
    halo = halo_ref[0].astype(F32) * (pl.program_id(1) > 0).astype(F32)
    ext = jnp.concatenate([halo, u], axis=0)
    conv = conv_b_ref[...] + u * conv_w_ref[CONV_WIDTH - 1:CONV_WIDTH, :]
    for back in range(1, CONV_WIDTH):
        tap = CONV_WIDTH - 1 - back
        conv = conv + ext[halo_rows - back:halo_rows - back + ts, :] * conv_w_ref[tap:tap + 1, :]
    uc = conv * jax.nn.sigmoid(conv)
    uc_ref[0] = uc.astype(BF16)
    uc_bf = uc.astype(BF16)

    k_scale = MLSTM_QK_DIM ** -0.5
    for h in range(MLSTM_HEADS):
        cols = slice(MLSTM_HEAD_START[h], MLSTM_HEAD_START[h] + MXU_DIM)
        qk = _dot(uc_bf[:, cols], wqk_ref[h])
        q_ref[0, h] = qk[:, :MLSTM_PAD_QK].astype(BF16)
        kt_ref[0, h] = (qk[:, MLSTM_PAD_QK:] * k_scale).T.astype(BF16)
        v_ref[0, h] = _dot(u_bf[:, cols], wv_ref[h]).astype(BF16)

    g = if_ref[0] + bias_ref[...]
    lane = lax.broadcasted_iota(jnp.int32, g.shape, 1)
    log_f = jnp.minimum(g, 0.0) - jnp.log1p(jnp.exp(-jnp.abs(g)))
    is_f = (lane >= MLSTM_HEADS) & (lane < 2 * MLSTM_HEADS)
    gates = jnp.where(lane < MLSTM_HEADS, g, jnp.where(is_f, log_f, 0.0))
    r = lax.broadcasted_iota(jnp.int32, (chunk, chunk), 0)
    c = lax.broadcasted_iota(jnp.int32, (chunk, chunk), 1)
    tril = (c <= r).astype(BF16)
    chunk_lane = lax.broadcasted_iota(jnp.int32, (chunk, LANES), 1)
    chunk_is_f = (chunk_lane >= MLSTM_HEADS) & (chunk_lane < 2 * MLSTM_HEADS)
    parts = []
    for j in range(ts // chunk):
        gj = gates[j * chunk:(j + 1) * chunk, :]
        parts.append(jnp.where(chunk_is_f, _split3_dot(tril, gj), gj))
    gc = jnp.concatenate(parts, axis=0) if len(parts) > 1 else parts[0]
    gc_ref[0] = gc
    gr_ref[0] = gc.T[0:2 * MLSTM_HEADS, :]


def _mlstm_prep(u, if_pre, gate_bias, conv_w, conv_b, w_q, w_k, w_v):
    B, S, _ = u.shape
    ts = min(ROW_TILE, S)
    chunk = min(MLSTM_CHUNK, S)
    H = MLSTM_HEADS
    wqk = jnp.zeros((H, MXU_DIM, 2 * MLSTM_PAD_QK), F32)
    wv = jnp.zeros((H, MXU_DIM, MLSTM_PAD_V), F32)
    for h in range(H):
        off = h * MLSTM_V_DIM - MLSTM_HEAD_START[h]
        wqk = wqk.at[h, off:off + MLSTM_V_DIM, 0:MLSTM_QK_DIM].set(w_q[h])
        wqk = wqk.at[h, off:off + MLSTM_V_DIM, MLSTM_PAD_QK:MLSTM_PAD_QK + MLSTM_QK_DIM].set(w_k[h])
        wv = wv.at[h, off:off + MLSTM_V_DIM, 0:MLSTM_V_DIM].set(w_v[h])
    bias = jnp.pad(gate_bias, (0, LANES - 2 * H)).reshape(1, LANES)
    halo_rows = BF16_SUBLANES
    blocks_per_tile = ts // halo_rows
    row = lambda b, i: (b, i, 0)
    head_row = lambda b, i: (b, 0, i, 0)
    out_shape = [
        jax.ShapeDtypeStruct((B, S, MIX_WIDTH), BF16),
        jax.ShapeDtypeStruct((B, H, S, MLSTM_PAD_QK), BF16),
        jax.ShapeDtypeStruct((B, H, MLSTM_PAD_QK, S), BF16),
        jax.ShapeDtypeStruct((B, H, S, MLSTM_PAD_V), BF16),
        jax.ShapeDtypeStruct((B, S, LANES), F32),
        jax.ShapeDtypeStruct((B, 2 * H, S), F32),
    ]
    out_specs = [
        pl.BlockSpec((1, ts, MIX_WIDTH), row),
        pl.BlockSpec((1, H, ts, MLSTM_PAD_QK), head_row),
        pl.BlockSpec((1, H, MLSTM_PAD_QK, ts), lambda b, i: (b, 0, 0, i)),
        pl.BlockSpec((1, H, ts, MLSTM_PAD_V), head_row),
        pl.BlockSpec((1, ts, LANES), row),
        pl.BlockSpec((1, 2 * H, ts), lambda b, i: (b, 0, i)),
    ]
    return pl.pallas_call(
        functools.partial(_mlstm_prep_kernel, chunk=chunk),
        grid=(B, S // ts),
        in_specs=[pl.BlockSpec((1, ts, MIX_WIDTH), row),
                  pl.BlockSpec((1, halo_rows, MIX_WIDTH),
                               lambda b, i: (b, jnp.maximum(i * blocks_per_tile - 1, 0), 0)),
                  pl.BlockSpec((1, ts, LANES), row),
                  _const_spec((CONV_WIDTH, MIX_WIDTH)), _const_spec((1, MIX_WIDTH)), _const_spec((1, LANES)),
                  _const_spec(wqk.shape), _const_spec(wv.shape)],
        out_specs=out_specs,
        out_shape=out_shape,
        compiler_params=_params("parallel", "parallel"),
        name="mlstm_prep",
    )(u, u, if_pre, conv_w, conv_b.reshape(1, -1), bias, wqk.astype(BF16), wv.astype(BF16))


def _mlstm_kernel(q_ref, kt_ref, v_ref, gc_ref, gr_ref, hg_ref, o_ref, *, chunk):
    S = q_ref.shape[2]
    L = chunk
    head = pl.program_id(1)
    tri = lax.broadcasted_iota(jnp.int32, (L, L), 1) <= lax.broadcasted_iota(jnp.int32, (L, L), 0)
    gate_lane = lax.broadcasted_iota(jnp.int32, (L, LANES), 1)
    vcol = lax.broadcasted_iota(jnp.int32, (L, MLSTM_PAD_V), 1)
    head_g = hg_ref[0]

    def step(c, carry):
        C, m = carry
        rows = pl.ds(pl.multiple_of(c * L, L), L)
        qc = q_ref[0, 0, rows, :]
        ktc = kt_ref[0, 0, :, rows]
        vc = jnp.where(vcol == MLSTM_V_DIM, 1.0, v_ref[0, 0, rows, :]).astype(BF16)
        b_col = jnp.sum(jnp.where(gate_lane == MLSTM_HEADS + head, gc_ref[0, rows, :], 0.0),
                        axis=1, keepdims=True)
        li_row = gr_ref[0, pl.ds(head, 1), rows]
        b_row = gr_ref[0, pl.ds(MLSTM_HEADS + head, 1), rows]

        d = jnp.where(tri, b_col - b_row + li_row, -jnp.inf)
        inter = b_col + m
        m_t = jnp.maximum(inter, jnp.max(d, axis=1, keepdims=True))
        w_intra = jnp.exp(d - m_t)
        w_inter = jnp.exp(inter - m_t)
        sqk = (_dot(qc, ktc) * w_intra).astype(BF16)
        num = w_inter * _dot(qc, C.astype(BF16)) + _dot(sqk, vc)
        den = jnp.sum(jnp.where(vcol == MLSTM_V_DIM, num, 0.0), axis=1, keepdims=True)
        den = jnp.maximum(jnp.abs(den), jnp.exp(-m_t))
        hv = jnp.where(vcol < MLSTM_V_DIM, num / den, 0.0)
        o_ref[0, 0, rows, :] = _rms(hv, head_g, MLSTM_V_DIM).astype(o_ref.dtype)

        b_last = b_row[:, L - 1:L]
        dec = b_last - b_row + li_row
        m_new = jnp.maximum(b_last + m, jnp.max(dec, axis=1, keepdims=True))
        decay = jnp.exp(b_last + m - m_new)
        ws = jnp.exp(dec - m_new)
        kw = (ktc.astype(F32) * ws).astype(BF16)
        return decay * C + _dot(kw, vc), m_new

    init = (jnp.zeros((MLSTM_PAD_QK, MLSTM_PAD_V), F32), jnp.zeros((1, 1), F32))
    lax.fori_loop(0, S // L, step, init)


def _mlstm(q, kt, v, gc, gr, head_g):
    B, H, S, _ = q.shape
    chunk = min(MLSTM_CHUNK, S)
    hg = jnp.pad(head_g, ((0, 0), (0, MLSTM_PAD_V - MLSTM_V_DIM))).reshape(H, 1, MLSTM_PAD_V)
    head = lambda b, h: (b, h, 0, 0)
    return pl.pallas_call(
        functools.partial(_mlstm_kernel, chunk=chunk),
        grid=(B, H),
        in_specs=[pl.BlockSpec((1, 1, S, MLSTM_PAD_QK), head),
                  pl.BlockSpec((1, 1, MLSTM_PAD_QK, S), head),
                  pl.BlockSpec((1, 1, S, MLSTM_PAD_V), head),
                  pl.BlockSpec((1, S, LANES), lambda b, h: (b, 0, 0)),
                  pl.BlockSpec((1, 2 * H, S), lambda b, h: (b, 0, 0)),
                  pl.BlockSpec((1, 1, MLSTM_PAD_V), lambda b, h: (h, 0, 0))],
        out_specs=pl.BlockSpec((1, 1, S, MLSTM_PAD_V), head),
        out_shape=jax.ShapeDtypeStruct((B, H, S, MLSTM_PAD_V), BF16),
        compiler_params=_params("parallel", "arbitrary"),
        name="mlstm_scan",
    )(q, kt, v, gc, gr, hg)


def _mla_layer(x, mem, cs, pre_g, w_in, q_a_g, w_uq, kv_a_g, w_ukv, mem_g, w_mem_kv, w_out, post_g):
    q, k, v, qmem, gate = _mla_in(x, cs, pre_g, w_in, q_a_g, w_uq, kv_a_g, w_ukv)
    mix = _attention(q, k, v)
    kexp, vexp = _mem_kv(mem, mem_g, w_mem_kv)
    specs = lambda tm: [pl.BlockSpec((1, tm, MIX_WIDTH), lambda b, i: (b, i, 0))]
    return _layer_out(_mla_out_kernel, "mla_out", x, [mix], specs, qmem, gate, kexp, vexp, w_out, post_g)


def _mlstm_layer(x, mem, pre_g, w_in, gate_bias, conv_w, conv_b, w_q, w_k, w_v, head_g, skip,
                 mem_g, w_mem_kv, w_out, post_g):
    u, og, qmem, gate, if_pre = _mlstm_in(x, pre_g, w_in)
    uc, q, kt, v, gc, gr = _mlstm_prep(u, if_pre, gate_bias, conv_w, conv_b, w_q, w_k, w_v)
    hn = _mlstm(q, kt, v, gc, gr, head_g)
    kexp, vexp = _mem_kv(mem, mem_g, w_mem_kv)
    row = lambda b, i: (b, i, 0)
    specs = lambda tm: [pl.BlockSpec((1, MLSTM_HEADS, tm, MLSTM_PAD_V), lambda b, i: (b, 0, i, 0)),
                        pl.BlockSpec((1, tm, MIX_WIDTH), row), pl.BlockSpec((1, tm, MIX_WIDTH), row),
                        _const_spec((1, MIX_WIDTH))]
    return _layer_out(_mlstm_out_kernel, "mlstm_out", x, [hn, og, uc, skip.reshape(1, -1)], specs,
                      qmem, gate, kexp, vexp, w_out, post_g)


def kernel(x, mem, positions, a_pre_g, a_w_in, a_q_a_g, a_w_uq, a_kv_a_g, a_w_ukv, a_mem_g, a_w_mem_kv, a_w_out, a_post_g, b_pre_g, b_w_in, b_gate_bias, b_conv_w, b_conv_b, b_w_q, b_w_k, b_w_v, b_head_g, b_skip, b_mem_g, b_w_mem_kv, b_w_out, b_post_g):
    depth = a_pre_g.shape[0] + b_pre_g.shape[0]
    cs = _rope_tables(positions)
    for i in range(depth):
        j = i // 2
        if i % 2 == 0:
            x = _mla_layer(x, mem, cs, a_pre_g[j], a_w_in[j], a_q_a_g[j], a_w_uq[j], a_kv_a_g[j],
                           a_w_ukv[j], a_mem_g[j], a_w_mem_kv[j], a_w_out[j], a_post_g[j])
        else:
            x = _mlstm_layer(x, mem, b_pre_g[j], b_w_in[j], b_gate_bias[j], b_conv_w[j], b_conv_b[j],
                             b_w_q[j], b_w_k[j], b_w_v[j], b_head_g[j], b_skip[j], b_mem_g[j],
                             b_w_mem_kv[j], b_w_out[j], b_post_g[j])
    return x
```

```python
import functools
import math

import jax
import jax.numpy as jnp
import numpy as np
from jax import lax
from jax.experimental import pallas as pl
from jax.experimental.pallas import tpu as pltpu

EPS = 1e-6
ROPE_THETA = 10000.0
MEM_HEADS = 4
MEM_HEAD_DIM = 64
MEM_WIDTH = MEM_HEADS * MEM_HEAD_DIM
QK_NOPE_DIM = 128
QK_ROPE_DIM = 64
V_HEAD_DIM = 128
MLA_HEADS = 6
V_EXT_DIM = V_HEAD_DIM + 16
Q_LORA_RANK = 384
KV_LORA_RANK = 256
MLSTM_HEADS = 4
MLSTM_V_DIM = 192
MLSTM_QK_DIM = 96
CONV_WIDTH = 4
MIX_WIDTH = 768

LANES = 128
MXU_DIM = 256
BF16_SUBLANES = 16
VMEM_LIMIT_BYTES = 56 * 1024 * 1024

ROW_TILE = 512
ATTN_TQ = 1024
ATTN_CW = 256
ATTN_TK = 512
ATTN_HEADS_PER_STEP = 2
MLSTM_CHUNK = 256
MLSTM_SEGMENT = 1024
ROPE_ROWS = 1024

LOG2E = 1.4426950408889634
MLSTM_PAD_QK = LANES
MLSTM_PAD_V = MXU_DIM
MLSTM_HEAD_START = tuple((h * MLSTM_V_DIM // LANES) * LANES for h in range(MLSTM_HEADS))

F32 = jnp.float32
BF16 = jnp.bfloat16


def _dot(a, b):
    return jnp.dot(a, b, preferred_element_type=F32)


def _dot_nt(a, b):
    return lax.dot_general(a, b, (((1,), (1,)), ((), ())), preferred_element_type=F32)


def _rms(x, g, width=None):
    width = x.shape[-1] if width is None else width
    ms = jnp.sum(x * x, axis=-1, keepdims=True) * (1.0 / width)
    return x * lax.rsqrt(ms + EPS) * g


def _params(*semantics):
    return pltpu.CompilerParams(dimension_semantics=semantics, vmem_limit_bytes=VMEM_LIMIT_BYTES)


def _const_spec(shape):
    zeros = (0,) * len(shape)
    return pl.BlockSpec(shape, lambda *_: zeros)


def _rope_kernel(pos_ref, invf_ref, cos_ref, sin_ref):
    ang = pos_ref[...].astype(F32) * invf_ref[...]
    cos_ref[...] = jnp.cos(ang)
    sin_ref[...] = jnp.sin(ang)


def _rope_tables(positions):
    B, S = positions.shape
    T = B * S
    half = QK_ROPE_DIM // 2
    per_row = LANES // half
    rows = T // per_row
    inv_freq = ROPE_THETA ** (-jnp.arange(0, QK_ROPE_DIM, 2, dtype=F32) / QK_ROPE_DIM)
    pos4 = jnp.repeat(positions.reshape(rows, per_row), half, axis=1)
    invf = jnp.tile(inv_freq, per_row).reshape(1, LANES)
    rb = min(ROPE_ROWS, rows)
    cos4, sin4 = pl.pallas_call(
        _rope_kernel,
        grid=(rows // rb,),
        in_specs=[pl.BlockSpec((rb, LANES), lambda i: (i, 0)), _const_spec((1, LANES))],
        out_specs=[pl.BlockSpec((rb, LANES), lambda i: (i, 0))] * 2,
        out_shape=[jax.ShapeDtypeStruct((rows, LANES), F32)] * 2,
        compiler_params=_params("parallel"),
        name="rope_tables",
    )(pos4, invf)
    cos = cos4.reshape(B, S, half)
    sin = sin4.reshape(B, S, half)
    return jnp.concatenate([cos, cos, -sin, sin], axis=-1)


def _mem_kv_kernel(mem_ref, g_ref, w_ref, k_ref, v_ref):
    n_mem = mem_ref.shape[1]
    hn = _rms(mem_ref[0], g_ref[...]).astype(BF16)
    kv = _dot(hn, w_ref[...])
    k = kv[:, :MEM_WIDTH] * (MEM_HEAD_DIM ** -0.5 * LOG2E)
    v = kv[:, MEM_WIDTH:]
    col_head = lax.broadcasted_iota(jnp.int32, (n_mem, MEM_WIDTH), 1) // MEM_HEAD_DIM
    for h in range(MEM_HEADS):
        rows = pl.ds(h * n_mem, n_mem)
        k_ref[0, rows, :] = jnp.where(col_head == h, k, 0.0).astype(BF16)
        v_ref[0, rows, :] = jnp.where(col_head == h, v, 0.0).astype(BF16)


def _mem_kv(mem, mem_g, w_mem_kv):
    B, n_mem, D = mem.shape
    out = jax.ShapeDtypeStruct((B, MEM_HEADS * n_mem, MEM_WIDTH), BF16)
    spec = pl.BlockSpec((1, MEM_HEADS * n_mem, MEM_WIDTH), lambda b: (b, 0, 0))
    return pl.pallas_call(
        _mem_kv_kernel,
        grid=(B,),
        in_specs=[pl.BlockSpec((1, n_mem, D), lambda b: (b, 0, 0)), _const_spec((1, D)),
                  _const_spec((D, 2 * MEM_WIDTH))],
        out_specs=[spec, spec],
        out_shape=[out, out],
        compiler_params=_params("parallel"),
        name="mem_kv",
    )(mem, mem_g.reshape(1, D), w_mem_kv.astype(BF16))


A_CQ = (0, Q_LORA_RANK)
A_CKV = (A_CQ[1], A_CQ[1] + KV_LORA_RANK)
A_QMEM = (A_CKV[1], A_CKV[1] + MEM_WIDTH)
A_GATE = (A_QMEM[1], A_QMEM[1] + 1024)
A_KROPE = (A_GATE[1], A_GATE[1] + 2 * QK_ROPE_DIM)
A_COLS = A_KROPE[1]
Q_ROPE_OFF = MLA_HEADS * QK_NOPE_DIM


def _mla_in_kernel(x_ref, cs_ref, pre_g_ref, w_in_ref, qa_g_ref, w_uq_ref, kva_g_ref, w_ukv_ref,
                   q_ref, k_ref, v_ref, qmem_ref, gate_ref):
    q_scale = (QK_NOPE_DIM + QK_ROPE_DIM) ** -0.5 * LOG2E
    h = _rms(x_ref[0], pre_g_ref[...]).astype(BF16)
    p = _dot(h, w_in_ref[...])
    cs = cs_ref[0]
    qmem_ref[0] = p[:, A_QMEM[0]:A_QMEM[1]].astype(BF16)
    gate = p[:, A_GATE[0]:A_GATE[1]]
    gate_ref[0] = (gate * jax.nn.sigmoid(gate)).astype(BF16)

    c_q = _rms(p[:, A_CQ[0]:A_CQ[1]], qa_g_ref[...]).astype(BF16)
    q = _dot(c_q, w_uq_ref[...])
    c_kv = _rms(p[:, A_CKV[0]:A_CKV[1]], kva_g_ref[...]).astype(BF16)
    kv = _dot(c_kv, w_ukv_ref[...])

    kr = p[:, A_KROPE[0]:A_KROPE[1]] * cs
    k_rot = (kr + pltpu.roll(kr, QK_ROPE_DIM, 1)).astype(BF16)
    tm = cs.shape[0]
    pad_row = lax.broadcasted_iota(jnp.int32, (V_EXT_DIM - V_HEAD_DIM, tm), 0)
    ones_row = jnp.where(pad_row == 0, 1.0, 0.0).astype(BF16)
    for hd in range(MLA_HEADS):
        nope = slice(hd * QK_NOPE_DIM, (hd + 1) * QK_NOPE_DIM)
        rope = slice(Q_ROPE_OFF + hd * LANES, Q_ROPE_OFF + (hd + 1) * LANES)
        q_ref[0, hd, 0:LANES, :] = (q[:, nope] * q_scale).T.astype(BF16)
        q_ref[0, hd, LANES:2 * LANES, :] = (q[:, rope] * cs * q_scale).T.astype(BF16)
        k_ref[0, hd, :, 0:LANES] = kv[:, 2 * hd * LANES:(2 * hd + 1) * LANES].astype(BF16)
        k_ref[0, hd, :, LANES:2 * LANES] = k_rot
        v_ref[0, hd, 0:V_HEAD_DIM, :] = kv[:, (2 * hd + 1) * LANES:(2 * hd + 2) * LANES].T.astype(BF16)
        v_ref[0, hd, V_HEAD_DIM:V_EXT_DIM, :] = ones_row


def _swap_halves(idx):
    half = len(idx) // 2
    return np.concatenate([idx[half:], idx[:half]])


def _mla_in(x, cs, pre_g, w_in, q_a_g, w_uq, kv_a_g, w_ukv):
    B, S, D = x.shape
    tm = min(ROW_TILE, S)
    o_cq, o_ckv, o_kr, o_qm, o_gate = np.cumsum([0, Q_LORA_RANK, KV_LORA_RANK, QK_ROPE_DIM, MEM_WIDTH])
    kr_idx = np.arange(o_kr, o_kr + QK_ROPE_DIM)
    in_idx = np.concatenate([np.arange(o_cq, o_kr), np.arange(o_qm, o_gate + 1024), kr_idx,
                             _swap_halves(kr_idx)])
    w_in_p = w_in[:, in_idx].astype(BF16)
    head_w = QK_NOPE_DIM + QK_ROPE_DIM
    nope_idx = np.concatenate([np.arange(h * head_w, h * head_w + QK_NOPE_DIM) for h in range(MLA_HEADS)])
    rope_idx = []
    for h in range(MLA_HEADS):
        r = np.arange(h * head_w + QK_NOPE_DIM, (h + 1) * head_w)
        rope_idx += [r, _swap_halves(r)]
    uq_idx = np.concatenate([nope_idx] + rope_idx)
    w_uq_p = w_uq[:, uq_idx].astype(BF16)
    w_ukv_p = w_ukv.astype(BF16)

    row = lambda b, i: (b, i, 0)
    head_row = lambda b, i: (b, 0, i, 0)
    head_col = lambda b, i: (b, 0, 0, i)
    out_shape = [
        jax.ShapeDtypeStruct((B, MLA_HEADS, 2 * LANES, S), BF16),
        jax.ShapeDtypeStruct((B, MLA_HEADS, S, 2 * LANES), BF16),
        jax.ShapeDtypeStruct((B, MLA_HEADS, V_EXT_DIM, S), BF16),
        jax.ShapeDtypeStruct((B, S, MEM_WIDTH), BF16),
        jax.ShapeDtypeStruct((B, S, 1024), BF16),
    ]
    out_specs = [
        pl.BlockSpec((1, MLA_HEADS, 2 * LANES, tm), head_col),
        pl.BlockSpec((1, MLA_HEADS, tm, 2 * LANES), head_row),
        pl.BlockSpec((1, MLA_HEADS, V_EXT_DIM, tm), head_col),
        pl.BlockSpec((1, tm, MEM_WIDTH), row),
        pl.BlockSpec((1, tm, 1024), row),
    ]
    return pl.pallas_call(
        _mla_in_kernel,
        grid=(B, S // tm),
        in_specs=[pl.BlockSpec((1, tm, D), row), pl.BlockSpec((1, tm, LANES), row),
                  _const_spec((1, D)), _const_spec(w_in_p.shape),
                  _const_spec((1, Q_LORA_RANK)), _const_spec(w_uq_p.shape),
                  _const_spec((1, KV_LORA_RANK)), _const_spec(w_ukv_p.shape)],
        out_specs=out_specs,
        out_shape=out_shape,
        compiler_params=_params("parallel", "parallel"),
        name="mla_in",
    )(x, cs, pre_g.reshape(1, D), w_in_p, q_a_g.reshape(1, -1), w_uq_p, kv_a_g.reshape(1, -1), w_ukv_p)


def _attn_kernel(qt_ref, k_ref, vt_ref, o_ref, *, tq, tk, cw):
    heads = k_ref.shape[1]
    S = k_ref.shape[2]
    dve = vt_ref.shape[2]
    dv = V_HEAD_DIM
    nc = tq // cw
    causal = (lax.broadcasted_iota(jnp.int32, (cw, cw), 0) <= lax.broadcasted_iota(jnp.int32, (cw, cw), 1))

    def q_tile(j, _):
        q0 = pl.multiple_of(j * tq, tq)
        qts = [[qt_ref[0, h, :, pl.ds(q0 + c * cw, cw)] for c in range(nc)] for h in range(heads)]

        def softmax_pv(carry, s, vt_tile):
            m, acc = carry
            m_new = jnp.maximum(m, jnp.max(s, axis=0, keepdims=True))
            p = jnp.exp2(s - m_new).astype(BF16)
            return m_new, jnp.exp2(m - m_new) * acc + _dot(vt_tile, p)

        def body(ki, carries):
            rows = pl.ds(pl.multiple_of(ki * tk, tk), tk)
            k_tiles = [k_ref[0, h, rows, :] for h in range(heads)]
            scores = [[_dot(k_tiles[h], qts[h][c]) for c in range(nc)] for h in range(heads)]
            vt_tiles = [vt_ref[0, h, :, rows] for h in range(heads)]
            return tuple(tuple(softmax_pv(carries[h][c], scores[h][c], vt_tiles[h]) for c in range(nc))
                         for h in range(heads))

        init = tuple(tuple((jnp.full((1, cw), -jnp.inf, F32), jnp.zeros((dve, cw), F32)) for _ in range(nc))
                     for _ in range(heads))
        carries = lax.fori_loop(0, j * (tq // tk), body, init)

        chains = [(h, c) for h in range(heads) for c in range(nc)]
        scores = {}
        for h, c in chains:
            s = _dot(k_ref[0, h, pl.ds(q0, (c + 1) * cw), :], qts[h][c])
            s_diag = jnp.where(causal, s[c * cw:, :], -jnp.inf)
            scores[h, c] = jnp.concatenate([s[:c * cw, :], s_diag], axis=0) if c > 0 else s_diag
        for h, c in chains:
            _, acc = softmax_pv(carries[h][c], scores[h, c], vt_ref[0, h, :, pl.ds(q0, (c + 1) * cw)])
            out = acc[0:dv, :] / acc[dv:dv + 1, :]
            o_ref[0, pl.ds(q0 + c * cw, cw), h * dv:(h + 1) * dv] = out.T.astype(o_ref.dtype)
        return 0

    lax.fori_loop(0, S // tq, q_tile, 0)


def _attention(qt, k, vt):
    B, H, S, dqk = k.shape
    dve = vt.shape[2]
    tq = min(ATTN_TQ, S)
    tk = min(ATTN_TK, tq)
    cw = min(ATTN_CW, tq)
    hp = ATTN_HEADS_PER_STEP
    head = lambda b, h: (b, h, 0, 0)
    return pl.pallas_call(
        functools.partial(_attn_kernel, tq=tq, tk=tk, cw=cw),
        grid=(B, H // hp),
        in_specs=[pl.BlockSpec((1, hp, dqk, S), head), pl.BlockSpec((1, hp, S, dqk), head),
                  pl.BlockSpec((1, hp, dve, S), head)],
        out_specs=pl.BlockSpec((1, S, hp * V_HEAD_DIM), lambda b, h: (b, 0, h)),
        out_shape=jax.ShapeDtypeStruct((B, S, H * V_HEAD_DIM), BF16),
        compiler_params=_params("parallel", "parallel"),
        name="mla_attention",
    )(qt, k, vt)


def _memory_attention(qmem, kexp, vexp):
    n_mem = kexp.shape[0] // MEM_HEADS
    s = _dot_nt(qmem, kexp)
    probs = []
    for h in range(MEM_HEADS):
        sh = s[:, h * n_mem:(h + 1) * n_mem]
        e = jnp.exp2(sh - jnp.max(sh, axis=-1, keepdims=True))
        probs.append((e / jnp.sum(e, axis=-1, keepdims=True)).astype(BF16))
    return _dot(jnp.concatenate(probs, axis=-1), vexp)


def _tail(mix, x_ref, qmem_ref, gate_ref, kexp_ref, vexp_ref, w_out_ref, post_g_ref, o_ref):
    mo = _memory_attention(qmem_ref[0], kexp_ref[0], vexp_ref[0])
    gate = gate_ref[0].astype(F32)
    y_mix = (mix * gate[:, :MIX_WIDTH]).astype(BF16)
    y_mem = (mo * gate[:, MIX_WIDTH:]).astype(BF16)
    y = _dot(y_mix, w_out_ref[0:MIX_WIDTH, :]) + _dot(y_mem, w_out_ref[MIX_WIDTH:, :])
    o_ref[0] = x_ref[0] + _rms(y, post_g_ref[...])


def _mla_out_kernel(x_ref, mix_ref, qmem_ref, gate_ref, kexp_ref, vexp_ref, w_out_ref, post_g_ref, o_ref):
    _tail(mix_ref[0].astype(F32), x_ref, qmem_ref, gate_ref, kexp_ref, vexp_ref, w_out_ref, post_g_ref,
          o_ref)


def _mlstm_out_kernel(x_ref, hn_ref, og_ref, uc_ref, skip_ref, qmem_ref, gate_ref, kexp_ref, vexp_ref,
                      w_out_ref, post_g_ref, o_ref):
    hn = jnp.concatenate([hn_ref[0, h].astype(F32)[:, 0:MLSTM_V_DIM] for h in range(MLSTM_HEADS)], axis=-1)
    mix = og_ref[0].astype(F32) * hn + skip_ref[...] * uc_ref[0].astype(F32)
    _tail(mix, x_ref, qmem_ref, gate_ref, kexp_ref, vexp_ref, w_out_ref, post_g_ref, o_ref)


def _layer_out(kernel_fn, name, x, mixer_inputs, mixer_specs, qmem, gate, kexp, vexp, w_out, post_g):
    B, S, D = x.shape
    tm = min(ROW_TILE, S)
    row = lambda b, i: (b, i, 0)
    per_batch = lambda b, i: (b, 0, 0)
    in_specs = ([pl.BlockSpec((1, tm, D), row)] + mixer_specs(tm) +
                [pl.BlockSpec((1, tm, MEM_WIDTH), row), pl.BlockSpec((1, tm, gate.shape[-1]), row),
                 pl.BlockSpec((1,) + kexp.shape[1:], per_batch), pl.BlockSpec((1,) + vexp.shape[1:], per_batch),
                 _const_spec(w_out.shape), _const_spec((1, D))])
    return pl.pallas_call(
        kernel_fn,
        grid=(B, S // tm),
        in_specs=in_specs,
        out_specs=pl.BlockSpec((1, tm, D), row),
        out_shape=jax.ShapeDtypeStruct((B, S, D), F32),
        compiler_params=_params("parallel", "parallel"),
        name=name,
    )(x, *mixer_inputs, qmem, gate, kexp, vexp, w_out.astype(BF16), post_g.reshape(1, D))


B_U = (0, MIX_WIDTH)
B_O = (B_U[1], B_U[1] + MIX_WIDTH)
B_QMEM = (B_O[1], B_O[1] + MEM_WIDTH)
B_GATE = (B_QMEM[1], B_QMEM[1] + 1024)
B_IF = (B_GATE[1], B_GATE[1] + LANES)
B_COLS = B_IF[1]


def _mlstm_in_kernel(x_ref, pre_g_ref, w_in_ref, u_ref, og_ref, qmem_ref, gate_ref, if_ref):
    h = _rms(x_ref[0], pre_g_ref[...]).astype(BF16)
    p = _dot(h, w_in_ref[...])
    u_ref[0] = p[:, B_U[0]:B_U[1]].astype(BF16)
    og_ref[0] = jax.nn.sigmoid(p[:, B_O[0]:B_O[1]]).astype(BF16)
    qmem_ref[0] = p[:, B_QMEM[0]:B_QMEM[1]].astype(BF16)
    gate = p[:, B_GATE[0]:B_GATE[1]]
    gate_ref[0] = (gate * jax.nn.sigmoid(gate)).astype(BF16)
    if_ref[0] = p[:, B_IF[0]:B_IF[1]]


def _mlstm_in(x, pre_g, w_in):
    B, S, D = x.shape
    tm = min(ROW_TILE, S)
    n_if = 2 * MLSTM_HEADS
    o_u, o_if, o_o, o_qm, o_gate = np.cumsum([0, MIX_WIDTH, n_if, MIX_WIDTH, MEM_WIDTH])
    w_in_p = jnp.concatenate(
        [w_in[:, o_u:o_if], w_in[:, o_o:o_gate + 1024], w_in[:, o_if:o_o],
         jnp.zeros((D, LANES - n_if), w_in.dtype)], axis=1).astype(BF16)
    row = lambda b, i: (b, i, 0)
    widths = [(MIX_WIDTH, BF16), (MIX_WIDTH, BF16), (MEM_WIDTH, BF16), (1024, BF16), (LANES, F32)]
    return pl.pallas_call(
        _mlstm_in_kernel,
        grid=(B, S // tm),
        in_specs=[pl.BlockSpec((1, tm, D), row), _const_spec((1, D)), _const_spec(w_in_p.shape)],
        out_specs=[pl.BlockSpec((1, tm, w), row) for w, _ in widths],
        out_shape=[jax.ShapeDtypeStruct((B, S, w), dt) for w, dt in widths],
        compiler_params=_params("parallel", "parallel"),
        name="mlstm_in",
    )(x, pre_g.reshape(1, D), w_in_p)


def _split3_dot(a, b):
    b1 = b.astype(BF16)
    r1 = b - b1.astype(F32)
    b2 = r1.astype(BF16)
    b3 = (r1 - b2.astype(F32)).astype(BF16)
    return _dot(a, b1) + _dot(a, b2) + _dot(a, b3)


def _mlstm_prep_kernel(u_ref, halo_ref, if_ref, conv_w_ref, conv_b_ref, bias_ref, wqk_ref, wv_ref,
                       uc_ref, qt_ref, k_ref, vt_ref, gc_ref, gr_ref, *, chunk):
    ts = u_ref.shape[1]
    halo_rows = halo_ref.shape[1]
    u_bf = u_ref[0]
    u = u_bf.astype(F32)
    halo = halo_ref[0].astype(F32) * (pl.program_id(1) > 0).astype(F32)
    ext = jnp.concatenate([halo, u], axis=0)
    conv = conv_b_ref[...] + u * conv_w_ref[CONV_WIDTH - 1:CONV_WIDTH, :]
    for back in range(1, CONV_WIDTH):
        tap = CONV_WIDTH - 1 - back
        conv = conv + ext[halo_rows - back:halo_rows - back + ts, :] * conv_w_ref[tap:tap + 1, :]
    uc = conv * jax.nn.sigmoid(conv)
    uc_ref[0] = uc.astype(BF16)
    uc_bf = uc.astype(BF16)

    k_scale = MLSTM_QK_DIM ** -0.5
    vrow = lax.broadcasted_iota(jnp.int32, (MLSTM_PAD_V, ts), 0)
    for h in range(MLSTM_HEADS):
        cols = slice(MLSTM_HEAD_START[h], MLSTM_HEAD_START[h] + MXU_DIM)
        qk = _dot(uc_bf[:, cols], wqk_ref[h])
        qt_ref[0, h] = qk[:, :MLSTM_PAD_QK].T.astype(BF16)
        k_ref[0, h] = (qk[:, MLSTM_PAD_QK:] * k_scale).astype(BF16)
        vt = _dot(u_bf[:, cols], wv_ref[h]).T
        vt_ref[0, h] = jnp.where(vrow == MLSTM_V_DIM, 1.0, vt).astype(BF16)

    g = if_ref[0] + bias_ref[...]
    lane = lax.broadcasted_iota(jnp.int32, g.shape, 1)
    log_f = jnp.minimum(g, 0.0) - jnp.log1p(jnp.exp(-jnp.abs(g)))
    is_f = (lane >= MLSTM_HEADS) & (lane < 2 * MLSTM_HEADS)
    gates = jnp.where(lane < MLSTM_HEADS, g, jnp.where(is_f, log_f, 0.0))
    r = lax.broadcasted_iota(jnp.int32, (chunk, chunk), 0)
    c = lax.broadcasted_iota(jnp.int32, (chunk, chunk), 1)
    tril = (c <= r).astype(BF16)
    chunk_lane = lax.broadcasted_iota(jnp.int32, (chunk, LANES), 1)
    chunk_is_f = (chunk_lane >= MLSTM_HEADS) & (chunk_lane < 2 * MLSTM_HEADS)
    parts = []
    for j in range(ts // chunk):
        gj = gates[j * chunk:(j + 1) * chunk, :]
        parts.append(jnp.where(chunk_is_f, _split3_dot(tril, gj), gj))
    gc = jnp.concatenate(parts, axis=0) if len(parts) > 1 else parts[0]
    gr_ref[0] = gc.T[0:2 * MLSTM_HEADS, :]
    gc_ref[0] = gc - pltpu.roll(gc, LANES - MLSTM_HEADS, 1)


def _mlstm_prep(u, if_pre, gate_bias, conv_w, conv_b, w_q, w_k, w_v):
    B, S, _ = u.shape
    ts = min(ROW_TILE, S)
    chunk = min(MLSTM_CHUNK, S)
    H = MLSTM_HEADS
    wqk = jnp.zeros((H, MXU_DIM, 2 * MLSTM_PAD_QK), F32)
    wv = jnp.zeros((H, MXU_DIM, MLSTM_PAD_V), F32)
    for h in range(H):
        off = h * MLSTM_V_DIM - MLSTM_HEAD_START[h]
        wqk = wqk.at[h, off:off + MLSTM_V_DIM, 0:MLSTM_QK_DIM].set(w_q[h])
        wqk = wqk.at[h, off:off + MLSTM_V_DIM, MLSTM_PAD_QK:MLSTM_PAD_QK + MLSTM_QK_DIM].set(w_k[h])
        wv = wv.at[h, off:off + MLSTM_V_DIM, 0:MLSTM_V_DIM].set(w_v[h])
    bias = jnp.pad(gate_bias, (0, LANES - 2 * H)).reshape(1, LANES)
    halo_rows = BF16_SUBLANES
    blocks_per_tile = ts // halo_rows
    row = lambda b, i: (b, i, 0)
    head_row = lambda b, i: (b, 0, i, 0)
    out_shape = [
        jax.ShapeDtypeStruct((B, S, MIX_WIDTH), BF16),
        jax.ShapeDtypeStruct((B, H, MLSTM_PAD_QK, S), BF16),
        jax.ShapeDtypeStruct((B, H, S, MLSTM_PAD_QK), BF16),
        jax.ShapeDtypeStruct((B, H, MLSTM_PAD_V, S), BF16),
        jax.ShapeDtypeStruct((B, S, LANES), F32),
        jax.ShapeDtypeStruct((B, 2 * H, S), F32),
    ]
    head_col = lambda b, i: (b, 0, 0, i)
    out_specs = [
        pl.BlockSpec((1, ts, MIX_WIDTH), row),
        pl.BlockSpec((1, H, MLSTM_PAD_QK, ts), head_col),
        pl.BlockSpec((1, H, ts, MLSTM_PAD_QK), head_row),
        pl.BlockSpec((1, H, MLSTM_PAD_V, ts), head_col),
        pl.BlockSpec((1, ts, LANES), row),
        pl.BlockSpec((1, 2 * H, ts), lambda b, i: (b, 0, i)),
    ]
    return pl.pallas_call(
        functools.partial(_mlstm_prep_kernel, chunk=chunk),
        grid=(B, S // ts),
        in_specs=[pl.BlockSpec((1, ts, MIX_WIDTH), row),
                  pl.BlockSpec((1, halo_rows, MIX_WIDTH),
                               lambda b, i: (b, jnp.maximum(i * blocks_per_tile - 1, 0), 0)),
                  pl.BlockSpec((1, ts, LANES), row),
                  _const_spec((CONV_WIDTH, MIX_WIDTH)), _const_spec((1, MIX_WIDTH)), _const_spec((1, LANES)),
                  _const_spec(wqk.shape), _const_spec(wv.shape)],
        out_specs=out_specs,
        out_shape=out_shape,
        compiler_params=_params("parallel", "parallel"),
        name="mlstm_prep",
    )(u, u, if_pre, conv_w, conv_b.reshape(1, -1), bias, wqk.astype(BF16), wv.astype(BF16))


def _mlstm_kernel(qt_ref, k_ref, vt_ref, gc_ref, gr_ref, hg_ref, o_ref, c_sc, m_sc, *, chunk):
    H = k_ref.shape[1]
    seg = k_ref.shape[2]
    L = chunk

    @pl.when(pl.program_id(1) == 0)
    def _():
        c_sc[...] = jnp.zeros(c_sc.shape, F32)
        m_sc[...] = jnp.zeros(m_sc.shape, F32)

    causal = lax.broadcasted_iota(jnp.int32, (L, L), 0) <= lax.broadcasted_iota(jnp.int32, (L, L), 1)
    vrow = lax.broadcasted_iota(jnp.int32, (MLSTM_PAD_V, L), 0)
    gains = [jnp.broadcast_to(hg_ref[h], (MLSTM_PAD_V, L)) for h in range(H)]

    def step(c, carry):
        rows = pl.ds(pl.multiple_of(c * L, L), L)
        qts = [qt_ref[0, h, :, rows] for h in range(H)]
        ks = [k_ref[0, h, rows, :] for h in range(H)]
        vts = [vt_ref[0, h, :, rows] for h in range(H)]
        qk = [_dot(ks[h], qts[h]) for h in range(H)]
        cq = [_dot(carry[h][0].astype(BF16), qts[h]) for h in range(H)]
        out = []
        for h in range(H):
            C, m = carry[h]
            c_col = gc_ref[0, rows, h:h + 1]
            li_row = gr_ref[0, h:h + 1, rows]
            b_row = gr_ref[0, MLSTM_HEADS + h:MLSTM_HEADS + h + 1, rows]
            d = jnp.where(causal, c_col + b_row, -jnp.inf)
            inter = b_row + m
            m_t = jnp.maximum(inter, jnp.max(d, axis=0, keepdims=True))
            sqk = (qk[h] * jnp.exp(d - m_t)).astype(BF16)
            num = jnp.exp(inter - m_t) * cq[h] + _dot(vts[h], sqk)
            den = jnp.maximum(jnp.abs(num[MLSTM_V_DIM:MLSTM_V_DIM + 1, :]), jnp.exp(-m_t))
            hv = jnp.where(vrow < MLSTM_V_DIM, num / den, 0.0)
            ms = jnp.sum(hv * hv, axis=0, keepdims=True) * (1.0 / MLSTM_V_DIM)
            hn = hv * lax.rsqrt(ms + EPS) * gains[h]
            o_ref[0, h, rows, :] = hn.T.astype(o_ref.dtype)

            b_last = b_row[:, L - 1:L]
            dec = b_last - b_row + li_row
            m_new = jnp.maximum(b_last + m, jnp.max(dec, axis=1, keepdims=True))
            vw = (vts[h].astype(F32) * jnp.exp(dec - m_new)).astype(BF16)
            out.append((jnp.exp(b_last + m - m_new) * C + _dot(vw, ks[h]), m_new))
        return tuple(out)

    init = tuple((c_sc[h], m_sc[h, 0:1, 0:1]) for h in range(H))
    final = lax.fori_loop(0, seg // L, step, init)
    for h in range(H):
        c_sc[h] = final[h][0]
        m_sc[h] = jnp.broadcast_to(final[h][1], m_sc.shape[1:])


def _mlstm(qt, k, vt, gc, gr, head_g):
    B, H, S, _ = k.shape
    chunk = min(MLSTM_CHUNK, S)
    seg = min(MLSTM_SEGMENT, S)
    hg = jnp.pad(head_g, ((0, 0), (0, MLSTM_PAD_V - MLSTM_V_DIM))).reshape(H, MLSTM_PAD_V, 1)
    head_row = lambda b, i: (b, 0, i, 0)
    head_col = lambda b, i: (b, 0, 0, i)
    return pl.pallas_call(
        functools.partial(_mlstm_kernel, chunk=chunk),
        grid=(B, S // seg),
        in_specs=[pl.BlockSpec((1, H, MLSTM_PAD_QK, seg), head_col),
                  pl.BlockSpec((1, H, seg, MLSTM_PAD_QK), head_row),
                  pl.BlockSpec((1, H, MLSTM_PAD_V, seg), head_col),
                  pl.BlockSpec((1, seg, LANES), lambda b, i: (b, i, 0)),
                  pl.BlockSpec((1, 2 * H, seg), lambda b, i: (b, 0, i)),
                  _const_spec(hg.shape)],
        out_specs=pl.BlockSpec((1, H, seg, MLSTM_PAD_V), head_row),
        out_shape=jax.ShapeDtypeStruct((B, H, S, MLSTM_PAD_V), BF16),
        scratch_shapes=[pltpu.VMEM((H, MLSTM_PAD_V, MLSTM_PAD_QK), F32), pltpu.VMEM((H, 8, LANES), F32)],
        compiler_params=_params("parallel", "arbitrary"),
        name="mlstm_scan",
    )(qt, k, vt, gc, gr, hg)


def _mla_layer(x, mem, cs, pre_g, w_in, q_a_g, w_uq, kv_a_g, w_ukv, mem_g, w_mem_kv, w_out, post_g):
    q, k, v, qmem, gate = _mla_in(x, cs, pre_g, w_in, q_a_g, w_uq, kv_a_g, w_ukv)
    mix = _attention(q, k, v)
    kexp, vexp = _mem_kv(mem, mem_g, w_mem_kv)
    specs = lambda tm: [pl.BlockSpec((1, tm, MIX_WIDTH), lambda b, i: (b, i, 0))]
    return _layer_out(_mla_out_kernel, "mla_out", x, [mix], specs, qmem, gate, kexp, vexp, w_out, post_g)


def _mlstm_layer(x, mem, pre_g, w_in, gate_bias, conv_w, conv_b, w_q, w_k, w_v, head_g, skip,
                 mem_g, w_mem_kv, w_out, post_g):
    u, og, qmem, gate, if_pre = _mlstm_in(x, pre_g, w_in)
    uc, qt, k, vt, gc, gr = _mlstm_prep(u, if_pre, gate_bias, conv_w, conv_b, w_q, w_k, w_v)
    hn = _mlstm(qt, k, vt, gc, gr, head_g)
    kexp, vexp = _mem_kv(mem, mem_g, w_mem_kv)
    row = lambda b, i: (b, i, 0)
    specs = lambda tm: [pl.BlockSpec((1, MLSTM_HEADS, tm, MLSTM_PAD_V), lambda b, i: (b, 0, i, 0)),
                        pl.BlockSpec((1, tm, MIX_WIDTH), row), pl.BlockSpec((1, tm, MIX_WIDTH), row),
                        _const_spec((1, MIX_WIDTH))]
    return _layer_out(_mlstm_out_kernel, "mlstm_out", x, [hn, og, uc, skip.reshape(1, -1)], specs,
                      qmem, gate, kexp, vexp, w_out, post_g)


def kernel(x, mem, positions, a_pre_g, a_w_in, a_q_a_g, a_w_uq, a_kv_a_g, a_w_ukv, a_mem_g, a_w_mem_kv, a_w_out, a_post_g, b_pre_g, b_w_in, b_gate_bias, b_conv_w, b_conv_b, b_w_q, b_w_k, b_w_v, b_head_g, b_skip, b_mem_g, b_w_mem_kv, b_w_out, b_post_g):
    depth = a_pre_g.shape[0] + b_pre_g.shape[0]
    cs = _rope_tables(positions)
    for i in range(depth):
        j = i // 2
        if i % 2 == 0:
            x = _mla_layer(x, mem, cs, a_pre_g[j], a_w_in[j], a_q_a_g[j], a_w_uq[j], a_kv_a_g[j],
                           a_w_ukv[j], a_mem_g[j], a_w_mem_kv[j], a_w_out[j], a_post_g[j])
        else:
            x = _mlstm_layer(x, mem, b_pre_g[j], b_w_in[j], b_gate_bias[j], b_conv_w[j], b_conv_b[j],
                             b_w_q[j], b_w_k[j], b_w_v[j], b_head_g[j], b_skip[j], b_mem_g[j],
                             b_w_mem_kv[j], b_w_out[j], b_post_g[j])
    return x
```

```python
import functools
import math

import jax
import jax.numpy as jnp
import numpy as np
from jax import lax
from jax.experimental import pallas as pl
from jax.experimental.pallas import tpu as pltpu

EPS = 1e-6
ROPE_THETA = 10000.0
MEM_HEADS = 4
MEM_HEAD_DIM = 64
MEM_WIDTH = MEM_HEADS * MEM_HEAD_DIM
QK_NOPE_DIM = 128
QK_ROPE_DIM = 64
V_HEAD_DIM = 128
MLA_HEADS = 6
V_EXT_DIM = V_HEAD_DIM + 16
Q_LORA_RANK = 384
KV_LORA_RANK = 256
MLSTM_HEADS = 4
MLSTM_V_DIM = 192
MLSTM_QK_DIM = 96
CONV_WIDTH = 4
MIX_WIDTH = 768

LANES = 128
MXU_DIM = 256
BF16_SUBLANES = 16
VMEM_LIMIT_BYTES = 56 * 1024 * 1024

ROW_TILE = 512
TAIL_SUB_ROWS = 128
ATTN_TQ = 1024
ATTN_CW = 256
ATTN_TK = 512
ATTN_HEADS_PER_STEP = 2
MLSTM_CHUNK = 256
MLSTM_SEGMENT = 1024
ROPE_ROWS = 1024

LOG2E = 1.4426950408889634
MLSTM_PAD_QK = LANES
MLSTM_PAD_V = MXU_DIM
MLSTM_HEAD_START = tuple((h * MLSTM_V_DIM // LANES) * LANES for h in range(MLSTM_HEADS))

F32 = jnp.float32
BF16 = jnp.bfloat16


def _dot(a, b):
    return jnp.dot(a, b, preferred_element_type=F32)


def _dot_nt(a, b):
    return lax.dot_general(a, b, (((1,), (1,)), ((), ())), preferred_element_type=F32)


def _rms(x, g, width=None):
    width = x.shape[-1] if width is None else width
    ms = jnp.sum(x * x, axis=-1, keepdims=True) * (1.0 / width)
    return x * lax.rsqrt(ms + EPS) * g


def _split3(x):
    b1 = x.astype(BF16)
    r1 = x - b1.astype(F32)
    b2 = r1.astype(BF16)
    b3 = (r1 - b2.astype(F32)).astype(BF16)
    return b1, b2, b3


def _params(*semantics):
    return pltpu.CompilerParams(dimension_semantics=semantics, vmem_limit_bytes=VMEM_LIMIT_BYTES)


def _const_spec(shape):
    zeros = (0,) * len(shape)
    return pl.BlockSpec(shape, lambda *_: zeros)


def _rope_kernel(pos_ref, invf_ref, sel_ref, cs_ref):
    ang = pos_ref[...].astype(F32) * invf_ref[...]
    trig = jnp.concatenate([jnp.cos(ang), jnp.sin(ang)], axis=1)
    cs_ref[...] = sum(_dot(term, sel_ref[...]) for term in _split3(trig))


def _rope_tables(positions):
    B, S = positions.shape
    T = B * S
    half = QK_ROPE_DIM // 2
    per_row = LANES // half
    rows = T // per_row
    inv_freq = ROPE_THETA ** (-jnp.arange(0, QK_ROPE_DIM, 2, dtype=F32) / QK_ROPE_DIM)
    pos4 = jnp.repeat(positions.reshape(rows, per_row), half, axis=1)
    invf = jnp.tile(inv_freq, per_row).reshape(1, LANES)
    sel = np.zeros((2 * LANES, per_row * LANES), np.float32)
    for t in range(per_row):
        for f in range(half):
            sel[t * half + f, t * LANES + f] = 1.0
            sel[t * half + f, t * LANES + half + f] = 1.0
            sel[LANES + t * half + f, t * LANES + 2 * half + f] = -1.0
            sel[LANES + t * half + f, t * LANES + 3 * half + f] = 1.0
    rb = min(ROPE_ROWS, rows)
    cs = pl.pallas_call(
        _rope_kernel,
        grid=(rows // rb,),
        in_specs=[pl.BlockSpec((rb, LANES), lambda i: (i, 0)), _const_spec((1, LANES)),
                  _const_spec(sel.shape)],
        out_specs=pl.BlockSpec((rb, per_row * LANES), lambda i: (i, 0)),
        out_shape=jax.ShapeDtypeStruct((rows, per_row * LANES), F32),
        compiler_params=_params("parallel"),
        name="rope_tables",
    )(pos4, invf, jnp.asarray(sel, BF16))
    return cs.reshape(B, S, LANES)


def _mem_kv_kernel(mem_ref, g_ref, w_ref, k_ref, v_ref):
    n_mem = mem_ref.shape[1]
    hn = _rms(mem_ref[0], g_ref[...]).astype(BF16)
    kv = _dot(hn, w_ref[...])
    k = kv[:, :MEM_WIDTH] * (MEM_HEAD_DIM ** -0.5 * LOG2E)
    v = kv[:, MEM_WIDTH:]
    col_head = lax.broadcasted_iota(jnp.int32, (n_mem, MEM_WIDTH), 1) // MEM_HEAD_DIM
    for h in range(MEM_HEADS):
        rows = pl.ds(h * n_mem, n_mem)
        k_ref[0, rows, :] = jnp.where(col_head == h, k, 0.0).astype(BF16)
        v_ref[0, rows, :] = jnp.where(col_head == h, v, 0.0).astype(BF16)


def _mem_kv(mem, mem_g, w_mem_kv):
    B, n_mem, D = mem.shape
    out = jax.ShapeDtypeStruct((B, MEM_HEADS * n_mem, MEM_WIDTH), BF16)
    spec = pl.BlockSpec((1, MEM_HEADS * n_mem, MEM_WIDTH), lambda b: (b, 0, 0))
    return pl.pallas_call(
        _mem_kv_kernel,
        grid=(B,),
        in_specs=[pl.BlockSpec((1, n_mem, D), lambda b: (b, 0, 0)), _const_spec((1, D)),
                  _const_spec((D, 2 * MEM_WIDTH))],
        out_specs=[spec, spec],
        out_shape=[out, out],
        compiler_params=_params("parallel"),
        name="mem_kv",
    )(mem, mem_g.reshape(1, D), w_mem_kv.astype(BF16))


A_CQ = (0, Q_LORA_RANK)
A_CKV = (A_CQ[1], A_CQ[1] + KV_LORA_RANK)
A_QMEM = (A_CKV[1], A_CKV[1] + MEM_WIDTH)
A_GATE = (A_QMEM[1], A_QMEM[1] + 1024)
A_KROPE = (A_GATE[1], A_GATE[1] + 2 * QK_ROPE_DIM)
A_COLS = A_KROPE[1]
Q_ROPE_OFF = MLA_HEADS * QK_NOPE_DIM


def _mla_in_kernel(x_ref, cs_ref, pre_g_ref, w_in_ref, qa_g_ref, w_uq_ref, kva_g_ref, w_ukv_ref,
                   q_ref, k_ref, v_ref, qmem_ref, gate_ref):
    q_scale = (QK_NOPE_DIM + QK_ROPE_DIM) ** -0.5 * LOG2E
    h = _rms(x_ref[0], pre_g_ref[...]).astype(BF16)
    p = _dot(h, w_in_ref[...])
    cs = cs_ref[0]
    qmem_ref[0] = p[:, A_QMEM[0]:A_QMEM[1]].astype(BF16)
    gate = p[:, A_GATE[0]:A_GATE[1]]
    gate_ref[0] = (gate * jax.nn.sigmoid(gate)).astype(BF16)

    c_q = _rms(p[:, A_CQ[0]:A_CQ[1]], qa_g_ref[...]).astype(BF16)
    q = _dot(c_q, w_uq_ref[...])
    c_kv = _rms(p[:, A_CKV[0]:A_CKV[1]], kva_g_ref[...]).astype(BF16)
    kv = _dot(c_kv, w_ukv_ref[...])

    kr = p[:, A_KROPE[0]:A_KROPE[1]] * cs
    k_rot = (kr + pltpu.roll(kr, QK_ROPE_DIM, 1)).astype(BF16)
    tm = cs.shape[0]
    pad_row = lax.broadcasted_iota(jnp.int32, (V_EXT_DIM - V_HEAD_DIM, tm), 0)
    ones_row = jnp.where(pad_row == 0, 1.0, 0.0).astype(BF16)
    for hd in range(MLA_HEADS):
        nope = slice(hd * QK_NOPE_DIM, (hd + 1) * QK_NOPE_DIM)
        rope = slice(Q_ROPE_OFF + hd * LANES, Q_ROPE_OFF + (hd + 1) * LANES)
        q_ref[0, hd, 0:LANES, :] = (q[:, nope] * q_scale).T.astype(BF16)
        q_ref[0, hd, LANES:2 * LANES, :] = (q[:, rope] * cs * q_scale).T.astype(BF16)
        k_ref[0, hd, :, 0:LANES] = kv[:, 2 * hd * LANES:(2 * hd + 1) * LANES].astype(BF16)
        k_ref[0, hd, :, LANES:2 * LANES] = k_rot
        v_ref[0, hd, 0:V_HEAD_DIM, :] = kv[:, (2 * hd + 1) * LANES:(2 * hd + 2) * LANES].T.astype(BF16)
        v_ref[0, hd, V_HEAD_DIM:V_EXT_DIM, :] = ones_row


def _rope_cols(w, start):
    half = QK_ROPE_DIM // 2
    return [w[:, start:start + QK_ROPE_DIM], w[:, start + half:start + QK_ROPE_DIM], w[:, start:start + half]]


def _mla_in(x, cs, pre_g, w_in, q_a_g, w_uq, kv_a_g, w_ukv):
    B, S, D = x.shape
    tm = min(ROW_TILE, S)
    o_kr = Q_LORA_RANK + KV_LORA_RANK
    o_qm = o_kr + QK_ROPE_DIM
    w_in_p = jnp.concatenate([w_in[:, :o_kr], w_in[:, o_qm:]] + _rope_cols(w_in, o_kr), axis=1).astype(BF16)
    head_w = QK_NOPE_DIM + QK_ROPE_DIM
    uq_cols = [w_uq[:, h * head_w:h * head_w + QK_NOPE_DIM] for h in range(MLA_HEADS)]
    for h in range(MLA_HEADS):
        uq_cols += _rope_cols(w_uq, h * head_w + QK_NOPE_DIM)
    w_uq_p = jnp.concatenate(uq_cols, axis=1).astype(BF16)
    w_ukv_p = w_ukv.astype(BF16)

    row = lambda b, i: (b, i, 0)
    head_row = lambda b, i: (b, 0, i, 0)
    head_col = lambda b, i: (b, 0, 0, i)
    out_shape = [
        jax.ShapeDtypeStruct((B, MLA_HEADS, 2 * LANES, S), BF16),
        jax.ShapeDtypeStruct((B, MLA_HEADS, S, 2 * LANES), BF16),
        jax.ShapeDtypeStruct((B, MLA_HEADS, V_EXT_DIM, S), BF16),
        jax.ShapeDtypeStruct((B, S, MEM_WIDTH), BF16),
        jax.ShapeDtypeStruct((B, S, 1024), BF16),
    ]
    out_specs = [
        pl.BlockSpec((1, MLA_HEADS, 2 * LANES, tm), head_col),
        pl.BlockSpec((1, MLA_HEADS, tm, 2 * LANES), head_row),
        pl.BlockSpec((1, MLA_HEADS, V_EXT_DIM, tm), head_col),
        pl.BlockSpec((1, tm, MEM_WIDTH), row),
        pl.BlockSpec((1, tm, 1024), row),
    ]
    return pl.pallas_call(
        _mla_in_kernel,
        grid=(B, S // tm),
        in_specs=[pl.BlockSpec((1, tm, D), row), pl.BlockSpec((1, tm, LANES), row),
                  _const_spec((1, D)), _const_spec(w_in_p.shape),
                  _const_spec((1, Q_LORA_RANK)), _const_spec(w_uq_p.shape),
                  _const_spec((1, KV_LORA_RANK)), _const_spec(w_ukv_p.shape)],
        out_specs=out_specs,
        out_shape=out_shape,
        compiler_params=_params("parallel", "parallel"),
        name="mla_in",
    )(x, cs, pre_g.reshape(1, D), w_in_p, q_a_g.reshape(1, -1), w_uq_p, kv_a_g.reshape(1, -1), w_ukv_p)


def _attn_kernel(qt_ref, k_ref, vt_ref, o_ref, *, tq, tk, cw):
    heads = k_ref.shape[1]
    S = k_ref.shape[2]
    dve = vt_ref.shape[2]
    dv = V_HEAD_DIM
    nc = tq // cw
    causal = (lax.broadcasted_iota(jnp.int32, (cw, cw), 0) <= lax.broadcasted_iota(jnp.int32, (cw, cw), 1))

    def q_tile(j, _):
        q0 = pl.multiple_of(j * tq, tq)
        qts = [[qt_ref[0, h, :, pl.ds(q0 + c * cw, cw)] for c in range(nc)] for h in range(heads)]

        def softmax_pv(carry, s, vt_tile):
            m, acc = carry
            m_new = jnp.maximum(m, jnp.max(s, axis=0, keepdims=True))
            p = jnp.exp2(s - m_new).astype(BF16)
            return m_new, jnp.exp2(m - m_new) * acc + _dot(vt_tile, p)

        def body(ki, carries):
            rows = pl.ds(pl.multiple_of(ki * tk, tk), tk)
            k_tiles = [k_ref[0, h, rows, :] for h in range(heads)]
            scores = [[_dot(k_tiles[h], qts[h][c]) for c in range(nc)] for h in range(heads)]
            vt_tiles = [vt_ref[0, h, :, rows] for h in range(heads)]
            return tuple(tuple(softmax_pv(carries[h][c], scores[h][c], vt_tiles[h]) for c in range(nc))
                         for h in range(heads))

        init = tuple(tuple((jnp.full((1, cw), -jnp.inf, F32), jnp.zeros((dve, cw), F32)) for _ in range(nc))
                     for _ in range(heads))
        carries = lax.fori_loop(0, j * (tq // tk), body, init)

        chains = [(h, c) for h in range(heads) for c in range(nc)]
        scores = {}
        for h, c in chains:
            s = _dot(k_ref[0, h, pl.ds(q0, (c + 1) * cw), :], qts[h][c])
            s_diag = jnp.where(causal, s[c * cw:, :], -jnp.inf)
            scores[h, c] = jnp.concatenate([s[:c * cw, :], s_diag], axis=0) if c > 0 else s_diag
        for h, c in chains:
            _, acc = softmax_pv(carries[h][c], scores[h, c], vt_ref[0, h, :, pl.ds(q0, (c + 1) * cw)])
            out = acc[0:dv, :] / acc[dv:dv + 1, :]
            o_ref[0, pl.ds(q0 + c * cw, cw), h * dv:(h + 1) * dv] = out.T.astype(o_ref.dtype)
        return 0

    lax.fori_loop(0, S // tq, q_tile, 0)


def _attention(qt, k, vt):
    B, H, S, dqk = k.shape
    dve = vt.shape[2]
    tq = min(ATTN_TQ, S)
    tk = min(ATTN_TK, tq)
    cw = min(ATTN_CW, tq)
    hp = ATTN_HEADS_PER_STEP
    head = lambda b, h: (b, h, 0, 0)
    return pl.pallas_call(
        functools.partial(_attn_kernel, tq=tq, tk=tk, cw=cw),
        grid=(B, H // hp),
        in_specs=[pl.BlockSpec((1, hp, dqk, S), head), pl.BlockSpec((1, hp, S, dqk), head),
                  pl.BlockSpec((1, hp, dve, S), head)],
        out_specs=pl.BlockSpec((1, S, hp * V_HEAD_DIM), lambda b, h: (b, 0, h)),
        out_shape=jax.ShapeDtypeStruct((B, S, H * V_HEAD_DIM), BF16),
        compiler_params=_params("parallel", "parallel"),
        name="mla_attention",
    )(qt, k, vt)


def _memory_probs(s, n_mem):
    probs = []
    for h in range(MEM_HEADS):
        sh = s[:, h * n_mem:(h + 1) * n_mem]
        e = jnp.exp2(sh - jnp.max(sh, axis=-1, keepdims=True))
        probs.append((e / jnp.sum(e, axis=-1, keepdims=True)).astype(BF16))
    return jnp.concatenate(probs, axis=-1)


def _tail(mix_rows, x_ref, qmem_ref, gate_ref, kexp_ref, vexp_ref, w_out_ref, post_g_ref, o_ref):
    tm = x_ref.shape[1]
    sub = min(TAIL_SUB_ROWS, tm)
    blocks = [slice(i * sub, (i + 1) * sub) for i in range(tm // sub)]
    n_mem = kexp_ref.shape[1] // MEM_HEADS
    kexp = kexp_ref[0]
    scores = [_dot_nt(qmem_ref[0, r, :], kexp) for r in blocks]
    for r, s in zip(blocks, scores):
        mo = _dot(_memory_probs(s, n_mem), vexp_ref[0])
        gate = gate_ref[0, r, :].astype(F32)
        y_mix = (mix_rows(r) * gate[:, :MIX_WIDTH]).astype(BF16)
        y_mem = (mo * gate[:, MIX_WIDTH:]).astype(BF16)
        y = _dot(y_mix, w_out_ref[0:MIX_WIDTH, :]) + _dot(y_mem, w_out_ref[MIX_WIDTH:, :])
        o_ref[0, r, :] = x_ref[0, r, :] + _rms(y, post_g_ref[...])


def _mla_out_kernel(x_ref, mix_ref, qmem_ref, gate_ref, kexp_ref, vexp_ref, w_out_ref, post_g_ref, o_ref):
    mix_rows = lambda r: mix_ref[0, r, :].astype(F32)
    _tail(mix_rows, x_ref, qmem_ref, gate_ref, kexp_ref, vexp_ref, w_out_ref, post_g_ref, o_ref)


def _mlstm_out_kernel(x_ref, hn_ref, og_ref, uc_ref, skip_ref, qmem_ref, gate_ref, kexp_ref, vexp_ref,
                      w_out_ref, post_g_ref, o_ref):
    def mix_rows(r):
        hn = jnp.concatenate([hn_ref[0, h, r, :].astype(F32)[:, 0:MLSTM_V_DIM] for h in range(MLSTM_HEADS)],
                             axis=-1)
        return og_ref[0, r, :].astype(F32) * hn + skip_ref[...] * uc_ref[0, r, :].astype(F32)

    _tail(mix_rows, x_ref, qmem_ref, gate_ref, kexp_ref, vexp_ref, w_out_ref, post_g_ref, o_ref)


def _layer_out(kernel_fn, name, x, mixer_inputs, mixer_specs, qmem, gate, kexp, vexp, w_out, post_g):
    B, S, D = x.shape
    tm = min(ROW_TILE, S)
    row = lambda b, i: (b, i, 0)
    per_batch = lambda b, i: (b, 0, 0)
    in_specs = ([pl.BlockSpec((1, tm, D), row)] + mixer_specs(tm) +
                [pl.BlockSpec((1, tm, MEM_WIDTH), row), pl.BlockSpec((1, tm, gate.shape[-1]), row),
                 pl.BlockSpec((1,) + kexp.shape[1:], per_batch), pl.BlockSpec((1,) + vexp.shape[1:], per_batch),
                 _const_spec(w_out.shape), _const_spec((1, D))])
    return pl.pallas_call(
        kernel_fn,
        grid=(B, S // tm),
        in_specs=in_specs,
        out_specs=pl.BlockSpec((1, tm, D), row),
        out_shape=jax.ShapeDtypeStruct((B, S, D), F32),
        compiler_params=_params("parallel", "parallel"),
        name=name,
    )(x, *mixer_inputs, qmem, gate, kexp, vexp, w_out.astype(BF16), post_g.reshape(1, D))


B_U = (0, MIX_WIDTH)
B_O = (B_U[1], B_U[1] + MIX_WIDTH)
B_QMEM = (B_O[1], B_O[1] + MEM_WIDTH)
B_GATE = (B_QMEM[1], B_QMEM[1] + 1024)
B_IF = (B_GATE[1], B_GATE[1] + LANES)
B_COLS = B_IF[1]


def _mlstm_in_kernel(x_ref, pre_g_ref, w_in_ref, u_ref, og_ref, qmem_ref, gate_ref, if_ref):
    h = _rms(x_ref[0], pre_g_ref[...]).astype(BF16)
    p = _dot(h, w_in_ref[...])
    u_ref[0] = p[:, B_U[0]:B_U[1]].astype(BF16)
    og_ref[0] = jax.nn.sigmoid(p[:, B_O[0]:B_O[1]]).astype(BF16)
    qmem_ref[0] = p[:, B_QMEM[0]:B_QMEM[1]].astype(BF16)
    gate = p[:, B_GATE[0]:B_GATE[1]]
    gate_ref[0] = (gate * jax.nn.sigmoid(gate)).astype(BF16)
    if_ref[0] = p[:, B_IF[0]:B_IF[1]]


def _mlstm_in(x, pre_g, w_in):
    B, S, D = x.shape
    tm = min(ROW_TILE, S)
    n_if = 2 * MLSTM_HEADS
    o_u, o_if, o_o, o_qm, o_gate = np.cumsum([0, MIX_WIDTH, n_if, MIX_WIDTH, MEM_WIDTH])
    w_in_p = jnp.concatenate(
        [w_in[:, o_u:o_if], w_in[:, o_o:o_gate + 1024], w_in[:, o_if:o_o],
         jnp.zeros((D, LANES - n_if), w_in.dtype)], axis=1).astype(BF16)
    row = lambda b, i: (b, i, 0)
    widths = [(MIX_WIDTH, BF16), (MIX_WIDTH, BF16), (MEM_WIDTH, BF16), (1024, BF16), (LANES, F32)]
    return pl.pallas_call(
        _mlstm_in_kernel,
        grid=(B, S // tm),
        in_specs=[pl.BlockSpec((1, tm, D), row), _const_spec((1, D)), _const_spec(w_in_p.shape)],
        out_specs=[pl.BlockSpec((1, tm, w), row) for w, _ in widths],
        out_shape=[jax.ShapeDtypeStruct((B, S, w), dt) for w, dt in widths],
        compiler_params=_params("parallel", "parallel"),
        name="mlstm_in",
    )(x, pre_g.reshape(1, D), w_in_p)


def _mlstm_prep_kernel(u_ref, halo_ref, if_ref, conv_w_ref, conv_b_ref, bias_ref, wqk_ref, wv_ref,
                       uc_ref, qt_ref, k_ref, vt_ref, gc_ref, gr_ref, *, chunk):
    ts = u_ref.shape[1]
    halo_rows = halo_ref.shape[1]
    u_bf = u_ref[0]
    u = u_bf.astype(F32)
    halo = halo_ref[0].astype(F32) * (pl.program_id(1) > 0).astype(F32)
    ext = jnp.concatenate([halo, u], axis=0)
    conv = conv_b_ref[...] + u * conv_w_ref[CONV_WIDTH - 1:CONV_WIDTH, :]
    for back in range(1, CONV_WIDTH):
        tap = CONV_WIDTH - 1 - back
        conv = conv + ext[halo_rows - back:halo_rows - back + ts, :] * conv_w_ref[tap:tap + 1, :]
    uc = conv * jax.nn.sigmoid(conv)
    uc_ref[0] = uc.astype(BF16)
    uc_bf = uc.astype(BF16)

    k_scale = MLSTM_QK_DIM ** -0.5
    vrow = lax.broadcasted_iota(jnp.int32, (MLSTM_PAD_V, ts), 0)
    for h in range(MLSTM_HEADS):
        cols = slice(MLSTM_HEAD_START[h], MLSTM_HEAD_START[h] + MXU_DIM)
        qk = _dot(uc_bf[:, cols], wqk_ref[h])
        qt_ref[0, h] = qk[:, :MLSTM_PAD_QK].T.astype(BF16)
        k_ref[0, h] = (qk[:, MLSTM_PAD_QK:] * k_scale).astype(BF16)
        vt = _dot(u_bf[:, cols], wv_ref[h]).T
        vt_ref[0, h] = jnp.where(vrow == MLSTM_V_DIM, 1.0, vt).astype(BF16)

    g = if_ref[0] + bias_ref[...]
    lane = lax.broadcasted_iota(jnp.int32, g.shape, 1)
    log_f = jnp.minimum(g, 0.0) - jnp.log1p(jnp.exp(-jnp.abs(g)))
    is_f = (lane >= MLSTM_HEADS) & (lane < 2 * MLSTM_HEADS)
    gates = jnp.where(lane < MLSTM_HEADS, g, jnp.where(is_f, log_f, 0.0))
    r = lax.broadcasted_iota(jnp.int32, (chunk, chunk), 0)
    c = lax.broadcasted_iota(jnp.int32, (chunk, chunk), 1)
    tril = (c <= r).astype(BF16)
    chunk_lane = lax.broadcasted_iota(jnp.int32, (chunk, LANES), 1)
    chunk_is_f = (chunk_lane >= MLSTM_HEADS) & (chunk_lane < 2 * MLSTM_HEADS)
    parts = []
    for j in range(ts // chunk):
        gj = gates[j * chunk:(j + 1) * chunk, :]
        cum = sum(_dot(tril, term) for term in _split3(gj))
        parts.append(jnp.where(chunk_is_f, cum, gj))
    gc = jnp.concatenate(parts, axis=0) if len(parts) > 1 else parts[0]
    gr_ref[0] = gc.T[0:2 * MLSTM_HEADS, :]
    gc_ref[0] = gc - pltpu.roll(gc, LANES - MLSTM_HEADS, 1)


def _mlstm_prep(u, if_pre, gate_bias, conv_w, conv_b, w_q, w_k, w_v):
    B, S, _ = u.shape
    ts = min(ROW_TILE, S)
    chunk = min(MLSTM_CHUNK, S)
    H = MLSTM_HEADS
    wqk = jnp.zeros((H, MXU_DIM, 2 * MLSTM_PAD_QK), F32)
    wv = jnp.zeros((H, MXU_DIM, MLSTM_PAD_V), F32)
    for h in range(H):
        off = h * MLSTM_V_DIM - MLSTM_HEAD_START[h]
        wqk = wqk.at[h, off:off + MLSTM_V_DIM, 0:MLSTM_QK_DIM].set(w_q[h])
        wqk = wqk.at[h, off:off + MLSTM_V_DIM, MLSTM_PAD_QK:MLSTM_PAD_QK + MLSTM_QK_DIM].set(w_k[h])
        wv = wv.at[h, off:off + MLSTM_V_DIM, 0:MLSTM_V_DIM].set(w_v[h])
    bias = jnp.pad(gate_bias, (0, LANES - 2 * H)).reshape(1, LANES)
    halo_rows = BF16_SUBLANES
    blocks_per_tile = ts // halo_rows
    row = lambda b, i: (b, i, 0)
    head_row = lambda b, i: (b, 0, i, 0)
    out_shape = [
        jax.ShapeDtypeStruct((B, S, MIX_WIDTH), BF16),
        jax.ShapeDtypeStruct((B, H, MLSTM_PAD_QK, S), BF16),
        jax.ShapeDtypeStruct((B, H, S, MLSTM_PAD_QK), BF16),
        jax.ShapeDtypeStruct((B, H, MLSTM_PAD_V, S), BF16),
        jax.ShapeDtypeStruct((B, S, LANES), F32),
        jax.ShapeDtypeStruct((B, 2 * H, S), F32),
    ]
    head_col = lambda b, i: (b, 0, 0, i)
    out_specs = [
        pl.BlockSpec((1, ts, MIX_WIDTH), row),
        pl.BlockSpec((1, H, MLSTM_PAD_QK, ts), head_col),
        pl.BlockSpec((1, H, ts, MLSTM_PAD_QK), head_row),
        pl.BlockSpec((1, H, MLSTM_PAD_V, ts), head_col),
        pl.BlockSpec((1, ts, LANES), row),
        pl.BlockSpec((1, 2 * H, ts), lambda b, i: (b, 0, i)),
    ]
    return pl.pallas_call(
        functools.partial(_mlstm_prep_kernel, chunk=chunk),
        grid=(B, S // ts),
        in_specs=[pl.BlockSpec((1, ts, MIX_WIDTH), row),
                  pl.BlockSpec((1, halo_rows, MIX_WIDTH),
                               lambda b, i: (b, jnp.maximum(i * blocks_per_tile - 1, 0), 0)),
                  pl.BlockSpec((1, ts, LANES), row),
                  _const_spec((CONV_WIDTH, MIX_WIDTH)), _const_spec((1, MIX_WIDTH)), _const_spec((1, LANES)),
                  _const_spec(wqk.shape), _const_spec(wv.shape)],
        out_specs=out_specs,
        out_shape=out_shape,
        compiler_params=_params("parallel", "parallel"),
        name="mlstm_prep",
    )(u, u, if_pre, conv_w, conv_b.reshape(1, -1), bias, wqk.astype(BF16), wv.astype(BF16))


def _mlstm_kernel(qt_ref, k_ref, vt_ref, gc_ref, gr_ref, hg_ref, o_ref, c_sc, m_sc, *, chunk):
    H = k_ref.shape[1]
    seg = k_ref.shape[2]
    L = chunk

    @pl.when(pl.program_id(1) == 0)
    def _():
        c_sc[...] = jnp.zeros(c_sc.shape, F32)
        m_sc[...] = jnp.zeros(m_sc.shape, F32)

    causal = lax.broadcasted_iota(jnp.int32, (L, L), 0) <= lax.broadcasted_iota(jnp.int32, (L, L), 1)
    vrow = lax.broadcasted_iota(jnp.int32, (MLSTM_PAD_V, L), 0)
    gains = [jnp.broadcast_to(hg_ref[h], (MLSTM_PAD_V, L)) for h in range(H)]

    def step(c, carry):
        rows = pl.ds(pl.multiple_of(c * L, L), L)
        qts = [qt_ref[0, h, :, rows] for h in range(H)]
        ks = [k_ref[0, h, rows, :] for h in range(H)]
        vts = [vt_ref[0, h, :, rows] for h in range(H)]
        qk = [_dot(ks[h], qts[h]) for h in range(H)]
        cq = [_dot(carry[h][0].astype(BF16), qts[h]) for h in range(H)]
        out = []
        for h in range(H):
            C, m = carry[h]
            c_col = gc_ref[0, rows, h:h + 1]
            li_row = gr_ref[0, h:h + 1, rows]
            b_row = gr_ref[0, MLSTM_HEADS + h:MLSTM_HEADS + h + 1, rows]
            d = jnp.where(causal, c_col + b_row, -jnp.inf)
            inter = b_row + m
            m_t = jnp.maximum(inter, jnp.max(d, axis=0, keepdims=True))
            sqk = (qk[h] * jnp.exp(d - m_t)).astype(BF16)
            num = jnp.exp(inter - m_t) * cq[h] + _dot(vts[h], sqk)
            den = jnp.maximum(jnp.abs(num[MLSTM_V_DIM:MLSTM_V_DIM + 1, :]), jnp.exp(-m_t))
            hv = jnp.where(vrow < MLSTM_V_DIM, num / den, 0.0)
            ms = jnp.sum(hv * hv, axis=0, keepdims=True) * (1.0 / MLSTM_V_DIM)
            hn = hv * lax.rsqrt(ms + EPS) * gains[h]
            o_ref[0, h, rows, :] = hn.T.astype(o_ref.dtype)

            b_last = b_row[:, L - 1:L]
            dec = b_last - b_row + li_row
            m_new = jnp.maximum(b_last + m, jnp.max(dec, axis=1, keepdims=True))
            vw = (vts[h].astype(F32) * jnp.exp(dec - m_new)).astype(BF16)
            out.append((jnp.exp(b_last + m - m_new) * C + _dot(vw, ks[h]), m_new))
        return tuple(out)

    init = tuple((c_sc[h], m_sc[h, 0:1, 0:1]) for h in range(H))
    final = lax.fori_loop(0, seg // L, step, init)
    for h in range(H):
        c_sc[h] = final[h][0]
        m_sc[h] = jnp.broadcast_to(final[h][1], m_sc.shape[1:])


def _mlstm(qt, k, vt, gc, gr, head_g):
    B, H, S, _ = k.shape
    chunk = min(MLSTM_CHUNK, S)
    seg = min(MLSTM_SEGMENT, S)
    hg = jnp.pad(head_g, ((0, 0), (0, MLSTM_PAD_V - MLSTM_V_DIM))).reshape(H, MLSTM_PAD_V, 1)
    head_row = lambda b, i: (b, 0, i, 0)
    head_col = lambda b, i: (b, 0, 0, i)
    return pl.pallas_call(
        functools.partial(_mlstm_kernel, chunk=chunk),
        grid=(B, S // seg),
        in_specs=[pl.BlockSpec((1, H, MLSTM_PAD_QK, seg), head_col),
                  pl.BlockSpec((1, H, seg, MLSTM_PAD_QK), head_row),
                  pl.BlockSpec((1, H, MLSTM_PAD_V, seg), head_col),
                  pl.BlockSpec((1, seg, LANES), lambda b, i: (b, i, 0)),
                  pl.BlockSpec((1, 2 * H, seg), lambda b, i: (b, 0, i)),
                  _const_spec(hg.shape)],
        out_specs=pl.BlockSpec((1, H, seg, MLSTM_PAD_V), head_row),
        out_shape=jax.ShapeDtypeStruct((B, H, S, MLSTM_PAD_V), BF16),
        scratch_shapes=[pltpu.VMEM((H, MLSTM_PAD_V, MLSTM_PAD_QK), F32), pltpu.VMEM((H, 8, LANES), F32)],
        compiler_params=_params("parallel", "arbitrary"),
        name="mlstm_scan",
    )(qt, k, vt, gc, gr, hg)


def _mla_layer(x, mem, cs, pre_g, w_in, q_a_g, w_uq, kv_a_g, w_ukv, mem_g, w_mem_kv, w_out, post_g):
    q, k, v, qmem, gate = _mla_in(x, cs, pre_g, w_in, q_a_g, w_uq, kv_a_g, w_ukv)
    mix = _attention(q, k, v)
    kexp, vexp = _mem_kv(mem, mem_g, w_mem_kv)
    specs = lambda tm: [pl.BlockSpec((1, tm, MIX_WIDTH), lambda b, i: (b, i, 0))]
    return _layer_out(_mla_out_kernel, "mla_out", x, [mix], specs, qmem, gate, kexp, vexp, w_out, post_g)


def _mlstm_layer(x, mem, pre_g, w_in, gate_bias, conv_w, conv_b, w_q, w_k, w_v, head_g, skip,
                 mem_g, w_mem_kv, w_out, post_g):
    u, og, qmem, gate, if_pre = _mlstm_in(x, pre_g, w_in)
    uc, qt, k, vt, gc, gr = _mlstm_prep(u, if_pre, gate_bias, conv_w, conv_b, w_q, w_k, w_v)
    hn = _mlstm(qt, k, vt, gc, gr, head_g)
    kexp, vexp = _mem_kv(mem, mem_g, w_mem_kv)
    row = lambda b, i: (b, i, 0)
    specs = lambda tm: [pl.BlockSpec((1, MLSTM_HEADS, tm, MLSTM_PAD_V), lambda b, i: (b, 0, i, 0)),
                        pl.BlockSpec((1, tm, MIX_WIDTH), row), pl.BlockSpec((1, tm, MIX_WIDTH), row),
                        _const_spec((1, MIX_WIDTH))]
    return _layer_out(_mlstm_out_kernel, "mlstm_out", x, [hn, og, uc, skip.reshape(1, -1)], specs,
                      qmem, gate, kexp, vexp, w_out, post_g)


def kernel(x, mem, positions, a_pre_g, a_w_in, a_q_a_g, a_w_uq, a_kv_a_g, a_w_ukv, a_mem_g, a_w_mem_kv, a_w_out, a_post_g, b_pre_g, b_w_in, b_gate_bias, b_conv_w, b_conv_b, b_w_q, b_w_k, b_w_v, b_head_g, b_skip, b_mem_g, b_w_mem_kv, b_w_out, b_post_g):
    depth = a_pre_g.shape[0] + b_pre_g.shape[0]
    cs = _rope_tables(positions)
    for i in range(depth):
        j = i // 2
        if i % 2 == 0:
            x = _mla_layer(x, mem, cs, a_pre_g[j], a_w_in[j], a_q_a_g[j], a_w_uq[j], a_kv_a_g[j],
                           a_w_ukv[j], a_mem_g[j], a_w_mem_kv[j], a_w_out[j], a_post_g[j])
        else:
            x = _mlstm_layer(x, mem, b_pre_g[j], b_w_in[j], b_gate_bias[j], b_conv_w[j], b_conv_b[j],
                             b_w_q[j], b_w_k[j], b_w_v[j], b_head_g[j], b_skip[j], b_mem_g[j],
                             b_w_mem_kv[j], b_w_out[j], b_post_g[j])
    return x
```

```python
import functools
import math

import jax
import jax.numpy as jnp
import numpy as np
from jax import lax
from jax.experimental import pallas as pl
from jax.experimental.pallas import tpu as pltpu

EPS = 1e-6
ROPE_THETA = 10000.0
MEM_HEADS = 4
MEM_HEAD_DIM = 64
MEM_WIDTH = MEM_HEADS * MEM_HEAD_DIM
QK_NOPE_DIM = 128
QK_ROPE_DIM = 64
V_HEAD_DIM = 128
MLA_HEADS = 6
V_EXT_DIM = V_HEAD_DIM + 16
Q_LORA_RANK = 384
KV_LORA_RANK = 256
MLSTM_HEADS = 4
MLSTM_V_DIM = 192
MLSTM_QK_DIM = 96
CONV_WIDTH = 4
MIX_WIDTH = 768

LANES = 128
MXU_DIM = 256
BF16_SUBLANES = 16
VMEM_LIMIT_BYTES = 56 * 1024 * 1024

ROW_TILE = 512
TAIL_ROW_TILE = 1024
TAIL_SUB_ROWS = 128
ATTN_TQ = 1024
ATTN_CW = 256
ATTN_TK = 1024
ATTN_HEADS_PER_STEP = 2
MLSTM_CHUNK = 256
MLSTM_SEGMENT = 1024
ROPE_ROWS = 1024

LOG2E = 1.4426950408889634
MLSTM_PAD_QK = LANES
MLSTM_PAD_V = MXU_DIM
MLSTM_HEAD_START = tuple((h * MLSTM_V_DIM // LANES) * LANES for h in range(MLSTM_HEADS))

F32 = jnp.float32
BF16 = jnp.bfloat16


def _dot(a, b):
    return jnp.dot(a, b, preferred_element_type=F32)


def _dot_nt(a, b):
    return lax.dot_general(a, b, (((1,), (1,)), ((), ())), preferred_element_type=F32)


def _rms(x, g, width=None):
    width = x.shape[-1] if width is None else width
    ms = jnp.sum(x * x, axis=-1, keepdims=True) * (1.0 / width)
    return x * lax.rsqrt(ms + EPS) * g


def _split3(x):
    b1 = x.astype(BF16)
    r1 = x - b1.astype(F32)
    b2 = r1.astype(BF16)
    b3 = (r1 - b2.astype(F32)).astype(BF16)
    return b1, b2, b3


def _params(*semantics):
    return pltpu.CompilerParams(dimension_semantics=semantics, vmem_limit_bytes=VMEM_LIMIT_BYTES)


def _const_spec(shape):
    zeros = (0,) * len(shape)
    return pl.BlockSpec(shape, lambda *_: zeros)


def _rope_kernel(pos_ref, invf_ref, sel_ref, cs_ref):
    ang = pos_ref[...].astype(F32) * invf_ref[...]
    trig = jnp.concatenate([jnp.cos(ang), jnp.sin(ang)], axis=1)
    spread = sum(_dot(term, sel_ref[...]) for term in _split3(trig))
    rb = trig.shape[0]
    per_row = spread.shape[1] // LANES
    for t in range(per_row):
        cs_ref[pl.ds(t, rb, stride=per_row), :] = spread[:, t * LANES:(t + 1) * LANES]


def _rope_tables(positions):
    B, S = positions.shape
    T = B * S
    half = QK_ROPE_DIM // 2
    per_row = LANES // half
    rows = T // per_row
    inv_freq = ROPE_THETA ** (-jnp.arange(0, QK_ROPE_DIM, 2, dtype=F32) / QK_ROPE_DIM)
    pos4 = jnp.repeat(positions.reshape(rows, per_row), half, axis=1)
    invf = jnp.tile(inv_freq, per_row).reshape(1, LANES)
    sel = np.zeros((2 * LANES, per_row * LANES), np.float32)
    for t in range(per_row):
        for f in range(half):
            sel[t * half + f, t * LANES + f] = 1.0
            sel[t * half + f, t * LANES + half + f] = 1.0
            sel[LANES + t * half + f, t * LANES + 2 * half + f] = -1.0
            sel[LANES + t * half + f, t * LANES + 3 * half + f] = 1.0
    rb = min(ROPE_ROWS, rows)
    cs = pl.pallas_call(
        _rope_kernel,
        grid=(rows // rb,),
        in_specs=[pl.BlockSpec((rb, LANES), lambda i: (i, 0)), _const_spec((1, LANES)),
                  _const_spec(sel.shape)],
        out_specs=pl.BlockSpec((rb * per_row, LANES), lambda i: (i, 0)),
        out_shape=jax.ShapeDtypeStruct((T, LANES), F32),
        compiler_params=_params("parallel"),
        name="rope_tables",
    )(pos4, invf, jnp.asarray(sel, BF16))
    return cs.reshape(B, S, LANES)


def _mem_kv_kernel(mem_ref, g_ref, w_ref, k_ref, v_ref):
    n_mem = mem_ref.shape[1]
    hn = _rms(mem_ref[0], g_ref[...]).astype(BF16)
    kv = _dot(hn, w_ref[...])
    k = kv[:, :MEM_WIDTH] * (MEM_HEAD_DIM ** -0.5 * LOG2E)
    v = kv[:, MEM_WIDTH:]
    col_head = lax.broadcasted_iota(jnp.int32, (n_mem, MEM_WIDTH), 1) // MEM_HEAD_DIM
    for h in range(MEM_HEADS):
        rows = pl.ds(h * n_mem, n_mem)
        k_ref[0, rows, :] = jnp.where(col_head == h, k, 0.0).astype(BF16)
        v_ref[0, rows, :] = jnp.where(col_head == h, v, 0.0).astype(BF16)


def _mem_kv(mem, mem_g, w_mem_kv):
    B, n_mem, D = mem.shape
    out = jax.ShapeDtypeStruct((B, MEM_HEADS * n_mem, MEM_WIDTH), BF16)
    spec = pl.BlockSpec((1, MEM_HEADS * n_mem, MEM_WIDTH), lambda b: (b, 0, 0))
    return pl.pallas_call(
        _mem_kv_kernel,
        grid=(B,),
        in_specs=[pl.BlockSpec((1, n_mem, D), lambda b: (b, 0, 0)), _const_spec((1, D)),
                  _const_spec((D, 2 * MEM_WIDTH))],
        out_specs=[spec, spec],
        out_shape=[out, out],
        compiler_params=_params("parallel"),
        name="mem_kv",
    )(mem, mem_g.reshape(1, D), w_mem_kv.astype(BF16))


A_CQ = (0, Q_LORA_RANK)
A_CKV = (A_CQ[1], A_CQ[1] + KV_LORA_RANK)
A_QMEM = (A_CKV[1], A_CKV[1] + MEM_WIDTH)
A_GATE = (A_QMEM[1], A_QMEM[1] + 1024)
A_KROPE = (A_GATE[1], A_GATE[1] + 2 * QK_ROPE_DIM)
A_COLS = A_KROPE[1]
Q_ROPE_OFF = MLA_HEADS * QK_NOPE_DIM


def _mla_in_kernel(x_ref, cs_ref, pre_g_ref, w_in_ref, qa_g_ref, w_uq_ref, kva_g_ref, w_ukv_ref,
                   q_ref, k_ref, v_ref, qmem_ref, gate_ref):
    q_scale = (QK_NOPE_DIM + QK_ROPE_DIM) ** -0.5 * LOG2E
    h = _rms(x_ref[0], pre_g_ref[...]).astype(BF16)
    p = _dot(h, w_in_ref[...])
    cs = cs_ref[0]
    qmem_ref[0] = p[:, A_QMEM[0]:A_QMEM[1]].astype(BF16)
    gate = p[:, A_GATE[0]:A_GATE[1]]
    gate_ref[0] = (gate * jax.nn.sigmoid(gate)).astype(BF16)

    c_q = _rms(p[:, A_CQ[0]:A_CQ[1]], qa_g_ref[...]).astype(BF16)
    q = _dot(c_q, w_uq_ref[...])
    c_kv = _rms(p[:, A_CKV[0]:A_CKV[1]], kva_g_ref[...]).astype(BF16)
    kv = _dot(c_kv, w_ukv_ref[...])

    kr = p[:, A_KROPE[0]:A_KROPE[1]] * cs
    k_rot = (kr + pltpu.roll(kr, QK_ROPE_DIM, 1)).astype(BF16)
    tm = cs.shape[0]
    pad_row = lax.broadcasted_iota(jnp.int32, (V_EXT_DIM - V_HEAD_DIM, tm), 0)
    ones_row = jnp.where(pad_row == 0, 1.0, 0.0).astype(BF16)
    for hd in range(MLA_HEADS):
        nope = slice(hd * QK_NOPE_DIM, (hd + 1) * QK_NOPE_DIM)
        rope = slice(Q_ROPE_OFF + hd * LANES, Q_ROPE_OFF + (hd + 1) * LANES)
        q_ref[0, hd, 0:LANES, :] = (q[:, nope] * q_scale).T.astype(BF16)
        q_ref[0, hd, LANES:2 * LANES, :] = (q[:, rope] * cs * q_scale).T.astype(BF16)
        k_ref[0, hd, :, 0:LANES] = kv[:, 2 * hd * LANES:(2 * hd + 1) * LANES].astype(BF16)
        k_ref[0, hd, :, LANES:2 * LANES] = k_rot
        v_ref[0, hd, 0:V_HEAD_DIM, :] = kv[:, (2 * hd + 1) * LANES:(2 * hd + 2) * LANES].T.astype(BF16)
        v_ref[0, hd, V_HEAD_DIM:V_EXT_DIM, :] = ones_row


def _rope_cols(w, start):
    half = QK_ROPE_DIM // 2
    return [w[:, start:start + QK_ROPE_DIM], w[:, start + half:start + QK_ROPE_DIM], w[:, start:start + half]]


def _mla_in(x, cs, pre_g, w_in, q_a_g, w_uq, kv_a_g, w_ukv):
    B, S, D = x.shape
    tm = min(ROW_TILE, S)
    o_kr = Q_LORA_RANK + KV_LORA_RANK
    o_qm = o_kr + QK_ROPE_DIM
    w_in_p = jnp.concatenate([w_in[:, :o_kr], w_in[:, o_qm:]] + _rope_cols(w_in, o_kr), axis=1).astype(BF16)
    head_w = QK_NOPE_DIM + QK_ROPE_DIM
    uq_cols = [w_uq[:, h * head_w:h * head_w + QK_NOPE_DIM] for h in range(MLA_HEADS)]
    for h in range(MLA_HEADS):
        uq_cols += _rope_cols(w_uq, h * head_w + QK_NOPE_DIM)
    w_uq_p = jnp.concatenate(uq_cols, axis=1).astype(BF16)
    w_ukv_p = w_ukv.astype(BF16)

    row = lambda b, i: (b, i, 0)
    head_row = lambda b, i: (b, 0, i, 0)
    head_col = lambda b, i: (b, 0, 0, i)
    out_shape = [
        jax.ShapeDtypeStruct((B, MLA_HEADS, 2 * LANES, S), BF16),
        jax.ShapeDtypeStruct((B, MLA_HEADS, S, 2 * LANES), BF16),
        jax.ShapeDtypeStruct((B, MLA_HEADS, V_EXT_DIM, S), BF16),
        jax.ShapeDtypeStruct((B, S, MEM_WIDTH), BF16),
        jax.ShapeDtypeStruct((B, S, 1024), BF16),
    ]
    out_specs = [
        pl.BlockSpec((1, MLA_HEADS, 2 * LANES, tm), head_col),
        pl.BlockSpec((1, MLA_HEADS, tm, 2 * LANES), head_row),
        pl.BlockSpec((1, MLA_HEADS, V_EXT_DIM, tm), head_col),
        pl.BlockSpec((1, tm, MEM_WIDTH), row),
        pl.BlockSpec((1, tm, 1024), row),
    ]
    return pl.pallas_call(
        _mla_in_kernel,
        grid=(B, S // tm),
        in_specs=[pl.BlockSpec((1, tm, D), row), pl.BlockSpec((1, tm, LANES), row),
                  _const_spec((1, D)), _const_spec(w_in_p.shape),
                  _const_spec((1, Q_LORA_RANK)), _const_spec(w_uq_p.shape),
                  _const_spec((1, KV_LORA_RANK)), _const_spec(w_ukv_p.shape)],
        out_specs=out_specs,
        out_shape=out_shape,
        compiler_params=_params("parallel", "parallel"),
        name="mla_in",
    )(x, cs, pre_g.reshape(1, D), w_in_p, q_a_g.reshape(1, -1), w_uq_p, kv_a_g.reshape(1, -1), w_ukv_p)


def _attn_kernel(qt_ref, k_ref, vt_ref, o_ref, *, tq, tk, cw):
    heads = k_ref.shape[1]
    S = k_ref.shape[2]
    dve = vt_ref.shape[2]
    dv = V_HEAD_DIM
    nc = tq // cw
    causal = (lax.broadcasted_iota(jnp.int32, (cw, cw), 0) <= lax.broadcasted_iota(jnp.int32, (cw, cw), 1))

    def q_tile(j, _):
        q0 = pl.multiple_of(j * tq, tq)
        qts = [[qt_ref[0, h, :, pl.ds(q0 + c * cw, cw)] for c in range(nc)] for h in range(heads)]

        def softmax_pv(carry, s, vt_tile):
            m, acc = carry
            m_new = jnp.maximum(m, jnp.max(s, axis=0, keepdims=True))
            p = jnp.exp2(s - m_new).astype(BF16)
            return m_new, jnp.exp2(m - m_new) * acc + _dot(vt_tile, p)

        def body(ki, carries):
            rows = pl.ds(pl.multiple_of(ki * tk, tk), tk)
            k_tiles = [k_ref[0, h, rows, :] for h in range(heads)]
            scores = [[_dot(k_tiles[h], qts[h][c]) for c in range(nc)] for h in range(heads)]
            vt_tiles = [vt_ref[0, h, :, rows] for h in range(heads)]
            return tuple(tuple(softmax_pv(carries[h][c], scores[h][c], vt_tiles[h]) for c in range(nc))
                         for h in range(heads))

        init = tuple(tuple((jnp.full((1, cw), -jnp.inf, F32), jnp.zeros((dve, cw), F32)) for _ in range(nc))
                     for _ in range(heads))
        carries = lax.fori_loop(0, j * (tq // tk), body, init)

        chains = [(h, c) for h in range(heads) for c in range(nc)]
        scores = {}
        for h, c in chains:
            s = _dot(k_ref[0, h, pl.ds(q0, (c + 1) * cw), :], qts[h][c])
            s_diag = jnp.where(causal, s[c * cw:, :], -jnp.inf)
            scores[h, c] = jnp.concatenate([s[:c * cw, :], s_diag], axis=0) if c > 0 else s_diag
        for h, c in chains:
            _, acc = softmax_pv(carries[h][c], scores[h, c], vt_ref[0, h, :, pl.ds(q0, (c + 1) * cw)])
            out = acc[0:dv, :] / acc[dv:dv + 1, :]
            o_ref[0, pl.ds(q0 + c * cw, cw), h * dv:(h + 1) * dv] = out.T.astype(o_ref.dtype)
        return 0

    lax.fori_loop(0, S // tq, q_tile, 0)


def _attention(qt, k, vt):
    B, H, S, dqk = k.shape
    dve = vt.shape[2]
    tq = min(ATTN_TQ, S)
    tk = min(ATTN_TK, tq)
    cw = min(ATTN_CW, tq)
    hp = ATTN_HEADS_PER_STEP
    head = lambda b, h: (b, h, 0, 0)
    return pl.pallas_call(
        functools.partial(_attn_kernel, tq=tq, tk=tk, cw=cw),
        grid=(B, H // hp),
        in_specs=[pl.BlockSpec((1, hp, dqk, S), head), pl.BlockSpec((1, hp, S, dqk), head),
                  pl.BlockSpec((1, hp, dve, S), head)],
        out_specs=pl.BlockSpec((1, S, hp * V_HEAD_DIM), lambda b, h: (b, 0, h)),
        out_shape=jax.ShapeDtypeStruct((B, S, H * V_HEAD_DIM), BF16),
        compiler_params=_params("parallel", "parallel"),
        name="mla_attention",
    )(qt, k, vt)


def _memory_probs(s, n_mem):
    probs = []
    for h in range(MEM_HEADS):
        sh = s[:, h * n_mem:(h + 1) * n_mem]
        e = jnp.exp2(sh - jnp.max(sh, axis=-1, keepdims=True))
        probs.append((e / jnp.sum(e, axis=-1, keepdims=True)).astype(BF16))
    return jnp.concatenate(probs, axis=-1)


def _tail(mix_rows, x_ref, qmem_ref, gate_ref, kexp_ref, vexp_ref, w_out_ref, post_g_ref, o_ref):
    tm = x_ref.shape[1]
    sub = min(TAIL_SUB_ROWS, tm)
    blocks = [slice(i * sub, (i + 1) * sub) for i in range(tm // sub)]
    n_mem = kexp_ref.shape[1] // MEM_HEADS
    kexp = kexp_ref[0]
    scores = [_dot_nt(qmem_ref[0, r, :], kexp) for r in blocks]
    for r, s in zip(blocks, scores):
        mo = _dot(_memory_probs(s, n_mem), vexp_ref[0])
        gate = gate_ref[0, r, :].astype(F32)
        y_mix = (mix_rows(r) * gate[:, :MIX_WIDTH]).astype(BF16)
        y_mem = (mo * gate[:, MIX_WIDTH:]).astype(BF16)
        y = _dot(y_mix, w_out_ref[0:MIX_WIDTH, :]) + _dot(y_mem, w_out_ref[MIX_WIDTH:, :])
        o_ref[0, r, :] = x_ref[0, r, :] + _rms(y, post_g_ref[...])


def _mla_out_kernel(x_ref, mix_ref, qmem_ref, gate_ref, kexp_ref, vexp_ref, w_out_ref, post_g_ref, o_ref):
    mix_rows = lambda r: mix_ref[0, r, :].astype(F32)
    _tail(mix_rows, x_ref, qmem_ref, gate_ref, kexp_ref, vexp_ref, w_out_ref, post_g_ref, o_ref)


def _mlstm_out_kernel(x_ref, hn_ref, og_ref, uc_ref, skip_ref, qmem_ref, gate_ref, kexp_ref, vexp_ref,
                      w_out_ref, post_g_ref, o_ref):
    def mix_rows(r):
        return (og_ref[0, r, :].astype(F32) * hn_ref[0, r, :].astype(F32)
                + skip_ref[...] * uc_ref[0, r, :].astype(F32))

    _tail(mix_rows, x_ref, qmem_ref, gate_ref, kexp_ref, vexp_ref, w_out_ref, post_g_ref, o_ref)


def _layer_out(kernel_fn, name, x, mixer_inputs, mixer_specs, qmem, gate, kexp, vexp, w_out, post_g):
    B, S, D = x.shape
    tm = min(TAIL_ROW_TILE, S)
    row = lambda b, i: (b, i, 0)
    per_batch = lambda b, i: (b, 0, 0)
    in_specs = ([pl.BlockSpec((1, tm, D), row)] + mixer_specs(tm) +
                [pl.BlockSpec((1, tm, MEM_WIDTH), row), pl.BlockSpec((1, tm, gate.shape[-1]), row),
                 pl.BlockSpec((1,) + kexp.shape[1:], per_batch), pl.BlockSpec((1,) + vexp.shape[1:], per_batch),
                 _const_spec(w_out.shape), _const_spec((1, D))])
    return pl.pallas_call(
        kernel_fn,
        grid=(B, S // tm),
        in_specs=in_specs,
        out_specs=pl.BlockSpec((1, tm, D), row),
        out_shape=jax.ShapeDtypeStruct((B, S, D), F32),
        compiler_params=_params("parallel", "parallel"),
        name=name,
    )(x, *mixer_inputs, qmem, gate, kexp, vexp, w_out.astype(BF16), post_g.reshape(1, D))


B_U = (0, MIX_WIDTH)
B_O = (B_U[1], B_U[1] + MIX_WIDTH)
B_QMEM = (B_O[1], B_O[1] + MEM_WIDTH)
B_GATE = (B_QMEM[1], B_QMEM[1] + 1024)
B_IF = (B_GATE[1], B_GATE[1] + LANES)
B_COLS = B_IF[1]


def _mlstm_in_kernel(x_ref, pre_g_ref, w_in_ref, u_ref, og_ref, qmem_ref, gate_ref, if_ref):
    h = _rms(x_ref[0], pre_g_ref[...]).astype(BF16)
    p = _dot(h, w_in_ref[...])
    u_ref[0] = p[:, B_U[0]:B_U[1]].astype(BF16)
    og_ref[0] = jax.nn.sigmoid(p[:, B_O[0]:B_O[1]]).astype(BF16)
    qmem_ref[0] = p[:, B_QMEM[0]:B_QMEM[1]].astype(BF16)
    gate = p[:, B_GATE[0]:B_GATE[1]]
    gate_ref[0] = (gate * jax.nn.sigmoid(gate)).astype(BF16)
    if_ref[0] = p[:, B_IF[0]:B_IF[1]]


def _mlstm_in(x, pre_g, w_in):
    B, S, D = x.shape
    tm = min(ROW_TILE, S)
    n_if = 2 * MLSTM_HEADS
    o_u, o_if, o_o, o_qm, o_gate = np.cumsum([0, MIX_WIDTH, n_if, MIX_WIDTH, MEM_WIDTH])
    w_in_p = jnp.concatenate(
        [w_in[:, o_u:o_if], w_in[:, o_o:o_gate + 1024], w_in[:, o_if:o_o],
         jnp.zeros((D, LANES - n_if), w_in.dtype)], axis=1).astype(BF16)
    row = lambda b, i: (b, i, 0)
    widths = [(MIX_WIDTH, BF16), (MIX_WIDTH, BF16), (MEM_WIDTH, BF16), (1024, BF16), (LANES, F32)]
    return pl.pallas_call(
        _mlstm_in_kernel,
        grid=(B, S // tm),
        in_specs=[pl.BlockSpec((1, tm, D), row), _const_spec((1, D)), _const_spec(w_in_p.shape)],
        out_specs=[pl.BlockSpec((1, tm, w), row) for w, _ in widths],
        out_shape=[jax.ShapeDtypeStruct((B, S, w), dt) for w, dt in widths],
        compiler_params=_params("parallel", "parallel"),
        name="mlstm_in",
    )(x, pre_g.reshape(1, D), w_in_p)


def _mlstm_prep_kernel(u_ref, halo_ref, if_ref, conv_w_ref, conv_b_ref, bias_ref, wqk_ref, wv_ref,
                       uc_ref, qt_ref, k_ref, vt_ref, gc_ref, gr_ref, *, chunk):
    ts = u_ref.shape[1]
    halo_rows = halo_ref.shape[1]
    u_bf = u_ref[0]
    u = u_bf.astype(F32)
    halo = halo_ref[0].astype(F32) * (pl.program_id(1) > 0).astype(F32)
    ext = jnp.concatenate([halo, u], axis=0)
    conv = conv_b_ref[...] + u * conv_w_ref[CONV_WIDTH - 1:CONV_WIDTH, :]
    for back in range(1, CONV_WIDTH):
        tap = CONV_WIDTH - 1 - back
        conv = conv + ext[halo_rows - back:halo_rows - back + ts, :] * conv_w_ref[tap:tap + 1, :]
    uc = conv * jax.nn.sigmoid(conv)
    uc_ref[0] = uc.astype(BF16)
    uc_bf = uc.astype(BF16)

    k_scale = MLSTM_QK_DIM ** -0.5
    vrow = lax.broadcasted_iota(jnp.int32, (MLSTM_PAD_V, ts), 0)
    for h in range(MLSTM_HEADS):
        cols = slice(MLSTM_HEAD_START[h], MLSTM_HEAD_START[h] + MXU_DIM)
        qk = _dot(uc_bf[:, cols], wqk_ref[h])
        qt_ref[0, h] = qk[:, :MLSTM_PAD_QK].T.astype(BF16)
        k_ref[0, h] = (qk[:, MLSTM_PAD_QK:] * k_scale).astype(BF16)
        vt = _dot(u_bf[:, cols], wv_ref[h]).T
        vt_ref[0, h] = jnp.where(vrow == MLSTM_V_DIM, 1.0, vt).astype(BF16)

    g = if_ref[0] + bias_ref[...]
    lane = lax.broadcasted_iota(jnp.int32, g.shape, 1)
    log_f = jnp.minimum(g, 0.0) - jnp.log1p(jnp.exp(-jnp.abs(g)))
    is_f = (lane >= MLSTM_HEADS) & (lane < 2 * MLSTM_HEADS)
    gates = jnp.where(lane < MLSTM_HEADS, g, jnp.where(is_f, log_f, 0.0))
    r = lax.broadcasted_iota(jnp.int32, (chunk, chunk), 0)
    c = lax.broadcasted_iota(jnp.int32, (chunk, chunk), 1)
    tril = (c <= r).astype(BF16)
    chunk_lane = lax.broadcasted_iota(jnp.int32, (chunk, LANES), 1)
    chunk_is_f = (chunk_lane >= MLSTM_HEADS) & (chunk_lane < 2 * MLSTM_HEADS)
    parts = []
    for j in range(ts // chunk):
        gj = gates[j * chunk:(j + 1) * chunk, :]
        cum = sum(_dot(tril, term) for term in _split3(gj))
        parts.append(jnp.where(chunk_is_f, cum, gj))
    gc = jnp.concatenate(parts, axis=0) if len(parts) > 1 else parts[0]
    gr_ref[0] = gc.T[0:2 * MLSTM_HEADS, :]
    gc_ref[0] = gc - pltpu.roll(gc, LANES - MLSTM_HEADS, 1)


def _mlstm_prep(u, if_pre, gate_bias, conv_w, conv_b, w_q, w_k, w_v):
    B, S, _ = u.shape
    ts = min(ROW_TILE, S)
    chunk = min(MLSTM_CHUNK, S)
    H = MLSTM_HEADS
    wqk = jnp.zeros((H, MXU_DIM, 2 * MLSTM_PAD_QK), F32)
    wv = jnp.zeros((H, MXU_DIM, MLSTM_PAD_V), F32)
    for h in range(H):
        off = h * MLSTM_V_DIM - MLSTM_HEAD_START[h]
        wqk = wqk.at[h, off:off + MLSTM_V_DIM, 0:MLSTM_QK_DIM].set(w_q[h])
        wqk = wqk.at[h, off:off + MLSTM_V_DIM, MLSTM_PAD_QK:MLSTM_PAD_QK + MLSTM_QK_DIM].set(w_k[h])
        wv = wv.at[h, off:off + MLSTM_V_DIM, 0:MLSTM_V_DIM].set(w_v[h])
    bias = jnp.pad(gate_bias, (0, LANES - 2 * H)).reshape(1, LANES)
    halo_rows = BF16_SUBLANES
    blocks_per_tile = ts // halo_rows
    row = lambda b, i: (b, i, 0)
    head_row = lambda b, i: (b, 0, i, 0)
    out_shape = [
        jax.ShapeDtypeStruct((B, S, MIX_WIDTH), BF16),
        jax.ShapeDtypeStruct((B, H, MLSTM_PAD_QK, S), BF16),
        jax.ShapeDtypeStruct((B, H, S, MLSTM_PAD_QK), BF16),
        jax.ShapeDtypeStruct((B, H, MLSTM_PAD_V, S), BF16),
        jax.ShapeDtypeStruct((B, S, LANES), F32),
        jax.ShapeDtypeStruct((B, 2 * H, S), F32),
    ]
    head_col = lambda b, i: (b, 0, 0, i)
    out_specs = [
        pl.BlockSpec((1, ts, MIX_WIDTH), row),
        pl.BlockSpec((1, H, MLSTM_PAD_QK, ts), head_col),
        pl.BlockSpec((1, H, ts, MLSTM_PAD_QK), head_row),
        pl.BlockSpec((1, H, MLSTM_PAD_V, ts), head_col),
        pl.BlockSpec((1, ts, LANES), row),
        pl.BlockSpec((1, 2 * H, ts), lambda b, i: (b, 0, i)),
    ]
    return pl.pallas_call(
        functools.partial(_mlstm_prep_kernel, chunk=chunk),
        grid=(B, S // ts),
        in_specs=[pl.BlockSpec((1, ts, MIX_WIDTH), row),
                  pl.BlockSpec((1, halo_rows, MIX_WIDTH),
                               lambda b, i: (b, jnp.maximum(i * blocks_per_tile - 1, 0), 0)),
                  pl.BlockSpec((1, ts, LANES), row),
                  _const_spec((CONV_WIDTH, MIX_WIDTH)), _const_spec((1, MIX_WIDTH)), _const_spec((1, LANES)),
                  _const_spec(wqk.shape), _const_spec(wv.shape)],
        out_specs=out_specs,
        out_shape=out_shape,
        compiler_params=_params("parallel", "parallel"),
        name="mlstm_prep",
    )(u, u, if_pre, conv_w, conv_b.reshape(1, -1), bias, wqk.astype(BF16), wv.astype(BF16))


def _mlstm_kernel(qt_ref, k_ref, vt_ref, gc_ref, gr_ref, hg_ref, o_ref, c_sc, m_sc, *, chunk):
    H = k_ref.shape[1]
    seg = k_ref.shape[2]
    L = chunk

    @pl.when(pl.program_id(1) == 0)
    def _():
        c_sc[...] = jnp.zeros(c_sc.shape, F32)
        m_sc[...] = jnp.zeros(m_sc.shape, F32)

    causal = lax.broadcasted_iota(jnp.int32, (L, L), 0) <= lax.broadcasted_iota(jnp.int32, (L, L), 1)
    vrow = lax.broadcasted_iota(jnp.int32, (MLSTM_PAD_V, L), 0)
    gains = [jnp.broadcast_to(hg_ref[h], (MLSTM_PAD_V, L)) for h in range(H)]

    def step(c, carry):
        rows = pl.ds(pl.multiple_of(c * L, L), L)
        qts = [qt_ref[0, h, :, rows] for h in range(H)]
        ks = [k_ref[0, h, rows, :] for h in range(H)]
        vts = [vt_ref[0, h, :, rows] for h in range(H)]
        qk = [_dot(ks[h], qts[h]) for h in range(H)]
        cq = [_dot(carry[h][0].astype(BF16), qts[h]) for h in range(H)]
        out, normed = [], []
        for h in range(H):
            C, m = carry[h]
            c_col = gc_ref[0, rows, h:h + 1]
            li_row = gr_ref[0, h:h + 1, rows]
            b_row = gr_ref[0, MLSTM_HEADS + h:MLSTM_HEADS + h + 1, rows]
            d = jnp.where(causal, c_col + b_row, -jnp.inf)
            inter = b_row + m
            m_t = jnp.maximum(inter, jnp.max(d, axis=0, keepdims=True))
            sqk = (qk[h] * jnp.exp(d - m_t)).astype(BF16)
            num = jnp.exp(inter - m_t) * cq[h] + _dot(vts[h], sqk)
            den = jnp.maximum(jnp.abs(num[MLSTM_V_DIM:MLSTM_V_DIM + 1, :]), jnp.exp(-m_t))
            hv = jnp.where(vrow < MLSTM_V_DIM, num / den, 0.0)
            ms = jnp.sum(hv * hv, axis=0, keepdims=True) * (1.0 / MLSTM_V_DIM)
            normed.append((hv * lax.rsqrt(ms + EPS) * gains[h])[0:MLSTM_V_DIM, :])

            b_last = b_row[:, L - 1:L]
            dec = b_last - b_row + li_row
            m_new = jnp.maximum(b_last + m, jnp.max(dec, axis=1, keepdims=True))
            vw = (vts[h].astype(F32) * jnp.exp(dec - m_new)).astype(BF16)
            out.append((jnp.exp(b_last + m - m_new) * C + _dot(vw, ks[h]), m_new))
        o_ref[0, rows, :] = jnp.concatenate(normed, axis=0).T.astype(o_ref.dtype)
        return tuple(out)

    init = tuple((c_sc[h], m_sc[h, 0:1, 0:1]) for h in range(H))
    final = lax.fori_loop(0, seg // L, step, init)
    for h in range(H):
        c_sc[h] = final[h][0]
        m_sc[h] = jnp.broadcast_to(final[h][1], m_sc.shape[1:])


def _mlstm(qt, k, vt, gc, gr, head_g):
    B, H, S, _ = k.shape
    chunk = min(MLSTM_CHUNK, S)
    seg = min(MLSTM_SEGMENT, S)
    hg = jnp.pad(head_g, ((0, 0), (0, MLSTM_PAD_V - MLSTM_V_DIM))).reshape(H, MLSTM_PAD_V, 1)
    head_row = lambda b, i: (b, 0, i, 0)
    head_col = lambda b, i: (b, 0, 0, i)
    return pl.pallas_call(
        functools.partial(_mlstm_kernel, chunk=chunk),
        grid=(B, S // seg),
        in_specs=[pl.BlockSpec((1, H, MLSTM_PAD_QK, seg), head_col),
                  pl.BlockSpec((1, H, seg, MLSTM_PAD_QK), head_row),
                  pl.BlockSpec((1, H, MLSTM_PAD_V, seg), head_col),
                  pl.BlockSpec((1, seg, LANES), lambda b, i: (b, i, 0)),
                  pl.BlockSpec((1, 2 * H, seg), lambda b, i: (b, 0, i)),
                  _const_spec(hg.shape)],
        out_specs=pl.BlockSpec((1, seg, H * MLSTM_V_DIM), lambda b, i: (b, i, 0)),
        out_shape=jax.ShapeDtypeStruct((B, S, H * MLSTM_V_DIM), BF16),
        scratch_shapes=[pltpu.VMEM((H, MLSTM_PAD_V, MLSTM_PAD_QK), F32), pltpu.VMEM((H, 8, LANES), F32)],
        compiler_params=_params("parallel", "arbitrary"),
        name="mlstm_scan",
    )(qt, k, vt, gc, gr, hg)


def _mla_layer(x, mem, cs, pre_g, w_in, q_a_g, w_uq, kv_a_g, w_ukv, mem_g, w_mem_kv, w_out, post_g):
    q, k, v, qmem, gate = _mla_in(x, cs, pre_g, w_in, q_a_g, w_uq, kv_a_g, w_ukv)
    mix = _attention(q, k, v)
    kexp, vexp = _mem_kv(mem, mem_g, w_mem_kv)
    specs = lambda tm: [pl.BlockSpec((1, tm, MIX_WIDTH), lambda b, i: (b, i, 0))]
    return _layer_out(_mla_out_kernel, "mla_out", x, [mix], specs, qmem, gate, kexp, vexp, w_out, post_g)


def _mlstm_layer(x, mem, pre_g, w_in, gate_bias, conv_w, conv_b, w_q, w_k, w_v, head_g, skip,
                 mem_g, w_mem_kv, w_out, post_g):
    u, og, qmem, gate, if_pre = _mlstm_in(x, pre_g, w_in)
    uc, qt, k, vt, gc, gr = _mlstm_prep(u, if_pre, gate_bias, conv_w, conv_b, w_q, w_k, w_v)
    hn = _mlstm(qt, k, vt, gc, gr, head_g)
    kexp, vexp = _mem_kv(mem, mem_g, w_mem_kv)
    row = lambda b, i: (b, i, 0)
    specs = lambda tm: [pl.BlockSpec((1, tm, MIX_WIDTH), row),
                        pl.BlockSpec((1, tm, MIX_WIDTH), row), pl.BlockSpec((1, tm, MIX_WIDTH), row),
                        _const_spec((1, MIX_WIDTH))]
    return _layer_out(_mlstm_out_kernel, "mlstm_out", x, [hn, og, uc, skip.reshape(1, -1)], specs,
                      qmem, gate, kexp, vexp, w_out, post_g)


def kernel(x, mem, positions, a_pre_g, a_w_in, a_q_a_g, a_w_uq, a_kv_a_g, a_w_ukv, a_mem_g, a_w_mem_kv, a_w_out, a_post_g, b_pre_g, b_w_in, b_gate_bias, b_conv_w, b_conv_b, b_w_q, b_w_k, b_w_v, b_head_g, b_skip, b_mem_g, b_w_mem_kv, b_w_out, b_post_g):
    depth = a_pre_g.shape[0] + b_pre_g.shape[0]
    cs = _rope_tables(positions)
    for i in range(depth):
        j = i // 2
        if i % 2 == 0:
            x = _mla_layer(x, mem, cs, a_pre_g[j], a_w_in[j], a_q_a_g[j], a_w_uq[j], a_kv_a_g[j],
                           a_w_ukv[j], a_mem_g[j], a_w_mem_kv[j], a_w_out[j], a_post_g[j])
        else:
            x = _mlstm_layer(x, mem, b_pre_g[j], b_w_in[j], b_gate_bias[j], b_conv_w[j], b_conv_b[j],
                             b_w_q[j], b_w_k[j], b_w_v[j], b_head_g[j], b_skip[j], b_mem_g[j],
                             b_w_mem_kv[j], b_w_out[j], b_post_g[j])
    return x
```

```python
import functools

import jax
import jax.numpy as jnp
import numpy as np
from jax import lax
from jax.experimental import pallas as pl
from jax.experimental.pallas import tpu as pltpu

EPS = 1e-6
ROPE_THETA = 10000.0
MEM_HEADS = 4
MEM_HEAD_DIM = 64
MEM_WIDTH = MEM_HEADS * MEM_HEAD_DIM
QK_NOPE_DIM = 128
QK_ROPE_DIM = 64
V_HEAD_DIM = 128
MLA_HEADS = 6
V_EXT_DIM = V_HEAD_DIM + 16
Q_LORA_RANK = 384
KV_LORA_RANK = 256
MLSTM_HEADS = 4
MLSTM_V_DIM = 192
MLSTM_QK_DIM = 96
CONV_WIDTH = 4
MIX_WIDTH = 768

LANES = 128
MXU_DIM = 256
VMEM_LIMIT_BYTES = 56 * 1024 * 1024

ROW_TILE = 512
TAIL_ROW_TILE = 1024
TAIL_SUB_ROWS = 128
ATTN_TQ = 1024
ATTN_CW = 256
ATTN_TK = 1024
ATTN_HEADS_PER_STEP = 2
ATTN_LOOKAHEAD = 5
MLSTM_CHUNK = 256
MLSTM_SEGMENT = 1024
ROPE_ROWS = 1024

LOG2E = 1.4426950408889634
MLSTM_PAD_QK = LANES
MLSTM_PAD_V = MXU_DIM
MLSTM_HEAD_START = tuple((h * MLSTM_V_DIM // LANES) * LANES for h in range(MLSTM_HEADS))

F32 = jnp.float32
BF16 = jnp.bfloat16


def _dot(a, b):
    return jnp.dot(a, b, preferred_element_type=F32)


def _dot_nt(a, b):
    return lax.dot_general(a, b, (((1,), (1,)), ((), ())), preferred_element_type=F32)


def _rms(x, g, width=None):
    width = x.shape[-1] if width is None else width
    ms = jnp.sum(x * x, axis=-1, keepdims=True) * (1.0 / width)
    return x * lax.rsqrt(ms + EPS) * g


def _split3(x):
    b1 = x.astype(BF16)
    r1 = x - b1.astype(F32)
    b2 = r1.astype(BF16)
    b3 = (r1 - b2.astype(F32)).astype(BF16)
    return b1, b2, b3


def _params(*semantics):
    return pltpu.CompilerParams(dimension_semantics=semantics, vmem_limit_bytes=VMEM_LIMIT_BYTES)


def _const_spec(shape):
    zeros = (0,) * len(shape)
    return pl.BlockSpec(shape, lambda *_: zeros)


def _rope_kernel(pos_ref, invf_ref, sel_ref, cs_ref):
    ang = pos_ref[...].astype(F32) * invf_ref[...]
    trig = jnp.concatenate([jnp.cos(ang), jnp.sin(ang)], axis=1)
    spread = sum(_dot(term, sel_ref[...]) for term in _split3(trig))
    rb = trig.shape[0]
    per_row = spread.shape[1] // LANES
    for t in range(per_row):
        cs_ref[pl.ds(t, rb, stride=per_row), :] = spread[:, t * LANES:(t + 1) * LANES]


def _rope_tables(positions):
    B, S = positions.shape
    T = B * S
    half = QK_ROPE_DIM // 2
    per_row = LANES // half
    rows = T // per_row
    inv_freq = ROPE_THETA ** (-jnp.arange(0, QK_ROPE_DIM, 2, dtype=F32) / QK_ROPE_DIM)
    pos4 = jnp.repeat(positions.reshape(rows, per_row), half, axis=1)
    invf = jnp.tile(inv_freq, per_row).reshape(1, LANES)
    sel = np.zeros((2 * LANES, per_row * LANES), np.float32)
    for t in range(per_row):
        for f in range(half):
            sel[t * half + f, t * LANES + f] = 1.0
            sel[t * half + f, t * LANES + half + f] = 1.0
            sel[LANES + t * half + f, t * LANES + 2 * half + f] = -1.0
            sel[LANES + t * half + f, t * LANES + 3 * half + f] = 1.0
    rb = min(ROPE_ROWS, rows)
    cs = pl.pallas_call(
        _rope_kernel,
        grid=(rows // rb,),
        in_specs=[pl.BlockSpec((rb, LANES), lambda i: (i, 0)), _const_spec((1, LANES)),
                  _const_spec(sel.shape)],
        out_specs=pl.BlockSpec((rb * per_row, LANES), lambda i: (i, 0)),
        out_shape=jax.ShapeDtypeStruct((T, LANES), F32),
        compiler_params=_params("parallel"),
        name="rope_tables",
    )(pos4, invf, jnp.asarray(sel, BF16))
    return cs.reshape(B, S, LANES)


def _mem_kv_kernel(mem_ref, g_ref, w_ref, k_ref, v_ref):
    n_mem = mem_ref.shape[1]
    hn = _rms(mem_ref[0], g_ref[...]).astype(BF16)
    kv = _dot(hn, w_ref[...])
    k = kv[:, :MEM_WIDTH] * (MEM_HEAD_DIM ** -0.5 * LOG2E)
    v = kv[:, MEM_WIDTH:]
    col_head = lax.broadcasted_iota(jnp.int32, (n_mem, MEM_WIDTH), 1) // MEM_HEAD_DIM
    for h in range(MEM_HEADS):
        rows = pl.ds(h * n_mem, n_mem)
        k_ref[0, rows, :] = jnp.where(col_head == h, k, 0.0).astype(BF16)
        v_ref[0, rows, :] = jnp.where(col_head == h, v, 0.0).astype(BF16)


def _mem_kv(mem, mem_g, w_mem_kv):
    B, n_mem, D = mem.shape
    out = jax.ShapeDtypeStruct((B, MEM_HEADS * n_mem, MEM_WIDTH), BF16)
    spec = pl.BlockSpec((1, MEM_HEADS * n_mem, MEM_WIDTH), lambda b: (b, 0, 0))
    return pl.pallas_call(
        _mem_kv_kernel,
        grid=(B,),
        in_specs=[pl.BlockSpec((1, n_mem, D), lambda b: (b, 0, 0)), _const_spec((1, D)),
                  _const_spec((D, 2 * MEM_WIDTH))],
        out_specs=[spec, spec],
        out_shape=[out, out],
        compiler_params=_params("parallel"),
        name="mem_kv",
    )(mem, mem_g.reshape(1, D), w_mem_kv.astype(BF16))


A_CQ = (0, Q_LORA_RANK)
A_CKV = (A_CQ[1], A_CQ[1] + KV_LORA_RANK)
A_QMEM = (A_CKV[1], A_CKV[1] + MEM_WIDTH)
A_GATE = (A_QMEM[1], A_QMEM[1] + 1024)
A_KROPE = (A_GATE[1], A_GATE[1] + 2 * QK_ROPE_DIM)
Q_ROPE_OFF = MLA_HEADS * QK_NOPE_DIM


def _mla_in_kernel(x_ref, cs_ref, pre_g_ref, w_in_ref, qa_g_ref, w_uq_ref, kva_g_ref, w_ukv_ref,
                   q_ref, k_ref, v_ref, qmem_ref, gate_ref):
    q_scale = (QK_NOPE_DIM + QK_ROPE_DIM) ** -0.5 * LOG2E
    h = _rms(x_ref[0], pre_g_ref[...]).astype(BF16)
    p = _dot(h, w_in_ref[...])
    cs = cs_ref[0]
    qmem_ref[0] = p[:, A_QMEM[0]:A_QMEM[1]].astype(BF16)
    gate = p[:, A_GATE[0]:A_GATE[1]]
    gate_ref[0] = (gate * jax.nn.sigmoid(gate)).astype(BF16)

    c_q = _rms(p[:, A_CQ[0]:A_CQ[1]], qa_g_ref[...]).astype(BF16)
    q = _dot(c_q, w_uq_ref[...])
    c_kv = _rms(p[:, A_CKV[0]:A_CKV[1]], kva_g_ref[...]).astype(BF16)
    kv = _dot(c_kv, w_ukv_ref[...])

    kr = p[:, A_KROPE[0]:A_KROPE[1]] * cs
    k_rot = (kr + pltpu.roll(kr, QK_ROPE_DIM, 1)).astype(BF16)
    tm = cs.shape[0]
    pad_row = lax.broadcasted_iota(jnp.int32, (V_EXT_DIM - V_HEAD_DIM, tm), 0)
    ones_row = jnp.where(pad_row == 0, 1.0, 0.0).astype(BF16)
    for hd in range(MLA_HEADS):
        nope = slice(hd * QK_NOPE_DIM, (hd + 1) * QK_NOPE_DIM)
        rope = slice(Q_ROPE_OFF + hd * LANES, Q_ROPE_OFF + (hd + 1) * LANES)
        q_ref[0, hd, 0:LANES, :] = (q[:, nope] * q_scale).T.astype(BF16)
        q_ref[0, hd, LANES:2 * LANES, :] = (q[:, rope] * cs * q_scale).T.astype(BF16)
        k_ref[0, hd, :, 0:LANES] = kv[:, 2 * hd * LANES:(2 * hd + 1) * LANES].astype(BF16)
        k_ref[0, hd, :, LANES:2 * LANES] = k_rot
        v_ref[0, hd, 0:V_HEAD_DIM, :] = kv[:, (2 * hd + 1) * LANES:(2 * hd + 2) * LANES].T.astype(BF16)
        v_ref[0, hd, V_HEAD_DIM:V_EXT_DIM, :] = ones_row


def _rope_cols(w, start):
    half = QK_ROPE_DIM // 2
    return [w[:, start:start + QK_ROPE_DIM], w[:, start + half:start + QK_ROPE_DIM], w[:, start:start + half]]


def _mla_in(x, cs, pre_g, w_in, q_a_g, w_uq, kv_a_g, w_ukv):
    B, S, D = x.shape
    tm = min(ROW_TILE, S)
    o_kr = Q_LORA_RANK + KV_LORA_RANK
    o_qm = o_kr + QK_ROPE_DIM
    w_in_p = jnp.concatenate([w_in[:, :o_kr], w_in[:, o_qm:]] + _rope_cols(w_in, o_kr), axis=1).astype(BF16)
    head_w = QK_NOPE_DIM + QK_ROPE_DIM
    uq_cols = [w_uq[:, h * head_w:h * head_w + QK_NOPE_DIM] for h in range(MLA_HEADS)]
    for h in range(MLA_HEADS):
        uq_cols += _rope_cols(w_uq, h * head_w + QK_NOPE_DIM)
    w_uq_p = jnp.concatenate(uq_cols, axis=1).astype(BF16)
    w_ukv_p = w_ukv.astype(BF16)

    row = lambda b, i: (b, i, 0)
    head_row = lambda b, i: (b, 0, i, 0)
    head_col = lambda b, i: (b, 0, 0, i)
    out_shape = [
        jax.ShapeDtypeStruct((B, MLA_HEADS, 2 * LANES, S), BF16),
        jax.ShapeDtypeStruct((B, MLA_HEADS, S, 2 * LANES), BF16),
        jax.ShapeDtypeStruct((B, MLA_HEADS, V_EXT_DIM, S), BF16),
        jax.ShapeDtypeStruct((B, S, MEM_WIDTH), BF16),
        jax.ShapeDtypeStruct((B, S, 1024), BF16),
    ]
    out_specs = [
        pl.BlockSpec((1, MLA_HEADS, 2 * LANES, tm), head_col),
        pl.BlockSpec((1, MLA_HEADS, tm, 2 * LANES), head_row),
        pl.BlockSpec((1, MLA_HEADS, V_EXT_DIM, tm), head_col),
        pl.BlockSpec((1, tm, MEM_WIDTH), row),
        pl.BlockSpec((1, tm, 1024), row),
    ]
    return pl.pallas_call(
        _mla_in_kernel,
        grid=(B, S // tm),
        in_specs=[pl.BlockSpec((1, tm, D), row), pl.BlockSpec((1, tm, LANES), row),
                  _const_spec((1, D)), _const_spec(w_in_p.shape),
                  _const_spec((1, Q_LORA_RANK)), _const_spec(w_uq_p.shape),
                  _const_spec((1, KV_LORA_RANK)), _const_spec(w_ukv_p.shape)],
        out_specs=out_specs,
        out_shape=out_shape,
        compiler_params=_params("parallel", "parallel"),
        name="mla_in",
    )(x, cs, pre_g.reshape(1, D), w_in_p, q_a_g.reshape(1, -1), w_uq_p, kv_a_g.reshape(1, -1), w_ukv_p)


def _attn_kernel(qt_ref, k_ref, vt_ref, o_ref, *, tq, tk, cw):
    heads = k_ref.shape[1]
    S = k_ref.shape[2]
    dve = vt_ref.shape[2]
    dv = V_HEAD_DIM
    nc = tq // cw
    causal = (lax.broadcasted_iota(jnp.int32, (cw, cw), 0) <= lax.broadcasted_iota(jnp.int32, (cw, cw), 1))

    def q_tile(j, _):
        q0 = pl.multiple_of(j * tq, tq)
        qts = [[qt_ref[0, h, :, pl.ds(q0 + c * cw, cw)] for c in range(nc)] for h in range(heads)]

        def softmax_pv(carry, s, vt_tile):
            m, acc = carry
            m_new = jnp.maximum(m, jnp.max(s, axis=0, keepdims=True))
            p = jnp.exp2(s - m_new).astype(BF16)
            return m_new, jnp.exp2(m - m_new) * acc + _dot(vt_tile, p)

        def body(ki, carries):
            rows = pl.ds(pl.multiple_of(ki * tk, tk), tk)
            k_tiles = [k_ref[0, h, rows, :] for h in range(heads)]
            vt_tiles = [vt_ref[0, h, :, rows] for h in range(heads)]
            chains = [(h, c) for c in range(nc) for h in range(heads)]
            scores, out = {}, {}
            for i in range(len(chains) + ATTN_LOOKAHEAD):
                if i < len(chains):
                    h, c = chains[i]
                    scores[h, c] = _dot(k_tiles[h], qts[h][c])
                if i >= ATTN_LOOKAHEAD:
                    h, c = chains[i - ATTN_LOOKAHEAD]
                    out[h, c] = softmax_pv(carries[h][c], scores.pop((h, c)), vt_tiles[h])
            return tuple(tuple(out[h, c] for c in range(nc)) for h in range(heads))

        init = tuple(tuple((jnp.full((1, cw), -jnp.inf, F32), jnp.zeros((dve, cw), F32)) for _ in range(nc))
                     for _ in range(heads))
        carries = lax.fori_loop(0, j * (tq // tk), body, init)

        chains = [(h, c) for h in range(heads) for c in range(nc)]
        scores = {}
        for h, c in chains:
            s = _dot(k_ref[0, h, pl.ds(q0, (c + 1) * cw), :], qts[h][c])
            s_diag = jnp.where(causal, s[c * cw:, :], -jnp.inf)
            scores[h, c] = jnp.concatenate([s[:c * cw, :], s_diag], axis=0) if c > 0 else s_diag
        for h, c in chains:
            _, acc = softmax_pv(carries[h][c], scores[h, c], vt_ref[0, h, :, pl.ds(q0, (c + 1) * cw)])
            out = acc[0:dv, :] / acc[dv:dv + 1, :]
            o_ref[0, pl.ds(q0 + c * cw, cw), h * dv:(h + 1) * dv] = out.T.astype(o_ref.dtype)
        return 0

    lax.fori_loop(0, S // tq, q_tile, 0)


def _attention(qt, k, vt):
    B, H, S, dqk = k.shape
    dve = vt.shape[2]
    tq = min(ATTN_TQ, S)
    tk = min(ATTN_TK, tq)
    cw = min(ATTN_CW, tq)
    hp = ATTN_HEADS_PER_STEP
    head = lambda b, h: (b, h, 0, 0)
    return pl.pallas_call(
        functools.partial(_attn_kernel, tq=tq, tk=tk, cw=cw),
        grid=(B, H // hp),
        in_specs=[pl.BlockSpec((1, hp, dqk, S), head), pl.BlockSpec((1, hp, S, dqk), head),
                  pl.BlockSpec((1, hp, dve, S), head)],
        out_specs=pl.BlockSpec((1, S, hp * V_HEAD_DIM), lambda b, h: (b, 0, h)),
        out_shape=jax.ShapeDtypeStruct((B, S, H * V_HEAD_DIM), BF16),
        compiler_params=_params("parallel", "parallel"),
        name="mla_attention",
    )(qt, k, vt)


def _memory_probs(s, n_mem):
    probs = []
    for h in range(MEM_HEADS):
        sh = s[:, h * n_mem:(h + 1) * n_mem]
        e = jnp.exp2(sh - jnp.max(sh, axis=-1, keepdims=True))
        probs.append((e / jnp.sum(e, axis=-1, keepdims=True)).astype(BF16))
    return jnp.concatenate(probs, axis=-1)


def _tail(mix_rows, x_ref, qmem_ref, gate_ref, kexp_ref, vexp_ref, w_out_ref, post_g_ref, o_ref):
    tm = x_ref.shape[1]
    sub = min(TAIL_SUB_ROWS, tm)
    blocks = [slice(i * sub, (i + 1) * sub) for i in range(tm // sub)]
    n_mem = kexp_ref.shape[1] // MEM_HEADS
    kexp = kexp_ref[0]
    scores = [_dot_nt(qmem_ref[0, r, :], kexp) for r in blocks]
    for r, s in zip(blocks, scores):
        mo = _dot(_memory_probs(s, n_mem), vexp_ref[0])
        gate = gate_ref[0, r, :].astype(F32)
        y_mix = (mix_rows(r) * gate[:, :MIX_WIDTH]).astype(BF16)
        y_mem = (mo * gate[:, MIX_WIDTH:]).astype(BF16)
        y = _dot(y_mix, w_out_ref[0:MIX_WIDTH, :]) + _dot(y_mem, w_out_ref[MIX_WIDTH:, :])
        o_ref[0, r, :] = x_ref[0, r, :] + _rms(y, post_g_ref[...])


def _mla_out_kernel(x_ref, mix_ref, qmem_ref, gate_ref, kexp_ref, vexp_ref, w_out_ref, post_g_ref, o_ref):
    mix_rows = lambda r: mix_ref[0, r, :].astype(F32)
    _tail(mix_rows, x_ref, qmem_ref, gate_ref, kexp_ref, vexp_ref, w_out_ref, post_g_ref, o_ref)


def _mlstm_out_kernel(x_ref, hn_ref, og_ref, uc_ref, skip_ref, qmem_ref, gate_ref, kexp_ref, vexp_ref,
                      w_out_ref, post_g_ref, o_ref):
    def mix_rows(r):
        return (og_ref[0, r, :].astype(F32) * hn_ref[0, r, :].astype(F32)
                + skip_ref[...] * uc_ref[0, r, :].astype(F32))

    _tail(mix_rows, x_ref, qmem_ref, gate_ref, kexp_ref, vexp_ref, w_out_ref, post_g_ref, o_ref)


def _layer_out(kernel_fn, name, x, mixer_inputs, mixer_specs, qmem, gate, kexp, vexp, w_out, post_g):
    B, S, D = x.shape
    tm = min(TAIL_ROW_TILE, S)
    row = lambda b, i: (b, i, 0)
    per_batch = lambda b, i: (b, 0, 0)
    in_specs = ([pl.BlockSpec((1, tm, D), row)] + mixer_specs(tm) +
                [pl.BlockSpec((1, tm, MEM_WIDTH), row), pl.BlockSpec((1, tm, gate.shape[-1]), row),
                 pl.BlockSpec((1,) + kexp.shape[1:], per_batch), pl.BlockSpec((1,) + vexp.shape[1:], per_batch),
                 _const_spec(w_out.shape), _const_spec((1, D))])
    return pl.pallas_call(
        kernel_fn,
        grid=(B, S // tm),
        in_specs=in_specs,
        out_specs=pl.BlockSpec((1, tm, D), row),
        out_shape=jax.ShapeDtypeStruct((B, S, D), F32),
        compiler_params=_params("parallel", "parallel"),
        name=name,
    )(x, *mixer_inputs, qmem, gate, kexp, vexp, w_out.astype(BF16), post_g.reshape(1, D))


B_O = (0, MIX_WIDTH)
B_QMEM = (B_O[1], B_O[1] + MEM_WIDTH)
B_GATE = (B_QMEM[1], B_QMEM[1] + 1024)
B_IF = (B_GATE[1], B_GATE[1] + LANES)
CONV_HALO = 8


def _mlstm_front_kernel(x_ref, pre_g_ref, w_u_ref, w_rest_ref, conv_w_ref, conv_b_ref, bias_ref, wqk_ref,
                        wv_ref, og_ref, qmem_ref, gate_ref, uc_ref, qt_ref, k_ref, vt_ref, gc_ref, gr_ref,
                        tail_sc, *, chunk):
    tm = x_ref.shape[1]

    @pl.when(pl.program_id(1) == 0)
    def _():
        tail_sc[...] = jnp.zeros(tail_sc.shape, F32)

    h = _rms(x_ref[0], pre_g_ref[...]).astype(BF16)
    u = _dot(h, w_u_ref[...])
    p = _dot(h, w_rest_ref[...])

    ext = jnp.concatenate([tail_sc[...], u], axis=0)
    tail_sc[...] = u[tm - CONV_HALO:tm, :]
    conv = conv_b_ref[...] + u * conv_w_ref[CONV_WIDTH - 1:CONV_WIDTH, :]
    for back in range(1, CONV_WIDTH):
        tap = CONV_WIDTH - 1 - back
        conv = conv + ext[CONV_HALO - back:CONV_HALO - back + tm, :] * conv_w_ref[tap:tap + 1, :]
    uc = conv * jax.nn.sigmoid(conv)
    uc_bf = uc.astype(BF16)
    uc_ref[0] = uc_bf
    u_bf = u.astype(BF16)

    og_ref[0] = jax.nn.sigmoid(p[:, B_O[0]:B_O[1]]).astype(BF16)
    qmem_ref[0] = p[:, B_QMEM[0]:B_QMEM[1]].astype(BF16)
    gate = p[:, B_GATE[0]:B_GATE[1]]
    gate_ref[0] = (gate * jax.nn.sigmoid(gate)).astype(BF16)

    k_scale = MLSTM_QK_DIM ** -0.5
    vrow = lax.broadcasted_iota(jnp.int32, (MLSTM_PAD_V, tm), 0)
    for hd in range(MLSTM_HEADS):
        cols = slice(MLSTM_HEAD_START[hd], MLSTM_HEAD_START[hd] + MXU_DIM)
        qk = _dot(uc_bf[:, cols], wqk_ref[hd])
        qt_ref[0, hd] = qk[:, :MLSTM_PAD_QK].T.astype(BF16)
        k_ref[0, hd] = (qk[:, MLSTM_PAD_QK:] * k_scale).astype(BF16)
        vt = _dot(u_bf[:, cols], wv_ref[hd]).T
        vt_ref[0, hd] = jnp.where(vrow == MLSTM_V_DIM, 1.0, vt).astype(BF16)

    g = p[:, B_IF[0]:B_IF[1]] + bias_ref[...]
    lane = lax.broadcasted_iota(jnp.int32, g.shape, 1)
    log_f = jnp.minimum(g, 0.0) - jnp.log1p(jnp.exp(-jnp.abs(g)))
    is_f = (lane >= MLSTM_HEADS) & (lane < 2 * MLSTM_HEADS)
    gates = jnp.where(lane < MLSTM_HEADS, g, jnp.where(is_f, log_f, 0.0))
    r = lax.broadcasted_iota(jnp.int32, (chunk, chunk), 0)
    c = lax.broadcasted_iota(jnp.int32, (chunk, chunk), 1)
    tril = (c <= r).astype(BF16)
    chunk_lane = lax.broadcasted_iota(jnp.int32, (chunk, LANES), 1)
    chunk_is_f = (chunk_lane >= MLSTM_HEADS) & (chunk_lane < 2 * MLSTM_HEADS)
    parts = []
    for j in range(tm // chunk):
        gj = gates[j * chunk:(j + 1) * chunk, :]
        cum = sum(_dot(tril, term) for term in _split3(gj))
        parts.append(jnp.where(chunk_is_f, cum, gj))
    gc = jnp.concatenate(parts, axis=0) if len(parts) > 1 else parts[0]
    gr_ref[0] = gc.T[0:2 * MLSTM_HEADS, :]
    gc_ref[0] = gc - pltpu.roll(gc, LANES - MLSTM_HEADS, 1)


def _mlstm_front(x, pre_g, w_in, gate_bias, conv_w, conv_b, w_q, w_k, w_v):
    B, S, D = x.shape
    tm = min(ROW_TILE, S)
    chunk = min(MLSTM_CHUNK, S)
    H = MLSTM_HEADS
    n_if = 2 * H
    o_if = MIX_WIDTH
    o_o = o_if + n_if
    w_u = w_in[:, :o_if].astype(BF16)
    w_rest = jnp.concatenate([w_in[:, o_o:], w_in[:, o_if:o_o], jnp.zeros((D, LANES - n_if), w_in.dtype)],
                             axis=1).astype(BF16)
    wqk = jnp.zeros((H, MXU_DIM, 2 * MLSTM_PAD_QK), F32)
    wv = jnp.zeros((H, MXU_DIM, MLSTM_PAD_V), F32)
    for h in range(H):
        off = h * MLSTM_V_DIM - MLSTM_HEAD_START[h]
        wqk = wqk.at[h, off:off + MLSTM_V_DIM, 0:MLSTM_QK_DIM].set(w_q[h])
        wqk = wqk.at[h, off:off + MLSTM_V_DIM, MLSTM_PAD_QK:MLSTM_PAD_QK + MLSTM_QK_DIM].set(w_k[h])
        wv = wv.at[h, off:off + MLSTM_V_DIM, 0:MLSTM_V_DIM].set(w_v[h])
    bias = jnp.pad(gate_bias, (0, LANES - n_if)).reshape(1, LANES)
    row = lambda b, i: (b, i, 0)
    head_row = lambda b, i: (b, 0, i, 0)
    head_col = lambda b, i: (b, 0, 0, i)
    out_shape = [
        jax.ShapeDtypeStruct((B, S, MIX_WIDTH), BF16),
        jax.ShapeDtypeStruct((B, S, MEM_WIDTH), BF16),
        jax.ShapeDtypeStruct((B, S, 1024), BF16),
        jax.ShapeDtypeStruct((B, S, MIX_WIDTH), BF16),
        jax.ShapeDtypeStruct((B, H, MLSTM_PAD_QK, S), BF16),
        jax.ShapeDtypeStruct((B, H, S, MLSTM_PAD_QK), BF16),
        jax.ShapeDtypeStruct((B, H, MLSTM_PAD_V, S), BF16),
        jax.ShapeDtypeStruct((B, S, LANES), F32),
        jax.ShapeDtypeStruct((B, 2 * H, S), F32),
    ]
    out_specs = [
        pl.BlockSpec((1, tm, MIX_WIDTH), row),
        pl.BlockSpec((1, tm, MEM_WIDTH), row),
        pl.BlockSpec((1, tm, 1024), row),
        pl.BlockSpec((1, tm, MIX_WIDTH), row),
        pl.BlockSpec((1, H, MLSTM_PAD_QK, tm), head_col),
        pl.BlockSpec((1, H, tm, MLSTM_PAD_QK), head_row),
        pl.BlockSpec((1, H, MLSTM_PAD_V, tm), head_col),
        pl.BlockSpec((1, tm, LANES), row),
        pl.BlockSpec((1, 2 * H, tm), lambda b, i: (b, 0, i)),
    ]
    return pl.pallas_call(
        functools.partial(_mlstm_front_kernel, chunk=chunk),
        grid=(B, S // tm),
        in_specs=[pl.BlockSpec((1, tm, D), row), _const_spec((1, D)), _const_spec(w_u.shape),
                  _const_spec(w_rest.shape), _const_spec((CONV_WIDTH, MIX_WIDTH)), _const_spec((1, MIX_WIDTH)),
                  _const_spec((1, LANES)), _const_spec(wqk.shape), _const_spec(wv.shape)],
        out_specs=out_specs,
        out_shape=out_shape,
        scratch_shapes=[pltpu.VMEM((CONV_HALO, MIX_WIDTH), F32)],
        compiler_params=_params("parallel", "arbitrary"),
        name="mlstm_front",
    )(x, pre_g.reshape(1, D), w_u, w_rest, conv_w, conv_b.reshape(1, -1), bias, wqk.astype(BF16),
      wv.astype(BF16))


def _mlstm_kernel(qt_ref, k_ref, vt_ref, gc_ref, gr_ref, hg_ref, o_ref, c_sc, m_sc, *, chunk):
    H = k_ref.shape[1]
    seg = k_ref.shape[2]
    L = chunk

    @pl.when(pl.program_id(1) == 0)
    def _():
        c_sc[...] = jnp.zeros(c_sc.shape, F32)
        m_sc[...] = jnp.zeros(m_sc.shape, F32)

    causal = lax.broadcasted_iota(jnp.int32, (L, L), 0) <= lax.broadcasted_iota(jnp.int32, (L, L), 1)
    vrow = lax.broadcasted_iota(jnp.int32, (MLSTM_PAD_V, L), 0)
    gains = [jnp.broadcast_to(hg_ref[h], (MLSTM_PAD_V, L)) for h in range(H)]

    def step(c, carry):
        rows = pl.ds(pl.multiple_of(c * L, L), L)
        qts = [qt_ref[0, h, :, rows] for h in range(H)]
        ks = [k_ref[0, h, rows, :] for h in range(H)]
        vts = [vt_ref[0, h, :, rows] for h in range(H)]
        qk = [_dot(ks[h], qts[h]) for h in range(H)]
        cq = [_dot(carry[h][0].astype(BF16), qts[h]) for h in range(H)]
        out, normed = [], []
        for h in range(H):
            C, m = carry[h]
            c_col = gc_ref[0, rows, h:h + 1]
            li_row = gr_ref[0, h:h + 1, rows]
            b_row = gr_ref[0, MLSTM_HEADS + h:MLSTM_HEADS + h + 1, rows]
            d = jnp.where(causal, c_col + b_row, -jnp.inf)
            inter = b_row + m
            m_t = jnp.maximum(inter, jnp.max(d, axis=0, keepdims=True))
            sqk = (qk[h] * jnp.exp(d - m_t)).astype(BF16)
            num = jnp.exp(inter - m_t) * cq[h] + _dot(vts[h], sqk)
            den = jnp.maximum(jnp.abs(num[MLSTM_V_DIM:MLSTM_V_DIM + 1, :]), jnp.exp(-m_t))
            hv = jnp.where(vrow < MLSTM_V_DIM, num / den, 0.0)
            ms = jnp.sum(hv * hv, axis=0, keepdims=True) * (1.0 / MLSTM_V_DIM)
            normed.append((hv * lax.rsqrt(ms + EPS) * gains[h])[0:MLSTM_V_DIM, :])

            b_last = b_row[:, L - 1:L]
            dec = b_last - b_row + li_row
            m_new = jnp.maximum(b_last + m, jnp.max(dec, axis=1, keepdims=True))
            vw = (vts[h].astype(F32) * jnp.exp(dec - m_new)).astype(BF16)
            out.append((jnp.exp(b_last + m - m_new) * C + _dot(vw, ks[h]), m_new))
        o_ref[0, rows, :] = jnp.concatenate(normed, axis=0).T.astype(o_ref.dtype)
        return tuple(out)

    init = tuple((c_sc[h], m_sc[h, 0:1, 0:1]) for h in range(H))
    final = lax.fori_loop(0, seg // L, step, init)
    for h in range(H):
        c_sc[h] = final[h][0]
        m_sc[h] = jnp.broadcast_to(final[h][1], m_sc.shape[1:])


def _mlstm(qt, k, vt, gc, gr, head_g):
    B, H, S, _ = k.shape
    chunk = min(MLSTM_CHUNK, S)
    seg = min(MLSTM_SEGMENT, S)
    hg = jnp.pad(head_g, ((0, 0), (0, MLSTM_PAD_V - MLSTM_V_DIM))).reshape(H, MLSTM_PAD_V, 1)
    head_row = lambda b, i: (b, 0, i, 0)
    head_col = lambda b, i: (b, 0, 0, i)
    return pl.pallas_call(
        functools.partial(_mlstm_kernel, chunk=chunk),
        grid=(B, S // seg),
        in_specs=[pl.BlockSpec((1, H, MLSTM_PAD_QK, seg), head_col),
                  pl.BlockSpec((1, H, seg, MLSTM_PAD_QK), head_row),
                  pl.BlockSpec((1, H, MLSTM_PAD_V, seg), head_col),
                  pl.BlockSpec((1, seg, LANES), lambda b, i: (b, i, 0)),
                  pl.BlockSpec((1, 2 * H, seg), lambda b, i: (b, 0, i)),
                  _const_spec(hg.shape)],
        out_specs=pl.BlockSpec((1, seg, H * MLSTM_V_DIM), lambda b, i: (b, i, 0)),
        out_shape=jax.ShapeDtypeStruct((B, S, H * MLSTM_V_DIM), BF16),
        scratch_shapes=[pltpu.VMEM((H, MLSTM_PAD_V, MLSTM_PAD_QK), F32), pltpu.VMEM((H, 8, LANES), F32)],
        compiler_params=_params("parallel", "arbitrary"),
        name="mlstm_scan",
    )(qt, k, vt, gc, gr, hg)


def _mla_layer(x, mem, cs, pre_g, w_in, q_a_g, w_uq, kv_a_g, w_ukv, mem_g, w_mem_kv, w_out, post_g):
    q, k, v, qmem, gate = _mla_in(x, cs, pre_g, w_in, q_a_g, w_uq, kv_a_g, w_ukv)
    mix = _attention(q, k, v)
    kexp, vexp = _mem_kv(mem, mem_g, w_mem_kv)
    specs = lambda tm: [pl.BlockSpec((1, tm, MIX_WIDTH), lambda b, i: (b, i, 0))]
    return _layer_out(_mla_out_kernel, "mla_out", x, [mix], specs, qmem, gate, kexp, vexp, w_out, post_g)


def _mlstm_layer(x, mem, pre_g, w_in, gate_bias, conv_w, conv_b, w_q, w_k, w_v, head_g, skip,
                 mem_g, w_mem_kv, w_out, post_g):
    og, qmem, gate, uc, qt, k, vt, gc, gr = _mlstm_front(x, pre_g, w_in, gate_bias, conv_w, conv_b,
                                                         w_q, w_k, w_v)
    hn = _mlstm(qt, k, vt, gc, gr, head_g)
    kexp, vexp = _mem_kv(mem, mem_g, w_mem_kv)
    row = lambda b, i: (b, i, 0)
    specs = lambda tm: [pl.BlockSpec((1, tm, MIX_WIDTH), row),
                        pl.BlockSpec((1, tm, MIX_WIDTH), row), pl.BlockSpec((1, tm, MIX_WIDTH), row),
                        _const_spec((1, MIX_WIDTH))]
    return _layer_out(_mlstm_out_kernel, "mlstm_out", x, [hn, og, uc, skip.reshape(1, -1)], specs,
                      qmem, gate, kexp, vexp, w_out, post_g)


def kernel(x, mem, positions, a_pre_g, a_w_in, a_q_a_g, a_w_uq, a_kv_a_g, a_w_ukv, a_mem_g, a_w_mem_kv, a_w_out, a_post_g, b_pre_g, b_w_in, b_gate_bias, b_conv_w, b_conv_b, b_w_q, b_w_k, b_w_v, b_head_g, b_skip, b_mem_g, b_w_mem_kv, b_w_out, b_post_g):
    depth = a_pre_g.shape[0] + b_pre_g.shape[0]
    cs = _rope_tables(positions)
    for i in range(depth):
        j = i // 2
        if i % 2 == 0:
            x = _mla_layer(x, mem, cs, a_pre_g[j], a_w_in[j], a_q_a_g[j], a_w_uq[j], a_kv_a_g[j],
                           a_w_ukv[j], a_mem_g[j], a_w_mem_kv[j], a_w_out[j], a_post_g[j])
        else:
            x = _mlstm_layer(x, mem, b_pre_g[j], b_w_in[j], b_gate_bias[j], b_conv_w[j], b_conv_b[j],
                             b_w_q[j], b_w_k[j], b_w_v[j], b_head_g[j], b_skip[j], b_mem_g[j],
                             b_w_mem_kv[j], b_w_out[j], b_post_g[j])
    return x
```

```python
import functools

import jax
import jax.numpy as jnp
import numpy as np
from jax import lax
from jax.experimental import pallas as pl
from jax.experimental.pallas import tpu as pltpu

EPS = 1e-6
ROPE_THETA = 10000.0
MEM_HEADS = 4
MEM_HEAD_DIM = 64
MEM_WIDTH = MEM_HEADS * MEM_HEAD_DIM
QK_NOPE_DIM = 128
QK_ROPE_DIM = 64
V_HEAD_DIM = 128
MLA_HEADS = 6
V_EXT_DIM = V_HEAD_DIM + 16
Q_LORA_RANK = 384
KV_LORA_RANK = 256
MLSTM_HEADS = 4
MLSTM_V_DIM = 192
MLSTM_QK_DIM = 96
CONV_WIDTH = 4
MIX_WIDTH = 768

LANES = 128
MXU_DIM = 256
VMEM_LIMIT_BYTES = 56 * 1024 * 1024

ROW_TILE = 512
TAIL_ROW_TILE = 1024
TAIL_SUB_ROWS = 256
ATTN_TQ = 1024
ATTN_CW = 256
ATTN_TK = 1024
ATTN_HEADS_PER_STEP = 2
ATTN_LOOKAHEAD = 5
MLSTM_CHUNK = 256
MLSTM_SEGMENT = 1024

LOG2E = 1.4426950408889634
MLSTM_PAD_QK = LANES
MLSTM_PAD_V = MXU_DIM
MLSTM_HEAD_START = tuple((h * MLSTM_V_DIM // LANES) * LANES for h in range(MLSTM_HEADS))

F32 = jnp.float32
BF16 = jnp.bfloat16


def _dot(a, b):
    return jnp.dot(a, b, preferred_element_type=F32)


def _dot_nt(a, b):
    return lax.dot_general(a, b, (((1,), (1,)), ((), ())), preferred_element_type=F32)


def _rms(x, g, width=None):
    width = x.shape[-1] if width is None else width
    ms = jnp.sum(x * x, axis=-1, keepdims=True) * (1.0 / width)
    return x * lax.rsqrt(ms + EPS) * g


def _split3(x):
    b1 = x.astype(BF16)
    r1 = x - b1.astype(F32)
    b2 = r1.astype(BF16)
    b3 = (r1 - b2.astype(F32)).astype(BF16)
    return b1, b2, b3


def _params(*semantics):
    return pltpu.CompilerParams(dimension_semantics=semantics, vmem_limit_bytes=VMEM_LIMIT_BYTES)


def _const_spec(shape):
    zeros = (0,) * len(shape)
    return pl.BlockSpec(shape, lambda *_: zeros)


ROPE_HALF = QK_ROPE_DIM // 2
ROPE_PER_ROW = LANES // ROPE_HALF


def _rope_operands(positions):
    B, S = positions.shape
    rows = B * S // ROPE_PER_ROW
    inv_freq = ROPE_THETA ** (-jnp.arange(0, QK_ROPE_DIM, 2, dtype=F32) / QK_ROPE_DIM)
    pos4 = jnp.repeat(positions.reshape(rows, ROPE_PER_ROW), ROPE_HALF, axis=1)
    invf = jnp.tile(inv_freq, ROPE_PER_ROW).reshape(1, LANES)
    sel = np.zeros((2 * LANES, ROPE_PER_ROW * LANES), np.float32)
    for t in range(ROPE_PER_ROW):
        for f in range(ROPE_HALF):
            sel[t * ROPE_HALF + f, t * LANES + f] = 1.0
            sel[t * ROPE_HALF + f, t * LANES + ROPE_HALF + f] = 1.0
            sel[LANES + t * ROPE_HALF + f, t * LANES + 2 * ROPE_HALF + f] = -1.0
            sel[LANES + t * ROPE_HALF + f, t * LANES + 3 * ROPE_HALF + f] = 1.0
    return pos4, invf, jnp.asarray(sel, BF16)


def _rope_table(pos_ref, invf_ref, sel_ref, cs_sc):
    ang = pos_ref[...].astype(F32) * invf_ref[...]
    trig = jnp.concatenate([jnp.cos(ang), jnp.sin(ang)], axis=1)
    spread = sum(_dot(term, sel_ref[...]) for term in _split3(trig))
    rows = trig.shape[0]
    for t in range(ROPE_PER_ROW):
        cs_sc[pl.ds(t, rows, stride=ROPE_PER_ROW), :] = spread[:, t * LANES:(t + 1) * LANES]
    return cs_sc[...]


def _mem_kv_kernel(mem_ref, g_ref, w_ref, k_ref, v_ref):
    n_mem = mem_ref.shape[1]
    hn = _rms(mem_ref[0], g_ref[...]).astype(BF16)
    kv = _dot(hn, w_ref[...])
    k = kv[:, :MEM_WIDTH] * (MEM_HEAD_DIM ** -0.5 * LOG2E)
    v = kv[:, MEM_WIDTH:]
    col_head = lax.broadcasted_iota(jnp.int32, (n_mem, MEM_WIDTH), 1) // MEM_HEAD_DIM
    for h in range(MEM_HEADS):
        rows = pl.ds(h * n_mem, n_mem)
        k_ref[0, rows, :] = jnp.where(col_head == h, k, 0.0).astype(BF16)
        v_ref[0, rows, :] = jnp.where(col_head == h, v, 0.0).astype(BF16)


def _mem_kv(mem, mem_g, w_mem_kv):
    B, n_mem, D = mem.shape
    out = jax.ShapeDtypeStruct((B, MEM_HEADS * n_mem, MEM_WIDTH), BF16)
    spec = pl.BlockSpec((1, MEM_HEADS * n_mem, MEM_WIDTH), lambda b: (b, 0, 0))
    return pl.pallas_call(
        _mem_kv_kernel,
        grid=(B,),
        in_specs=[pl.BlockSpec((1, n_mem, D), lambda b: (b, 0, 0)), _const_spec((1, D)),
                  _const_spec((D, 2 * MEM_WIDTH))],
        out_specs=[spec, spec],
        out_shape=[out, out],
        compiler_params=_params("parallel"),
        name="mem_kv",
    )(mem, mem_g.reshape(1, D), w_mem_kv.astype(BF16))


A_CQ = (0, Q_LORA_RANK)
A_CKV = (A_CQ[1], A_CQ[1] + KV_LORA_RANK)
A_QMEM = (A_CKV[1], A_CKV[1] + MEM_WIDTH)
A_GATE = (A_QMEM[1], A_QMEM[1] + 1024)
A_KROPE = (A_GATE[1], A_GATE[1] + 2 * QK_ROPE_DIM)
Q_ROPE_OFF = MLA_HEADS * QK_NOPE_DIM


def _mla_in_kernel(x_ref, pos_ref, invf_ref, sel_ref, pre_g_ref, w_in_ref, qa_g_ref, w_uq_ref, kva_g_ref,
                   w_ukv_ref, q_ref, k_ref, v_ref, qmem_ref, gate_ref, cs_sc):
    q_scale = (QK_NOPE_DIM + QK_ROPE_DIM) ** -0.5 * LOG2E
    h = _rms(x_ref[0], pre_g_ref[...]).astype(BF16)
    p = _dot(h, w_in_ref[...])
    cs = _rope_table(pos_ref, invf_ref, sel_ref, cs_sc)
    qmem_ref[0] = p[:, A_QMEM[0]:A_QMEM[1]].astype(BF16)
    gate = p[:, A_GATE[0]:A_GATE[1]]
    gate_ref[0] = (gate * jax.nn.sigmoid(gate)).astype(BF16)

    c_q = _rms(p[:, A_CQ[0]:A_CQ[1]], qa_g_ref[...]).astype(BF16)
    q = _dot(c_q, w_uq_ref[...])
    c_kv = _rms(p[:, A_CKV[0]:A_CKV[1]], kva_g_ref[...]).astype(BF16)
    kv = _dot(c_kv, w_ukv_ref[...])

    kr = p[:, A_KROPE[0]:A_KROPE[1]] * cs
    k_rot = (kr + pltpu.roll(kr, QK_ROPE_DIM, 1)).astype(BF16)
    tm = cs.shape[0]
    pad_row = lax.broadcasted_iota(jnp.int32, (V_EXT_DIM - V_HEAD_DIM, tm), 0)
    ones_row = jnp.where(pad_row == 0, 1.0, 0.0).astype(BF16)
    for hd in range(MLA_HEADS):
        nope = slice(hd * QK_NOPE_DIM, (hd + 1) * QK_NOPE_DIM)
        rope = slice(Q_ROPE_OFF + hd * LANES, Q_ROPE_OFF + (hd + 1) * LANES)
        q_ref[0, hd, 0:LANES, :] = (q[:, nope] * q_scale).T.astype(BF16)
        q_ref[0, hd, LANES:2 * LANES, :] = (q[:, rope] * cs * q_scale).T.astype(BF16)
        k_ref[0, hd, :, 0:LANES] = kv[:, 2 * hd * LANES:(2 * hd + 1) * LANES].astype(BF16)
        k_ref[0, hd, :, LANES:2 * LANES] = k_rot
        v_ref[0, hd, 0:V_HEAD_DIM, :] = kv[:, (2 * hd + 1) * LANES:(2 * hd + 2) * LANES].T.astype(BF16)
        v_ref[0, hd, V_HEAD_DIM:V_EXT_DIM, :] = ones_row


def _rope_cols(w, start):
    half = QK_ROPE_DIM // 2
    return [w[:, start:start + QK_ROPE_DIM], w[:, start + half:start + QK_ROPE_DIM], w[:, start:start + half]]


def _mla_in(x, rope_operands, pre_g, w_in, q_a_g, w_uq, kv_a_g, w_ukv):
    B, S, D = x.shape
    tm = min(ROW_TILE, S)
    pos4, invf, sel = rope_operands
    pos_rows = tm // ROPE_PER_ROW
    tiles_per_seq = S // tm
    o_kr = Q_LORA_RANK + KV_LORA_RANK
    o_qm = o_kr + QK_ROPE_DIM
    w_in_p = jnp.concatenate([w_in[:, :o_kr], w_in[:, o_qm:]] + _rope_cols(w_in, o_kr), axis=1).astype(BF16)
    head_w = QK_NOPE_DIM + QK_ROPE_DIM
    uq_cols = [w_uq[:, h * head_w:h * head_w + QK_NOPE_DIM] for h in range(MLA_HEADS)]
    for h in range(MLA_HEADS):
        uq_cols += _rope_cols(w_uq, h * head_w + QK_NOPE_DIM)
    w_uq_p = jnp.concatenate(uq_cols, axis=1).astype(BF16)
    w_ukv_p = w_ukv.astype(BF16)

    row = lambda b, i: (b, i, 0)
    head_row = lambda b, i: (b, 0, i, 0)
    head_col = lambda b, i: (b, 0, 0, i)
    out_shape = [
        jax.ShapeDtypeStruct((B, MLA_HEADS, 2 * LANES, S), BF16),
        jax.ShapeDtypeStruct((B, MLA_HEADS, S, 2 * LANES), BF16),
        jax.ShapeDtypeStruct((B, MLA_HEADS, V_EXT_DIM, S), BF16),
        jax.ShapeDtypeStruct((B, S, MEM_WIDTH), BF16),
        jax.ShapeDtypeStruct((B, S, 1024), BF16),
    ]
    out_specs = [
        pl.BlockSpec((1, MLA_HEADS, 2 * LANES, tm), head_col),
        pl.BlockSpec((1, MLA_HEADS, tm, 2 * LANES), head_row),
        pl.BlockSpec((1, MLA_HEADS, V_EXT_DIM, tm), head_col),
        pl.BlockSpec((1, tm, MEM_WIDTH), row),
        pl.BlockSpec((1, tm, 1024), row),
    ]
    return pl.pallas_call(
        _mla_in_kernel,
        grid=(B, S // tm),
        in_specs=[pl.BlockSpec((1, tm, D), row),
                  pl.BlockSpec((pos_rows, LANES), lambda b, i: (b * tiles_per_seq + i, 0)),
                  _const_spec(invf.shape), _const_spec(sel.shape),
                  _const_spec((1, D)), _const_spec(w_in_p.shape),
                  _const_spec((1, Q_LORA_RANK)), _const_spec(w_uq_p.shape),
                  _const_spec((1, KV_LORA_RANK)), _const_spec(w_ukv_p.shape)],
        out_specs=out_specs,
        out_shape=out_shape,
        scratch_shapes=[pltpu.VMEM((tm, LANES), F32)],
        compiler_params=_params("parallel", "parallel"),
        name="mla_in",
    )(x, pos4, invf, sel, pre_g.reshape(1, D), w_in_p, q_a_g.reshape(1, -1), w_uq_p, kv_a_g.reshape(1, -1),
      w_ukv_p)


def _attn_kernel(qt_ref, k_ref, vt_ref, o_ref, *, tq, tk, cw):
    heads = k_ref.shape[1]
    S = k_ref.shape[2]
    dve = vt_ref.shape[2]
    dv = V_HEAD_DIM
    nc = tq // cw
    causal = (lax.broadcasted_iota(jnp.int32, (cw, cw), 0) <= lax.broadcasted_iota(jnp.int32, (cw, cw), 1))

    def q_tile(j, _):
        q0 = pl.multiple_of(j * tq, tq)
        qts = [[qt_ref[0, h, :, pl.ds(q0 + c * cw, cw)] for c in range(nc)] for h in range(heads)]

        def softmax_pv(carry, s, vt_tile):
            m, acc = carry
            m_new = jnp.maximum(m, jnp.max(s, axis=0, keepdims=True))
            p = jnp.exp2(s - m_new).astype(BF16)
            return m_new, jnp.exp2(m - m_new) * acc + _dot(vt_tile, p)

        def body(ki, carries):
            rows = pl.ds(pl.multiple_of(ki * tk, tk), tk)
            k_tiles = [k_ref[0, h, rows, :] for h in range(heads)]
            vt_tiles = [vt_ref[0, h, :, rows] for h in range(heads)]
            chains = [(h, c) for c in range(nc) for h in range(heads)]
            scores, out = {}, {}
            for i in range(len(chains) + ATTN_LOOKAHEAD):
                if i < len(chains):
                    h, c = chains[i]
                    scores[h, c] = _dot(k_tiles[h], qts[h][c])
                if i >= ATTN_LOOKAHEAD:
                    h, c = chains[i - ATTN_LOOKAHEAD]
                    out[h, c] = softmax_pv(carries[h][c], scores.pop((h, c)), vt_tiles[h])
            return tuple(tuple(out[h, c] for c in range(nc)) for h in range(heads))

        init = tuple(tuple((jnp.full((1, cw), -jnp.inf, F32), jnp.zeros((dve, cw), F32)) for _ in range(nc))
                     for _ in range(heads))
        carries = lax.fori_loop(0, j * (tq // tk), body, init)

        chains = [(h, c) for h in range(heads) for c in range(nc)]
        scores = {}
        for h, c in chains:
            s = _dot(k_ref[0, h, pl.ds(q0, (c + 1) * cw), :], qts[h][c])
            s_diag = jnp.where(causal, s[c * cw:, :], -jnp.inf)
            scores[h, c] = jnp.concatenate([s[:c * cw, :], s_diag], axis=0) if c > 0 else s_diag
        for h, c in chains:
            _, acc = softmax_pv(carries[h][c], scores[h, c], vt_ref[0, h, :, pl.ds(q0, (c + 1) * cw)])
            out = acc[0:dv, :] / acc[dv:dv + 1, :]
            o_ref[0, pl.ds(q0 + c * cw, cw), h * dv:(h + 1) * dv] = out.T.astype(o_ref.dtype)
        return 0

    lax.fori_loop(0, S // tq, q_tile, 0)


def _attention(qt, k, vt):
    B, H, S, dqk = k.shape
    dve = vt.shape[2]
    tq = min(ATTN_TQ, S)
    tk = min(ATTN_TK, tq)
    cw = min(ATTN_CW, tq)
    hp = ATTN_HEADS_PER_STEP
    head = lambda b, h: (b, h, 0, 0)
    return pl.pallas_call(
        functools.partial(_attn_kernel, tq=tq, tk=tk, cw=cw),
        grid=(B, H // hp),
        in_specs=[pl.BlockSpec((1, hp, dqk, S), head), pl.BlockSpec((1, hp, S, dqk), head),
                  pl.BlockSpec((1, hp, dve, S), head)],
        out_specs=pl.BlockSpec((1, S, hp * V_HEAD_DIM), lambda b, h: (b, 0, h)),
        out_shape=jax.ShapeDtypeStruct((B, S, H * V_HEAD_DIM), BF16),
        compiler_params=_params("parallel", "parallel"),
        name="mla_attention",
    )(qt, k, vt)


def _memory_probs(s, n_mem):
    probs = []
    for h in range(MEM_HEADS):
        sh = s[:, h * n_mem:(h + 1) * n_mem]
        e = jnp.exp2(sh - jnp.max(sh, axis=-1, keepdims=True))
        probs.append((e / jnp.sum(e, axis=-1, keepdims=True)).astype(BF16))
    return jnp.concatenate(probs, axis=-1)


def _tail(mix_rows, x_ref, qmem_ref, gate_ref, kexp_ref, vexp_ref, w_out_ref, post_g_ref, o_ref):
    tm = x_ref.shape[1]
    sub = min(TAIL_SUB_ROWS, tm)
    blocks = [slice(i * sub, (i + 1) * sub) for i in range(tm // sub)]
    n_mem = kexp_ref.shape[1] // MEM_HEADS
    kexp = kexp_ref[0]
    scores = [_dot_nt(qmem_ref[0, r, :], kexp) for r in blocks]
    for r, s in zip(blocks, scores):
        mo = _dot(_memory_probs(s, n_mem), vexp_ref[0])
        gate = gate_ref[0, r, :].astype(F32)
        y_mix = (mix_rows(r) * gate[:, :MIX_WIDTH]).astype(BF16)
        y_mem = (mo * gate[:, MIX_WIDTH:]).astype(BF16)
        y = _dot(y_mix, w_out_ref[0:MIX_WIDTH, :]) + _dot(y_mem, w_out_ref[MIX_WIDTH:, :])
        o_ref[0, r, :] = x_ref[0, r, :] + _rms(y, post_g_ref[...])


def _mla_out_kernel(x_ref, mix_ref, qmem_ref, gate_ref, kexp_ref, vexp_ref, w_out_ref, post_g_ref, o_ref):
    mix_rows = lambda r: mix_ref[0, r, :].astype(F32)
    _tail(mix_rows, x_ref, qmem_ref, gate_ref, kexp_ref, vexp_ref, w_out_ref, post_g_ref, o_ref)


def _mlstm_out_kernel(x_ref, hn_ref, og_ref, uc_ref, skip_ref, qmem_ref, gate_ref, kexp_ref, vexp_ref,
                      w_out_ref, post_g_ref, o_ref):
    def mix_rows(r):
        return (og_ref[0, r, :].astype(F32) * hn_ref[0, r, :].astype(F32)
                + skip_ref[...] * uc_ref[0, r, :].astype(F32))

    _tail(mix_rows, x_ref, qmem_ref, gate_ref, kexp_ref, vexp_ref, w_out_ref, post_g_ref, o_ref)


def _layer_out(kernel_fn, name, x, mixer_inputs, mixer_specs, qmem, gate, kexp, vexp, w_out, post_g):
    B, S, D = x.shape
    tm = min(TAIL_ROW_TILE, S)
    row = lambda b, i: (b, i, 0)
    per_batch = lambda b, i: (b, 0, 0)
    in_specs = ([pl.BlockSpec((1, tm, D), row)] + mixer_specs(tm) +
                [pl.BlockSpec((1, tm, MEM_WIDTH), row), pl.BlockSpec((1, tm, gate.shape[-1]), row),
                 pl.BlockSpec((1,) + kexp.shape[1:], per_batch), pl.BlockSpec((1,) + vexp.shape[1:], per_batch),
                 _const_spec(w_out.shape), _const_spec((1, D))])
    return pl.pallas_call(
        kernel_fn,
        grid=(B, S // tm),
        in_specs=in_specs,
        out_specs=pl.BlockSpec((1, tm, D), row),
        out_shape=jax.ShapeDtypeStruct((B, S, D), F32),
        compiler_params=_params("parallel", "parallel"),
        name=name,
    )(x, *mixer_inputs, qmem, gate, kexp, vexp, w_out.astype(BF16), post_g.reshape(1, D))


B_O = (0, MIX_WIDTH)
B_QMEM = (B_O[1], B_O[1] + MEM_WIDTH)
B_GATE = (B_QMEM[1], B_QMEM[1] + 1024)
B_IF = (B_GATE[1], B_GATE[1] + LANES)
CONV_HALO = 8


def _mlstm_front_kernel(x_ref, pre_g_ref, w_u_ref, w_rest_ref, conv_w_ref, conv_b_ref, bias_ref, wqk_ref,
                        wv_ref, og_ref, qmem_ref, gate_ref, uc_ref, qt_ref, k_ref, vt_ref, gc_ref, gr_ref,
                        tail_sc, *, chunk):
    tm = x_ref.shape[1]

    @pl.when(pl.program_id(1) == 0)
    def _():
        tail_sc[...] = jnp.zeros(tail_sc.shape, F32)

    h = _rms(x_ref[0], pre_g_ref[...]).astype(BF16)
    u = _dot(h, w_u_ref[...])
    p = _dot(h, w_rest_ref[...])

    ext = jnp.concatenate([tail_sc[...], u], axis=0)
    tail_sc[...] = u[tm - CONV_HALO:tm, :]
    conv = conv_b_ref[...] + u * conv_w_ref[CONV_WIDTH - 1:CONV_WIDTH, :]
    for back in range(1, CONV_WIDTH):
        tap = CONV_WIDTH - 1 - back
        conv = conv + ext[CONV_HALO - back:CONV_HALO - back + tm, :] * conv_w_ref[tap:tap + 1, :]
    uc = conv * jax.nn.sigmoid(conv)
    uc_bf = uc.astype(BF16)
    uc_ref[0] = uc_bf
    u_bf = u.astype(BF16)

    og_ref[0] = jax.nn.sigmoid(p[:, B_O[0]:B_O[1]]).astype(BF16)
    qmem_ref[0] = p[:, B_QMEM[0]:B_QMEM[1]].astype(BF16)
    gate = p[:, B_GATE[0]:B_GATE[1]]
    gate_ref[0] = (gate * jax.nn.sigmoid(gate)).astype(BF16)

    k_scale = MLSTM_QK_DIM ** -0.5
    vrow = lax.broadcasted_iota(jnp.int32, (MLSTM_PAD_V, tm), 0)
    for hd in range(MLSTM_HEADS):
        cols = slice(MLSTM_HEAD_START[hd], MLSTM_HEAD_START[hd] + MXU_DIM)
        qk = _dot(uc_bf[:, cols], wqk_ref[hd])
        qt_ref[0, hd] = qk[:, :MLSTM_PAD_QK].T.astype(BF16)
        k_ref[0, hd] = (qk[:, MLSTM_PAD_QK:] * k_scale).astype(BF16)
        vt = _dot(u_bf[:, cols], wv_ref[hd]).T
        vt_ref[0, hd] = jnp.where(vrow == MLSTM_V_DIM, 1.0, vt).astype(BF16)

    g = p[:, B_IF[0]:B_IF[1]] + bias_ref[...]
    lane = lax.broadcasted_iota(jnp.int32, g.shape, 1)
    log_f = jnp.minimum(g, 0.0) - jnp.log1p(jnp.exp(-jnp.abs(g)))
    is_f = (lane >= MLSTM_HEADS) & (lane < 2 * MLSTM_HEADS)
    gates = jnp.where(lane < MLSTM_HEADS, g, jnp.where(is_f, log_f, 0.0))
    r = lax.broadcasted_iota(jnp.int32, (chunk, chunk), 0)
    c = lax.broadcasted_iota(jnp.int32, (chunk, chunk), 1)
    tril = (c <= r).astype(BF16)
    chunk_lane = lax.broadcasted_iota(jnp.int32, (chunk, LANES), 1)
    chunk_is_f = (chunk_lane >= MLSTM_HEADS) & (chunk_lane < 2 * MLSTM_HEADS)
    parts = []
    for j in range(tm // chunk):
        gj = gates[j * chunk:(j + 1) * chunk, :]
        cum = sum(_dot(tril, term) for term in _split3(gj))
        parts.append(jnp.where(chunk_is_f, cum, gj))
    gc = jnp.concatenate(parts, axis=0) if len(parts) > 1 else parts[0]
    gr_ref[0] = gc.T[0:2 * MLSTM_HEADS, :]
    gc_ref[0] = gc - pltpu.roll(gc, LANES - MLSTM_HEADS, 1)


def _mlstm_front(x, pre_g, w_in, gate_bias, conv_w, conv_b, w_q, w_k, w_v):
    B, S, D = x.shape
    tm = min(ROW_TILE, S)
    chunk = min(MLSTM_CHUNK, S)
    H = MLSTM_HEADS
    n_if = 2 * H
    o_if = MIX_WIDTH
    o_o = o_if + n_if
    w_u = w_in[:, :o_if].astype(BF16)
    w_rest = jnp.concatenate([w_in[:, o_o:], w_in[:, o_if:o_o], jnp.zeros((D, LANES - n_if), w_in.dtype)],
                             axis=1).astype(BF16)
    wqk = jnp.zeros((H, MXU_DIM, 2 * MLSTM_PAD_QK), F32)
    wv = jnp.zeros((H, MXU_DIM, MLSTM_PAD_V), F32)
    for h in range(H):
        off = h * MLSTM_V_DIM - MLSTM_HEAD_START[h]
        wqk = wqk.at[h, off:off + MLSTM_V_DIM, 0:MLSTM_QK_DIM].set(w_q[h])
        wqk = wqk.at[h, off:off + MLSTM_V_DIM, MLSTM_PAD_QK:MLSTM_PAD_QK + MLSTM_QK_DIM].set(w_k[h])
        wv = wv.at[h, off:off + MLSTM_V_DIM, 0:MLSTM_V_DIM].set(w_v[h])
    bias = jnp.pad(gate_bias, (0, LANES - n_if)).reshape(1, LANES)
    row = lambda b, i: (b, i, 0)
    head_row = lambda b, i: (b, 0, i, 0)
    head_col = lambda b, i: (b, 0, 0, i)
    out_shape = [
        jax.ShapeDtypeStruct((B, S, MIX_WIDTH), BF16),
        jax.ShapeDtypeStruct((B, S, MEM_WIDTH), BF16),
        jax.ShapeDtypeStruct((B, S, 1024), BF16),
        jax.ShapeDtypeStruct((B, S, MIX_WIDTH), BF16),
        jax.ShapeDtypeStruct((B, H, MLSTM_PAD_QK, S), BF16),
        jax.ShapeDtypeStruct((B, H, S, MLSTM_PAD_QK), BF16),
        jax.ShapeDtypeStruct((B, H, MLSTM_PAD_V, S), BF16),
        jax.ShapeDtypeStruct((B, S, LANES), F32),
        jax.ShapeDtypeStruct((B, 2 * H, S), F32),
    ]
    out_specs = [
        pl.BlockSpec((1, tm, MIX_WIDTH), row),
        pl.BlockSpec((1, tm, MEM_WIDTH), row),
        pl.BlockSpec((1, tm, 1024), row),
        pl.BlockSpec((1, tm, MIX_WIDTH), row),
        pl.BlockSpec((1, H, MLSTM_PAD_QK, tm), head_col),
        pl.BlockSpec((1, H, tm, MLSTM_PAD_QK), head_row),
        pl.BlockSpec((1, H, MLSTM_PAD_V, tm), head_col),
        pl.BlockSpec((1, tm, LANES), row),
        pl.BlockSpec((1, 2 * H, tm), lambda b, i: (b, 0, i)),
    ]
    return pl.pallas_call(
        functools.partial(_mlstm_front_kernel, chunk=chunk),
        grid=(B, S // tm),
        in_specs=[pl.BlockSpec((1, tm, D), row), _const_spec((1, D)), _const_spec(w_u.shape),
                  _const_spec(w_rest.shape), _const_spec((CONV_WIDTH, MIX_WIDTH)), _const_spec((1, MIX_WIDTH)),
                  _const_spec((1, LANES)), _const_spec(wqk.shape), _const_spec(wv.shape)],
        out_specs=out_specs,
        out_shape=out_shape,
        scratch_shapes=[pltpu.VMEM((CONV_HALO, MIX_WIDTH), F32)],
        compiler_params=_params("parallel", "arbitrary"),
        name="mlstm_front",
    )(x, pre_g.reshape(1, D), w_u, w_rest, conv_w, conv_b.reshape(1, -1), bias, wqk.astype(BF16),
      wv.astype(BF16))


def _mlstm_kernel(qt_ref, k_ref, vt_ref, gc_ref, gr_ref, hg_ref, o_ref, c_sc, m_sc, *, chunk):
    H = k_ref.shape[1]
    seg = k_ref.shape[2]
    L = chunk

    @pl.when(pl.program_id(1) == 0)
    def _():
        c_sc[...] = jnp.zeros(c_sc.shape, F32)
        m_sc[...] = jnp.zeros(m_sc.shape, F32)

    causal = lax.broadcasted_iota(jnp.int32, (L, L), 0) <= lax.broadcasted_iota(jnp.int32, (L, L), 1)
    vrow = lax.broadcasted_iota(jnp.int32, (MLSTM_PAD_V, L), 0)
    gains = [jnp.broadcast_to(hg_ref[h], (MLSTM_PAD_V, L)) for h in range(H)]

    def step(c, carry):
        rows = pl.ds(pl.multiple_of(c * L, L), L)
        qts = [qt_ref[0, h, :, rows] for h in range(H)]
        ks = [k_ref[0, h, rows, :] for h in range(H)]
        vts = [vt_ref[0, h, :, rows] for h in range(H)]
        qk = [_dot(ks[h], qts[h]) for h in range(H)]
        cq = [_dot(carry[h][0].astype(BF16), qts[h]) for h in range(H)]
        out, normed = [], []
        for h in range(H):
            C, m = carry[h]
            c_col = gc_ref[0, rows, h:h + 1]
            li_row = gr_ref[0, h:h + 1, rows]
            b_row = gr_ref[0, MLSTM_HEADS + h:MLSTM_HEADS + h + 1, rows]
            d = jnp.where(causal, c_col + b_row, -jnp.inf)
            inter = b_row + m
            m_t = jnp.maximum(inter, jnp.max(d, axis=0, keepdims=True))
            sqk = (qk[h] * jnp.exp(d - m_t)).astype(BF16)
            num = jnp.exp(inter - m_t) * cq[h] + _dot(vts[h], sqk)
            den = jnp.maximum(jnp.abs(num[MLSTM_V_DIM:MLSTM_V_DIM + 1, :]), jnp.exp(-m_t))
            hv = jnp.where(vrow < MLSTM_V_DIM, num / den, 0.0)
            ms = jnp.sum(hv * hv, axis=0, keepdims=True) * (1.0 / MLSTM_V_DIM)
            normed.append((hv * lax.rsqrt(ms + EPS) * gains[h])[0:MLSTM_V_DIM, :])

            b_last = b_row[:, L - 1:L]
            dec = b_last - b_row + li_row
            m_new = jnp.maximum(b_last + m, jnp.max(dec, axis=1, keepdims=True))
            vw = (vts[h].astype(F32) * jnp.exp(dec - m_new)).astype(BF16)
            out.append((jnp.exp(b_last + m - m_new) * C + _dot(vw, ks[h]), m_new))
        o_ref[0, rows, :] = jnp.concatenate(normed, axis=0).T.astype(o_ref.dtype)
        return tuple(out)

    init = tuple((c_sc[h], m_sc[h, 0:1, 0:1]) for h in range(H))
    final = lax.fori_loop(0, seg // L, step, init)
    for h in range(H):
        c_sc[h] = final[h][0]
        m_sc[h] = jnp.broadcast_to(final[h][1], m_sc.shape[1:])


def _mlstm(qt, k, vt, gc, gr, head_g):
    B, H, S, _ = k.shape
    chunk = min(MLSTM_CHUNK, S)
    seg = min(MLSTM_SEGMENT, S)
    hg = jnp.pad(head_g, ((0, 0), (0, MLSTM_PAD_V - MLSTM_V_DIM))).reshape(H, MLSTM_PAD_V, 1)
    head_row = lambda b, i: (b, 0, i, 0)
    head_col = lambda b, i: (b, 0, 0, i)
    return pl.pallas_call(
        functools.partial(_mlstm_kernel, chunk=chunk),
        grid=(B, S // seg),
        in_specs=[pl.BlockSpec((1, H, MLSTM_PAD_QK, seg), head_col),
                  pl.BlockSpec((1, H, seg, MLSTM_PAD_QK), head_row),
                  pl.BlockSpec((1, H, MLSTM_PAD_V, seg), head_col),
                  pl.BlockSpec((1, seg, LANES), lambda b, i: (b, i, 0)),
                  pl.BlockSpec((1, 2 * H, seg), lambda b, i: (b, 0, i)),
                  _const_spec(hg.shape)],
        out_specs=pl.BlockSpec((1, seg, H * MLSTM_V_DIM), lambda b, i: (b, i, 0)),
        out_shape=jax.ShapeDtypeStruct((B, S, H * MLSTM_V_DIM), BF16),
        scratch_shapes=[pltpu.VMEM((H, MLSTM_PAD_V, MLSTM_PAD_QK), F32), pltpu.VMEM((H, 8, LANES), F32)],
        compiler_params=_params("parallel", "arbitrary"),
        name="mlstm_scan",
    )(qt, k, vt, gc, gr, hg)


def _mla_layer(x, mem, rope_operands, pre_g, w_in, q_a_g, w_uq, kv_a_g, w_ukv, mem_g, w_mem_kv, w_out, post_g):
    q, k, v, qmem, gate = _mla_in(x, rope_operands, pre_g, w_in, q_a_g, w_uq, kv_a_g, w_ukv)
    mix = _attention(q, k, v)
    kexp, vexp = _mem_kv(mem, mem_g, w_mem_kv)
    specs = lambda tm: [pl.BlockSpec((1, tm, MIX_WIDTH), lambda b, i: (b, i, 0))]
    return _layer_out(_mla_out_kernel, "mla_out", x, [mix], specs, qmem, gate, kexp, vexp, w_out, post_g)


def _mlstm_layer(x, mem, pre_g, w_in, gate_bias, conv_w, conv_b, w_q, w_k, w_v, head_g, skip,
                 mem_g, w_mem_kv, w_out, post_g):
    og, qmem, gate, uc, qt, k, vt, gc, gr = _mlstm_front(x, pre_g, w_in, gate_bias, conv_w, conv_b,
                                                         w_q, w_k, w_v)
    hn = _mlstm(qt, k, vt, gc, gr, head_g)
    kexp, vexp = _mem_kv(mem, mem_g, w_mem_kv)
    row = lambda b, i: (b, i, 0)
    specs = lambda tm: [pl.BlockSpec((1, tm, MIX_WIDTH), row),
                        pl.BlockSpec((1, tm, MIX_WIDTH), row), pl.BlockSpec((1, tm, MIX_WIDTH), row),
                        _const_spec((1, MIX_WIDTH))]
    return _layer_out(_mlstm_out_kernel, "mlstm_out", x, [hn, og, uc, skip.reshape(1, -1)], specs,
                      qmem, gate, kexp, vexp, w_out, post_g)


def kernel(x, mem, positions, a_pre_g, a_w_in, a_q_a_g, a_w_uq, a_kv_a_g, a_w_ukv, a_mem_g, a_w_mem_kv, a_w_out, a_post_g, b_pre_g, b_w_in, b_gate_bias, b_conv_w, b_conv_b, b_w_q, b_w_k, b_w_v, b_head_g, b_skip, b_mem_g, b_w_mem_kv, b_w_out, b_post_g):
    depth = a_pre_g.shape[0] + b_pre_g.shape[0]
    rope_operands = _rope_operands(positions)
    for i in range(depth):
        j = i // 2
        if i % 2 == 0:
            x = _mla_layer(x, mem, rope_operands, a_pre_g[j], a_w_in[j], a_q_a_g[j], a_w_uq[j], a_kv_a_g[j],
                           a_w_ukv[j], a_mem_g[j], a_w_mem_kv[j], a_w_out[j], a_post_g[j])
        else:
            x = _mlstm_layer(x, mem, b_pre_g[j], b_w_in[j], b_gate_bias[j], b_conv_w[j], b_conv_b[j],
                             b_w_q[j], b_w_k[j], b_w_v[j], b_head_g[j], b_skip[j], b_mem_g[j],
                             b_w_mem_kv[j], b_w_out[j], b_post_g[j])
    return x
```

```python
import functools

import jax
import jax.numpy as jnp
import numpy as np
from jax import lax
from jax.experimental import pallas as pl
from jax.experimental.pallas import tpu as pltpu

EPS = 1e-6
ROPE_THETA = 10000.0
MEM_HEADS = 4
MEM_HEAD_DIM = 64
MEM_WIDTH = MEM_HEADS * MEM_HEAD_DIM
QK_NOPE_DIM = 128
QK_ROPE_DIM = 64
V_HEAD_DIM = 128
MLA_HEADS = 6
V_EXT_DIM = V_HEAD_DIM + 16
Q_LORA_RANK = 384
KV_LORA_RANK = 256
MLSTM_HEADS = 4
MLSTM_V_DIM = 192
MLSTM_QK_DIM = 96
CONV_WIDTH = 4
MIX_WIDTH = 768

LANES = 128
MXU_DIM = 256
VMEM_LIMIT_BYTES = 56 * 1024 * 1024

ROW_TILE = 512
TAIL_ROW_TILE = 1024
TAIL_SUB_ROWS = 256
ATTN_TQ = 1024
ATTN_CW = 256
ATTN_TK = 1024
ATTN_HEADS_PER_STEP = 2
ATTN_LOOKAHEAD = 5
MLSTM_CHUNK = 256
MLSTM_SEGMENT = 1024

LOG2E = 1.4426950408889634
MLSTM_PAD_QK = LANES
MLSTM_PAD_V = MXU_DIM
MLSTM_HEAD_START = tuple((h * MLSTM_V_DIM // LANES) * LANES for h in range(MLSTM_HEADS))

F32 = jnp.float32
BF16 = jnp.bfloat16


def _dot(a, b):
    return jnp.dot(a, b, preferred_element_type=F32)


def _dot_nt(a, b):
    return lax.dot_general(a, b, (((1,), (1,)), ((), ())), preferred_element_type=F32)


def _rms(x, g, width=None):
    width = x.shape[-1] if width is None else width
    ms = jnp.sum(x * x, axis=-1, keepdims=True) * (1.0 / width)
    return x * lax.rsqrt(ms + EPS) * g


def _split3(x):
    b1 = x.astype(BF16)
    r1 = x - b1.astype(F32)
    b2 = r1.astype(BF16)
    b3 = (r1 - b2.astype(F32)).astype(BF16)
    return b1, b2, b3


def _params(*semantics):
    return pltpu.CompilerParams(dimension_semantics=semantics, vmem_limit_bytes=VMEM_LIMIT_BYTES)


def _const_spec(shape):
    zeros = (0,) * len(shape)
    return pl.BlockSpec(shape, lambda *_: zeros)


ROPE_HALF = QK_ROPE_DIM // 2
ROPE_PER_ROW = LANES // ROPE_HALF


def _rope_operands(positions):
    B, S = positions.shape
    rows = B * S // ROPE_PER_ROW
    inv_freq = ROPE_THETA ** (-jnp.arange(0, QK_ROPE_DIM, 2, dtype=F32) / QK_ROPE_DIM)
    pos4 = jnp.repeat(positions.reshape(rows, ROPE_PER_ROW), ROPE_HALF, axis=1)
    invf = jnp.tile(inv_freq, ROPE_PER_ROW).reshape(1, LANES)
    sel = np.zeros((2 * LANES, ROPE_PER_ROW * LANES), np.float32)
    for t in range(ROPE_PER_ROW):
        for f in range(ROPE_HALF):
            sel[t * ROPE_HALF + f, t * LANES + f] = 1.0
            sel[t * ROPE_HALF + f, t * LANES + ROPE_HALF + f] = 1.0
            sel[LANES + t * ROPE_HALF + f, t * LANES + 2 * ROPE_HALF + f] = -1.0
            sel[LANES + t * ROPE_HALF + f, t * LANES + 3 * ROPE_HALF + f] = 1.0
    return pos4, invf, jnp.asarray(sel, BF16)


def _rope_table(pos_ref, invf_ref, sel_ref, cs_sc):
    ang = pos_ref[...].astype(F32) * invf_ref[...]
    trig = jnp.concatenate([jnp.cos(ang), jnp.sin(ang)], axis=1)
    spread = sum(_dot(term, sel_ref[...]) for term in _split3(trig))
    rows = trig.shape[0]
    for t in range(ROPE_PER_ROW):
        cs_sc[pl.ds(t, rows, stride=ROPE_PER_ROW), :] = spread[:, t * LANES:(t + 1) * LANES]
    return cs_sc[...]


def _mem_kv_kernel(mem_ref, g_ref, w_ref, k_ref, v_ref):
    n_mem = mem_ref.shape[1]
    hn = _rms(mem_ref[0], g_ref[...]).astype(BF16)
    kv = _dot(hn, w_ref[...])
    k = kv[:, :MEM_WIDTH] * (MEM_HEAD_DIM ** -0.5 * LOG2E)
    v = kv[:, MEM_WIDTH:]
    col_head = lax.broadcasted_iota(jnp.int32, (n_mem, MEM_WIDTH), 1) // MEM_HEAD_DIM
    for h in range(MEM_HEADS):
        rows = pl.ds(h * n_mem, n_mem)
        k_ref[0, rows, :] = jnp.where(col_head == h, k, 0.0).astype(BF16)
        v_ref[0, rows, :] = jnp.where(col_head == h, v, 0.0).astype(BF16)


def _mem_kv(mem, mem_g, w_mem_kv):
    B, n_mem, D = mem.shape
    out = jax.ShapeDtypeStruct((B, MEM_HEADS * n_mem, MEM_WIDTH), BF16)
    spec = pl.BlockSpec((1, MEM_HEADS * n_mem, MEM_WIDTH), lambda b: (b, 0, 0))
    return pl.pallas_call(
        _mem_kv_kernel,
        grid=(B,),
        in_specs=[pl.BlockSpec((1, n_mem, D), lambda b: (b, 0, 0)), _const_spec((1, D)),
                  _const_spec((D, 2 * MEM_WIDTH))],
        out_specs=[spec, spec],
        out_shape=[out, out],
        compiler_params=_params("parallel"),
        name="mem_kv",
    )(mem, mem_g.reshape(1, D), w_mem_kv.astype(BF16))


A_CQ = (0, Q_LORA_RANK)
A_CKV = (A_CQ[1], A_CQ[1] + KV_LORA_RANK)
A_QMEM = (A_CKV[1], A_CKV[1] + MEM_WIDTH)
A_GATE = (A_QMEM[1], A_QMEM[1] + 1024)
A_KROPE = (A_GATE[1], A_GATE[1] + 2 * QK_ROPE_DIM)
Q_ROPE_OFF = MLA_HEADS * QK_NOPE_DIM


def _mla_in_kernel(x_ref, pos_ref, invf_ref, sel_ref, pre_g_ref, w_in_ref, qa_g_ref, w_uq_ref, kva_g_ref,
                   w_ukv_ref, q_ref, k_ref, v_ref, qmem_ref, gate_ref, cs_sc):
    q_scale = (QK_NOPE_DIM + QK_ROPE_DIM) ** -0.5 * LOG2E
    h = _rms(x_ref[0], pre_g_ref[...]).astype(BF16)
    p = _dot(h, w_in_ref[...])
    cs = _rope_table(pos_ref, invf_ref, sel_ref, cs_sc)
    qmem_ref[0] = p[:, A_QMEM[0]:A_QMEM[1]].astype(BF16)
    gate = p[:, A_GATE[0]:A_GATE[1]]
    gate_ref[0] = (gate * jax.nn.sigmoid(gate)).astype(BF16)

    c_q = _rms(p[:, A_CQ[0]:A_CQ[1]], qa_g_ref[...]).astype(BF16)
    q = _dot(c_q, w_uq_ref[...])
    c_kv = _rms(p[:, A_CKV[0]:A_CKV[1]], kva_g_ref[...]).astype(BF16)
    kv = _dot(c_kv, w_ukv_ref[...])

    kr = p[:, A_KROPE[0]:A_KROPE[1]] * cs
    k_rot = (kr + pltpu.roll(kr, QK_ROPE_DIM, 1)).astype(BF16)
    tm = cs.shape[0]
    pad_row = lax.broadcasted_iota(jnp.int32, (V_EXT_DIM - V_HEAD_DIM, tm), 0)
    ones_row = jnp.where(pad_row == 0, 1.0, 0.0).astype(BF16)
    for hd in range(MLA_HEADS):
        nope = slice(hd * QK_NOPE_DIM, (hd + 1) * QK_NOPE_DIM)
        rope = slice(Q_ROPE_OFF + hd * LANES, Q_ROPE_OFF + (hd + 1) * LANES)
        q_ref[0, hd, 0:LANES, :] = (q[:, nope] * q_scale).T.astype(BF16)
        q_ref[0, hd, LANES:2 * LANES, :] = (q[:, rope] * cs * q_scale).T.astype(BF16)
        k_ref[0, hd, :, 0:LANES] = kv[:, 2 * hd * LANES:(2 * hd + 1) * LANES].astype(BF16)
        k_ref[0, hd, :, LANES:2 * LANES] = k_rot
        v_ref[0, hd, 0:V_HEAD_DIM, :] = kv[:, (2 * hd + 1) * LANES:(2 * hd + 2) * LANES].T.astype(BF16)
        v_ref[0, hd, V_HEAD_DIM:V_EXT_DIM, :] = ones_row


def _rope_cols(w, start):
    half = QK_ROPE_DIM // 2
    return [w[:, start:start + QK_ROPE_DIM], w[:, start + half:start + QK_ROPE_DIM], w[:, start:start + half]]


def _mla_in(x, rope_operands, pre_g, w_in, q_a_g, w_uq, kv_a_g, w_ukv):
    B, S, D = x.shape
    tm = min(ROW_TILE, S)
    pos4, invf, sel = rope_operands
    pos_rows = tm // ROPE_PER_ROW
    tiles_per_seq = S // tm
    o_kr = Q_LORA_RANK + KV_LORA_RANK
    o_qm = o_kr + QK_ROPE_DIM
    w_in_p = jnp.concatenate([w_in[:, :o_kr], w_in[:, o_qm:]] + _rope_cols(w_in, o_kr), axis=1).astype(BF16)
    head_w = QK_NOPE_DIM + QK_ROPE_DIM
    uq_cols = [w_uq[:, h * head_w:h * head_w + QK_NOPE_DIM] for h in range(MLA_HEADS)]
    for h in range(MLA_HEADS):
        uq_cols += _rope_cols(w_uq, h * head_w + QK_NOPE_DIM)
    w_uq_p = jnp.concatenate(uq_cols, axis=1).astype(BF16)
    w_ukv_p = w_ukv.astype(BF16)

    row = lambda b, i: (b, i, 0)
    head_row = lambda b, i: (b, 0, i, 0)
    head_col = lambda b, i: (b, 0, 0, i)
    out_shape = [
        jax.ShapeDtypeStruct((B, MLA_HEADS, 2 * LANES, S), BF16),
        jax.ShapeDtypeStruct((B, MLA_HEADS, S, 2 * LANES), BF16),
        jax.ShapeDtypeStruct((B, MLA_HEADS, V_EXT_DIM, S), BF16),
        jax.ShapeDtypeStruct((B, S, MEM_WIDTH), BF16),
        jax.ShapeDtypeStruct((B, S, 1024), BF16),
    ]
    out_specs = [
        pl.BlockSpec((1, MLA_HEADS, 2 * LANES, tm), head_col),
        pl.BlockSpec((1, MLA_HEADS, tm, 2 * LANES), head_row),
        pl.BlockSpec((1, MLA_HEADS, V_EXT_DIM, tm), head_col),
        pl.BlockSpec((1, tm, MEM_WIDTH), row),
        pl.BlockSpec((1, tm, 1024), row),
    ]
    return pl.pallas_call(
        _mla_in_kernel,
        grid=(B, S // tm),
        in_specs=[pl.BlockSpec((1, tm, D), row),
                  pl.BlockSpec((pos_rows, LANES), lambda b, i: (b * tiles_per_seq + i, 0)),
                  _const_spec(invf.shape), _const_spec(sel.shape),
                  _const_spec((1, D)), _const_spec(w_in_p.shape),
                  _const_spec((1, Q_LORA_RANK)), _const_spec(w_uq_p.shape),
                  _const_spec((1, KV_LORA_RANK)), _const_spec(w_ukv_p.shape)],
        out_specs=out_specs,
        out_shape=out_shape,
        scratch_shapes=[pltpu.VMEM((tm, LANES), F32)],
        compiler_params=_params("parallel", "parallel"),
        name="mla_in",
    )(x, pos4, invf, sel, pre_g.reshape(1, D), w_in_p, q_a_g.reshape(1, -1), w_uq_p, kv_a_g.reshape(1, -1),
      w_ukv_p)


def _attn_kernel(qt_ref, k_ref, vt_ref, o_ref, *, tq, tk, cw):
    heads = k_ref.shape[1]
    S = k_ref.shape[2]
    dve = vt_ref.shape[2]
    dv = V_HEAD_DIM
    nc = tq // cw
    causal = (lax.broadcasted_iota(jnp.int32, (cw, cw), 0) <= lax.broadcasted_iota(jnp.int32, (cw, cw), 1))

    def q_tile(j, _):
        q0 = pl.multiple_of(j * tq, tq)
        qts = [[qt_ref[0, h, :, pl.ds(q0 + c * cw, cw)] for c in range(nc)] for h in range(heads)]

        def softmax_pv(carry, s, vt_tile):
            m, acc = carry
            m_new = jnp.maximum(m, jnp.max(s, axis=0, keepdims=True))
            p = jnp.exp2(s - m_new).astype(BF16)
            return m_new, jnp.exp2(m - m_new) * acc + _dot(vt_tile, p)

        def body(ki, carries):
            rows = pl.ds(pl.multiple_of(ki * tk, tk), tk)
            k_tiles = [k_ref[0, h, rows, :] for h in range(heads)]
            vt_tiles = [vt_ref[0, h, :, rows] for h in range(heads)]
            chains = [(h, c) for c in range(nc) for h in range(heads)]
            scores, out = {}, {}
            for i in range(len(chains) + ATTN_LOOKAHEAD):
                if i < len(chains):
                    h, c = chains[i]
                    scores[h, c] = _dot(k_tiles[h], qts[h][c])
                if i >= ATTN_LOOKAHEAD:
                    h, c = chains[i - ATTN_LOOKAHEAD]
                    out[h, c] = softmax_pv(carries[h][c], scores.pop((h, c)), vt_tiles[h])
            return tuple(tuple(out[h, c] for c in range(nc)) for h in range(heads))

        init = tuple(tuple((jnp.full((1, cw), -jnp.inf, F32), jnp.zeros((dve, cw), F32)) for _ in range(nc))
                     for _ in range(heads))
        carries = lax.fori_loop(0, j * (tq // tk), body, init)

        chains = [(h, c) for c in range(nc) for h in range(heads)]
        scores = {}
        for i in range(len(chains) + ATTN_LOOKAHEAD):
            if i < len(chains):
                h, c = chains[i]
                s = _dot(k_ref[0, h, pl.ds(q0, (c + 1) * cw), :], qts[h][c])
                s_diag = jnp.where(causal, s[c * cw:, :], -jnp.inf)
                scores[h, c] = jnp.concatenate([s[:c * cw, :], s_diag], axis=0) if c > 0 else s_diag
            if i >= ATTN_LOOKAHEAD:
                h, c = chains[i - ATTN_LOOKAHEAD]
                _, acc = softmax_pv(carries[h][c], scores.pop((h, c)), vt_ref[0, h, :, pl.ds(q0, (c + 1) * cw)])
                out = acc[0:dv, :] / acc[dv:dv + 1, :]
                o_ref[0, pl.ds(q0 + c * cw, cw), h * dv:(h + 1) * dv] = out.T.astype(o_ref.dtype)
        return 0

    lax.fori_loop(0, S // tq, q_tile, 0)


def _attention(qt, k, vt):
    B, H, S, dqk = k.shape
    dve = vt.shape[2]
    tq = min(ATTN_TQ, S)
    tk = min(ATTN_TK, tq)
    cw = min(ATTN_CW, tq)
    hp = ATTN_HEADS_PER_STEP
    head = lambda b, h: (b, h, 0, 0)
    return pl.pallas_call(
        functools.partial(_attn_kernel, tq=tq, tk=tk, cw=cw),
        grid=(B, H // hp),
        in_specs=[pl.BlockSpec((1, hp, dqk, S), head), pl.BlockSpec((1, hp, S, dqk), head),
                  pl.BlockSpec((1, hp, dve, S), head)],
        out_specs=pl.BlockSpec((1, S, hp * V_HEAD_DIM), lambda b, h: (b, 0, h)),
        out_shape=jax.ShapeDtypeStruct((B, S, H * V_HEAD_DIM), BF16),
        compiler_params=_params("parallel", "parallel"),
        name="mla_attention",
    )(qt, k, vt)


def _memory_probs(s, n_mem):
    probs = []
    for h in range(MEM_HEADS):
        sh = s[:, h * n_mem:(h + 1) * n_mem]
        e = jnp.exp2(sh - jnp.max(sh, axis=-1, keepdims=True))
        probs.append((e / jnp.sum(e, axis=-1, keepdims=True)).astype(BF16))
    return jnp.concatenate(probs, axis=-1)


def _tail(mix_rows, x_ref, qmem_ref, gate_ref, kexp_ref, vexp_ref, w_out_ref, post_g_ref, o_ref):
    tm = x_ref.shape[1]
    sub = min(TAIL_SUB_ROWS, tm)
    blocks = [slice(i * sub, (i + 1) * sub) for i in range(tm // sub)]
    n_mem = kexp_ref.shape[1] // MEM_HEADS
    kexp = kexp_ref[0]
    scores = [_dot_nt(qmem_ref[0, r, :], kexp) for r in blocks]
    for r, s in zip(blocks, scores):
        mo = _dot(_memory_probs(s, n_mem), vexp_ref[0])
        gate = gate_ref[0, r, :].astype(F32)
        y_mix = (mix_rows(r) * gate[:, :MIX_WIDTH]).astype(BF16)
        y_mem = (mo * gate[:, MIX_WIDTH:]).astype(BF16)
        y = _dot(y_mix, w_out_ref[0:MIX_WIDTH, :]) + _dot(y_mem, w_out_ref[MIX_WIDTH:, :])
        o_ref[0, r, :] = x_ref[0, r, :] + _rms(y, post_g_ref[...])


def _mla_out_kernel(x_ref, mix_ref, qmem_ref, gate_ref, kexp_ref, vexp_ref, w_out_ref, post_g_ref, o_ref):
    mix_rows = lambda r: mix_ref[0, r, :].astype(F32)
    _tail(mix_rows, x_ref, qmem_ref, gate_ref, kexp_ref, vexp_ref, w_out_ref, post_g_ref, o_ref)


def _mlstm_out_kernel(x_ref, hn_ref, og_ref, uc_ref, skip_ref, qmem_ref, gate_ref, kexp_ref, vexp_ref,
                      w_out_ref, post_g_ref, o_ref):
    def mix_rows(r):
        return (og_ref[0, r, :].astype(F32) * hn_ref[0, r, :].astype(F32)
                + skip_ref[...] * uc_ref[0, r, :].astype(F32))

    _tail(mix_rows, x_ref, qmem_ref, gate_ref, kexp_ref, vexp_ref, w_out_ref, post_g_ref, o_ref)


def _layer_out(kernel_fn, name, x, mixer_inputs, mixer_specs, qmem, gate, kexp, vexp, w_out, post_g):
    B, S, D = x.shape
    tm = min(TAIL_ROW_TILE, S)
    row = lambda b, i: (b, i, 0)
    per_batch = lambda b, i: (b, 0, 0)
    in_specs = ([pl.BlockSpec((1, tm, D), row)] + mixer_specs(tm) +
                [pl.BlockSpec((1, tm, MEM_WIDTH), row), pl.BlockSpec((1, tm, gate.shape[-1]), row),
                 pl.BlockSpec((1,) + kexp.shape[1:], per_batch), pl.BlockSpec((1,) + vexp.shape[1:], per_batch),
                 _const_spec(w_out.shape), _const_spec((1, D))])
    return pl.pallas_call(
        kernel_fn,
        grid=(B, S // tm),
        in_specs=in_specs,
        out_specs=pl.BlockSpec((1, tm, D), row),
        out_shape=jax.ShapeDtypeStruct((B, S, D), F32),
        compiler_params=_params("parallel", "parallel"),
        name=name,
    )(x, *mixer_inputs, qmem, gate, kexp, vexp, w_out.astype(BF16), post_g.reshape(1, D))


B_O = (0, MIX_WIDTH)
B_QMEM = (B_O[1], B_O[1] + MEM_WIDTH)
B_GATE = (B_QMEM[1], B_QMEM[1] + 1024)
B_IF = (B_GATE[1], B_GATE[1] + LANES)
CONV_HALO = 8
V_ONES_ROWS = 16


def _mlstm_front_kernel(x_ref, pre_g_ref, w_u_ref, w_rest_ref, conv_w_ref, conv_b_ref, bias_ref, hg_ref, wqk_ref,
                        wv_ref, og_ref, qmem_ref, gate_ref, uc_ref, qt_ref, k_ref, vt_ref, gc_ref, gr_ref,
                        tail_sc, *, chunk):
    tm = x_ref.shape[1]

    @pl.when(pl.program_id(1) == 0)
    def _():
        tail_sc[...] = jnp.zeros(tail_sc.shape, F32)

    h = _rms(x_ref[0], pre_g_ref[...]).astype(BF16)
    u = _dot(h, w_u_ref[...])
    p = _dot(h, w_rest_ref[...])

    ext = jnp.concatenate([tail_sc[...], u], axis=0)
    tail_sc[...] = u[tm - CONV_HALO:tm, :]
    conv = conv_b_ref[...] + u * conv_w_ref[CONV_WIDTH - 1:CONV_WIDTH, :]
    for back in range(1, CONV_WIDTH):
        tap = CONV_WIDTH - 1 - back
        conv = conv + ext[CONV_HALO - back:CONV_HALO - back + tm, :] * conv_w_ref[tap:tap + 1, :]
    uc = conv * jax.nn.sigmoid(conv)
    uc_bf = uc.astype(BF16)
    uc_ref[0] = uc_bf
    u_bf = u.astype(BF16)

    og_ref[0] = (jax.nn.sigmoid(p[:, B_O[0]:B_O[1]]) * hg_ref[...]).astype(BF16)
    qmem_ref[0] = p[:, B_QMEM[0]:B_QMEM[1]].astype(BF16)
    gate = p[:, B_GATE[0]:B_GATE[1]]
    gate_ref[0] = (gate * jax.nn.sigmoid(gate)).astype(BF16)

    k_scale = MLSTM_QK_DIM ** -0.5
    ones_rows = jnp.where(lax.broadcasted_iota(jnp.int32, (V_ONES_ROWS, tm), 0) == 0, 1.0, 0.0).astype(BF16)
    for hd in range(MLSTM_HEADS):
        cols = slice(MLSTM_HEAD_START[hd], MLSTM_HEAD_START[hd] + MXU_DIM)
        qk = _dot(uc_bf[:, cols], wqk_ref[hd])
        qt_ref[0, hd] = qk[:, :MLSTM_PAD_QK].T.astype(BF16)
        k_ref[0, hd] = (qk[:, MLSTM_PAD_QK:] * k_scale).astype(BF16)
        vt_ref[0, hd] = _dot(u_bf[:, cols], wv_ref[hd]).T.astype(BF16)
        vt_ref[0, hd, MLSTM_V_DIM:MLSTM_V_DIM + V_ONES_ROWS, :] = ones_rows

    g = p[:, B_IF[0]:B_IF[1]] + bias_ref[...]
    lane = lax.broadcasted_iota(jnp.int32, g.shape, 1)
    log_f = jnp.minimum(g, 0.0) - jnp.log1p(jnp.exp(-jnp.abs(g)))
    is_f = (lane >= MLSTM_HEADS) & (lane < 2 * MLSTM_HEADS)
    gates = jnp.where(lane < MLSTM_HEADS, g, jnp.where(is_f, log_f, 0.0))
    r = lax.broadcasted_iota(jnp.int32, (chunk, chunk), 0)
    c = lax.broadcasted_iota(jnp.int32, (chunk, chunk), 1)
    tril = (c <= r).astype(BF16)
    chunk_lane = lax.broadcasted_iota(jnp.int32, (chunk, LANES), 1)
    chunk_is_f = (chunk_lane >= MLSTM_HEADS) & (chunk_lane < 2 * MLSTM_HEADS)
    parts = []
    for j in range(tm // chunk):
        gj = gates[j * chunk:(j + 1) * chunk, :]
        cum = sum(_dot(tril, term) for term in _split3(gj))
        parts.append(jnp.where(chunk_is_f, cum, gj))
    gc = jnp.concatenate(parts, axis=0) if len(parts) > 1 else parts[0]
    gr_ref[0] = gc.T[0:2 * MLSTM_HEADS, :]
    gc_ref[0] = gc - pltpu.roll(gc, LANES - MLSTM_HEADS, 1)


def _mlstm_front(x, pre_g, w_in, gate_bias, conv_w, conv_b, w_q, w_k, w_v, head_g):
    B, S, D = x.shape
    tm = min(ROW_TILE, S)
    chunk = min(MLSTM_CHUNK, S)
    H = MLSTM_HEADS
    n_if = 2 * H
    o_if = MIX_WIDTH
    o_o = o_if + n_if
    w_u = w_in[:, :o_if].astype(BF16)
    w_rest = jnp.concatenate([w_in[:, o_o:], w_in[:, o_if:o_o], jnp.zeros((D, LANES - n_if), w_in.dtype)],
                             axis=1).astype(BF16)
    wqk = jnp.zeros((H, MXU_DIM, 2 * MLSTM_PAD_QK), F32)
    wv = jnp.zeros((H, MXU_DIM, MLSTM_PAD_V), F32)
    for h in range(H):
        off = h * MLSTM_V_DIM - MLSTM_HEAD_START[h]
        wqk = wqk.at[h, off:off + MLSTM_V_DIM, 0:MLSTM_QK_DIM].set(w_q[h])
        wqk = wqk.at[h, off:off + MLSTM_V_DIM, MLSTM_PAD_QK:MLSTM_PAD_QK + MLSTM_QK_DIM].set(w_k[h])
        wv = wv.at[h, off:off + MLSTM_V_DIM, 0:MLSTM_V_DIM].set(w_v[h])
    bias = jnp.pad(gate_bias, (0, LANES - n_if)).reshape(1, LANES)
    row = lambda b, i: (b, i, 0)
    head_row = lambda b, i: (b, 0, i, 0)
    head_col = lambda b, i: (b, 0, 0, i)
    out_shape = [
        jax.ShapeDtypeStruct((B, S, MIX_WIDTH), BF16),
        jax.ShapeDtypeStruct((B, S, MEM_WIDTH), BF16),
        jax.ShapeDtypeStruct((B, S, 1024), BF16),
        jax.ShapeDtypeStruct((B, S, MIX_WIDTH), BF16),
        jax.ShapeDtypeStruct((B, H, MLSTM_PAD_QK, S), BF16),
        jax.ShapeDtypeStruct((B, H, S, MLSTM_PAD_QK), BF16),
        jax.ShapeDtypeStruct((B, H, MLSTM_PAD_V, S), BF16),
        jax.ShapeDtypeStruct((B, S, LANES), F32),
        jax.ShapeDtypeStruct((B, 2 * H, S), F32),
    ]
    out_specs = [
        pl.BlockSpec((1, tm, MIX_WIDTH), row),
        pl.BlockSpec((1, tm, MEM_WIDTH), row),
        pl.BlockSpec((1, tm, 1024), row),
        pl.BlockSpec((1, tm, MIX_WIDTH), row),
        pl.BlockSpec((1, H, MLSTM_PAD_QK, tm), head_col),
        pl.BlockSpec((1, H, tm, MLSTM_PAD_QK), head_row),
        pl.BlockSpec((1, H, MLSTM_PAD_V, tm), head_col),
        pl.BlockSpec((1, tm, LANES), row),
        pl.BlockSpec((1, 2 * H, tm), lambda b, i: (b, 0, i)),
    ]
    return pl.pallas_call(
        functools.partial(_mlstm_front_kernel, chunk=chunk),
        grid=(B, S // tm),
        in_specs=[pl.BlockSpec((1, tm, D), row), _const_spec((1, D)), _const_spec(w_u.shape),
                  _const_spec(w_rest.shape), _const_spec((CONV_WIDTH, MIX_WIDTH)), _const_spec((1, MIX_WIDTH)),
                  _const_spec((1, LANES)), _const_spec((1, MIX_WIDTH)), _const_spec(wqk.shape),
                  _const_spec(wv.shape)],
        out_specs=out_specs,
        out_shape=out_shape,
        scratch_shapes=[pltpu.VMEM((CONV_HALO, MIX_WIDTH), F32)],
        compiler_params=_params("parallel", "arbitrary"),
        name="mlstm_front",
    )(x, pre_g.reshape(1, D), w_u, w_rest, conv_w, conv_b.reshape(1, -1), bias, head_g.reshape(1, MIX_WIDTH),
      wqk.astype(BF16), wv.astype(BF16))


def _mlstm_kernel(qt_ref, k_ref, vt_ref, gc_ref, gr_ref, o_ref, c_sc, m_sc, *, chunk):
    H = k_ref.shape[1]
    seg = k_ref.shape[2]
    L = chunk

    @pl.when(pl.program_id(1) == 0)
    def _():
        c_sc[...] = jnp.zeros(c_sc.shape, F32)
        m_sc[...] = jnp.zeros(m_sc.shape, F32)

    causal = lax.broadcasted_iota(jnp.int32, (L, L), 0) <= lax.broadcasted_iota(jnp.int32, (L, L), 1)

    def step(c, carry):
        rows = pl.ds(pl.multiple_of(c * L, L), L)
        qts = [qt_ref[0, h, :, rows] for h in range(H)]
        ks = [k_ref[0, h, rows, :] for h in range(H)]
        vts = [vt_ref[0, h, :, rows] for h in range(H)]
        qk = [_dot(ks[h], qts[h]) for h in range(H)]
        cq = [_dot(carry[h][0].astype(BF16), qts[h]) for h in range(H)]
        out, normed = [], []
        for h in range(H):
            C, m = carry[h]
            c_col = gc_ref[0, rows, h:h + 1]
            li_row = gr_ref[0, h:h + 1, rows]
            b_row = gr_ref[0, MLSTM_HEADS + h:MLSTM_HEADS + h + 1, rows]
            d = jnp.where(causal, c_col + b_row, -jnp.inf)
            inter = b_row + m
            m_t = jnp.maximum(inter, jnp.max(d, axis=0, keepdims=True))
            sqk = (qk[h] * jnp.exp(d - m_t)).astype(BF16)
            num = jnp.exp(inter - m_t) * cq[h] + _dot(vts[h], sqk)
            den = jnp.maximum(jnp.abs(num[MLSTM_V_DIM:MLSTM_V_DIM + 1, :]), jnp.exp(-m_t))
            hnum = num[0:MLSTM_V_DIM, :]
            inv_den = 1.0 / den
            ms = jnp.sum(hnum * hnum, axis=0, keepdims=True) * (inv_den * inv_den * (1.0 / MLSTM_V_DIM))
            normed.append(hnum * (inv_den * lax.rsqrt(ms + EPS)))

            b_last = b_row[:, L - 1:L]
            dec = b_last - b_row + li_row
            m_new = jnp.maximum(b_last + m, jnp.max(dec, axis=1, keepdims=True))
            vw = (vts[h].astype(F32) * jnp.exp(dec - m_new)).astype(BF16)
            out.append((jnp.exp(b_last + m - m_new) * C + _dot(vw, ks[h]), m_new))
        o_ref[0, rows, :] = jnp.concatenate(normed, axis=0).T.astype(o_ref.dtype)
        return tuple(out)

    init = tuple((c_sc[h], m_sc[h, 0:1, 0:1]) for h in range(H))
    final = lax.fori_loop(0, seg // L, step, init)
    for h in range(H):
        c_sc[h] = final[h][0]
        m_sc[h] = jnp.broadcast_to(final[h][1], m_sc.shape[1:])


def _mlstm(qt, k, vt, gc, gr):
    B, H, S, _ = k.shape
    chunk = min(MLSTM_CHUNK, S)
    seg = min(MLSTM_SEGMENT, S)
    head_row = lambda b, i: (b, 0, i, 0)
    head_col = lambda b, i: (b, 0, 0, i)
    return pl.pallas_call(
        functools.partial(_mlstm_kernel, chunk=chunk),
        grid=(B, S // seg),
        in_specs=[pl.BlockSpec((1, H, MLSTM_PAD_QK, seg), head_col),
                  pl.BlockSpec((1, H, seg, MLSTM_PAD_QK), head_row),
                  pl.BlockSpec((1, H, MLSTM_PAD_V, seg), head_col),
                  pl.BlockSpec((1, seg, LANES), lambda b, i: (b, i, 0)),
                  pl.BlockSpec((1, 2 * H, seg), lambda b, i: (b, 0, i))],
        out_specs=pl.BlockSpec((1, seg, H * MLSTM_V_DIM), lambda b, i: (b, i, 0)),
        out_shape=jax.ShapeDtypeStruct((B, S, H * MLSTM_V_DIM), BF16),
        scratch_shapes=[pltpu.VMEM((H, MLSTM_PAD_V, MLSTM_PAD_QK), F32), pltpu.VMEM((H, 8, LANES), F32)],
        compiler_params=_params("parallel", "arbitrary"),
        name="mlstm_scan",
    )(qt, k, vt, gc, gr)


def _mla_layer(x, mem, rope_operands, pre_g, w_in, q_a_g, w_uq, kv_a_g, w_ukv, mem_g, w_mem_kv, w_out, post_g):
    q, k, v, qmem, gate = _mla_in(x, rope_operands, pre_g, w_in, q_a_g, w_uq, kv_a_g, w_ukv)
    mix = _attention(q, k, v)
    kexp, vexp = _mem_kv(mem, mem_g, w_mem_kv)
    specs = lambda tm: [pl.BlockSpec((1, tm, MIX_WIDTH), lambda b, i: (b, i, 0))]
    return _layer_out(_mla_out_kernel, "mla_out", x, [mix], specs, qmem, gate, kexp, vexp, w_out, post_g)


def _mlstm_layer(x, mem, pre_g, w_in, gate_bias, conv_w, conv_b, w_q, w_k, w_v, head_g, skip,
                 mem_g, w_mem_kv, w_out, post_g):
    og, qmem, gate, uc, qt, k, vt, gc, gr = _mlstm_front(x, pre_g, w_in, gate_bias, conv_w, conv_b,
                                                         w_q, w_k, w_v, head_g)
    hn = _mlstm(qt, k, vt, gc, gr)
    kexp, vexp = _mem_kv(mem, mem_g, w_mem_kv)
    row = lambda b, i: (b, i, 0)
    specs = lambda tm: [pl.BlockSpec((1, tm, MIX_WIDTH), row),
                        pl.BlockSpec((1, tm, MIX_WIDTH), row), pl.BlockSpec((1, tm, MIX_WIDTH), row),
                        _const_spec((1, MIX_WIDTH))]
    return _layer_out(_mlstm_out_kernel, "mlstm_out", x, [hn, og, uc, skip.reshape(1, -1)], specs,
                      qmem, gate, kexp, vexp, w_out, post_g)


def kernel(x, mem, positions, a_pre_g, a_w_in, a_q_a_g, a_w_uq, a_kv_a_g, a_w_ukv, a_mem_g, a_w_mem_kv, a_w_out, a_post_g, b_pre_g, b_w_in, b_gate_bias, b_conv_w, b_conv_b, b_w_q, b_w_k, b_w_v, b_head_g, b_skip, b_mem_g, b_w_mem_kv, b_w_out, b_post_g):
    depth = a_pre_g.shape[0] + b_pre_g.shape[0]
    rope_operands = _rope_operands(positions)
    for i in range(depth):
        j = i // 2
        if i % 2 == 0:
            x = _mla_layer(x, mem, rope_operands, a_pre_g[j], a_w_in[j], a_q_a_g[j], a_w_uq[j], a_kv_a_g[j],
                           a_w_ukv[j], a_mem_g[j], a_w_mem_kv[j], a_w_out[j], a_post_g[j])
        else:
            x = _mlstm_layer(x, mem, b_pre_g[j], b_w_in[j], b_gate_bias[j], b_conv_w[j], b_conv_b[j],
                             b_w_q[j], b_w_k[j], b_w_v[j], b_head_g[j], b_skip[j], b_mem_g[j],
                             b_w_mem_kv[j], b_w_out[j], b_post_g[j])
    return x
```

```python
import functools

import jax
import jax.numpy as jnp
import numpy as np
from jax import lax
from jax.experimental import pallas as pl
from jax.experimental.pallas import tpu as pltpu

EPS = 1e-6
ROPE_THETA = 10000.0
MEM_HEADS = 4
MEM_HEAD_DIM = 64
MEM_WIDTH = MEM_HEADS * MEM_HEAD_DIM
QK_NOPE_DIM = 128
QK_ROPE_DIM = 64
V_HEAD_DIM = 128
MLA_HEADS = 6
V_EXT_DIM = V_HEAD_DIM + 16
Q_LORA_RANK = 384
KV_LORA_RANK = 256
MLSTM_HEADS = 4
MLSTM_V_DIM = 192
MLSTM_QK_DIM = 96
CONV_WIDTH = 4
MIX_WIDTH = 768

LANES = 128
MXU_DIM = 256
VMEM_LIMIT_BYTES = 56 * 1024 * 1024

ROW_TILE = 512
IN_SUB_ROWS = 256
TAIL_ROW_TILE = 1024
TAIL_SUB_ROWS = 256
ATTN_TQ = 1024
ATTN_CW = 256
ATTN_TK = 1024
ATTN_HEADS_PER_STEP = 2
ATTN_LOOKAHEAD = 5
MLSTM_CHUNK = 256
MLSTM_SEGMENT = 1024

LOG2E = 1.4426950408889634
MLSTM_PAD_QK = LANES
MLSTM_PAD_V = MXU_DIM
MLSTM_HEAD_START = tuple((h * MLSTM_V_DIM // LANES) * LANES for h in range(MLSTM_HEADS))

F32 = jnp.float32
BF16 = jnp.bfloat16


def _dot(a, b):
    return jnp.dot(a, b, preferred_element_type=F32)


def _dot_nt(a, b):
    return lax.dot_general(a, b, (((1,), (1,)), ((), ())), preferred_element_type=F32)


def _rms(x, g, width=None):
    width = x.shape[-1] if width is None else width
    ms = jnp.sum(x * x, axis=-1, keepdims=True) * (1.0 / width)
    return x * lax.rsqrt(ms + EPS) * g


def _split3(x):
    b1 = x.astype(BF16)
    r1 = x - b1.astype(F32)
    b2 = r1.astype(BF16)
    b3 = (r1 - b2.astype(F32)).astype(BF16)
    return b1, b2, b3


def _params(*semantics):
    return pltpu.CompilerParams(dimension_semantics=semantics, vmem_limit_bytes=VMEM_LIMIT_BYTES)


def _const_spec(shape):
    zeros = (0,) * len(shape)
    return pl.BlockSpec(shape, lambda *_: zeros)


ROPE_HALF = QK_ROPE_DIM // 2
ROPE_PER_ROW = LANES // ROPE_HALF


def _rope_operands(positions):
    B, S = positions.shape
    rows = B * S // ROPE_PER_ROW
    inv_freq = ROPE_THETA ** (-jnp.arange(0, QK_ROPE_DIM, 2, dtype=F32) / QK_ROPE_DIM)
    pos4 = jnp.repeat(positions.reshape(rows, ROPE_PER_ROW), ROPE_HALF, axis=1)
    invf = jnp.tile(inv_freq, ROPE_PER_ROW).reshape(1, LANES)
    sel = np.zeros((2 * LANES, ROPE_PER_ROW * LANES), np.float32)
    for t in range(ROPE_PER_ROW):
        for f in range(ROPE_HALF):
            sel[t * ROPE_HALF + f, t * LANES + f] = 1.0
            sel[t * ROPE_HALF + f, t * LANES + ROPE_HALF + f] = 1.0
            sel[LANES + t * ROPE_HALF + f, t * LANES + 2 * ROPE_HALF + f] = -1.0
            sel[LANES + t * ROPE_HALF + f, t * LANES + 3 * ROPE_HALF + f] = 1.0
    return pos4, invf, jnp.asarray(sel, BF16)


def _rope_table(pos_ref, invf_ref, sel_ref, cs_sc):
    ang = pos_ref[...].astype(F32) * invf_ref[...]
    trig = jnp.concatenate([jnp.cos(ang), jnp.sin(ang)], axis=1)
    spread = sum(_dot(term, sel_ref[...]) for term in _split3(trig))
    rows = trig.shape[0]
    for t in range(ROPE_PER_ROW):
        cs_sc[pl.ds(t, rows, stride=ROPE_PER_ROW), :] = spread[:, t * LANES:(t + 1) * LANES]
    return cs_sc[...]


def _mem_kv_kernel(mem_ref, g_ref, w_ref, k_ref, v_ref):
    n_mem = mem_ref.shape[1]
    hn = _rms(mem_ref[0], g_ref[...]).astype(BF16)
    kv = _dot(hn, w_ref[...])
    k = kv[:, :MEM_WIDTH] * (MEM_HEAD_DIM ** -0.5 * LOG2E)
    v = kv[:, MEM_WIDTH:]
    col_head = lax.broadcasted_iota(jnp.int32, (n_mem, MEM_WIDTH), 1) // MEM_HEAD_DIM
    for h in range(MEM_HEADS):
        rows = pl.ds(h * n_mem, n_mem)
        k_ref[0, rows, :] = jnp.where(col_head == h, k, 0.0).astype(BF16)
        v_ref[0, rows, :] = jnp.where(col_head == h, v, 0.0).astype(BF16)


def _mem_kv(mem, mem_g, w_mem_kv):
    B, n_mem, D = mem.shape
    out = jax.ShapeDtypeStruct((B, MEM_HEADS * n_mem, MEM_WIDTH), BF16)
    spec = pl.BlockSpec((1, MEM_HEADS * n_mem, MEM_WIDTH), lambda b: (b, 0, 0))
    return pl.pallas_call(
        _mem_kv_kernel,
        grid=(B,),
        in_specs=[pl.BlockSpec((1, n_mem, D), lambda b: (b, 0, 0)), _const_spec((1, D)),
                  _const_spec((D, 2 * MEM_WIDTH))],
        out_specs=[spec, spec],
        out_shape=[out, out],
        compiler_params=_params("parallel"),
        name="mem_kv",
    )(mem, mem_g.reshape(1, D), w_mem_kv.astype(BF16))


A_CQ = (0, Q_LORA_RANK)
A_CKV = (A_CQ[1], A_CQ[1] + KV_LORA_RANK)
A_QMEM = (A_CKV[1], A_CKV[1] + MEM_WIDTH)
A_GATE = (A_QMEM[1], A_QMEM[1] + 1024)
A_KROPE = (A_GATE[1], A_GATE[1] + 2 * QK_ROPE_DIM)
Q_ROPE_OFF = MLA_HEADS * QK_NOPE_DIM


def _mla_in_kernel(x_ref, pos_ref, invf_ref, sel_ref, pre_g_ref, w_in_ref, qa_g_ref, w_uq_ref, kva_g_ref,
                   w_ukv_ref, q_ref, k_ref, v_ref, qmem_ref, gate_ref, cs_sc):
    q_scale = (QK_NOPE_DIM + QK_ROPE_DIM) ** -0.5 * LOG2E
    tm = x_ref.shape[1]
    sub = min(IN_SUB_ROWS, tm)
    blocks = [slice(i * sub, (i + 1) * sub) for i in range(tm // sub)]
    projected = [_dot(_rms(x_ref[0, r, :], pre_g_ref[...]).astype(BF16), w_in_ref[...]) for r in blocks]
    cs_tile = _rope_table(pos_ref, invf_ref, sel_ref, cs_sc)
    pad_row = lax.broadcasted_iota(jnp.int32, (V_EXT_DIM - V_HEAD_DIM, sub), 0)
    ones_row = jnp.where(pad_row == 0, 1.0, 0.0).astype(BF16)
    for r, p in zip(blocks, projected):
        cs = cs_tile[r, :]
        qmem_ref[0, r, :] = p[:, A_QMEM[0]:A_QMEM[1]].astype(BF16)
        gate = p[:, A_GATE[0]:A_GATE[1]]
        gate_ref[0, r, :] = (gate * jax.nn.sigmoid(gate)).astype(BF16)

        c_q = _rms(p[:, A_CQ[0]:A_CQ[1]], qa_g_ref[...]).astype(BF16)
        q = _dot(c_q, w_uq_ref[...])
        c_kv = _rms(p[:, A_CKV[0]:A_CKV[1]], kva_g_ref[...]).astype(BF16)
        kv = _dot(c_kv, w_ukv_ref[...])

        kr = p[:, A_KROPE[0]:A_KROPE[1]] * cs
        k_rot = (kr + pltpu.roll(kr, QK_ROPE_DIM, 1)).astype(BF16)
        for hd in range(MLA_HEADS):
            nope = slice(hd * QK_NOPE_DIM, (hd + 1) * QK_NOPE_DIM)
            rope = slice(Q_ROPE_OFF + hd * LANES, Q_ROPE_OFF + (hd + 1) * LANES)
            q_ref[0, hd, 0:LANES, r] = (q[:, nope] * q_scale).T.astype(BF16)
            q_ref[0, hd, LANES:2 * LANES, r] = (q[:, rope] * cs * q_scale).T.astype(BF16)
            k_ref[0, hd, r, 0:LANES] = kv[:, 2 * hd * LANES:(2 * hd + 1) * LANES].astype(BF16)
            k_ref[0, hd, r, LANES:2 * LANES] = k_rot
            v_ref[0, hd, 0:V_HEAD_DIM, r] = kv[:, (2 * hd + 1) * LANES:(2 * hd + 2) * LANES].T.astype(BF16)
            v_ref[0, hd, V_HEAD_DIM:V_EXT_DIM, r] = ones_row


def _rope_cols(w, start):
    half = QK_ROPE_DIM // 2
    return [w[:, start:start + QK_ROPE_DIM], w[:, start + half:start + QK_ROPE_DIM], w[:, start:start + half]]


def _mla_in(x, rope_operands, pre_g, w_in, q_a_g, w_uq, kv_a_g, w_ukv):
    B, S, D = x.shape
    tm = min(ROW_TILE, S)
    pos4, invf, sel = rope_operands
    pos_rows = tm // ROPE_PER_ROW
    tiles_per_seq = S // tm
    o_kr = Q_LORA_RANK + KV_LORA_RANK
    o_qm = o_kr + QK_ROPE_DIM
    w_in_p = jnp.concatenate([w_in[:, :o_kr], w_in[:, o_qm:]] + _rope_cols(w_in, o_kr), axis=1).astype(BF16)
    head_w = QK_NOPE_DIM + QK_ROPE_DIM
    uq_cols = [w_uq[:, h * head_w:h * head_w + QK_NOPE_DIM] for h in range(MLA_HEADS)]
    for h in range(MLA_HEADS):
        uq_cols += _rope_cols(w_uq, h * head_w + QK_NOPE_DIM)
    w_uq_p = jnp.concatenate(uq_cols, axis=1).astype(BF16)
    w_ukv_p = w_ukv.astype(BF16)

    row = lambda b, i: (b, i, 0)
    head_row = lambda b, i: (b, 0, i, 0)
    head_col = lambda b, i: (b, 0, 0, i)
    out_shape = [
        jax.ShapeDtypeStruct((B, MLA_HEADS, 2 * LANES, S), BF16),
        jax.ShapeDtypeStruct((B, MLA_HEADS, S, 2 * LANES), BF16),
        jax.ShapeDtypeStruct((B, MLA_HEADS, V_EXT_DIM, S), BF16),
        jax.ShapeDtypeStruct((B, S, MEM_WIDTH), BF16),
        jax.ShapeDtypeStruct((B, S, 1024), BF16),
    ]
    out_specs = [
        pl.BlockSpec((1, MLA_HEADS, 2 * LANES, tm), head_col),
        pl.BlockSpec((1, MLA_HEADS, tm, 2 * LANES), head_row),
        pl.BlockSpec((1, MLA_HEADS, V_EXT_DIM, tm), head_col),
        pl.BlockSpec((1, tm, MEM_WIDTH), row),
        pl.BlockSpec((1, tm, 1024), row),
    ]
    return pl.pallas_call(
        _mla_in_kernel,
        grid=(B, S // tm),
        in_specs=[pl.BlockSpec((1, tm, D), row),
                  pl.BlockSpec((pos_rows, LANES), lambda b, i: (b * tiles_per_seq + i, 0)),
                  _const_spec(invf.shape), _const_spec(sel.shape),
                  _const_spec((1, D)), _const_spec(w_in_p.shape),
                  _const_spec((1, Q_LORA_RANK)), _const_spec(w_uq_p.shape),
                  _const_spec((1, KV_LORA_RANK)), _const_spec(w_ukv_p.shape)],
        out_specs=out_specs,
        out_shape=out_shape,
        scratch_shapes=[pltpu.VMEM((tm, LANES), F32)],
        compiler_params=_params("parallel", "parallel"),
        name="mla_in",
    )(x, pos4, invf, sel, pre_g.reshape(1, D), w_in_p, q_a_g.reshape(1, -1), w_uq_p, kv_a_g.reshape(1, -1),
      w_ukv_p)


def _attn_kernel(qt_ref, k_ref, vt_ref, o_ref, *, tq, tk, cw):
    heads = k_ref.shape[1]
    S = k_ref.shape[2]
    dve = vt_ref.shape[2]
    dv = V_HEAD_DIM
    nc = tq // cw
    causal = (lax.broadcasted_iota(jnp.int32, (cw, cw), 0) <= lax.broadcasted_iota(jnp.int32, (cw, cw), 1))

    def q_tile(j, _):
        q0 = pl.multiple_of(j * tq, tq)
        qts = [[qt_ref[0, h, :, pl.ds(q0 + c * cw, cw)] for c in range(nc)] for h in range(heads)]

        def softmax_pv(carry, s, vt_tile):
            m, acc = carry
            m_new = jnp.maximum(m, jnp.max(s, axis=0, keepdims=True))
            p = jnp.exp2(s - m_new).astype(BF16)
            return m_new, jnp.exp2(m - m_new) * acc + _dot(vt_tile, p)

        def body(ki, carries):
            rows = pl.ds(pl.multiple_of(ki * tk, tk), tk)
            k_tiles = [k_ref[0, h, rows, :] for h in range(heads)]
            vt_tiles = [vt_ref[0, h, :, rows] for h in range(heads)]
            chains = [(h, c) for c in range(nc) for h in range(heads)]
            scores, out = {}, {}
            for i in range(len(chains) + ATTN_LOOKAHEAD):
                if i < len(chains):
                    h, c = chains[i]
                    scores[h, c] = _dot(k_tiles[h], qts[h][c])
                if i >= ATTN_LOOKAHEAD:
                    h, c = chains[i - ATTN_LOOKAHEAD]
                    out[h, c] = softmax_pv(carries[h][c], scores.pop((h, c)), vt_tiles[h])
            return tuple(tuple(out[h, c] for c in range(nc)) for h in range(heads))

        init = tuple(tuple((jnp.full((1, cw), -jnp.inf, F32), jnp.zeros((dve, cw), F32)) for _ in range(nc))
                     for _ in range(heads))
        carries = lax.fori_loop(0, j * (tq // tk), body, init)

        chains = [(h, c) for h in range(heads) for c in range(nc)]
        scores = {}
        for h, c in chains:
            s = _dot(k_ref[0, h, pl.ds(q0, (c + 1) * cw), :], qts[h][c])
            s_diag = jnp.where(causal, s[c * cw:, :], -jnp.inf)
            scores[h, c] = jnp.concatenate([s[:c * cw, :], s_diag], axis=0) if c > 0 else s_diag
        for h, c in chains:
            _, acc = softmax_pv(carries[h][c], scores[h, c], vt_ref[0, h, :, pl.ds(q0, (c + 1) * cw)])
            out = acc[0:dv, :] / acc[dv:dv + 1, :]
            o_ref[0, pl.ds(q0 + c * cw, cw), h * dv:(h + 1) * dv] = out.T.astype(o_ref.dtype)
        return 0

    lax.fori_loop(0, S // tq, q_tile, 0)


def _attention(qt, k, vt):
    B, H, S, dqk = k.shape
    dve = vt.shape[2]
    tq = min(ATTN_TQ, S)
    tk = min(ATTN_TK, tq)
    cw = min(ATTN_CW, tq)
    hp = ATTN_HEADS_PER_STEP
    head = lambda b, h: (b, h, 0, 0)
    return pl.pallas_call(
        functools.partial(_attn_kernel, tq=tq, tk=tk, cw=cw),
        grid=(B, H // hp),
        in_specs=[pl.BlockSpec((1, hp, dqk, S), head), pl.BlockSpec((1, hp, S, dqk), head),
                  pl.BlockSpec((1, hp, dve, S), head)],
        out_specs=pl.BlockSpec((1, S, hp * V_HEAD_DIM), lambda b, h: (b, 0, h)),
        out_shape=jax.ShapeDtypeStruct((B, S, H * V_HEAD_DIM), BF16),
        compiler_params=_params("parallel", "parallel"),
        name="mla_attention",
    )(qt, k, vt)


def _memory_probs(s, n_mem):
    probs = []
    for h in range(MEM_HEADS):
        sh = s[:, h * n_mem:(h + 1) * n_mem]
        e = jnp.exp2(sh - jnp.max(sh, axis=-1, keepdims=True))
        probs.append((e / jnp.sum(e, axis=-1, keepdims=True)).astype(BF16))
    return jnp.concatenate(probs, axis=-1)


def _tail(gated_mix, x_ref, qmem_ref, gate_ref, kexp_ref, vexp_ref, w_out_ref, post_g_ref, o_ref):
    tm = x_ref.shape[1]
    sub = min(TAIL_SUB_ROWS, tm)
    blocks = [slice(i * sub, (i + 1) * sub) for i in range(tm // sub)]
    n_mem = kexp_ref.shape[1] // MEM_HEADS
    kexp = kexp_ref[0]
    scores = [_dot_nt(qmem_ref[0, r, :], kexp) for r in blocks]
    for r, s in zip(blocks, scores):
        mo = _dot(_memory_probs(s, n_mem), vexp_ref[0])
        y_mix = gated_mix(r, gate_ref[0, r, 0:MIX_WIDTH])
        y_mem = (mo * gate_ref[0, r, MIX_WIDTH:].astype(F32)).astype(BF16)
        y = _dot(y_mix, w_out_ref[0:MIX_WIDTH, :]) + _dot(y_mem, w_out_ref[MIX_WIDTH:, :])
        o_ref[0, r, :] = x_ref[0, r, :] + _rms(y, post_g_ref[...])


def _mla_out_kernel(x_ref, mix_ref, qmem_ref, gate_ref, kexp_ref, vexp_ref, w_out_ref, post_g_ref, o_ref):
    gated_mix = lambda r, gate: mix_ref[0, r, :] * gate
    _tail(gated_mix, x_ref, qmem_ref, gate_ref, kexp_ref, vexp_ref, w_out_ref, post_g_ref, o_ref)


def _mlstm_out_kernel(x_ref, hn_ref, og_ref, uc_ref, skip_ref, qmem_ref, gate_ref, kexp_ref, vexp_ref,
                      w_out_ref, post_g_ref, o_ref):
    def gated_mix(r, gate):
        mix = og_ref[0, r, :].astype(F32) * hn_ref[0, r, :].astype(F32) + skip_ref[...] * uc_ref[0, r, :].astype(F32)
        return (mix * gate.astype(F32)).astype(BF16)

    _tail(gated_mix, x_ref, qmem_ref, gate_ref, kexp_ref, vexp_ref, w_out_ref, post_g_ref, o_ref)


def _layer_out(kernel_fn, name, x, mixer_inputs, mixer_specs, qmem, gate, kexp, vexp, w_out, post_g):
    B, S, D = x.shape
    tm = min(TAIL_ROW_TILE, S)
    row = lambda b, i: (b, i, 0)
    per_batch = lambda b, i: (b, 0, 0)
    in_specs = ([pl.BlockSpec((1, tm, D), row)] + mixer_specs(tm) +
                [pl.BlockSpec((1, tm, MEM_WIDTH), row), pl.BlockSpec((1, tm, gate.shape[-1]), row),
                 pl.BlockSpec((1,) + kexp.shape[1:], per_batch), pl.BlockSpec((1,) + vexp.shape[1:], per_batch),
                 _const_spec(w_out.shape), _const_spec((1, D))])
    return pl.pallas_call(
        kernel_fn,
        grid=(B, S // tm),
        in_specs=in_specs,
        out_specs=pl.BlockSpec((1, tm, D), row),
        out_shape=jax.ShapeDtypeStruct((B, S, D), F32),
        compiler_params=_params("parallel", "parallel"),
        name=name,
    )(x, *mixer_inputs, qmem, gate, kexp, vexp, w_out.astype(BF16), post_g.reshape(1, D))


B_O = (0, MIX_WIDTH)
B_QMEM = (B_O[1], B_O[1] + MEM_WIDTH)
B_GATE = (B_QMEM[1], B_QMEM[1] + 1024)
B_IF = (B_GATE[1], B_GATE[1] + LANES)
CONV_HALO = 8
V_ONES_ROWS = 16


def _mlstm_front_kernel(x_ref, pre_g_ref, w_u_ref, w_rest_ref, conv_w_ref, conv_b_ref, bias_ref, hg_ref, wqk_ref,
                        wv_ref, og_ref, qmem_ref, gate_ref, uc_ref, qt_ref, k_ref, vt_ref, gc_ref, gr_ref,
                        tail_sc, *, chunk):
    tm = x_ref.shape[1]
    blocks = [slice(j * chunk, (j + 1) * chunk) for j in range(tm // chunk)]

    @pl.when(pl.program_id(1) == 0)
    def _():
        tail_sc[...] = jnp.zeros(tail_sc.shape, F32)

    hs = [_rms(x_ref[0, r, :], pre_g_ref[...]).astype(BF16) for r in blocks]
    us = [_dot(h, w_u_ref[...]) for h in hs]
    ps = [_dot(h, w_rest_ref[...]) for h in hs]
    halos = [tail_sc[...]] + [u[chunk - CONV_HALO:chunk, :] for u in us[:-1]]
    tail_sc[...] = us[-1][chunk - CONV_HALO:chunk, :]

    k_scale = MLSTM_QK_DIM ** -0.5
    ones_rows = jnp.where(lax.broadcasted_iota(jnp.int32, (V_ONES_ROWS, chunk), 0) == 0, 1.0, 0.0).astype(BF16)
    lane = lax.broadcasted_iota(jnp.int32, (chunk, LANES), 1)
    is_f = (lane >= MLSTM_HEADS) & (lane < 2 * MLSTM_HEADS)
    tril = (lax.broadcasted_iota(jnp.int32, (chunk, chunk), 1)
            <= lax.broadcasted_iota(jnp.int32, (chunk, chunk), 0)).astype(BF16)

    for r, u, p, halo in zip(blocks, us, ps, halos):
        ext = jnp.concatenate([halo, u], axis=0)
        conv = conv_b_ref[...] + u * conv_w_ref[CONV_WIDTH - 1:CONV_WIDTH, :]
        for back in range(1, CONV_WIDTH):
            tap = CONV_WIDTH - 1 - back
            conv = conv + ext[CONV_HALO - back:CONV_HALO - back + chunk, :] * conv_w_ref[tap:tap + 1, :]
        uc_bf = (conv * jax.nn.sigmoid(conv)).astype(BF16)
        uc_ref[0, r, :] = uc_bf
        u_bf = u.astype(BF16)

        og_ref[0, r, :] = (jax.nn.sigmoid(p[:, B_O[0]:B_O[1]]) * hg_ref[...]).astype(BF16)
        qmem_ref[0, r, :] = p[:, B_QMEM[0]:B_QMEM[1]].astype(BF16)
        gate = p[:, B_GATE[0]:B_GATE[1]]
        gate_ref[0, r, :] = (gate * jax.nn.sigmoid(gate)).astype(BF16)

        for hd in range(MLSTM_HEADS):
            cols = slice(MLSTM_HEAD_START[hd], MLSTM_HEAD_START[hd] + MXU_DIM)
            qk = _dot(uc_bf[:, cols], wqk_ref[hd])
            qt_ref[0, hd, :, r] = qk[:, :MLSTM_PAD_QK].T.astype(BF16)
            k_ref[0, hd, r, :] = (qk[:, MLSTM_PAD_QK:] * k_scale).astype(BF16)
            vt_ref[0, hd, :, r] = _dot(u_bf[:, cols], wv_ref[hd]).T.astype(BF16)
            vt_ref[0, hd, MLSTM_V_DIM:MLSTM_V_DIM + V_ONES_ROWS, r] = ones_rows

        g = p[:, B_IF[0]:B_IF[1]] + bias_ref[...]
        log_f = jnp.minimum(g, 0.0) - jnp.log1p(jnp.exp(-jnp.abs(g)))
        gates = jnp.where(lane < MLSTM_HEADS, g, jnp.where(is_f, log_f, 0.0))
        cum = sum(_dot(tril, term) for term in _split3(gates))
        gc = jnp.where(is_f, cum, gates)
        gr_ref[0, :, r] = gc.T[0:2 * MLSTM_HEADS, :]
        gc_ref[0, r, :] = gc - pltpu.roll(gc, LANES - MLSTM_HEADS, 1)


def _mlstm_front(x, pre_g, w_in, gate_bias, conv_w, conv_b, w_q, w_k, w_v, head_g):
    B, S, D = x.shape
    tm = min(ROW_TILE, S)
    chunk = min(MLSTM_CHUNK, S)
    H = MLSTM_HEADS
    n_if = 2 * H
    o_if = MIX_WIDTH
    o_o = o_if + n_if
    w_u = w_in[:, :o_if].astype(BF16)
    w_rest = jnp.concatenate([w_in[:, o_o:], w_in[:, o_if:o_o], jnp.zeros((D, LANES - n_if), w_in.dtype)],
                             axis=1).astype(BF16)
    wqk = jnp.zeros((H, MXU_DIM, 2 * MLSTM_PAD_QK), F32)
    wv = jnp.zeros((H, MXU_DIM, MLSTM_PAD_V), F32)
    for h in range(H):
        off = h * MLSTM_V_DIM - MLSTM_HEAD_START[h]
        wqk = wqk.at[h, off:off + MLSTM_V_DIM, 0:MLSTM_QK_DIM].set(w_q[h])
        wqk = wqk.at[h, off:off + MLSTM_V_DIM, MLSTM_PAD_QK:MLSTM_PAD_QK + MLSTM_QK_DIM].set(w_k[h])
        wv = wv.at[h, off:off + MLSTM_V_DIM, 0:MLSTM_V_DIM].set(w_v[h])
    bias = jnp.pad(gate_bias, (0, LANES - n_if)).reshape(1, LANES)
    row = lambda b, i: (b, i, 0)
    head_row = lambda b, i: (b, 0, i, 0)
    head_col = lambda b, i: (b, 0, 0, i)
    out_shape = [
        jax.ShapeDtypeStruct((B, S, MIX_WIDTH), BF16),
        jax.ShapeDtypeStruct((B, S, MEM_WIDTH), BF16),
        jax.ShapeDtypeStruct((B, S, 1024), BF16),
        jax.ShapeDtypeStruct((B, S, MIX_WIDTH), BF16),
        jax.ShapeDtypeStruct((B, H, MLSTM_PAD_QK, S), BF16),
        jax.ShapeDtypeStruct((B, H, S, MLSTM_PAD_QK), BF16),
        jax.ShapeDtypeStruct((B, H, MLSTM_PAD_V, S), BF16),
        jax.ShapeDtypeStruct((B, S, LANES), F32),
        jax.ShapeDtypeStruct((B, 2 * H, S), F32),
    ]
    out_specs = [
        pl.BlockSpec((1, tm, MIX_WIDTH), row),
        pl.BlockSpec((1, tm, MEM_WIDTH), row),
        pl.BlockSpec((1, tm, 1024), row),
        pl.BlockSpec((1, tm, MIX_WIDTH), row),
        pl.BlockSpec((1, H, MLSTM_PAD_QK, tm), head_col),
        pl.BlockSpec((1, H, tm, MLSTM_PAD_QK), head_row),
        pl.BlockSpec((1, H, MLSTM_PAD_V, tm), head_col),
        pl.BlockSpec((1, tm, LANES), row),
        pl.BlockSpec((1, 2 * H, tm), lambda b, i: (b, 0, i)),
    ]
    return pl.pallas_call(
        functools.partial(_mlstm_front_kernel, chunk=chunk),
        grid=(B, S // tm),
        in_specs=[pl.BlockSpec((1, tm, D), row), _const_spec((1, D)), _const_spec(w_u.shape),
                  _const_spec(w_rest.shape), _const_spec((CONV_WIDTH, MIX_WIDTH)), _const_spec((1, MIX_WIDTH)),
                  _const_spec((1, LANES)), _const_spec((1, MIX_WIDTH)), _const_spec(wqk.shape),
                  _const_spec(wv.shape)],
        out_specs=out_specs,
        out_shape=out_shape,
        scratch_shapes=[pltpu.VMEM((CONV_HALO, MIX_WIDTH), F32)],
        compiler_params=_params("parallel", "arbitrary"),
        name="mlstm_front",
    )(x, pre_g.reshape(1, D), w_u, w_rest, conv_w, conv_b.reshape(1, -1), bias, head_g.reshape(1, MIX_WIDTH),
      wqk.astype(BF16), wv.astype(BF16))


def _mlstm_kernel(qt_ref, k_ref, vt_ref, gc_ref, gr_ref, o_ref, c_sc, m_sc, *, chunk):
    H = k_ref.shape[1]
    seg = k_ref.shape[2]
    L = chunk

    @pl.when(pl.program_id(1) == 0)
    def _():
        c_sc[...] = jnp.zeros(c_sc.shape, F32)
        m_sc[...] = jnp.zeros(m_sc.shape, F32)

    causal = lax.broadcasted_iota(jnp.int32, (L, L), 0) <= lax.broadcasted_iota(jnp.int32, (L, L), 1)

    def step(c, carry):
        rows = pl.ds(pl.multiple_of(c * L, L), L)
        qts = [qt_ref[0, h, :, rows] for h in range(H)]
        ks = [k_ref[0, h, rows, :] for h in range(H)]
        vts = [vt_ref[0, h, :, rows] for h in range(H)]
        qk = [_dot(ks[h], qts[h]) for h in range(H)]
        cq = [_dot(carry[h][0].astype(BF16), qts[h]) for h in range(H)]
        out, normed = [], []
        for h in range(H):
            C, m = carry[h]
            c_col = gc_ref[0, rows, h:h + 1]
            b_row = gr_ref[0, MLSTM_HEADS + h:MLSTM_HEADS + h + 1, rows]
            li_row = gr_ref[0, h:h + 1, rows]
            d = jnp.where(causal, c_col + b_row, -jnp.inf)
            inter = b_row + m
            m_t = jnp.maximum(inter, jnp.max(d, axis=0, keepdims=True))
            sqk = (qk[h] * jnp.exp(d - m_t)).astype(BF16)
            num = jnp.exp(inter - m_t) * cq[h] + _dot(vts[h], sqk)
            den = jnp.maximum(jnp.abs(num[MLSTM_V_DIM:MLSTM_V_DIM + 1, :]), jnp.exp(-m_t))
            hnum = num[0:MLSTM_V_DIM, :]
            inv_den = 1.0 / den
            ms = jnp.sum(hnum * hnum, axis=0, keepdims=True) * (inv_den * inv_den * (1.0 / MLSTM_V_DIM))
            normed.append(hnum * (inv_den * lax.rsqrt(ms + EPS)))

            b_last = b_row[:, L - 1:L]
            dec = b_last - b_row + li_row
            m_new = jnp.maximum(b_last + m, jnp.max(dec, axis=1, keepdims=True))
            vw = (vts[h].astype(F32) * jnp.exp(dec - m_new)).astype(BF16)
            out.append((jnp.exp(b_last + m - m_new) * C + _dot(vw, ks[h]), m_new))
        o_ref[0, rows, :] = jnp.concatenate(normed, axis=0).T.astype(o_ref.dtype)
        return tuple(out)

    init = tuple((c_sc[h], m_sc[h, 0:1, 0:1]) for h in range(H))
    final = lax.fori_loop(0, seg // L, step, init)
    for h in range(H):
        c_sc[h] = final[h][0]
        m_sc[h] = jnp.broadcast_to(final[h][1], m_sc.shape[1:])


def _mlstm(qt, k, vt, gc, gr):
    B, H, S, _ = k.shape
    chunk = min(MLSTM_CHUNK, S)
    seg = min(MLSTM_SEGMENT, S)
    head_row = lambda b, i: (b, 0, i, 0)
    head_col = lambda b, i: (b, 0, 0, i)
    return pl.pallas_call(
        functools.partial(_mlstm_kernel, chunk=chunk),
        grid=(B, S // seg),
        in_specs=[pl.BlockSpec((1, H, MLSTM_PAD_QK, seg), head_col),
                  pl.BlockSpec((1, H, seg, MLSTM_PAD_QK), head_row),
                  pl.BlockSpec((1, H, MLSTM_PAD_V, seg), head_col),
                  pl.BlockSpec((1, seg, LANES), lambda b, i: (b, i, 0)),
                  pl.BlockSpec((1, 2 * H, seg), lambda b, i: (b, 0, i))],
        out_specs=pl.BlockSpec((1, seg, H * MLSTM_V_DIM), lambda b, i: (b, i, 0)),
        out_shape=jax.ShapeDtypeStruct((B, S, H * MLSTM_V_DIM), BF16),
        scratch_shapes=[pltpu.VMEM((H, MLSTM_PAD_V, MLSTM_PAD_QK), F32), pltpu.VMEM((H, 8, LANES), F32)],
        compiler_params=_params("parallel", "arbitrary"),
        name="mlstm_scan",
    )(qt, k, vt, gc, gr)


def _mla_layer(x, mem, rope_operands, pre_g, w_in, q_a_g, w_uq, kv_a_g, w_ukv, mem_g, w_mem_kv, w_out, post_g):
    q, k, v, qmem, gate = _mla_in(x, rope_operands, pre_g, w_in, q_a_g, w_uq, kv_a_g, w_ukv)
    mix = _attention(q, k, v)
    kexp, vexp = _mem_kv(mem, mem_g, w_mem_kv)
    specs = lambda tm: [pl.BlockSpec((1, tm, MIX_WIDTH), lambda b, i: (b, i, 0))]
    return _layer_out(_mla_out_kernel, "mla_out", x, [mix], specs, qmem, gate, kexp, vexp, w_out, post_g)


def _mlstm_layer(x, mem, pre_g, w_in, gate_bias, conv_w, conv_b, w_q, w_k, w_v, head_g, skip,
                 mem_g, w_mem_kv, w_out, post_g):
    og, qmem, gate, uc, qt, k, vt, gc, gr = _mlstm_front(x, pre_g, w_in, gate_bias, conv_w, conv_b,
                                                         w_q, w_k, w_v, head_g)
    hn = _mlstm(qt, k, vt, gc, gr)
    kexp, vexp = _mem_kv(mem, mem_g, w_mem_kv)
    row = lambda b, i: (b, i, 0)
    specs = lambda tm: [pl.BlockSpec((1, tm, MIX_WIDTH), row),
                        pl.BlockSpec((1, tm, MIX_WIDTH), row), pl.BlockSpec((1, tm, MIX_WIDTH), row),
                        _const_spec((1, MIX_WIDTH))]
    return _layer_out(_mlstm_out_kernel, "mlstm_out", x, [hn, og, uc, skip.reshape(1, -1)], specs,
                      qmem, gate, kexp, vexp, w_out, post_g)


def kernel(x, mem, positions, a_pre_g, a_w_in, a_q_a_g, a_w_uq, a_kv_a_g, a_w_ukv, a_mem_g, a_w_mem_kv, a_w_out, a_post_g, b_pre_g, b_w_in, b_gate_bias, b_conv_w, b_conv_b, b_w_q, b_w_k, b_w_v, b_head_g, b_skip, b_mem_g, b_w_mem_kv, b_w_out, b_post_g):
    depth = a_pre_g.shape[0] + b_pre_g.shape[0]
    rope_operands = _rope_operands(positions)
    for i in range(depth):
        j = i // 2
        if i % 2 == 0:
            x = _mla_layer(x, mem, rope_operands, a_pre_g[j], a_w_in[j], a_q_a_g[j], a_w_uq[j], a_kv_a_g[j],
                           a_w_ukv[j], a_mem_g[j], a_w_mem_kv[j], a_w_out[j], a_post_g[j])
        else:
            x = _mlstm_layer(x, mem, b_pre_g[j], b_w_in[j], b_gate_bias[j], b_conv_w[j], b_conv_b[j],
                             b_w_q[j], b_w_k[j], b_w_v[j], b_head_g[j], b_skip[j], b_mem_g[j],
                             b_w_mem_kv[j], b_w_out[j], b_post_g[j])
    return x
```

```python
import functools

import jax
import jax.numpy as jnp
import numpy as np
from jax import lax
from jax.experimental import pallas as pl
from jax.experimental.pallas import tpu as pltpu

EPS = 1e-6
ROPE_THETA = 10000.0
MEM_HEADS = 4
MEM_HEAD_DIM = 64
MEM_WIDTH = MEM_HEADS * MEM_HEAD_DIM
QK_NOPE_DIM = 128
QK_ROPE_DIM = 64
V_HEAD_DIM = 128
MLA_HEADS = 6
V_EXT_DIM = V_HEAD_DIM + 16
Q_LORA_RANK = 384
KV_LORA_RANK = 256
MLSTM_HEADS = 4
MLSTM_V_DIM = 192
MLSTM_QK_DIM = 96
CONV_WIDTH = 4
MIX_WIDTH = 768

LANES = 128
MXU_DIM = 256
VMEM_LIMIT_BYTES = 56 * 1024 * 1024

ROW_TILE = 512
IN_SUB_ROWS = 256
TAIL_ROW_TILE = 1024
TAIL_SUB_ROWS = 256
ATTN_TQ = 1024
ATTN_CW = 256
ATTN_TK = 1024
ATTN_HEADS_PER_STEP = 2
ATTN_LOOKAHEAD = 5
MLSTM_CHUNK = 256

LOG2E = 1.4426950408889634
MLSTM_PAD_QK = LANES
MLSTM_PAD_V = MXU_DIM
MLSTM_HEAD_START = tuple((h * MLSTM_V_DIM // LANES) * LANES for h in range(MLSTM_HEADS))

F32 = jnp.float32
BF16 = jnp.bfloat16


def _dot(a, b):
    return jnp.dot(a, b, preferred_element_type=F32)


def _dot_nt(a, b):
    return lax.dot_general(a, b, (((1,), (1,)), ((), ())), preferred_element_type=F32)


def _rms(x, g, width=None):
    width = x.shape[-1] if width is None else width
    ms = jnp.sum(x * x, axis=-1, keepdims=True) * (1.0 / width)
    return x * lax.rsqrt(ms + EPS) * g


def _split3(x):
    b1 = x.astype(BF16)
    r1 = x - b1.astype(F32)
    b2 = r1.astype(BF16)
    b3 = (r1 - b2.astype(F32)).astype(BF16)
    return b1, b2, b3


def _params(*semantics):
    return pltpu.CompilerParams(dimension_semantics=semantics, vmem_limit_bytes=VMEM_LIMIT_BYTES)


def _const_spec(shape):
    zeros = (0,) * len(shape)
    return pl.BlockSpec(shape, lambda *_: zeros)


ROPE_HALF = QK_ROPE_DIM // 2
ROPE_PER_ROW = LANES // ROPE_HALF


def _rope_operands(positions):
    B, S = positions.shape
    rows = B * S // ROPE_PER_ROW
    inv_freq = ROPE_THETA ** (-jnp.arange(0, QK_ROPE_DIM, 2, dtype=F32) / QK_ROPE_DIM)
    pos4 = jnp.repeat(positions.reshape(rows, ROPE_PER_ROW), ROPE_HALF, axis=1)
    invf = jnp.tile(inv_freq, ROPE_PER_ROW).reshape(1, LANES)
    sel = np.zeros((2 * LANES, ROPE_PER_ROW * LANES), np.float32)
    for t in range(ROPE_PER_ROW):
        for f in range(ROPE_HALF):
            sel[t * ROPE_HALF + f, t * LANES + f] = 1.0
            sel[t * ROPE_HALF + f, t * LANES + ROPE_HALF + f] = 1.0
            sel[LANES + t * ROPE_HALF + f, t * LANES + 2 * ROPE_HALF + f] = -1.0
            sel[LANES + t * ROPE_HALF + f, t * LANES + 3 * ROPE_HALF + f] = 1.0
    return pos4, invf, jnp.asarray(sel, BF16)


def _rope_table(pos_ref, invf_ref, sel_ref, cs_sc):
    ang = pos_ref[...].astype(F32) * invf_ref[...]
    trig = jnp.concatenate([jnp.cos(ang), jnp.sin(ang)], axis=1)
    spread = sum(_dot(term, sel_ref[...]) for term in _split3(trig))
    rows = trig.shape[0]
    for t in range(ROPE_PER_ROW):
        cs_sc[pl.ds(t, rows, stride=ROPE_PER_ROW), :] = spread[:, t * LANES:(t + 1) * LANES]
    return cs_sc[...]


def _mem_kv_kernel(mem_ref, g_ref, w_ref, k_ref, v_ref):
    n_mem = mem_ref.shape[1]
    hn = _rms(mem_ref[0], g_ref[...]).astype(BF16)
    kv = _dot(hn, w_ref[...])
    k = kv[:, :MEM_WIDTH] * (MEM_HEAD_DIM ** -0.5 * LOG2E)
    v = kv[:, MEM_WIDTH:]
    col_head = lax.broadcasted_iota(jnp.int32, (n_mem, MEM_WIDTH), 1) // MEM_HEAD_DIM
    for h in range(MEM_HEADS):
        rows = pl.ds(h * n_mem, n_mem)
        k_ref[0, rows, :] = jnp.where(col_head == h, k, 0.0).astype(BF16)
        v_ref[0, rows, :] = jnp.where(col_head == h, v, 0.0).astype(BF16)


def _mem_kv(mem, mem_g, w_mem_kv):
    B, n_mem, D = mem.shape
    out = jax.ShapeDtypeStruct((B, MEM_HEADS * n_mem, MEM_WIDTH), BF16)
    spec = pl.BlockSpec((1, MEM_HEADS * n_mem, MEM_WIDTH), lambda b: (b, 0, 0))
    return pl.pallas_call(
        _mem_kv_kernel,
        grid=(B,),
        in_specs=[pl.BlockSpec((1, n_mem, D), lambda b: (b, 0, 0)), _const_spec((1, D)),
                  _const_spec((D, 2 * MEM_WIDTH))],
        out_specs=[spec, spec],
        out_shape=[out, out],
        compiler_params=_params("parallel"),
        name="mem_kv",
    )(mem, mem_g.reshape(1, D), w_mem_kv.astype(BF16))


A_CQ = (0, Q_LORA_RANK)
A_CKV = (A_CQ[1], A_CQ[1] + KV_LORA_RANK)
A_QMEM = (A_CKV[1], A_CKV[1] + MEM_WIDTH)
A_GATE = (A_QMEM[1], A_QMEM[1] + 1024)
A_KROPE = (A_GATE[1], A_GATE[1] + 2 * QK_ROPE_DIM)
Q_ROPE_OFF = MLA_HEADS * QK_NOPE_DIM


def _mla_in_kernel(x_ref, pos_ref, invf_ref, sel_ref, pre_g_ref, w_in_ref, qa_g_ref, w_uq_ref, kva_g_ref,
                   w_ukv_ref, q_ref, k_ref, v_ref, qmem_ref, gate_ref, cs_sc):
    q_scale = (QK_NOPE_DIM + QK_ROPE_DIM) ** -0.5 * LOG2E
    tm = x_ref.shape[1]
    sub = min(IN_SUB_ROWS, tm)
    blocks = [slice(i * sub, (i + 1) * sub) for i in range(tm // sub)]
    projected = [_dot(_rms(x_ref[0, r, :], pre_g_ref[...]).astype(BF16), w_in_ref[...]) for r in blocks]
    cs_tile = _rope_table(pos_ref, invf_ref, sel_ref, cs_sc)
    pad_row = lax.broadcasted_iota(jnp.int32, (V_EXT_DIM - V_HEAD_DIM, sub), 0)
    ones_row = jnp.where(pad_row == 0, 1.0, 0.0).astype(BF16)
    for r, p in zip(blocks, projected):
        cs = cs_tile[r, :]
        qmem_ref[0, r, :] = p[:, A_QMEM[0]:A_QMEM[1]].astype(BF16)
        gate = p[:, A_GATE[0]:A_GATE[1]]
        gate_ref[0, r, :] = (gate * jax.nn.sigmoid(gate)).astype(BF16)

        c_q = _rms(p[:, A_CQ[0]:A_CQ[1]], qa_g_ref[...]).astype(BF16)
        q = _dot(c_q, w_uq_ref[...])
        c_kv = _rms(p[:, A_CKV[0]:A_CKV[1]], kva_g_ref[...]).astype(BF16)
        kv = _dot(c_kv, w_ukv_ref[...])

        kr = p[:, A_KROPE[0]:A_KROPE[1]] * cs
        k_rot = (kr + pltpu.roll(kr, QK_ROPE_DIM, 1)).astype(BF16)
        for hd in range(MLA_HEADS):
            nope = slice(hd * QK_NOPE_DIM, (hd + 1) * QK_NOPE_DIM)
            rope = slice(Q_ROPE_OFF + hd * LANES, Q_ROPE_OFF + (hd + 1) * LANES)
            q_ref[0, hd, 0:LANES, r] = (q[:, nope] * q_scale).T.astype(BF16)
            q_ref[0, hd, LANES:2 * LANES, r] = (q[:, rope] * cs * q_scale).T.astype(BF16)
            k_ref[0, hd, r, 0:LANES] = kv[:, 2 * hd * LANES:(2 * hd + 1) * LANES].astype(BF16)
            k_ref[0, hd, r, LANES:2 * LANES] = k_rot
            v_ref[0, hd, 0:V_HEAD_DIM, r] = kv[:, (2 * hd + 1) * LANES:(2 * hd + 2) * LANES].T.astype(BF16)
            v_ref[0, hd, V_HEAD_DIM:V_EXT_DIM, r] = ones_row


def _rope_cols(w, start):
    half = QK_ROPE_DIM // 2
    return [w[:, start:start + QK_ROPE_DIM], w[:, start + half:start + QK_ROPE_DIM], w[:, start:start + half]]


def _mla_in(x, rope_operands, pre_g, w_in, q_a_g, w_uq, kv_a_g, w_ukv):
    B, S, D = x.shape
    tm = min(ROW_TILE, S)
    pos4, invf, sel = rope_operands
    pos_rows = tm // ROPE_PER_ROW
    tiles_per_seq = S // tm
    o_kr = Q_LORA_RANK + KV_LORA_RANK
    o_qm = o_kr + QK_ROPE_DIM
    w_in_p = jnp.concatenate([w_in[:, :o_kr], w_in[:, o_qm:]] + _rope_cols(w_in, o_kr), axis=1).astype(BF16)
    head_w = QK_NOPE_DIM + QK_ROPE_DIM
    uq_cols = [w_uq[:, h * head_w:h * head_w + QK_NOPE_DIM] for h in range(MLA_HEADS)]
    for h in range(MLA_HEADS):
        uq_cols += _rope_cols(w_uq, h * head_w + QK_NOPE_DIM)
    w_uq_p = jnp.concatenate(uq_cols, axis=1).astype(BF16)
    w_ukv_p = w_ukv.astype(BF16)

    row = lambda b, i: (b, i, 0)
    head_row = lambda b, i: (b, 0, i, 0)
    head_col = lambda b, i: (b, 0, 0, i)
    out_shape = [
        jax.ShapeDtypeStruct((B, MLA_HEADS, 2 * LANES, S), BF16),
        jax.ShapeDtypeStruct((B, MLA_HEADS, S, 2 * LANES), BF16),
        jax.ShapeDtypeStruct((B, MLA_HEADS, V_EXT_DIM, S), BF16),
        jax.ShapeDtypeStruct((B, S, MEM_WIDTH), BF16),
        jax.ShapeDtypeStruct((B, S, 1024), BF16),
    ]
    out_specs = [
        pl.BlockSpec((1, MLA_HEADS, 2 * LANES, tm), head_col),
        pl.BlockSpec((1, MLA_HEADS, tm, 2 * LANES), head_row),
        pl.BlockSpec((1, MLA_HEADS, V_EXT_DIM, tm), head_col),
        pl.BlockSpec((1, tm, MEM_WIDTH), row),
        pl.BlockSpec((1, tm, 1024), row),
    ]
    return pl.pallas_call(
        _mla_in_kernel,
        grid=(B, S // tm),
        in_specs=[pl.BlockSpec((1, tm, D), row),
                  pl.BlockSpec((pos_rows, LANES), lambda b, i: (b * tiles_per_seq + i, 0)),
                  _const_spec(invf.shape), _const_spec(sel.shape),
                  _const_spec((1, D)), _const_spec(w_in_p.shape),
                  _const_spec((1, Q_LORA_RANK)), _const_spec(w_uq_p.shape),
                  _const_spec((1, KV_LORA_RANK)), _const_spec(w_ukv_p.shape)],
        out_specs=out_specs,
        out_shape=out_shape,
        scratch_shapes=[pltpu.VMEM((tm, LANES), F32)],
        compiler_params=_params("parallel", "parallel"),
        name="mla_in",
    )(x, pos4, invf, sel, pre_g.reshape(1, D), w_in_p, q_a_g.reshape(1, -1), w_uq_p, kv_a_g.reshape(1, -1),
      w_ukv_p)


def _attn_kernel(qt_ref, k_ref, vt_ref, o_ref, *, tq, tk, cw):
    heads = k_ref.shape[1]
    S = k_ref.shape[2]
    dve = vt_ref.shape[2]
    dv = V_HEAD_DIM
    nc = tq // cw
    causal = (lax.broadcasted_iota(jnp.int32, (cw, cw), 0) <= lax.broadcasted_iota(jnp.int32, (cw, cw), 1))

    def q_tile(j, _):
        q0 = pl.multiple_of(j * tq, tq)
        qts = [[qt_ref[0, h, :, pl.ds(q0 + c * cw, cw)] for c in range(nc)] for h in range(heads)]

        def softmax_pv(carry, s, vt_tile):
            m, acc = carry
            m_new = jnp.maximum(m, jnp.max(s, axis=0, keepdims=True))
            p = jnp.exp2(s - m_new).astype(BF16)
            return m_new, jnp.exp2(m - m_new) * acc + _dot(vt_tile, p)

        def body(ki, carries):
            rows = pl.ds(pl.multiple_of(ki * tk, tk), tk)
            k_tiles = [k_ref[0, h, rows, :] for h in range(heads)]
            vt_tiles = [vt_ref[0, h, :, rows] for h in range(heads)]
            chains = [(h, c) for c in range(nc) for h in range(heads)]
            scores, out = {}, {}
            for i in range(len(chains) + ATTN_LOOKAHEAD):
                if i < len(chains):
                    h, c = chains[i]
                    scores[h, c] = _dot(k_tiles[h], qts[h][c])
                if i >= ATTN_LOOKAHEAD:
                    h, c = chains[i - ATTN_LOOKAHEAD]
                    out[h, c] = softmax_pv(carries[h][c], scores.pop((h, c)), vt_tiles[h])
            return tuple(tuple(out[h, c] for c in range(nc)) for h in range(heads))

        init = tuple(tuple((jnp.full((1, cw), -jnp.inf, F32), jnp.zeros((dve, cw), F32)) for _ in range(nc))
                     for _ in range(heads))
        carries = lax.fori_loop(0, j * (tq // tk), body, init)

        chains = [(h, c) for h in range(heads) for c in range(nc)]
        scores = {}
        for h, c in chains:
            s = _dot(k_ref[0, h, pl.ds(q0, (c + 1) * cw), :], qts[h][c])
            s_diag = jnp.where(causal, s[c * cw:, :], -jnp.inf)
            scores[h, c] = jnp.concatenate([s[:c * cw, :], s_diag], axis=0) if c > 0 else s_diag
        for h, c in chains:
            _, acc = softmax_pv(carries[h][c], scores[h, c], vt_ref[0, h, :, pl.ds(q0, (c + 1) * cw)])
            out = acc[0:dv, :] / acc[dv:dv + 1, :]
            o_ref[0, pl.ds(q0 + c * cw, cw), h * dv:(h + 1) * dv] = out.T.astype(o_ref.dtype)
        return 0

    lax.fori_loop(0, S // tq, q_tile, 0)


def _attention(qt, k, vt):
    B, H, S, dqk = k.shape
    dve = vt.shape[2]
    tq = min(ATTN_TQ, S)
    tk = min(ATTN_TK, tq)
    cw = min(ATTN_CW, tq)
    hp = ATTN_HEADS_PER_STEP
    head = lambda b, h: (b, h, 0, 0)
    return pl.pallas_call(
        functools.partial(_attn_kernel, tq=tq, tk=tk, cw=cw),
        grid=(B, H // hp),
        in_specs=[pl.BlockSpec((1, hp, dqk, S), head), pl.BlockSpec((1, hp, S, dqk), head),
                  pl.BlockSpec((1, hp, dve, S), head)],
        out_specs=pl.BlockSpec((1, S, hp * V_HEAD_DIM), lambda b, h: (b, 0, h)),
        out_shape=jax.ShapeDtypeStruct((B, S, H * V_HEAD_DIM), BF16),
        compiler_params=_params("parallel", "parallel"),
        name="mla_attention",
    )(qt, k, vt)


def _memory_probs(s, n_mem):
    probs = []
    for h in range(MEM_HEADS):
        sh = s[:, h * n_mem:(h + 1) * n_mem]
        e = jnp.exp2(sh - jnp.max(sh, axis=-1, keepdims=True))
        probs.append((e / jnp.sum(e, axis=-1, keepdims=True)).astype(BF16))
    return jnp.concatenate(probs, axis=-1)


def _tail(gated_mix, x_ref, qmem_ref, gate_ref, kexp_ref, vexp_ref, w_out_ref, post_g_ref, o_ref):
    tm = x_ref.shape[1]
    sub = min(TAIL_SUB_ROWS, tm)
    blocks = [slice(i * sub, (i + 1) * sub) for i in range(tm // sub)]
    n_mem = kexp_ref.shape[1] // MEM_HEADS
    kexp = kexp_ref[0]
    scores = [_dot_nt(qmem_ref[0, r, :], kexp) for r in blocks]
    for r, s in zip(blocks, scores):
        mo = _dot(_memory_probs(s, n_mem), vexp_ref[0])
        y_mix = gated_mix(r, gate_ref[0, r, 0:MIX_WIDTH])
        y_mem = (mo * gate_ref[0, r, MIX_WIDTH:].astype(F32)).astype(BF16)
        y = _dot(y_mix, w_out_ref[0:MIX_WIDTH, :]) + _dot(y_mem, w_out_ref[MIX_WIDTH:, :])
        o_ref[0, r, :] = x_ref[0, r, :] + _rms(y, post_g_ref[...])


def _mla_out_kernel(x_ref, mix_ref, qmem_ref, gate_ref, kexp_ref, vexp_ref, w_out_ref, post_g_ref, o_ref):
    gated_mix = lambda r, gate: mix_ref[0, r, :] * gate
    _tail(gated_mix, x_ref, qmem_ref, gate_ref, kexp_ref, vexp_ref, w_out_ref, post_g_ref, o_ref)


def _mlstm_out_kernel(x_ref, hn_ref, og_ref, uc_ref, skip_ref, qmem_ref, gate_ref, kexp_ref, vexp_ref,
                      w_out_ref, post_g_ref, o_ref):
    def gated_mix(r, gate):
        mix = og_ref[0, r, :].astype(F32) * hn_ref[0, r, :].astype(F32) + skip_ref[...] * uc_ref[0, r, :].astype(F32)
        return (mix * gate.astype(F32)).astype(BF16)

    _tail(gated_mix, x_ref, qmem_ref, gate_ref, kexp_ref, vexp_ref, w_out_ref, post_g_ref, o_ref)


def _layer_out(kernel_fn, name, x, mixer_inputs, mixer_specs, qmem, gate, kexp, vexp, w_out, post_g):
    B, S, D = x.shape
    tm = min(TAIL_ROW_TILE, S)
    row = lambda b, i: (b, i, 0)
    per_batch = lambda b, i: (b, 0, 0)
    in_specs = ([pl.BlockSpec((1, tm, D), row)] + mixer_specs(tm) +
                [pl.BlockSpec((1, tm, MEM_WIDTH), row), pl.BlockSpec((1, tm, gate.shape[-1]), row),
                 pl.BlockSpec((1,) + kexp.shape[1:], per_batch), pl.BlockSpec((1,) + vexp.shape[1:], per_batch),
                 _const_spec(w_out.shape), _const_spec((1, D))])
    return pl.pallas_call(
        kernel_fn,
        grid=(B, S // tm),
        in_specs=in_specs,
        out_specs=pl.BlockSpec((1, tm, D), row),
        out_shape=jax.ShapeDtypeStruct((B, S, D), F32),
        compiler_params=_params("parallel", "parallel"),
        name=name,
    )(x, *mixer_inputs, qmem, gate, kexp, vexp, w_out.astype(BF16), post_g.reshape(1, D))


B_GATES = (MIX_WIDTH, MIX_WIDTH + LANES)
B_O = (0, MIX_WIDTH)
B_QMEM = (B_O[1], B_O[1] + MEM_WIDTH)
B_GATE = (B_QMEM[1], B_QMEM[1] + 1024)
CONV_HALO = 8
V_ONES_ROWS = 16


def _mlstm_chunk(carry, qts, ks, vts, gcol, grow, causal, between):
    H = len(qts)
    L = ks[0].shape[0]
    qk = [_dot(ks[h], qts[h]) for h in range(H)]
    cq = [_dot(carry[h][0].astype(BF16), qts[h]) for h in range(H)]
    between()
    out, normed = [], []
    for h in range(H):
        C, m = carry[h]
        c_col = gcol[:, h:h + 1]
        li_row = grow[h:h + 1, :]
        b_row = grow[MLSTM_HEADS + h:MLSTM_HEADS + h + 1, :]
        d = jnp.where(causal, c_col + b_row, -jnp.inf)
        inter = b_row + m
        m_t = jnp.maximum(inter, jnp.max(d, axis=0, keepdims=True))
        sqk = (qk[h] * jnp.exp(d - m_t)).astype(BF16)
        num = jnp.exp(inter - m_t) * cq[h] + _dot(vts[h], sqk)
        den = jnp.maximum(jnp.abs(num[MLSTM_V_DIM:MLSTM_V_DIM + 1, :]), jnp.exp(-m_t))
        hnum = num[0:MLSTM_V_DIM, :]
        inv_den = 1.0 / den
        ms = jnp.sum(hnum * hnum, axis=0, keepdims=True) * (inv_den * inv_den * (1.0 / MLSTM_V_DIM))
        normed.append(hnum * (inv_den * lax.rsqrt(ms + EPS)))

        b_last = b_row[:, L - 1:L]
        dec = b_last - b_row + li_row
        m_new = jnp.maximum(b_last + m, jnp.max(dec, axis=1, keepdims=True))
        vw = (vts[h].astype(F32) * jnp.exp(dec - m_new)).astype(BF16)
        out.append((jnp.exp(b_last + m - m_new) * C + _dot(vw, ks[h]), m_new))
    return tuple(out), jnp.concatenate(normed, axis=0).T


def _mlstm_mixer_kernel(x_ref, pre_g_ref, w_first_ref, w_rest_ref, conv_w_ref, conv_b_ref, bias_ref, hg_ref,
                        wqk_ref, wv_ref, og_ref, qmem_ref, gate_ref, uc_ref, hn_ref, tail_sc, c_sc, m_sc, *, chunk):
    tm = x_ref.shape[1]
    blocks = [slice(j * chunk, (j + 1) * chunk) for j in range(tm // chunk)]

    @pl.when(pl.program_id(1) == 0)
    def _():
        tail_sc[...] = jnp.zeros(tail_sc.shape, F32)
        c_sc[...] = jnp.zeros(c_sc.shape, F32)
        m_sc[...] = jnp.zeros(m_sc.shape, F32)

    hs = [_rms(x_ref[0, r, :], pre_g_ref[...]).astype(BF16) for r in blocks]
    firsts = [_dot(h, w_first_ref[...]) for h in hs]
    halos = [tail_sc[...]] + [f[chunk - CONV_HALO:chunk, 0:MIX_WIDTH] for f in firsts[:-1]]
    tail_sc[...] = firsts[-1][chunk - CONV_HALO:chunk, 0:MIX_WIDTH]

    k_scale = MLSTM_QK_DIM ** -0.5
    ones_rows = jnp.where(lax.broadcasted_iota(jnp.int32, (V_ONES_ROWS, chunk), 0) == 0, 1.0, 0.0)
    lane = lax.broadcasted_iota(jnp.int32, (chunk, LANES), 1)
    is_f = (lane >= MLSTM_HEADS) & (lane < 2 * MLSTM_HEADS)
    tril = (lax.broadcasted_iota(jnp.int32, (chunk, chunk), 1)
            <= lax.broadcasted_iota(jnp.int32, (chunk, chunk), 0)).astype(BF16)
    causal = (lax.broadcasted_iota(jnp.int32, (chunk, chunk), 0)
              <= lax.broadcasted_iota(jnp.int32, (chunk, chunk), 1))
    carry = tuple((c_sc[h], m_sc[h, 0:1, 0:1]) for h in range(MLSTM_HEADS))

    for r, h, first, halo in zip(blocks, hs, firsts, halos):
        u = first[:, 0:MIX_WIDTH]
        ext = jnp.concatenate([halo, u], axis=0)
        conv = conv_b_ref[...] + u * conv_w_ref[CONV_WIDTH - 1:CONV_WIDTH, :]
        for back in range(1, CONV_WIDTH):
            tap = CONV_WIDTH - 1 - back
            conv = conv + ext[CONV_HALO - back:CONV_HALO - back + chunk, :] * conv_w_ref[tap:tap + 1, :]
        uc_bf = (conv * jax.nn.sigmoid(conv)).astype(BF16)
        uc_ref[0, r, :] = uc_bf
        u_bf = u.astype(BF16)

        g = first[:, B_GATES[0]:B_GATES[1]] + bias_ref[...]
        log_f = jnp.minimum(g, 0.0) - jnp.log1p(jnp.exp(-jnp.abs(g)))
        gates = jnp.where(lane < MLSTM_HEADS, g, jnp.where(is_f, log_f, 0.0))
        gc = jnp.where(is_f, sum(_dot(tril, term) for term in _split3(gates)), gates)
        grow = gc.T[0:2 * MLSTM_HEADS, :]
        gcol = gc - pltpu.roll(gc, LANES - MLSTM_HEADS, 1)

        qts, ks, vts = [], [], []
        for hd in range(MLSTM_HEADS):
            cols = slice(MLSTM_HEAD_START[hd], MLSTM_HEAD_START[hd] + MXU_DIM)
            qk = _dot(uc_bf[:, cols], wqk_ref[hd])
            qts.append(qk[:, :MLSTM_PAD_QK].T.astype(BF16))
            ks.append((qk[:, MLSTM_PAD_QK:] * k_scale).astype(BF16))
            vt = _dot(u_bf[:, cols], wv_ref[hd]).T
            vts.append(jnp.concatenate([vt[0:MLSTM_V_DIM, :], ones_rows, vt[MLSTM_V_DIM + V_ONES_ROWS:, :]],
                                       axis=0).astype(BF16))

        def rest_projection(r=r, h=h):
            p = _dot(h, w_rest_ref[...])
            og_ref[0, r, :] = (jax.nn.sigmoid(p[:, B_O[0]:B_O[1]]) * hg_ref[...]).astype(BF16)
            qmem_ref[0, r, :] = p[:, B_QMEM[0]:B_QMEM[1]].astype(BF16)
            gate = p[:, B_GATE[0]:B_GATE[1]]
            gate_ref[0, r, :] = (gate * jax.nn.sigmoid(gate)).astype(BF16)

        carry, hn = _mlstm_chunk(carry, qts, ks, vts, gcol, grow, causal, rest_projection)
        hn_ref[0, r, :] = hn.astype(BF16)

    for hd in range(MLSTM_HEADS):
        c_sc[hd] = carry[hd][0]
        m_sc[hd] = jnp.broadcast_to(carry[hd][1], m_sc.shape[1:])


def _mlstm_mixer(x, pre_g, w_in, gate_bias, conv_w, conv_b, w_q, w_k, w_v, head_g):
    B, S, D = x.shape
    tm = min(ROW_TILE, S)
    chunk = min(MLSTM_CHUNK, S)
    H = MLSTM_HEADS
    n_if = 2 * H
    o_if = MIX_WIDTH
    o_o = o_if + n_if
    w_first = jnp.concatenate([w_in[:, :o_o], jnp.zeros((D, LANES - n_if), w_in.dtype)], axis=1).astype(BF16)
    w_rest = w_in[:, o_o:].astype(BF16)
    wqk = jnp.zeros((H, MXU_DIM, 2 * MLSTM_PAD_QK), F32)
    wv = jnp.zeros((H, MXU_DIM, MLSTM_PAD_V), F32)
    for h in range(H):
        off = h * MLSTM_V_DIM - MLSTM_HEAD_START[h]
        wqk = wqk.at[h, off:off + MLSTM_V_DIM, 0:MLSTM_QK_DIM].set(w_q[h])
        wqk = wqk.at[h, off:off + MLSTM_V_DIM, MLSTM_PAD_QK:MLSTM_PAD_QK + MLSTM_QK_DIM].set(w_k[h])
        wv = wv.at[h, off:off + MLSTM_V_DIM, 0:MLSTM_V_DIM].set(w_v[h])
    bias = jnp.pad(gate_bias, (0, LANES - n_if)).reshape(1, LANES)
    row = lambda b, i: (b, i, 0)
    widths = [MIX_WIDTH, MEM_WIDTH, 1024, MIX_WIDTH, MIX_WIDTH]
    return pl.pallas_call(
        functools.partial(_mlstm_mixer_kernel, chunk=chunk),
        grid=(B, S // tm),
        in_specs=[pl.BlockSpec((1, tm, D), row), _const_spec((1, D)), _const_spec(w_first.shape),
                  _const_spec(w_rest.shape), _const_spec((CONV_WIDTH, MIX_WIDTH)), _const_spec((1, MIX_WIDTH)),
                  _const_spec((1, LANES)), _const_spec((1, MIX_WIDTH)), _const_spec(wqk.shape),
                  _const_spec(wv.shape)],
        out_specs=[pl.BlockSpec((1, tm, w), row) for w in widths],
        out_shape=[jax.ShapeDtypeStruct((B, S, w), BF16) for w in widths],
        scratch_shapes=[pltpu.VMEM((CONV_HALO, MIX_WIDTH), F32),
                        pltpu.VMEM((H, MLSTM_PAD_V, MLSTM_PAD_QK), F32), pltpu.VMEM((H, 8, LANES), F32)],
        compiler_params=_params("parallel", "arbitrary"),
        name="mlstm_mixer",
    )(x, pre_g.reshape(1, D), w_first, w_rest, conv_w, conv_b.reshape(1, -1), bias, head_g.reshape(1, MIX_WIDTH),
      wqk.astype(BF16), wv.astype(BF16))


def _mla_layer(x, mem, rope_operands, pre_g, w_in, q_a_g, w_uq, kv_a_g, w_ukv, mem_g, w_mem_kv, w_out, post_g):
    q, k, v, qmem, gate = _mla_in(x, rope_operands, pre_g, w_in, q_a_g, w_uq, kv_a_g, w_ukv)
    mix = _attention(q, k, v)
    kexp, vexp = _mem_kv(mem, mem_g, w_mem_kv)
    specs = lambda tm: [pl.BlockSpec((1, tm, MIX_WIDTH), lambda b, i: (b, i, 0))]
    return _layer_out(_mla_out_kernel, "mla_out", x, [mix], specs, qmem, gate, kexp, vexp, w_out, post_g)


def _mlstm_layer(x, mem, pre_g, w_in, gate_bias, conv_w, conv_b, w_q, w_k, w_v, head_g, skip,
                 mem_g, w_mem_kv, w_out, post_g):
    og, qmem, gate, uc, hn = _mlstm_mixer(x, pre_g, w_in, gate_bias, conv_w, conv_b, w_q, w_k, w_v, head_g)
    kexp, vexp = _mem_kv(mem, mem_g, w_mem_kv)
    row = lambda b, i: (b, i, 0)
    specs = lambda tm: [pl.BlockSpec((1, tm, MIX_WIDTH), row),
                        pl.BlockSpec((1, tm, MIX_WIDTH), row), pl.BlockSpec((1, tm, MIX_WIDTH), row),
                        _const_spec((1, MIX_WIDTH))]
    return _layer_out(_mlstm_out_kernel, "mlstm_out", x, [hn, og, uc, skip.reshape(1, -1)], specs,
                      qmem, gate, kexp, vexp, w_out, post_g)


def kernel(x, mem, positions, a_pre_g, a_w_in, a_q_a_g, a_w_uq, a_kv_a_g, a_w_ukv, a_mem_g, a_w_mem_kv, a_w_out, a_post_g, b_pre_g, b_w_in, b_gate_bias, b_conv_w, b_conv_b, b_w_q, b_w_k, b_w_v, b_head_g, b_skip, b_mem_g, b_w_mem_kv, b_w_out, b_post_g):
    depth = a_pre_g.shape[0] + b_pre_g.shape[0]
    rope_operands = _rope_operands(positions)
    for i in range(depth):
        j = i // 2
        if i % 2 == 0:
            x = _mla_layer(x, mem, rope_operands, a_pre_g[j], a_w_in[j], a_q_a_g[j], a_w_uq[j], a_kv_a_g[j],
                           a_w_ukv[j], a_mem_g[j], a_w_mem_kv[j], a_w_out[j], a_post_g[j])
        else:
            x = _mlstm_layer(x, mem, b_pre_g[j], b_w_in[j], b_gate_bias[j], b_conv_w[j], b_conv_b[j],
                             b_w_q[j], b_w_k[j], b_w_v[j], b_head_g[j], b_skip[j], b_mem_g[j],
                             b_w_mem_kv[j], b_w_out[j], b_post_g[j])
    return x
```

```python
import functools

import jax
import jax.numpy as jnp
import numpy as np
from jax import lax
from jax.experimental import pallas as pl
from jax.experimental.pallas import tpu as pltpu

EPS = 1e-6
ROPE_THETA = 10000.0
MEM_HEADS = 4
MEM_HEAD_DIM = 64
MEM_WIDTH = MEM_HEADS * MEM_HEAD_DIM
QK_NOPE_DIM = 128
QK_ROPE_DIM = 64
V_HEAD_DIM = 128
MLA_HEADS = 6
V_EXT_DIM = V_HEAD_DIM + 16
Q_LORA_RANK = 384
KV_LORA_RANK = 256
MLSTM_HEADS = 4
MLSTM_V_DIM = 192
MLSTM_QK_DIM = 96
CONV_WIDTH = 4
MIX_WIDTH = 768

LANES = 128
MXU_DIM = 256
VMEM_LIMIT_BYTES = 56 * 1024 * 1024

ROW_TILE = 512
IN_SUB_ROWS = 256
TAIL_ROW_TILE = 1024
TAIL_SUB_ROWS = 256
ATTN_TQ = 1024
ATTN_CW = 256
ATTN_TK = 1024
ATTN_HEADS_PER_STEP = 2
ATTN_LOOKAHEAD = 5
MLSTM_CHUNK = 256

LOG2E = 1.4426950408889634
MLSTM_PAD_QK = LANES
MLSTM_PAD_V = MXU_DIM
MLSTM_HEAD_START = tuple((h * MLSTM_V_DIM // LANES) * LANES for h in range(MLSTM_HEADS))

F32 = jnp.float32
BF16 = jnp.bfloat16


def _dot(a, b):
    return jnp.dot(a, b, preferred_element_type=F32)


def _dot_nt(a, b):
    return lax.dot_general(a, b, (((1,), (1,)), ((), ())), preferred_element_type=F32)


def _rms(x, g, width=None):
    width = x.shape[-1] if width is None else width
    ms = jnp.sum(x * x, axis=-1, keepdims=True) * (1.0 / width)
    return x * lax.rsqrt(ms + EPS) * g


def _split3(x):
    b1 = x.astype(BF16)
    r1 = x - b1.astype(F32)
    b2 = r1.astype(BF16)
    b3 = (r1 - b2.astype(F32)).astype(BF16)
    return b1, b2, b3


def _params(*semantics):
    return pltpu.CompilerParams(dimension_semantics=semantics, vmem_limit_bytes=VMEM_LIMIT_BYTES)


def _const_spec(shape):
    zeros = (0,) * len(shape)
    return pl.BlockSpec(shape, lambda *_: zeros)


ROPE_HALF = QK_ROPE_DIM // 2
ROPE_PER_ROW = LANES // ROPE_HALF


def _rope_operands(positions):
    B, S = positions.shape
    rows = B * S // ROPE_PER_ROW
    inv_freq = ROPE_THETA ** (-jnp.arange(0, QK_ROPE_DIM, 2, dtype=F32) / QK_ROPE_DIM)
    pos4 = jnp.repeat(positions.reshape(rows, ROPE_PER_ROW), ROPE_HALF, axis=1)
    invf = jnp.tile(inv_freq, ROPE_PER_ROW).reshape(1, LANES)
    sel = np.zeros((2 * LANES, ROPE_PER_ROW * LANES), np.float32)
    for t in range(ROPE_PER_ROW):
        for f in range(ROPE_HALF):
            sel[t * ROPE_HALF + f, t * LANES + f] = 1.0
            sel[t * ROPE_HALF + f, t * LANES + ROPE_HALF + f] = 1.0
            sel[LANES + t * ROPE_HALF + f, t * LANES + 2 * ROPE_HALF + f] = -1.0
            sel[LANES + t * ROPE_HALF + f, t * LANES + 3 * ROPE_HALF + f] = 1.0
    return pos4, invf, jnp.asarray(sel, BF16)


def _rope_table(pos_ref, invf_ref, sel_ref, cs_sc):
    ang = pos_ref[...].astype(F32) * invf_ref[...]
    trig = jnp.concatenate([jnp.cos(ang), jnp.sin(ang)], axis=1)
    spread = sum(_dot(term, sel_ref[...]) for term in _split3(trig))
    rows = trig.shape[0]
    for t in range(ROPE_PER_ROW):
        cs_sc[pl.ds(t, rows, stride=ROPE_PER_ROW), :] = spread[:, t * LANES:(t + 1) * LANES]
    return cs_sc[...]


def _mem_kv_kernel(mem_ref, g_ref, w_ref, k_ref, v_ref):
    n_mem = mem_ref.shape[1]
    hn = _rms(mem_ref[0], g_ref[...]).astype(BF16)
    kv = _dot(hn, w_ref[...])
    k = kv[:, :MEM_WIDTH] * (MEM_HEAD_DIM ** -0.5 * LOG2E)
    v = kv[:, MEM_WIDTH:]
    col_head = lax.broadcasted_iota(jnp.int32, (n_mem, MEM_WIDTH), 1) // MEM_HEAD_DIM
    for h in range(MEM_HEADS):
        rows = pl.ds(h * n_mem, n_mem)
        k_ref[0, rows, :] = jnp.where(col_head == h, k, 0.0).astype(BF16)
        v_ref[0, rows, :] = jnp.where(col_head == h, v, 0.0).astype(BF16)


def _mem_kv(mem, mem_g, w_mem_kv):
    B, n_mem, D = mem.shape
    out = jax.ShapeDtypeStruct((B, MEM_HEADS * n_mem, MEM_WIDTH), BF16)
    spec = pl.BlockSpec((1, MEM_HEADS * n_mem, MEM_WIDTH), lambda b: (b, 0, 0))
    return pl.pallas_call(
        _mem_kv_kernel,
        grid=(B,),
        in_specs=[pl.BlockSpec((1, n_mem, D), lambda b: (b, 0, 0)), _const_spec((1, D)),
                  _const_spec((D, 2 * MEM_WIDTH))],
        out_specs=[spec, spec],
        out_shape=[out, out],
        compiler_params=_params("parallel"),
        name="mem_kv",
    )(mem, mem_g.reshape(1, D), w_mem_kv.astype(BF16))


A_CQ = (0, Q_LORA_RANK)
A_CKV = (A_CQ[1], A_CQ[1] + KV_LORA_RANK)
A_QMEM = (A_CKV[1], A_CKV[1] + MEM_WIDTH)
A_GATE = (A_QMEM[1], A_QMEM[1] + 1024)
A_KROPE = (A_GATE[1], A_GATE[1] + 2 * QK_ROPE_DIM)
Q_ROPE_OFF = MLA_HEADS * QK_NOPE_DIM


def _mla_in_kernel(x_ref, pos_ref, invf_ref, sel_ref, pre_g_ref, w_in_ref, qa_g_ref, w_uq_ref, kva_g_ref,
                   w_ukv_ref, q_ref, k_ref, v_ref, qmem_ref, gate_ref, cs_sc):
    q_scale = (QK_NOPE_DIM + QK_ROPE_DIM) ** -0.5 * LOG2E
    tm = x_ref.shape[1]
    sub = min(IN_SUB_ROWS, tm)
    blocks = [slice(i * sub, (i + 1) * sub) for i in range(tm // sub)]
    projected = [_dot(_rms(x_ref[0, r, :], pre_g_ref[...]).astype(BF16), w_in_ref[...]) for r in blocks]
    cs_tile = _rope_table(pos_ref, invf_ref, sel_ref, cs_sc)
    pad_row = lax.broadcasted_iota(jnp.int32, (V_EXT_DIM - V_HEAD_DIM, sub), 0)
    ones_row = jnp.where(pad_row == 0, 1.0, 0.0).astype(BF16)
    for r, p in zip(blocks, projected):
        cs = cs_tile[r, :]
        qmem_ref[0, r, :] = p[:, A_QMEM[0]:A_QMEM[1]].astype(BF16)
        gate = p[:, A_GATE[0]:A_GATE[1]]
        gate_ref[0, r, :] = (gate * jax.nn.sigmoid(gate)).astype(BF16)

        c_q = _rms(p[:, A_CQ[0]:A_CQ[1]], qa_g_ref[...]).astype(BF16)
        q = _dot(c_q, w_uq_ref[...])
        c_kv = _rms(p[:, A_CKV[0]:A_CKV[1]], kva_g_ref[...]).astype(BF16)
        kv = _dot(c_kv, w_ukv_ref[...])

        kr = p[:, A_KROPE[0]:A_KROPE[1]] * cs
        k_rot = (kr + pltpu.roll(kr, QK_ROPE_DIM, 1)).astype(BF16)
        for hd in range(MLA_HEADS):
            nope = slice(hd * QK_NOPE_DIM, (hd + 1) * QK_NOPE_DIM)
            rope = slice(Q_ROPE_OFF + hd * LANES, Q_ROPE_OFF + (hd + 1) * LANES)
            q_ref[0, hd, 0:LANES, r] = (q[:, nope] * q_scale).T.astype(BF16)
            q_ref[0, hd, LANES:2 * LANES, r] = (q[:, rope] * cs * q_scale).T.astype(BF16)
            k_ref[0, hd, r, 0:LANES] = kv[:, 2 * hd * LANES:(2 * hd + 1) * LANES].astype(BF16)
            k_ref[0, hd, r, LANES:2 * LANES] = k_rot
            v_ref[0, hd, 0:V_HEAD_DIM, r] = kv[:, (2 * hd + 1) * LANES:(2 * hd + 2) * LANES].T.astype(BF16)
            v_ref[0, hd, V_HEAD_DIM:V_EXT_DIM, r] = ones_row


def _rope_cols(w, start):
    half = QK_ROPE_DIM // 2
    return [w[:, start:start + QK_ROPE_DIM], w[:, start + half:start + QK_ROPE_DIM], w[:, start:start + half]]


def _mla_in(x, rope_operands, pre_g, w_in, q_a_g, w_uq, kv_a_g, w_ukv):
    B, S, D = x.shape
    tm = min(ROW_TILE, S)
    pos4, invf, sel = rope_operands
    pos_rows = tm // ROPE_PER_ROW
    tiles_per_seq = S // tm
    o_kr = Q_LORA_RANK + KV_LORA_RANK
    o_qm = o_kr + QK_ROPE_DIM
    w_in_p = jnp.concatenate([w_in[:, :o_kr], w_in[:, o_qm:]] + _rope_cols(w_in, o_kr), axis=1).astype(BF16)
    head_w = QK_NOPE_DIM + QK_ROPE_DIM
    uq_cols = [w_uq[:, h * head_w:h * head_w + QK_NOPE_DIM] for h in range(MLA_HEADS)]
    for h in range(MLA_HEADS):
        uq_cols += _rope_cols(w_uq, h * head_w + QK_NOPE_DIM)
    w_uq_p = jnp.concatenate(uq_cols, axis=1).astype(BF16)
    w_ukv_p = w_ukv.astype(BF16)

    row = lambda b, i: (b, i, 0)
    head_row = lambda b, i: (b, 0, i, 0)
    head_col = lambda b, i: (b, 0, 0, i)
    out_shape = [
        jax.ShapeDtypeStruct((B, MLA_HEADS, 2 * LANES, S), BF16),
        jax.ShapeDtypeStruct((B, MLA_HEADS, S, 2 * LANES), BF16),
        jax.ShapeDtypeStruct((B, MLA_HEADS, V_EXT_DIM, S), BF16),
        jax.ShapeDtypeStruct((B, S, MEM_WIDTH), BF16),
        jax.ShapeDtypeStruct((B, S, 1024), BF16),
    ]
    out_specs = [
        pl.BlockSpec((1, MLA_HEADS, 2 * LANES, tm), head_col),
        pl.BlockSpec((1, MLA_HEADS, tm, 2 * LANES), head_row),
        pl.BlockSpec((1, MLA_HEADS, V_EXT_DIM, tm), head_col),
        pl.BlockSpec((1, tm, MEM_WIDTH), row),
        pl.BlockSpec((1, tm, 1024), row),
    ]
    return pl.pallas_call(
        _mla_in_kernel,
        grid=(B, S // tm),
        in_specs=[pl.BlockSpec((1, tm, D), row),
                  pl.BlockSpec((pos_rows, LANES), lambda b, i: (b * tiles_per_seq + i, 0)),
                  _const_spec(invf.shape), _const_spec(sel.shape),
                  _const_spec((1, D)), _const_spec(w_in_p.shape),
                  _const_spec((1, Q_LORA_RANK)), _const_spec(w_uq_p.shape),
                  _const_spec((1, KV_LORA_RANK)), _const_spec(w_ukv_p.shape)],
        out_specs=out_specs,
        out_shape=out_shape,
        scratch_shapes=[pltpu.VMEM((tm, LANES), F32)],
        compiler_params=_params("parallel", "parallel"),
        name="mla_in",
    )(x, pos4, invf, sel, pre_g.reshape(1, D), w_in_p, q_a_g.reshape(1, -1), w_uq_p, kv_a_g.reshape(1, -1),
      w_ukv_p)


def _attn_kernel(qt_ref, k_ref, vt_ref, o_ref, *, tq, tk, cw):
    heads = k_ref.shape[1]
    S = k_ref.shape[2]
    dve = vt_ref.shape[2]
    dv = V_HEAD_DIM
    nc = tq // cw
    causal = (lax.broadcasted_iota(jnp.int32, (cw, cw), 0) <= lax.broadcasted_iota(jnp.int32, (cw, cw), 1))

    def q_tile(j, _):
        q0 = j * tq
        qts = [[qt_ref[0, h, :, pl.ds(q0 + c * cw, cw)] for c in range(nc)] for h in range(heads)]

        def softmax_pv(carry, s, vt_tile):
            m, acc = carry
            m_new = jnp.maximum(m, jnp.max(s, axis=0, keepdims=True))
            p = jnp.exp2(s - m_new).astype(BF16)
            return m_new, jnp.exp2(m - m_new) * acc + _dot(vt_tile, p)

        def body(ki, carries):
            rows = pl.ds(ki * tk, tk)
            k_tiles = [k_ref[0, h, rows, :] for h in range(heads)]
            vt_tiles = [vt_ref[0, h, :, rows] for h in range(heads)]
            chains = [(h, c) for c in range(nc) for h in range(heads)]
            scores, out = {}, {}
            for i in range(len(chains) + ATTN_LOOKAHEAD):
                if i < len(chains):
                    h, c = chains[i]
                    scores[h, c] = _dot(k_tiles[h], qts[h][c])
                if i >= ATTN_LOOKAHEAD:
                    h, c = chains[i - ATTN_LOOKAHEAD]
                    out[h, c] = softmax_pv(carries[h][c], scores.pop((h, c)), vt_tiles[h])
            return tuple(tuple(out[h, c] for c in range(nc)) for h in range(heads))

        init = tuple(tuple((jnp.full((1, cw), -jnp.inf, F32), jnp.zeros((dve, cw), F32)) for _ in range(nc))
                     for _ in range(heads))
        carries = init
        for ki in range(j * (tq // tk)):
            carries = body(ki, carries)

        chains = [(h, c) for h in range(heads) for c in range(nc)]
        scores = {}
        for h, c in chains:
            s = _dot(k_ref[0, h, pl.ds(q0, (c + 1) * cw), :], qts[h][c])
            s_diag = jnp.where(causal, s[c * cw:, :], -jnp.inf)
            scores[h, c] = jnp.concatenate([s[:c * cw, :], s_diag], axis=0) if c > 0 else s_diag
        for h, c in chains:
            _, acc = softmax_pv(carries[h][c], scores[h, c], vt_ref[0, h, :, pl.ds(q0, (c + 1) * cw)])
            out = acc[0:dv, :] / acc[dv:dv + 1, :]
            o_ref[0, pl.ds(q0 + c * cw, cw), h * dv:(h + 1) * dv] = out.T.astype(o_ref.dtype)
        return 0

    for j in range(S // tq):
        q_tile(j, 0)


def _attention(qt, k, vt):
    B, H, S, dqk = k.shape
    dve = vt.shape[2]
    tq = min(ATTN_TQ, S)
    tk = min(ATTN_TK, tq)
    cw = min(ATTN_CW, tq)
    hp = ATTN_HEADS_PER_STEP
    head = lambda b, h: (b, h, 0, 0)
    return pl.pallas_call(
        functools.partial(_attn_kernel, tq=tq, tk=tk, cw=cw),
        grid=(B, H // hp),
        in_specs=[pl.BlockSpec((1, hp, dqk, S), head), pl.BlockSpec((1, hp, S, dqk), head),
                  pl.BlockSpec((1, hp, dve, S), head)],
        out_specs=pl.BlockSpec((1, S, hp * V_HEAD_DIM), lambda b, h: (b, 0, h)),
        out_shape=jax.ShapeDtypeStruct((B, S, H * V_HEAD_DIM), BF16),
        compiler_params=_params("parallel", "parallel"),
        name="mla_attention",
    )(qt, k, vt)


def _memory_probs(s, n_mem):
    probs = []
    for h in range(MEM_HEADS):
        sh = s[:, h * n_mem:(h + 1) * n_mem]
        e = jnp.exp2(sh - jnp.max(sh, axis=-1, keepdims=True))
        probs.append((e / jnp.sum(e, axis=-1, keepdims=True)).astype(BF16))
    return jnp.concatenate(probs, axis=-1)


def _tail(gated_mix, x_ref, qmem_ref, gate_ref, kexp_ref, vexp_ref, w_out_ref, post_g_ref, o_ref):
    tm = x_ref.shape[1]
    sub = min(TAIL_SUB_ROWS, tm)
    blocks = [slice(i * sub, (i + 1) * sub) for i in range(tm // sub)]
    n_mem = kexp_ref.shape[1] // MEM_HEADS
    kexp = kexp_ref[0]
    scores = [_dot_nt(qmem_ref[0, r, :], kexp) for r in blocks]
    for r, s in zip(blocks, scores):
        mo = _dot(_memory_probs(s, n_mem), vexp_ref[0])
        y_mix = gated_mix(r, gate_ref[0, r, 0:MIX_WIDTH])
        y_mem = (mo * gate_ref[0, r, MIX_WIDTH:].astype(F32)).astype(BF16)
        y = _dot(y_mix, w_out_ref[0:MIX_WIDTH, :]) + _dot(y_mem, w_out_ref[MIX_WIDTH:, :])
        o_ref[0, r, :] = x_ref[0, r, :] + _rms(y, post_g_ref[...])


def _mla_out_kernel(x_ref, mix_ref, qmem_ref, gate_ref, kexp_ref, vexp_ref, w_out_ref, post_g_ref, o_ref):
    gated_mix = lambda r, gate: mix_ref[0, r, :] * gate
    _tail(gated_mix, x_ref, qmem_ref, gate_ref, kexp_ref, vexp_ref, w_out_ref, post_g_ref, o_ref)


def _mlstm_out_kernel(x_ref, hn_ref, og_ref, uc_ref, skip_ref, qmem_ref, gate_ref, kexp_ref, vexp_ref,
                      w_out_ref, post_g_ref, o_ref):
    def gated_mix(r, gate):
        mix = og_ref[0, r, :].astype(F32) * hn_ref[0, r, :].astype(F32) + skip_ref[...] * uc_ref[0, r, :].astype(F32)
        return (mix * gate.astype(F32)).astype(BF16)

    _tail(gated_mix, x_ref, qmem_ref, gate_ref, kexp_ref, vexp_ref, w_out_ref, post_g_ref, o_ref)


def _layer_out(kernel_fn, name, x, mixer_inputs, mixer_specs, qmem, gate, kexp, vexp, w_out, post_g):
    B, S, D = x.shape
    tm = min(TAIL_ROW_TILE, S)
    row = lambda b, i: (b, i, 0)
    per_batch = lambda b, i: (b, 0, 0)
    in_specs = ([pl.BlockSpec((1, tm, D), row)] + mixer_specs(tm) +
                [pl.BlockSpec((1, tm, MEM_WIDTH), row), pl.BlockSpec((1, tm, gate.shape[-1]), row),
                 pl.BlockSpec((1,) + kexp.shape[1:], per_batch), pl.BlockSpec((1,) + vexp.shape[1:], per_batch),
                 _const_spec(w_out.shape), _const_spec((1, D))])
    return pl.pallas_call(
        kernel_fn,
        grid=(B, S // tm),
        in_specs=in_specs,
        out_specs=pl.BlockSpec((1, tm, D), row),
        out_shape=jax.ShapeDtypeStruct((B, S, D), F32),
        compiler_params=_params("parallel", "parallel"),
        name=name,
    )(x, *mixer_inputs, qmem, gate, kexp, vexp, w_out.astype(BF16), post_g.reshape(1, D))


B_GATES = (MIX_WIDTH, MIX_WIDTH + LANES)
B_O = (0, MIX_WIDTH)
B_QMEM = (B_O[1], B_O[1] + MEM_WIDTH)
B_GATE = (B_QMEM[1], B_QMEM[1] + 1024)
CONV_HALO = 8
V_ONES_ROWS = 16


def _mlstm_chunk(carry, qts, ks, vts, gcol, grow, causal, between):
    H = len(qts)
    L = ks[0].shape[0]
    qk = [_dot(ks[h], qts[h]) for h in range(H)]
    cq = [_dot(carry[h][0].astype(BF16), qts[h]) for h in range(H)]
    between()
    out, normed = [], []
    for h in range(H):
        C, m = carry[h]
        c_col = gcol[:, h:h + 1]
        li_row = grow[h:h + 1, :]
        b_row = grow[MLSTM_HEADS + h:MLSTM_HEADS + h + 1, :]
        d = jnp.where(causal, c_col + b_row, -jnp.inf)
        inter = b_row + m
        m_t = jnp.maximum(inter, jnp.max(d, axis=0, keepdims=True))
        sqk = (qk[h] * jnp.exp(d - m_t)).astype(BF16)
        num = jnp.exp(inter - m_t) * cq[h] + _dot(vts[h], sqk)
        den = jnp.maximum(jnp.abs(num[MLSTM_V_DIM:MLSTM_V_DIM + 1, :]), jnp.exp(-m_t))
        hnum = num[0:MLSTM_V_DIM, :]
        inv_den = 1.0 / den
        ms = jnp.sum(hnum * hnum, axis=0, keepdims=True) * (inv_den * inv_den * (1.0 / MLSTM_V_DIM))
        normed.append(hnum * (inv_den * lax.rsqrt(ms + EPS)))

        b_last = b_row[:, L - 1:L]
        dec = b_last - b_row + li_row
        m_new = jnp.maximum(b_last + m, jnp.max(dec, axis=1, keepdims=True))
        vw = (vts[h].astype(F32) * jnp.exp(dec - m_new)).astype(BF16)
        out.append((jnp.exp(b_last + m - m_new) * C + _dot(vw, ks[h]), m_new))
    return tuple(out), jnp.concatenate(normed, axis=0).T


def _mlstm_mixer_kernel(x_ref, pre_g_ref, w_first_ref, w_rest_ref, conv_w_ref, conv_b_ref, bias_ref, hg_ref,
                        wqk_ref, wv_ref, og_ref, qmem_ref, gate_ref, uc_ref, hn_ref, tail_sc, c_sc, m_sc, *, chunk):
    tm = x_ref.shape[1]
    blocks = [slice(j * chunk, (j + 1) * chunk) for j in range(tm // chunk)]

    @pl.when(pl.program_id(1) == 0)
    def _():
        tail_sc[...] = jnp.zeros(tail_sc.shape, F32)
        c_sc[...] = jnp.zeros(c_sc.shape, F32)
        m_sc[...] = jnp.zeros(m_sc.shape, F32)

    hs = [_rms(x_ref[0, r, :], pre_g_ref[...]).astype(BF16) for r in blocks]
    firsts = [_dot(h, w_first_ref[...]) for h in hs]
    halos = [tail_sc[...]] + [f[chunk - CONV_HALO:chunk, 0:MIX_WIDTH] for f in firsts[:-1]]
    tail_sc[...] = firsts[-1][chunk - CONV_HALO:chunk, 0:MIX_WIDTH]

    k_scale = MLSTM_QK_DIM ** -0.5
    ones_rows = jnp.where(lax.broadcasted_iota(jnp.int32, (V_ONES_ROWS, chunk), 0) == 0, 1.0, 0.0)
    lane = lax.broadcasted_iota(jnp.int32, (chunk, LANES), 1)
    is_f = (lane >= MLSTM_HEADS) & (lane < 2 * MLSTM_HEADS)
    tril = (lax.broadcasted_iota(jnp.int32, (chunk, chunk), 1)
            <= lax.broadcasted_iota(jnp.int32, (chunk, chunk), 0)).astype(BF16)
    causal = (lax.broadcasted_iota(jnp.int32, (chunk, chunk), 0)
              <= lax.broadcasted_iota(jnp.int32, (chunk, chunk), 1))
    carry = tuple((c_sc[h], m_sc[h, 0:1, 0:1]) for h in range(MLSTM_HEADS))

    for r, h, first, halo in zip(blocks, hs, firsts, halos):
        u = first[:, 0:MIX_WIDTH]
        ext = jnp.concatenate([halo, u], axis=0)
        conv = conv_b_ref[...] + u * conv_w_ref[CONV_WIDTH - 1:CONV_WIDTH, :]
        for back in range(1, CONV_WIDTH):
            tap = CONV_WIDTH - 1 - back
            conv = conv + ext[CONV_HALO - back:CONV_HALO - back + chunk, :] * conv_w_ref[tap:tap + 1, :]
        uc_bf = (conv * jax.nn.sigmoid(conv)).astype(BF16)
        uc_ref[0, r, :] = uc_bf
        u_bf = u.astype(BF16)

        g = first[:, B_GATES[0]:B_GATES[1]] + bias_ref[...]
        log_f = jnp.minimum(g, 0.0) - jnp.log1p(jnp.exp(-jnp.abs(g)))
        gates = jnp.where(lane < MLSTM_HEADS, g, jnp.where(is_f, log_f, 0.0))
        gc = jnp.where(is_f, sum(_dot(tril, term) for term in _split3(gates)), gates)
        grow = gc.T[0:2 * MLSTM_HEADS, :]
        gcol = gc - pltpu.roll(gc, LANES - MLSTM_HEADS, 1)

        qts, ks, vts = [], [], []
        for hd in range(MLSTM_HEADS):
            cols = slice(MLSTM_HEAD_START[hd], MLSTM_HEAD_START[hd] + MXU_DIM)
            qk = _dot(uc_bf[:, cols], wqk_ref[hd])
            qts.append(qk[:, :MLSTM_PAD_QK].T.astype(BF16))
            ks.append((qk[:, MLSTM_PAD_QK:] * k_scale).astype(BF16))
            vt = _dot(u_bf[:, cols], wv_ref[hd]).T
            vts.append(jnp.concatenate([vt[0:MLSTM_V_DIM, :], ones_rows, vt[MLSTM_V_DIM + V_ONES_ROWS:, :]],
                                       axis=0).astype(BF16))

        def rest_projection(r=r, h=h):
            p = _dot(h, w_rest_ref[...])
            og_ref[0, r, :] = (jax.nn.sigmoid(p[:, B_O[0]:B_O[1]]) * hg_ref[...]).astype(BF16)
            qmem_ref[0, r, :] = p[:, B_QMEM[0]:B_QMEM[1]].astype(BF16)
            gate = p[:, B_GATE[0]:B_GATE[1]]
            gate_ref[0, r, :] = (gate * jax.nn.sigmoid(gate)).astype(BF16)

        carry, hn = _mlstm_chunk(carry, qts, ks, vts, gcol, grow, causal, rest_projection)
        hn_ref[0, r, :] = hn.astype(BF16)

    for hd in range(MLSTM_HEADS):
        c_sc[hd] = carry[hd][0]
        m_sc[hd] = jnp.broadcast_to(carry[hd][1], m_sc.shape[1:])


def _mlstm_mixer(x, pre_g, w_in, gate_bias, conv_w, conv_b, w_q, w_k, w_v, head_g):
    B, S, D = x.shape
    tm = min(ROW_TILE, S)
    chunk = min(MLSTM_CHUNK, S)
    H = MLSTM_HEADS
    n_if = 2 * H
    o_if = MIX_WIDTH
    o_o = o_if + n_if
    w_first = jnp.concatenate([w_in[:, :o_o], jnp.zeros((D, LANES - n_if), w_in.dtype)], axis=1).astype(BF16)
    w_rest = w_in[:, o_o:].astype(BF16)
    wqk = jnp.zeros((H, MXU_DIM, 2 * MLSTM_PAD_QK), F32)
    wv = jnp.zeros((H, MXU_DIM, MLSTM_PAD_V), F32)
    for h in range(H):
        off = h * MLSTM_V_DIM - MLSTM_HEAD_START[h]
        wqk = wqk.at[h, off:off + MLSTM_V_DIM, 0:MLSTM_QK_DIM].set(w_q[h])
        wqk = wqk.at[h, off:off + MLSTM_V_DIM, MLSTM_PAD_QK:MLSTM_PAD_QK + MLSTM_QK_DIM].set(w_k[h])
        wv = wv.at[h, off:off + MLSTM_V_DIM, 0:MLSTM_V_DIM].set(w_v[h])
    bias = jnp.pad(gate_bias, (0, LANES - n_if)).reshape(1, LANES)
    row = lambda b, i: (b, i, 0)
    widths = [MIX_WIDTH, MEM_WIDTH, 1024, MIX_WIDTH, MIX_WIDTH]
    return pl.pallas_call(
        functools.partial(_mlstm_mixer_kernel, chunk=chunk),
        grid=(B, S // tm),
        in_specs=[pl.BlockSpec((1, tm, D), row), _const_spec((1, D)), _const_spec(w_first.shape),
                  _const_spec(w_rest.shape), _const_spec((CONV_WIDTH, MIX_WIDTH)), _const_spec((1, MIX_WIDTH)),
                  _const_spec((1, LANES)), _const_spec((1, MIX_WIDTH)), _const_spec(wqk.shape),
                  _const_spec(wv.shape)],
        out_specs=[pl.BlockSpec((1, tm, w), row) for w in widths],
        out_shape=[jax.ShapeDtypeStruct((B, S, w), BF16) for w in widths],
        scratch_shapes=[pltpu.VMEM((CONV_HALO, MIX_WIDTH), F32),
                        pltpu.VMEM((H, MLSTM_PAD_V, MLSTM_PAD_QK), F32), pltpu.VMEM((H, 8, LANES), F32)],
        compiler_params=_params("parallel", "arbitrary"),
        name="mlstm_mixer",
    )(x, pre_g.reshape(1, D), w_first, w_rest, conv_w, conv_b.reshape(1, -1), bias, head_g.reshape(1, MIX_WIDTH),
      wqk.astype(BF16), wv.astype(BF16))


def _mla_layer(x, mem, rope_operands, pre_g, w_in, q_a_g, w_uq, kv_a_g, w_ukv, mem_g, w_mem_kv, w_out, post_g):
    q, k, v, qmem, gate = _mla_in(x, rope_operands, pre_g, w_in, q_a_g, w_uq, kv_a_g, w_ukv)
    mix = _attention(q, k, v)
    kexp, vexp = _mem_kv(mem, mem_g, w_mem_kv)
    specs = lambda tm: [pl.BlockSpec((1, tm, MIX_WIDTH), lambda b, i: (b, i, 0))]
    return _layer_out(_mla_out_kernel, "mla_out", x, [mix], specs, qmem, gate, kexp, vexp, w_out, post_g)


def _mlstm_layer(x, mem, pre_g, w_in, gate_bias, conv_w, conv_b, w_q, w_k, w_v, head_g, skip,
                 mem_g, w_mem_kv, w_out, post_g):
    og, qmem, gate, uc, hn = _mlstm_mixer(x, pre_g, w_in, gate_bias, conv_w, conv_b, w_q, w_k, w_v, head_g)
    kexp, vexp = _mem_kv(mem, mem_g, w_mem_kv)
    row = lambda b, i: (b, i, 0)
    specs = lambda tm: [pl.BlockSpec((1, tm, MIX_WIDTH), row),
                        pl.BlockSpec((1, tm, MIX_WIDTH), row), pl.BlockSpec((1, tm, MIX_WIDTH), row),
                        _const_spec((1, MIX_WIDTH))]
    return _layer_out(_mlstm_out_kernel, "mlstm_out", x, [hn, og, uc, skip.reshape(1, -1)], specs,
                      qmem, gate, kexp, vexp, w_out, post_g)


def kernel(x, mem, positions, a_pre_g, a_w_in, a_q_a_g, a_w_uq, a_kv_a_g, a_w_ukv, a_mem_g, a_w_mem_kv, a_w_out, a_post_g, b_pre_g, b_w_in, b_gate_bias, b_conv_w, b_conv_b, b_w_q, b_w_k, b_w_v, b_head_g, b_skip, b_mem_g, b_w_mem_kv, b_w_out, b_post_g):
    depth = a_pre_g.shape[0] + b_pre_g.shape[0]
    rope_operands = _rope_operands(positions)
    for i in range(depth):
        j = i // 2
        if i % 2 == 0:
            x = _mla_layer(x, mem, rope_operands, a_pre_g[j], a_w_in[j], a_q_a_g[j], a_w_uq[j], a_kv_a_g[j],
                           a_w_ukv[j], a_mem_g[j], a_w_mem_kv[j], a_w_out[j], a_post_g[j])
        else:
            x = _mlstm_layer(x, mem, b_pre_g[j], b_w_in[j], b_gate_bias[j], b_conv_w[j], b_conv_b[j],
                             b_w_q[j], b_w_k[j], b_w_v[j], b_head_g[j], b_skip[j], b_mem_g[j],
                             b_w_mem_kv[j], b_w_out[j], b_post_g[j])
    return x
```

```python
import functools

import jax
import jax.numpy as jnp
import numpy as np
from jax import lax
from jax.experimental import pallas as pl
from jax.experimental.pallas import tpu as pltpu

EPS = 1e-6
ROPE_THETA = 10000.0
MEM_HEADS = 4
MEM_HEAD_DIM = 64
MEM_WIDTH = MEM_HEADS * MEM_HEAD_DIM
QK_NOPE_DIM = 128
QK_ROPE_DIM = 64
V_HEAD_DIM = 128
MLA_HEADS = 6
V_EXT_DIM = V_HEAD_DIM + 16
Q_LORA_RANK = 384
KV_LORA_RANK = 256
MLSTM_HEADS = 4
MLSTM_V_DIM = 192
MLSTM_QK_DIM = 96
CONV_WIDTH = 4
MIX_WIDTH = 768

LANES = 128
MXU_DIM = 256
VMEM_LIMIT_BYTES = 56 * 1024 * 1024

ROW_TILE = 512
IN_SUB_ROWS = 256
TAIL_ROW_TILE = 1024
TAIL_SUB_ROWS = 256
ATTN_TQ = 1024
ATTN_CW = 256
ATTN_TK = 1024
ATTN_HEADS_PER_STEP = 2
ATTN_LOOKAHEAD = 5
MLSTM_CHUNK = 256

LOG2E = 1.4426950408889634
MLSTM_PAD_QK = LANES
MLSTM_PAD_V = MXU_DIM
MLSTM_HEAD_START = tuple((h * MLSTM_V_DIM // LANES) * LANES for h in range(MLSTM_HEADS))

F32 = jnp.float32
BF16 = jnp.bfloat16


def _dot(a, b):
    return jnp.dot(a, b, preferred_element_type=F32)


def _dot_nt(a, b):
    return lax.dot_general(a, b, (((1,), (1,)), ((), ())), preferred_element_type=F32)


def _rms(x, g, width=None):
    width = x.shape[-1] if width is None else width
    ms = jnp.sum(x * x, axis=-1, keepdims=True) * (1.0 / width)
    return x * lax.rsqrt(ms + EPS) * g


def _split3(x):
    b1 = x.astype(BF16)
    r1 = x - b1.astype(F32)
    b2 = r1.astype(BF16)
    b3 = (r1 - b2.astype(F32)).astype(BF16)
    return b1, b2, b3


def _params(*semantics):
    return pltpu.CompilerParams(dimension_semantics=semantics, vmem_limit_bytes=VMEM_LIMIT_BYTES)


def _const_spec(shape):
    zeros = (0,) * len(shape)
    return pl.BlockSpec(shape, lambda *_: zeros)


ROPE_HALF = QK_ROPE_DIM // 2
ROPE_PER_ROW = LANES // ROPE_HALF


def _rope_operands(positions):
    B, S = positions.shape
    rows = B * S // ROPE_PER_ROW
    inv_freq = ROPE_THETA ** (-jnp.arange(0, QK_ROPE_DIM, 2, dtype=F32) / QK_ROPE_DIM)
    pos4 = jnp.repeat(positions.reshape(rows, ROPE_PER_ROW), ROPE_HALF, axis=1)
    invf = jnp.tile(inv_freq, ROPE_PER_ROW).reshape(1, LANES)
    sel = np.zeros((2 * LANES, ROPE_PER_ROW * LANES), np.float32)
    for t in range(ROPE_PER_ROW):
        for f in range(ROPE_HALF):
            sel[t * ROPE_HALF + f, t * LANES + f] = 1.0
            sel[t * ROPE_HALF + f, t * LANES + ROPE_HALF + f] = 1.0
            sel[LANES + t * ROPE_HALF + f, t * LANES + 2 * ROPE_HALF + f] = -1.0
            sel[LANES + t * ROPE_HALF + f, t * LANES + 3 * ROPE_HALF + f] = 1.0
    return pos4, invf, jnp.asarray(sel, BF16)


def _rope_table(pos_ref, invf_ref, sel_ref, cs_sc):
    ang = pos_ref[...].astype(F32) * invf_ref[...]
    trig = jnp.concatenate([jnp.cos(ang), jnp.sin(ang)], axis=1)
    spread = sum(_dot(term, sel_ref[...]) for term in _split3(trig))
    rows = trig.shape[0]
    for t in range(ROPE_PER_ROW):
        cs_sc[pl.ds(t, rows, stride=ROPE_PER_ROW), :] = spread[:, t * LANES:(t + 1) * LANES]
    return cs_sc[...]


def _mem_kv_kernel(mem_ref, g_ref, w_ref, k_ref, v_ref):
    n_mem = mem_ref.shape[1]
    hn = _rms(mem_ref[0], g_ref[...]).astype(BF16)
    kv = _dot(hn, w_ref[...])
    k = kv[:, :MEM_WIDTH] * (MEM_HEAD_DIM ** -0.5 * LOG2E)
    v = kv[:, MEM_WIDTH:]
    col_head = lax.broadcasted_iota(jnp.int32, (n_mem, MEM_WIDTH), 1) // MEM_HEAD_DIM
    for h in range(MEM_HEADS):
        rows = pl.ds(h * n_mem, n_mem)
        k_ref[0, rows, :] = jnp.where(col_head == h, k, 0.0).astype(BF16)
        v_ref[0, rows, :] = jnp.where(col_head == h, v, 0.0).astype(BF16)


def _mem_kv(mem, mem_g, w_mem_kv):
    B, n_mem, D = mem.shape
    out = jax.ShapeDtypeStruct((B, MEM_HEADS * n_mem, MEM_WIDTH), BF16)
    spec = pl.BlockSpec((1, MEM_HEADS * n_mem, MEM_WIDTH), lambda b: (b, 0, 0))
    return pl.pallas_call(
        _mem_kv_kernel,
        grid=(B,),
        in_specs=[pl.BlockSpec((1, n_mem, D), lambda b: (b, 0, 0)), _const_spec((1, D)),
                  _const_spec((D, 2 * MEM_WIDTH))],
        out_specs=[spec, spec],
        out_shape=[out, out],
        compiler_params=_params("parallel"),
        name="mem_kv",
    )(mem, mem_g.reshape(1, D), w_mem_kv.astype(BF16))


A_CQ = (0, Q_LORA_RANK)
A_CKV = (A_CQ[1], A_CQ[1] + KV_LORA_RANK)
A_QMEM = (A_CKV[1], A_CKV[1] + MEM_WIDTH)
A_GATE = (A_QMEM[1], A_QMEM[1] + 1024)
A_KROPE = (A_GATE[1], A_GATE[1] + 2 * QK_ROPE_DIM)
Q_ROPE_OFF = MLA_HEADS * QK_NOPE_DIM


def _mla_in_kernel(x_ref, pos_ref, invf_ref, sel_ref, pre_g_ref, w_in_ref, qa_g_ref, w_uq_ref, kva_g_ref,
                   w_ukv_ref, q_ref, k_ref, v_ref, qmem_ref, gate_ref, cs_sc):
    q_scale = (QK_NOPE_DIM + QK_ROPE_DIM) ** -0.5 * LOG2E
    tm = x_ref.shape[1]
    sub = min(IN_SUB_ROWS, tm)
    blocks = [slice(i * sub, (i + 1) * sub) for i in range(tm // sub)]
    projected = [_dot(_rms(x_ref[0, r, :], pre_g_ref[...]).astype(BF16), w_in_ref[...]) for r in blocks]
    cs_tile = _rope_table(pos_ref, invf_ref, sel_ref, cs_sc)
    pad_row = lax.broadcasted_iota(jnp.int32, (V_EXT_DIM - V_HEAD_DIM, sub), 0)
    ones_row = jnp.where(pad_row == 0, 1.0, 0.0).astype(BF16)
    for r, p in zip(blocks, projected):
        cs = cs_tile[r, :]
        qmem_ref[0, r, :] = p[:, A_QMEM[0]:A_QMEM[1]].astype(BF16)
        gate = p[:, A_GATE[0]:A_GATE[1]]
        gate_ref[0, r, :] = (gate * jax.nn.sigmoid(gate)).astype(BF16)

        c_q = _rms(p[:, A_CQ[0]:A_CQ[1]], qa_g_ref[...]).astype(BF16)
        q = _dot(c_q, w_uq_ref[...])
        c_kv = _rms(p[:, A_CKV[0]:A_CKV[1]], kva_g_ref[...]).astype(BF16)
        kv = _dot(c_kv, w_ukv_ref[...])

        kr = p[:, A_KROPE[0]:A_KROPE[1]] * cs
        k_rot = (kr + pltpu.roll(kr, QK_ROPE_DIM, 1)).astype(BF16)
        for hd in range(MLA_HEADS):
            nope = slice(hd * QK_NOPE_DIM, (hd + 1) * QK_NOPE_DIM)
            rope = slice(Q_ROPE_OFF + hd * LANES, Q_ROPE_OFF + (hd + 1) * LANES)
            q_ref[0, hd, 0:LANES, r] = (q[:, nope] * q_scale).T.astype(BF16)
            q_ref[0, hd, LANES:2 * LANES, r] = (q[:, rope] * cs * q_scale).T.astype(BF16)
            k_ref[0, hd, r, 0:LANES] = kv[:, 2 * hd * LANES:(2 * hd + 1) * LANES].astype(BF16)
            k_ref[0, hd, r, LANES:2 * LANES] = k_rot
            v_ref[0, hd, 0:V_HEAD_DIM, r] = kv[:, (2 * hd + 1) * LANES:(2 * hd + 2) * LANES].T.astype(BF16)
            v_ref[0, hd, V_HEAD_DIM:V_EXT_DIM, r] = ones_row


def _rope_cols(w, start):
    half = QK_ROPE_DIM // 2
    return [w[:, start:start + QK_ROPE_DIM], w[:, start + half:start + QK_ROPE_DIM], w[:, start:start + half]]


def _mla_in(x, rope_operands, pre_g, w_in, q_a_g, w_uq, kv_a_g, w_ukv):
    B, S, D = x.shape
    tm = min(ROW_TILE, S)
    pos4, invf, sel = rope_operands
    pos_rows = tm // ROPE_PER_ROW
    tiles_per_seq = S // tm
    o_kr = Q_LORA_RANK + KV_LORA_RANK
    o_qm = o_kr + QK_ROPE_DIM
    w_in_p = jnp.concatenate([w_in[:, :o_kr], w_in[:, o_qm:]] + _rope_cols(w_in, o_kr), axis=1).astype(BF16)
    head_w = QK_NOPE_DIM + QK_ROPE_DIM
    uq_cols = [w_uq[:, h * head_w:h * head_w + QK_NOPE_DIM] for h in range(MLA_HEADS)]
    for h in range(MLA_HEADS):
        uq_cols += _rope_cols(w_uq, h * head_w + QK_NOPE_DIM)
    w_uq_p = jnp.concatenate(uq_cols, axis=1).astype(BF16)
    w_ukv_p = w_ukv.astype(BF16)

    row = lambda b, i: (b, i, 0)
    head_row = lambda b, i: (b, 0, i, 0)
    head_col = lambda b, i: (b, 0, 0, i)
    out_shape = [
        jax.ShapeDtypeStruct((B, MLA_HEADS, 2 * LANES, S), BF16),
        jax.ShapeDtypeStruct((B, MLA_HEADS, S, 2 * LANES), BF16),
        jax.ShapeDtypeStruct((B, MLA_HEADS, V_EXT_DIM, S), BF16),
        jax.ShapeDtypeStruct((B, S, MEM_WIDTH), BF16),
        jax.ShapeDtypeStruct((B, S, 1024), BF16),
    ]
    out_specs = [
        pl.BlockSpec((1, MLA_HEADS, 2 * LANES, tm), head_col),
        pl.BlockSpec((1, MLA_HEADS, tm, 2 * LANES), head_row),
        pl.BlockSpec((1, MLA_HEADS, V_EXT_DIM, tm), head_col),
        pl.BlockSpec((1, tm, MEM_WIDTH), row),
        pl.BlockSpec((1, tm, 1024), row),
    ]
    return pl.pallas_call(
        _mla_in_kernel,
        grid=(B, S // tm),
        in_specs=[pl.BlockSpec((1, tm, D), row),
                  pl.BlockSpec((pos_rows, LANES), lambda b, i: (b * tiles_per_seq + i, 0)),
                  _const_spec(invf.shape), _const_spec(sel.shape),
                  _const_spec((1, D)), _const_spec(w_in_p.shape),
                  _const_spec((1, Q_LORA_RANK)), _const_spec(w_uq_p.shape),
                  _const_spec((1, KV_LORA_RANK)), _const_spec(w_ukv_p.shape)],
        out_specs=out_specs,
        out_shape=out_shape,
        scratch_shapes=[pltpu.VMEM((tm, LANES), F32)],
        compiler_params=_params("parallel", "parallel"),
        name="mla_in",
    )(x, pos4, invf, sel, pre_g.reshape(1, D), w_in_p, q_a_g.reshape(1, -1), w_uq_p, kv_a_g.reshape(1, -1),
      w_ukv_p)


def _attn_kernel(qt_ref, k_ref, vt_ref, o_ref, *, tq, tk, cw):
    heads = k_ref.shape[1]
    S = k_ref.shape[2]
    dve = vt_ref.shape[2]
    dv = V_HEAD_DIM
    nc = tq // cw
    causal = (lax.broadcasted_iota(jnp.int32, (cw, cw), 0) <= lax.broadcasted_iota(jnp.int32, (cw, cw), 1))

    units = []
    for j in range(S // tq):
        for ki in range(j * (tq // tk)):
            units += [(j, h, c, ki * tk, tk, False) for c in range(nc) for h in range(heads)]
        units += [(j, h, c, j * tq, (c + 1) * cw, True) for c in range(nc) for h in range(heads)]

    def score(unit):
        j, h, c, k0, n, diagonal = unit
        s = _dot(k_ref[0, h, pl.ds(k0, n), :], qt_ref[0, h, :, pl.ds(j * tq + c * cw, cw)])
        if diagonal:
            s_diag = jnp.where(causal, s[n - cw:, :], -jnp.inf)
            s = jnp.concatenate([s[:n - cw, :], s_diag], axis=0) if n > cw else s_diag
        return s

    scores, carries = {}, {}
    for i in range(len(units) + ATTN_LOOKAHEAD):
        if i < len(units):
            scores[i] = score(units[i])
        if i < ATTN_LOOKAHEAD:
            continue
        j, h, c, k0, n, diagonal = units[i - ATTN_LOOKAHEAD]
        m, acc = carries.pop((j, h, c), (jnp.full((1, cw), -jnp.inf, F32), jnp.zeros((dve, cw), F32)))
        s = scores.pop(i - ATTN_LOOKAHEAD)
        m_new = jnp.maximum(m, jnp.max(s, axis=0, keepdims=True))
        p = jnp.exp2(s - m_new).astype(BF16)
        acc = jnp.exp2(m - m_new) * acc + _dot(vt_ref[0, h, :, pl.ds(k0, n)], p)
        if diagonal:
            out = acc[0:dv, :] / acc[dv:dv + 1, :]
            o_ref[0, pl.ds(j * tq + c * cw, cw), h * dv:(h + 1) * dv] = out.T.astype(o_ref.dtype)
        else:
            carries[j, h, c] = (m_new, acc)


def _attention(qt, k, vt):
    B, H, S, dqk = k.shape
    dve = vt.shape[2]
    tq = min(ATTN_TQ, S)
    tk = min(ATTN_TK, tq)
    cw = min(ATTN_CW, tq)
    hp = ATTN_HEADS_PER_STEP
    head = lambda b, h: (b, h, 0, 0)
    return pl.pallas_call(
        functools.partial(_attn_kernel, tq=tq, tk=tk, cw=cw),
        grid=(B, H // hp),
        in_specs=[pl.BlockSpec((1, hp, dqk, S), head), pl.BlockSpec((1, hp, S, dqk), head),
                  pl.BlockSpec((1, hp, dve, S), head)],
        out_specs=pl.BlockSpec((1, S, hp * V_HEAD_DIM), lambda b, h: (b, 0, h)),
        out_shape=jax.ShapeDtypeStruct((B, S, H * V_HEAD_DIM), BF16),
        compiler_params=_params("parallel", "parallel"),
        name="mla_attention",
    )(qt, k, vt)


def _memory_probs(s, n_mem):
    probs = []
    for h in range(MEM_HEADS):
        sh = s[:, h * n_mem:(h + 1) * n_mem]
        e = jnp.exp2(sh - jnp.max(sh, axis=-1, keepdims=True))
        probs.append((e / jnp.sum(e, axis=-1, keepdims=True)).astype(BF16))
    return jnp.concatenate(probs, axis=-1)


def _tail(gated_mix, x_ref, qmem_ref, gate_ref, kexp_ref, vexp_ref, w_out_ref, post_g_ref, o_ref):
    tm = x_ref.shape[1]
    sub = min(TAIL_SUB_ROWS, tm)
    blocks = [slice(i * sub, (i + 1) * sub) for i in range(tm // sub)]
    n_mem = kexp_ref.shape[1] // MEM_HEADS
    kexp = kexp_ref[0]
    scores = [_dot_nt(qmem_ref[0, r, :], kexp) for r in blocks]
    for r, s in zip(blocks, scores):
        mo = _dot(_memory_probs(s, n_mem), vexp_ref[0])
        y_mix = gated_mix(r, gate_ref[0, r, 0:MIX_WIDTH])
        y_mem = (mo * gate_ref[0, r, MIX_WIDTH:].astype(F32)).astype(BF16)
        y = _dot(y_mix, w_out_ref[0:MIX_WIDTH, :]) + _dot(y_mem, w_out_ref[MIX_WIDTH:, :])
        o_ref[0, r, :] = x_ref[0, r, :] + _rms(y, post_g_ref[...])


def _mla_out_kernel(x_ref, mix_ref, qmem_ref, gate_ref, kexp_ref, vexp_ref, w_out_ref, post_g_ref, o_ref):
    gated_mix = lambda r, gate: mix_ref[0, r, :] * gate
    _tail(gated_mix, x_ref, qmem_ref, gate_ref, kexp_ref, vexp_ref, w_out_ref, post_g_ref, o_ref)


def _mlstm_out_kernel(x_ref, hn_ref, og_ref, uc_ref, skip_ref, qmem_ref, gate_ref, kexp_ref, vexp_ref,
                      w_out_ref, post_g_ref, o_ref):
    def gated_mix(r, gate):
        mix = og_ref[0, r, :].astype(F32) * hn_ref[0, r, :].astype(F32) + skip_ref[...] * uc_ref[0, r, :].astype(F32)
        return (mix * gate.astype(F32)).astype(BF16)

    _tail(gated_mix, x_ref, qmem_ref, gate_ref, kexp_ref, vexp_ref, w_out_ref, post_g_ref, o_ref)


def _layer_out(kernel_fn, name, x, mixer_inputs, mixer_specs, qmem, gate, kexp, vexp, w_out, post_g):
    B, S, D = x.shape
    tm = min(TAIL_ROW_TILE, S)
    row = lambda b, i: (b, i, 0)
    per_batch = lambda b, i: (b, 0, 0)
    in_specs = ([pl.BlockSpec((1, tm, D), row)] + mixer_specs(tm) +
                [pl.BlockSpec((1, tm, MEM_WIDTH), row), pl.BlockSpec((1, tm, gate.shape[-1]), row),
                 pl.BlockSpec((1,) + kexp.shape[1:], per_batch), pl.BlockSpec((1,) + vexp.shape[1:], per_batch),
                 _const_spec(w_out.shape), _const_spec((1, D))])
    return pl.pallas_call(
        kernel_fn,
        grid=(B, S // tm),
        in_specs=in_specs,
        out_specs=pl.BlockSpec((1, tm, D), row),
        out_shape=jax.ShapeDtypeStruct((B, S, D), F32),
        compiler_params=_params("parallel", "parallel"),
        name=name,
    )(x, *mixer_inputs, qmem, gate, kexp, vexp, w_out.astype(BF16), post_g.reshape(1, D))


B_GATES = (MIX_WIDTH, MIX_WIDTH + LANES)
B_O = (0, MIX_WIDTH)
B_QMEM = (B_O[1], B_O[1] + MEM_WIDTH)
B_GATE = (B_QMEM[1], B_QMEM[1] + 1024)
CONV_HALO = 8
V_ONES_ROWS = 16


def _mlstm_chunk(carry, qts, ks, vts, gcol, grow, causal, between):
    H = len(qts)
    L = ks[0].shape[0]
    qk = [_dot(ks[h], qts[h]) for h in range(H)]
    cq = [_dot(carry[h][0].astype(BF16), qts[h]) for h in range(H)]
    between()
    out, normed = [], []
    for h in range(H):
        C, m = carry[h]
        c_col = gcol[:, h:h + 1]
        li_row = grow[h:h + 1, :]
        b_row = grow[MLSTM_HEADS + h:MLSTM_HEADS + h + 1, :]
        d = jnp.where(causal, c_col + b_row, -jnp.inf)
        inter = b_row + m
        m_t = jnp.maximum(inter, jnp.max(d, axis=0, keepdims=True))
        sqk = (qk[h] * jnp.exp(d - m_t)).astype(BF16)
        num = jnp.exp(inter - m_t) * cq[h] + _dot(vts[h], sqk)
        den = jnp.maximum(jnp.abs(num[MLSTM_V_DIM:MLSTM_V_DIM + 1, :]), jnp.exp(-m_t))
        hnum = num[0:MLSTM_V_DIM, :]
        inv_den = 1.0 / den
        ms = jnp.sum(hnum * hnum, axis=0, keepdims=True) * (inv_den * inv_den * (1.0 / MLSTM_V_DIM))
        normed.append(hnum * (inv_den * lax.rsqrt(ms + EPS)))

        b_last = b_row[:, L - 1:L]
        dec = b_last - b_row + li_row
        m_new = jnp.maximum(b_last + m, jnp.max(dec, axis=1, keepdims=True))
        vw = (vts[h].astype(F32) * jnp.exp(dec - m_new)).astype(BF16)
        out.append((jnp.exp(b_last + m - m_new) * C + _dot(vw, ks[h]), m_new))
    return tuple(out), jnp.concatenate(normed, axis=0).T


def _mlstm_mixer_kernel(x_ref, pre_g_ref, w_first_ref, w_rest_ref, conv_w_ref, conv_b_ref, bias_ref, hg_ref,
                        wqk_ref, wv_ref, og_ref, qmem_ref, gate_ref, uc_ref, hn_ref, tail_sc, c_sc, m_sc, *, chunk):
    tm = x_ref.shape[1]
    blocks = [slice(j * chunk, (j + 1) * chunk) for j in range(tm // chunk)]

    @pl.when(pl.program_id(1) == 0)
    def _():
        tail_sc[...] = jnp.zeros(tail_sc.shape, F32)
        c_sc[...] = jnp.zeros(c_sc.shape, F32)
        m_sc[...] = jnp.zeros(m_sc.shape, F32)

    hs = [_rms(x_ref[0, r, :], pre_g_ref[...]).astype(BF16) for r in blocks]
    firsts = [_dot(h, w_first_ref[...]) for h in hs]
    halos = [tail_sc[...]] + [f[chunk - CONV_HALO:chunk, 0:MIX_WIDTH] for f in firsts[:-1]]
    tail_sc[...] = firsts[-1][chunk - CONV_HALO:chunk, 0:MIX_WIDTH]

    k_scale = MLSTM_QK_DIM ** -0.5
    ones_rows = jnp.where(lax.broadcasted_iota(jnp.int32, (V_ONES_ROWS, chunk), 0) == 0, 1.0, 0.0)
    lane = lax.broadcasted_iota(jnp.int32, (chunk, LANES), 1)
    is_f = (lane >= MLSTM_HEADS) & (lane < 2 * MLSTM_HEADS)
    tril = (lax.broadcasted_iota(jnp.int32, (chunk, chunk), 1)
            <= lax.broadcasted_iota(jnp.int32, (chunk, chunk), 0)).astype(BF16)
    causal = (lax.broadcasted_iota(jnp.int32, (chunk, chunk), 0)
              <= lax.broadcasted_iota(jnp.int32, (chunk, chunk), 1))
    carry = tuple((c_sc[h], m_sc[h, 0:1, 0:1]) for h in range(MLSTM_HEADS))

    for r, h, first, halo in zip(blocks, hs, firsts, halos):
        u = first[:, 0:MIX_WIDTH]
        ext = jnp.concatenate([halo, u], axis=0)
        conv = conv_b_ref[...] + u * conv_w_ref[CONV_WIDTH - 1:CONV_WIDTH, :]
        for back in range(1, CONV_WIDTH):
            tap = CONV_WIDTH - 1 - back
            conv = conv + ext[CONV_HALO - back:CONV_HALO - back + chunk, :] * conv_w_ref[tap:tap + 1, :]
        uc_bf = (conv * jax.nn.sigmoid(conv)).astype(BF16)
        uc_ref[0, r, :] = uc_bf
        u_bf = u.astype(BF16)

        g = first[:, B_GATES[0]:B_GATES[1]] + bias_ref[...]
        log_f = jnp.minimum(g, 0.0) - jnp.log1p(jnp.exp(-jnp.abs(g)))
        gates = jnp.where(lane < MLSTM_HEADS, g, jnp.where(is_f, log_f, 0.0))
        gc = jnp.where(is_f, sum(_dot(tril, term) for term in _split3(gates)), gates)
        grow = gc.T[0:2 * MLSTM_HEADS, :]
        gcol = gc - pltpu.roll(gc, LANES - MLSTM_HEADS, 1)

        qts, ks, vts = [], [], []
        for hd in range(MLSTM_HEADS):
            cols = slice(MLSTM_HEAD_START[hd], MLSTM_HEAD_START[hd] + MXU_DIM)
            qk = _dot(uc_bf[:, cols], wqk_ref[hd])
            qts.append(qk[:, :MLSTM_PAD_QK].T.astype(BF16))
            ks.append((qk[:, MLSTM_PAD_QK:] * k_scale).astype(BF16))
            vt = _dot(u_bf[:, cols], wv_ref[hd]).T
            vts.append(jnp.concatenate([vt[0:MLSTM_V_DIM, :], ones_rows, vt[MLSTM_V_DIM + V_ONES_ROWS:, :]],
                                       axis=0).astype(BF16))

        def rest_projection(r=r, h=h):
            p = _dot(h, w_rest_ref[...])
            og_ref[0, r, :] = (jax.nn.sigmoid(p[:, B_O[0]:B_O[1]]) * hg_ref[...]).astype(BF16)
            qmem_ref[0, r, :] = p[:, B_QMEM[0]:B_QMEM[1]].astype(BF16)
            gate = p[:, B_GATE[0]:B_GATE[1]]
            gate_ref[0, r, :] = (gate * jax.nn.sigmoid(gate)).astype(BF16)

        carry, hn = _mlstm_chunk(carry, qts, ks, vts, gcol, grow, causal, rest_projection)
        hn_ref[0, r, :] = hn.astype(BF16)

    for hd in range(MLSTM_HEADS):
        c_sc[hd] = carry[hd][0]
        m_sc[hd] = jnp.broadcast_to(carry[hd][1], m_sc.shape[1:])


def _mlstm_mixer(x, pre_g, w_in, gate_bias, conv_w, conv_b, w_q, w_k, w_v, head_g):
    B, S, D = x.shape
    tm = min(ROW_TILE, S)
    chunk = min(MLSTM_CHUNK, S)
    H = MLSTM_HEADS
    n_if = 2 * H
    o_if = MIX_WIDTH
    o_o = o_if + n_if
    w_first = jnp.concatenate([w_in[:, :o_o], jnp.zeros((D, LANES - n_if), w_in.dtype)], axis=1).astype(BF16)
    w_rest = w_in[:, o_o:].astype(BF16)
    wqk = jnp.zeros((H, MXU_DIM, 2 * MLSTM_PAD_QK), F32)
    wv = jnp.zeros((H, MXU_DIM, MLSTM_PAD_V), F32)
    for h in range(H):
        off = h * MLSTM_V_DIM - MLSTM_HEAD_START[h]
        wqk = wqk.at[h, off:off + MLSTM_V_DIM, 0:MLSTM_QK_DIM].set(w_q[h])
        wqk = wqk.at[h, off:off + MLSTM_V_DIM, MLSTM_PAD_QK:MLSTM_PAD_QK + MLSTM_QK_DIM].set(w_k[h])
        wv = wv.at[h, off:off + MLSTM_V_DIM, 0:MLSTM_V_DIM].set(w_v[h])
    bias = jnp.pad(gate_bias, (0, LANES - n_if)).reshape(1, LANES)
    row = lambda b, i: (b, i, 0)
    widths = [MIX_WIDTH, MEM_WIDTH, 1024, MIX_WIDTH, MIX_WIDTH]
    return pl.pallas_call(
        functools.partial(_mlstm_mixer_kernel, chunk=chunk),
        grid=(B, S // tm),
        in_specs=[pl.BlockSpec((1, tm, D), row), _const_spec((1, D)), _const_spec(w_first.shape),
                  _const_spec(w_rest.shape), _const_spec((CONV_WIDTH, MIX_WIDTH)), _const_spec((1, MIX_WIDTH)),
                  _const_spec((1, LANES)), _const_spec((1, MIX_WIDTH)), _const_spec(wqk.shape),
                  _const_spec(wv.shape)],
        out_specs=[pl.BlockSpec((1, tm, w), row) for w in widths],
        out_shape=[jax.ShapeDtypeStruct((B, S, w), BF16) for w in widths],
        scratch_shapes=[pltpu.VMEM((CONV_HALO, MIX_WIDTH), F32),
                        pltpu.VMEM((H, MLSTM_PAD_V, MLSTM_PAD_QK), F32), pltpu.VMEM((H, 8, LANES), F32)],
        compiler_params=_params("parallel", "arbitrary"),
        name="mlstm_mixer",
    )(x, pre_g.reshape(1, D), w_first, w_rest, conv_w, conv_b.reshape(1, -1), bias, head_g.reshape(1, MIX_WIDTH),
      wqk.astype(BF16), wv.astype(BF16))


def _mla_layer(x, mem, rope_operands, pre_g, w_in, q_a_g, w_uq, kv_a_g, w_ukv, mem_g, w_mem_kv, w_out, post_g):
    q, k, v, qmem, gate = _mla_in(x, rope_operands, pre_g, w_in, q_a_g, w_uq, kv_a_g, w_ukv)
    mix = _attention(q, k, v)
    kexp, vexp = _mem_kv(mem, mem_g, w_mem_kv)
    specs = lambda tm: [pl.BlockSpec((1, tm, MIX_WIDTH), lambda b, i: (b, i, 0))]
    return _layer_out(_mla_out_kernel, "mla_out", x, [mix], specs, qmem, gate, kexp, vexp, w_out, post_g)


def _mlstm_layer(x, mem, pre_g, w_in, gate_bias, conv_w, conv_b, w_q, w_k, w_v, head_g, skip,
                 mem_g, w_mem_kv, w_out, post_g):
    og, qmem, gate, uc, hn = _mlstm_mixer(x, pre_g, w_in, gate_bias, conv_w, conv_b, w_q, w_k, w_v, head_g)
    kexp, vexp = _mem_kv(mem, mem_g, w_mem_kv)
    row = lambda b, i: (b, i, 0)
    specs = lambda tm: [pl.BlockSpec((1, tm, MIX_WIDTH), row),
                        pl.BlockSpec((1, tm, MIX_WIDTH), row), pl.BlockSpec((1, tm, MIX_WIDTH), row),
                        _const_spec((1, MIX_WIDTH))]
    return _layer_out(_mlstm_out_kernel, "mlstm_out", x, [hn, og, uc, skip.reshape(1, -1)], specs,
                      qmem, gate, kexp, vexp, w_out, post_g)


def kernel(x, mem, positions, a_pre_g, a_w_in, a_q_a_g, a_w_uq, a_kv_a_g, a_w_ukv, a_mem_g, a_w_mem_kv, a_w_out, a_post_g, b_pre_g, b_w_in, b_gate_bias, b_conv_w, b_conv_b, b_w_q, b_w_k, b_w_v, b_head_g, b_skip, b_mem_g, b_w_mem_kv, b_w_out, b_post_g):
    depth = a_pre_g.shape[0] + b_pre_g.shape[0]
    rope_operands = _rope_operands(positions)
    for i in range(depth):
        j = i // 2
        if i % 2 == 0:
            x = _mla_layer(x, mem, rope_operands, a_pre_g[j], a_w_in[j], a_q_a_g[j], a_w_uq[j], a_kv_a_g[j],
                           a_w_ukv[j], a_mem_g[j], a_w_mem_kv[j], a_w_out[j], a_post_g[j])
        else:
            x = _mlstm_layer(x, mem, b_pre_g[j], b_w_in[j], b_gate_bias[j], b_conv_w[j], b_conv_b[j],
                             b_w_q[j], b_w_k[j], b_w_v[j], b_head_g[j], b_skip[j], b_mem_g[j],
                             b_w_mem_kv[j], b_w_out[j], b_post_g[j])
    return x
```

```python
import functools

import jax
import jax.numpy as jnp
import numpy as np
from jax import lax
from jax.experimental import pallas as pl
from jax.experimental.pallas import tpu as pltpu

EPS = 1e-6
ROPE_THETA = 10000.0
MEM_HEADS = 4
MEM_HEAD_DIM = 64
MEM_WIDTH = MEM_HEADS * MEM_HEAD_DIM
QK_NOPE_DIM = 128
QK_ROPE_DIM = 64
V_HEAD_DIM = 128
MLA_HEADS = 6
V_EXT_DIM = V_HEAD_DIM + 16
Q_LORA_RANK = 384
KV_LORA_RANK = 256
MLSTM_HEADS = 4
MLSTM_V_DIM = 192
MLSTM_QK_DIM = 96
CONV_WIDTH = 4
MIX_WIDTH = 768

LANES = 128
MXU_DIM = 256
VMEM_LIMIT_BYTES = 56 * 1024 * 1024

ROW_TILE = 512
IN_SUB_ROWS = 256
TAIL_ROW_TILE = 1024
TAIL_SUB_ROWS = 256
ATTN_TQ = 1024
ATTN_CW = 256
ATTN_TK = 512
ATTN_HEADS_PER_STEP = 2
ATTN_LOOKAHEAD = 8
MLSTM_CHUNK = 256

LOG2E = 1.4426950408889634
MLSTM_PAD_QK = LANES
MLSTM_PAD_V = MXU_DIM
MLSTM_HEAD_START = tuple((h * MLSTM_V_DIM // LANES) * LANES for h in range(MLSTM_HEADS))

F32 = jnp.float32
BF16 = jnp.bfloat16


def _dot(a, b):
    return jnp.dot(a, b, preferred_element_type=F32)


def _dot_nt(a, b):
    return lax.dot_general(a, b, (((1,), (1,)), ((), ())), preferred_element_type=F32)


def _rms(x, g, width=None):
    width = x.shape[-1] if width is None else width
    ms = jnp.sum(x * x, axis=-1, keepdims=True) * (1.0 / width)
    return x * lax.rsqrt(ms + EPS) * g


def _split3(x):
    b1 = x.astype(BF16)
    r1 = x - b1.astype(F32)
    b2 = r1.astype(BF16)
    b3 = (r1 - b2.astype(F32)).astype(BF16)
    return b1, b2, b3


def _params(*semantics):
    return pltpu.CompilerParams(dimension_semantics=semantics, vmem_limit_bytes=VMEM_LIMIT_BYTES)


def _const_spec(shape):
    zeros = (0,) * len(shape)
    return pl.BlockSpec(shape, lambda *_: zeros)


ROPE_HALF = QK_ROPE_DIM // 2
ROPE_PER_ROW = LANES // ROPE_HALF


def _rope_operands(positions):
    B, S = positions.shape
    rows = B * S // ROPE_PER_ROW
    inv_freq = ROPE_THETA ** (-jnp.arange(0, QK_ROPE_DIM, 2, dtype=F32) / QK_ROPE_DIM)
    pos4 = jnp.repeat(positions.reshape(rows, ROPE_PER_ROW), ROPE_HALF, axis=1)
    invf = jnp.tile(inv_freq, ROPE_PER_ROW).reshape(1, LANES)
    sel = np.zeros((2 * LANES, ROPE_PER_ROW * LANES), np.float32)
    for t in range(ROPE_PER_ROW):
        for f in range(ROPE_HALF):
            sel[t * ROPE_HALF + f, t * LANES + f] = 1.0
            sel[t * ROPE_HALF + f, t * LANES + ROPE_HALF + f] = 1.0
            sel[LANES + t * ROPE_HALF + f, t * LANES + 2 * ROPE_HALF + f] = -1.0
            sel[LANES + t * ROPE_HALF + f, t * LANES + 3 * ROPE_HALF + f] = 1.0
    return pos4, invf, jnp.asarray(sel, BF16)


def _rope_table(pos_ref, invf_ref, sel_ref, cs_sc):
    ang = pos_ref[...].astype(F32) * invf_ref[...]
    trig = jnp.concatenate([jnp.cos(ang), jnp.sin(ang)], axis=1)
    spread = sum(_dot(term, sel_ref[...]) for term in _split3(trig))
    rows = trig.shape[0]
    for t in range(ROPE_PER_ROW):
        cs_sc[pl.ds(t, rows, stride=ROPE_PER_ROW), :] = spread[:, t * LANES:(t + 1) * LANES]
    return cs_sc[...]


def _mem_kv_kernel(mem_ref, g_ref, w_ref, k_ref, v_ref):
    n_mem = mem_ref.shape[1]
    hn = _rms(mem_ref[0], g_ref[...]).astype(BF16)
    kv = _dot(hn, w_ref[...])
    k = kv[:, :MEM_WIDTH] * (MEM_HEAD_DIM ** -0.5 * LOG2E)
    v = kv[:, MEM_WIDTH:]
    col_head = lax.broadcasted_iota(jnp.int32, (n_mem, MEM_WIDTH), 1) // MEM_HEAD_DIM
    for h in range(MEM_HEADS):
        rows = pl.ds(h * n_mem, n_mem)
        k_ref[0, rows, :] = jnp.where(col_head == h, k, 0.0).astype(BF16)
        v_ref[0, rows, :] = jnp.where(col_head == h, v, 0.0).astype(BF16)


def _mem_kv(mem, mem_g, w_mem_kv):
    B, n_mem, D = mem.shape
    out = jax.ShapeDtypeStruct((B, MEM_HEADS * n_mem, MEM_WIDTH), BF16)
    spec = pl.BlockSpec((1, MEM_HEADS * n_mem, MEM_WIDTH), lambda b: (b, 0, 0))
    return pl.pallas_call(
        _mem_kv_kernel,
        grid=(B,),
        in_specs=[pl.BlockSpec((1, n_mem, D), lambda b: (b, 0, 0)), _const_spec((1, D)),
                  _const_spec((D, 2 * MEM_WIDTH))],
        out_specs=[spec, spec],
        out_shape=[out, out],
        compiler_params=_params("parallel"),
        name="mem_kv",
    )(mem, mem_g.reshape(1, D), w_mem_kv.astype(BF16))


A_CQ = (0, Q_LORA_RANK)
A_CKV = (A_CQ[1], A_CQ[1] + KV_LORA_RANK)
A_QMEM = (A_CKV[1], A_CKV[1] + MEM_WIDTH)
A_GATE = (A_QMEM[1], A_QMEM[1] + 1024)
A_KROPE = (A_GATE[1], A_GATE[1] + 2 * QK_ROPE_DIM)
Q_ROPE_OFF = MLA_HEADS * QK_NOPE_DIM


def _mla_in_kernel(x_ref, pos_ref, invf_ref, sel_ref, pre_g_ref, w_in_ref, qa_g_ref, w_uq_ref, kva_g_ref,
                   w_ukv_ref, q_ref, k_ref, v_ref, qmem_ref, gate_ref, cs_sc):
    q_scale = (QK_NOPE_DIM + QK_ROPE_DIM) ** -0.5 * LOG2E
    tm = x_ref.shape[1]
    sub = min(IN_SUB_ROWS, tm)
    blocks = [slice(i * sub, (i + 1) * sub) for i in range(tm // sub)]
    projected = [_dot(_rms(x_ref[0, r, :], pre_g_ref[...]).astype(BF16), w_in_ref[...]) for r in blocks]
    cs_tile = _rope_table(pos_ref, invf_ref, sel_ref, cs_sc)
    pad_row = lax.broadcasted_iota(jnp.int32, (V_EXT_DIM - V_HEAD_DIM, sub), 0)
    ones_row = jnp.where(pad_row == 0, 1.0, 0.0).astype(BF16)
    for r, p in zip(blocks, projected):
        cs = cs_tile[r, :]
        qmem_ref[0, r, :] = p[:, A_QMEM[0]:A_QMEM[1]].astype(BF16)
        gate = p[:, A_GATE[0]:A_GATE[1]]
        gate_ref[0, r, :] = (gate * jax.nn.sigmoid(gate)).astype(BF16)

        c_q = _rms(p[:, A_CQ[0]:A_CQ[1]], qa_g_ref[...]).astype(BF16)
        q = _dot(c_q, w_uq_ref[...])
        c_kv = _rms(p[:, A_CKV[0]:A_CKV[1]], kva_g_ref[...]).astype(BF16)
        kv = _dot(c_kv, w_ukv_ref[...])

        kr = p[:, A_KROPE[0]:A_KROPE[1]] * cs
        k_rot = (kr + pltpu.roll(kr, QK_ROPE_DIM, 1)).astype(BF16)
        for hd in range(MLA_HEADS):
            nope = slice(hd * QK_NOPE_DIM, (hd + 1) * QK_NOPE_DIM)
            rope = slice(Q_ROPE_OFF + hd * LANES, Q_ROPE_OFF + (hd + 1) * LANES)
            q_ref[0, hd, 0:LANES, r] = (q[:, nope] * q_scale).T.astype(BF16)
            q_ref[0, hd, LANES:2 * LANES, r] = (q[:, rope] * cs * q_scale).T.astype(BF16)
            k_ref[0, hd, r, 0:LANES] = kv[:, 2 * hd * LANES:(2 * hd + 1) * LANES].astype(BF16)
            k_ref[0, hd, r, LANES:2 * LANES] = k_rot
            v_ref[0, hd, 0:V_HEAD_DIM, r] = kv[:, (2 * hd + 1) * LANES:(2 * hd + 2) * LANES].T.astype(BF16)
            v_ref[0, hd, V_HEAD_DIM:V_EXT_DIM, r] = ones_row


def _rope_cols(w, start):
    half = QK_ROPE_DIM // 2
    return [w[:, start:start + QK_ROPE_DIM], w[:, start + half:start + QK_ROPE_DIM], w[:, start:start + half]]


def _mla_in(x, rope_operands, pre_g, w_in, q_a_g, w_uq, kv_a_g, w_ukv):
    B, S, D = x.shape
    tm = min(ROW_TILE, S)
    pos4, invf, sel = rope_operands
    pos_rows = tm // ROPE_PER_ROW
    tiles_per_seq = S // tm
    o_kr = Q_LORA_RANK + KV_LORA_RANK
    o_qm = o_kr + QK_ROPE_DIM
    w_in_p = jnp.concatenate([w_in[:, :o_kr], w_in[:, o_qm:]] + _rope_cols(w_in, o_kr), axis=1).astype(BF16)
    head_w = QK_NOPE_DIM + QK_ROPE_DIM
    uq_cols = [w_uq[:, h * head_w:h * head_w + QK_NOPE_DIM] for h in range(MLA_HEADS)]
    for h in range(MLA_HEADS):
        uq_cols += _rope_cols(w_uq, h * head_w + QK_NOPE_DIM)
    w_uq_p = jnp.concatenate(uq_cols, axis=1).astype(BF16)
    w_ukv_p = w_ukv.astype(BF16)

    row = lambda b, i: (b, i, 0)
    head_row = lambda b, i: (b, 0, i, 0)
    head_col = lambda b, i: (b, 0, 0, i)
    out_shape = [
        jax.ShapeDtypeStruct((B, MLA_HEADS, 2 * LANES, S), BF16),
        jax.ShapeDtypeStruct((B, MLA_HEADS, S, 2 * LANES), BF16),
        jax.ShapeDtypeStruct((B, MLA_HEADS, V_EXT_DIM, S), BF16),
        jax.ShapeDtypeStruct((B, S, MEM_WIDTH), BF16),
        jax.ShapeDtypeStruct((B, S, 1024), BF16),
    ]
    out_specs = [
        pl.BlockSpec((1, MLA_HEADS, 2 * LANES, tm), head_col),
        pl.BlockSpec((1, MLA_HEADS, tm, 2 * LANES), head_row),
        pl.BlockSpec((1, MLA_HEADS, V_EXT_DIM, tm), head_col),
        pl.BlockSpec((1, tm, MEM_WIDTH), row),
        pl.BlockSpec((1, tm, 1024), row),
    ]
    return pl.pallas_call(
        _mla_in_kernel,
        grid=(B, S // tm),
        in_specs=[pl.BlockSpec((1, tm, D), row),
                  pl.BlockSpec((pos_rows, LANES), lambda b, i: (b * tiles_per_seq + i, 0)),
                  _const_spec(invf.shape), _const_spec(sel.shape),
                  _const_spec((1, D)), _const_spec(w_in_p.shape),
                  _const_spec((1, Q_LORA_RANK)), _const_spec(w_uq_p.shape),
                  _const_spec((1, KV_LORA_RANK)), _const_spec(w_ukv_p.shape)],
        out_specs=out_specs,
        out_shape=out_shape,
        scratch_shapes=[pltpu.VMEM((tm, LANES), F32)],
        compiler_params=_params("parallel", "parallel"),
        name="mla_in",
    )(x, pos4, invf, sel, pre_g.reshape(1, D), w_in_p, q_a_g.reshape(1, -1), w_uq_p, kv_a_g.reshape(1, -1),
      w_ukv_p)


def _attn_kernel(qt_ref, k_ref, vt_ref, o_ref, *, tq, tk, cw):
    heads = k_ref.shape[1]
    S = k_ref.shape[2]
    dve = vt_ref.shape[2]
    dv = V_HEAD_DIM
    nc = tq // cw
    causal = (lax.broadcasted_iota(jnp.int32, (cw, cw), 0) <= lax.broadcasted_iota(jnp.int32, (cw, cw), 1))

    units = []
    for j in range(S // tq):
        for ki in range(j * (tq // tk)):
            units += [(j, h, c, ki * tk, tk, False) for c in range(nc) for h in range(heads)]
        units += [(j, h, c, j * tq, (c + 1) * cw, True) for c in range(nc) for h in range(heads)]

    def score(unit):
        j, h, c, k0, n, diagonal = unit
        s = _dot(k_ref[0, h, pl.ds(k0, n), :], qt_ref[0, h, :, pl.ds(j * tq + c * cw, cw)])
        if diagonal:
            s_diag = jnp.where(causal, s[n - cw:, :], -jnp.inf)
            s = jnp.concatenate([s[:n - cw, :], s_diag], axis=0) if n > cw else s_diag
        return s

    scores, carries = {}, {}
    for i in range(len(units) + ATTN_LOOKAHEAD):
        if i < len(units):
            scores[i] = score(units[i])
        if i < ATTN_LOOKAHEAD:
            continue
        j, h, c, k0, n, diagonal = units[i - ATTN_LOOKAHEAD]
        m, acc = carries.pop((j, h, c), (jnp.full((1, cw), -jnp.inf, F32), jnp.zeros((dve, cw), F32)))
        s = scores.pop(i - ATTN_LOOKAHEAD)
        m_new = jnp.maximum(m, jnp.max(s, axis=0, keepdims=True))
        p = jnp.exp2(s - m_new).astype(BF16)
        acc = jnp.exp2(m - m_new) * acc + _dot(vt_ref[0, h, :, pl.ds(k0, n)], p)
        if diagonal:
            out = acc[0:dv, :] / acc[dv:dv + 1, :]
            o_ref[0, pl.ds(j * tq + c * cw, cw), h * dv:(h + 1) * dv] = out.T.astype(o_ref.dtype)
        else:
            carries[j, h, c] = (m_new, acc)


def _attention(qt, k, vt):
    B, H, S, dqk = k.shape
    dve = vt.shape[2]
    tq = min(ATTN_TQ, S)
    tk = min(ATTN_TK, tq)
    cw = min(ATTN_CW, tq)
    hp = ATTN_HEADS_PER_STEP
    head = lambda b, h: (b, h, 0, 0)
    return pl.pallas_call(
        functools.partial(_attn_kernel, tq=tq, tk=tk, cw=cw),
        grid=(B, H // hp),
        in_specs=[pl.BlockSpec((1, hp, dqk, S), head), pl.BlockSpec((1, hp, S, dqk), head),
                  pl.BlockSpec((1, hp, dve, S), head)],
        out_specs=pl.BlockSpec((1, S, hp * V_HEAD_DIM), lambda b, h: (b, 0, h)),
        out_shape=jax.ShapeDtypeStruct((B, S, H * V_HEAD_DIM), BF16),
        compiler_params=_params("parallel", "parallel"),
        name="mla_attention",
    )(qt, k, vt)


def _memory_probs(s, n_mem):
    probs = []
    for h in range(MEM_HEADS):
        sh = s[:, h * n_mem:(h + 1) * n_mem]
        e = jnp.exp2(sh - jnp.max(sh, axis=-1, keepdims=True))
        probs.append((e / jnp.sum(e, axis=-1, keepdims=True)).astype(BF16))
    return jnp.concatenate(probs, axis=-1)


def _tail(gated_mix, x_ref, qmem_ref, gate_ref, kexp_ref, vexp_ref, w_out_ref, post_g_ref, o_ref):
    tm = x_ref.shape[1]
    sub = min(TAIL_SUB_ROWS, tm)
    blocks = [slice(i * sub, (i + 1) * sub) for i in range(tm // sub)]
    n_mem = kexp_ref.shape[1] // MEM_HEADS
    kexp = kexp_ref[0]
    scores = [_dot_nt(qmem_ref[0, r, :], kexp) for r in blocks]
    for r, s in zip(blocks, scores):
        mo = _dot(_memory_probs(s, n_mem), vexp_ref[0])
        y_mix = gated_mix(r, gate_ref[0, r, 0:MIX_WIDTH])
        y_mem = (mo * gate_ref[0, r, MIX_WIDTH:].astype(F32)).astype(BF16)
        y = _dot(y_mix, w_out_ref[0:MIX_WIDTH, :]) + _dot(y_mem, w_out_ref[MIX_WIDTH:, :])
        o_ref[0, r, :] = x_ref[0, r, :] + _rms(y, post_g_ref[...])


def _mla_out_kernel(x_ref, mix_ref, qmem_ref, gate_ref, kexp_ref, vexp_ref, w_out_ref, post_g_ref, o_ref):
    gated_mix = lambda r, gate: mix_ref[0, r, :] * gate
    _tail(gated_mix, x_ref, qmem_ref, gate_ref, kexp_ref, vexp_ref, w_out_ref, post_g_ref, o_ref)


def _mlstm_out_kernel(x_ref, hn_ref, og_ref, uc_ref, skip_ref, qmem_ref, gate_ref, kexp_ref, vexp_ref,
                      w_out_ref, post_g_ref, o_ref):
    def gated_mix(r, gate):
        mix = og_ref[0, r, :].astype(F32) * hn_ref[0, r, :].astype(F32) + skip_ref[...] * uc_ref[0, r, :].astype(F32)
        return (mix * gate.astype(F32)).astype(BF16)

    _tail(gated_mix, x_ref, qmem_ref, gate_ref, kexp_ref, vexp_ref, w_out_ref, post_g_ref, o_ref)


def _layer_out(kernel_fn, name, x, mixer_inputs, mixer_specs, qmem, gate, kexp, vexp, w_out, post_g):
    B, S, D = x.shape
    tm = min(TAIL_ROW_TILE, S)
    row = lambda b, i: (b, i, 0)
    per_batch = lambda b, i: (b, 0, 0)
    in_specs = ([pl.BlockSpec((1, tm, D), row)] + mixer_specs(tm) +
                [pl.BlockSpec((1, tm, MEM_WIDTH), row), pl.BlockSpec((1, tm, gate.shape[-1]), row),
                 pl.BlockSpec((1,) + kexp.shape[1:], per_batch), pl.BlockSpec((1,) + vexp.shape[1:], per_batch),
                 _const_spec(w_out.shape), _const_spec((1, D))])
    return pl.pallas_call(
        kernel_fn,
        grid=(B, S // tm),
        in_specs=in_specs,
        out_specs=pl.BlockSpec((1, tm, D), row),
        out_shape=jax.ShapeDtypeStruct((B, S, D), F32),
        compiler_params=_params("parallel", "parallel"),
        name=name,
    )(x, *mixer_inputs, qmem, gate, kexp, vexp, w_out.astype(BF16), post_g.reshape(1, D))


B_GATES = (MIX_WIDTH, MIX_WIDTH + LANES)
B_O = (0, MIX_WIDTH)
B_QMEM = (B_O[1], B_O[1] + MEM_WIDTH)
B_GATE = (B_QMEM[1], B_QMEM[1] + 1024)
CONV_HALO = 8
V_ONES_ROWS = 16


def _mlstm_chunk(carry, qts, ks, vts, gcol, grow, causal, between):
    H = len(qts)
    L = ks[0].shape[0]
    qk = [_dot(ks[h], qts[h]) for h in range(H)]
    cq = [_dot(carry[h][0].astype(BF16), qts[h]) for h in range(H)]
    between()
    out, normed = [], []
    for h in range(H):
        C, m = carry[h]
        c_col = gcol[:, h:h + 1]
        li_row = grow[h:h + 1, :]
        b_row = grow[MLSTM_HEADS + h:MLSTM_HEADS + h + 1, :]
        d = jnp.where(causal, c_col + b_row, -jnp.inf)
        inter = b_row + m
        m_t = jnp.maximum(inter, jnp.max(d, axis=0, keepdims=True))
        sqk = (qk[h] * jnp.exp(d - m_t)).astype(BF16)
        num = jnp.exp(inter - m_t) * cq[h] + _dot(vts[h], sqk)
        den = jnp.maximum(jnp.abs(num[MLSTM_V_DIM:MLSTM_V_DIM + 1, :]), jnp.exp(-m_t))
        hnum = num[0:MLSTM_V_DIM, :]
        inv_den = 1.0 / den
        ms = jnp.sum(hnum * hnum, axis=0, keepdims=True) * (inv_den * inv_den * (1.0 / MLSTM_V_DIM))
        normed.append(hnum * (inv_den * lax.rsqrt(ms + EPS)))

        b_last = b_row[:, L - 1:L]
        dec = b_last - b_row + li_row
        m_new = jnp.maximum(b_last + m, jnp.max(dec, axis=1, keepdims=True))
        vw = (vts[h].astype(F32) * jnp.exp(dec - m_new)).astype(BF16)
        out.append((jnp.exp(b_last + m - m_new) * C + _dot(vw, ks[h]), m_new))
    return tuple(out), jnp.concatenate(normed, axis=0).T


def _mlstm_mixer_kernel(x_ref, pre_g_ref, w_first_ref, w_rest_ref, conv_w_ref, conv_b_ref, bias_ref, hg_ref,
                        wqk_ref, wv_ref, og_ref, qmem_ref, gate_ref, uc_ref, hn_ref, tail_sc, c_sc, m_sc, *, chunk):
    tm = x_ref.shape[1]
    blocks = [slice(j * chunk, (j + 1) * chunk) for j in range(tm // chunk)]

    @pl.when(pl.program_id(1) == 0)
    def _():
        tail_sc[...] = jnp.zeros(tail_sc.shape, F32)
        c_sc[...] = jnp.zeros(c_sc.shape, F32)
        m_sc[...] = jnp.zeros(m_sc.shape, F32)

    hs = [_rms(x_ref[0, r, :], pre_g_ref[...]).astype(BF16) for r in blocks]
    firsts = [_dot(h, w_first_ref[...]) for h in hs]
    halos = [tail_sc[...]] + [f[chunk - CONV_HALO:chunk, 0:MIX_WIDTH] for f in firsts[:-1]]
    tail_sc[...] = firsts[-1][chunk - CONV_HALO:chunk, 0:MIX_WIDTH]

    k_scale = MLSTM_QK_DIM ** -0.5
    ones_rows = jnp.where(lax.broadcasted_iota(jnp.int32, (V_ONES_ROWS, chunk), 0) == 0, 1.0, 0.0)
    lane = lax.broadcasted_iota(jnp.int32, (chunk, LANES), 1)
    is_f = (lane >= MLSTM_HEADS) & (lane < 2 * MLSTM_HEADS)
    tril = (lax.broadcasted_iota(jnp.int32, (chunk, chunk), 1)
            <= lax.broadcasted_iota(jnp.int32, (chunk, chunk), 0)).astype(BF16)
    causal = (lax.broadcasted_iota(jnp.int32, (chunk, chunk), 0)
              <= lax.broadcasted_iota(jnp.int32, (chunk, chunk), 1))
    carry = tuple((c_sc[h], m_sc[h, 0:1, 0:1]) for h in range(MLSTM_HEADS))

    for r, h, first, halo in zip(blocks, hs, firsts, halos):
        u = first[:, 0:MIX_WIDTH]
        ext = jnp.concatenate([halo, u], axis=0)
        conv = conv_b_ref[...] + u * conv_w_ref[CONV_WIDTH - 1:CONV_WIDTH, :]
        for back in range(1, CONV_WIDTH):
            tap = CONV_WIDTH - 1 - back
            conv = conv + ext[CONV_HALO - back:CONV_HALO - back + chunk, :] * conv_w_ref[tap:tap + 1, :]
        uc_bf = (conv * jax.nn.sigmoid(conv)).astype(BF16)
        uc_ref[0, r, :] = uc_bf
        u_bf = u.astype(BF16)

        g = first[:, B_GATES[0]:B_GATES[1]] + bias_ref[...]
        log_f = jnp.minimum(g, 0.0) - jnp.log1p(jnp.exp(-jnp.abs(g)))
        gates = jnp.where(lane < MLSTM_HEADS, g, jnp.where(is_f, log_f, 0.0))
        gc = jnp.where(is_f, sum(_dot(tril, term) for term in _split3(gates)), gates)
        grow = gc.T[0:2 * MLSTM_HEADS, :]
        gcol = gc - pltpu.roll(gc, LANES - MLSTM_HEADS, 1)

        qts, ks, vts = [], [], []
        for hd in range(MLSTM_HEADS):
            cols = slice(MLSTM_HEAD_START[hd], MLSTM_HEAD_START[hd] + MXU_DIM)
            qk = _dot(uc_bf[:, cols], wqk_ref[hd])
            qts.append(qk[:, :MLSTM_PAD_QK].T.astype(BF16))
            ks.append((qk[:, MLSTM_PAD_QK:] * k_scale).astype(BF16))
            vt = _dot(u_bf[:, cols], wv_ref[hd]).T
            vts.append(jnp.concatenate([vt[0:MLSTM_V_DIM, :], ones_rows, vt[MLSTM_V_DIM + V_ONES_ROWS:, :]],
                                       axis=0).astype(BF16))

        def rest_projection(r=r, h=h):
            p = _dot(h, w_rest_ref[...])
            og_ref[0, r, :] = (jax.nn.sigmoid(p[:, B_O[0]:B_O[1]]) * hg_ref[...]).astype(BF16)
            qmem_ref[0, r, :] = p[:, B_QMEM[0]:B_QMEM[1]].astype(BF16)
            gate = p[:, B_GATE[0]:B_GATE[1]]
            gate_ref[0, r, :] = (gate * jax.nn.sigmoid(gate)).astype(BF16)

        carry, hn = _mlstm_chunk(carry, qts, ks, vts, gcol, grow, causal, rest_projection)
        hn_ref[0, r, :] = hn.astype(BF16)

    for hd in range(MLSTM_HEADS):
        c_sc[hd] = carry[hd][0]
        m_sc[hd] = jnp.broadcast_to(carry[hd][1], m_sc.shape[1:])


def _mlstm_mixer(x, pre_g, w_in, gate_bias, conv_w, conv_b, w_q, w_k, w_v, head_g):
    B, S, D = x.shape
    tm = min(ROW_TILE, S)
    chunk = min(MLSTM_CHUNK, S)
    H = MLSTM_HEADS
    n_if = 2 * H
    o_if = MIX_WIDTH
    o_o = o_if + n_if
    w_first = jnp.concatenate([w_in[:, :o_o], jnp.zeros((D, LANES - n_if), w_in.dtype)], axis=1).astype(BF16)
    w_rest = w_in[:, o_o:].astype(BF16)
    wqk = jnp.zeros((H, MXU_DIM, 2 * MLSTM_PAD_QK), F32)
    wv = jnp.zeros((H, MXU_DIM, MLSTM_PAD_V), F32)
    for h in range(H):
        off = h * MLSTM_V_DIM - MLSTM_HEAD_START[h]
        wqk = wqk.at[h, off:off + MLSTM_V_DIM, 0:MLSTM_QK_DIM].set(w_q[h])
        wqk = wqk.at[h, off:off + MLSTM_V_DIM, MLSTM_PAD_QK:MLSTM_PAD_QK + MLSTM_QK_DIM].set(w_k[h])
        wv = wv.at[h, off:off + MLSTM_V_DIM, 0:MLSTM_V_DIM].set(w_v[h])
    bias = jnp.pad(gate_bias, (0, LANES - n_if)).reshape(1, LANES)
    row = lambda b, i: (b, i, 0)
    widths = [MIX_WIDTH, MEM_WIDTH, 1024, MIX_WIDTH, MIX_WIDTH]
    return pl.pallas_call(
        functools.partial(_mlstm_mixer_kernel, chunk=chunk),
        grid=(B, S // tm),
        in_specs=[pl.BlockSpec((1, tm, D), row), _const_spec((1, D)), _const_spec(w_first.shape),
                  _const_spec(w_rest.shape), _const_spec((CONV_WIDTH, MIX_WIDTH)), _const_spec((1, MIX_WIDTH)),
                  _const_spec((1, LANES)), _const_spec((1, MIX_WIDTH)), _const_spec(wqk.shape),
                  _const_spec(wv.shape)],
        out_specs=[pl.BlockSpec((1, tm, w), row) for w in widths],
        out_shape=[jax.ShapeDtypeStruct((B, S, w), BF16) for w in widths],
        scratch_shapes=[pltpu.VMEM((CONV_HALO, MIX_WIDTH), F32),
                        pltpu.VMEM((H, MLSTM_PAD_V, MLSTM_PAD_QK), F32), pltpu.VMEM((H, 8, LANES), F32)],
        compiler_params=_params("parallel", "arbitrary"),
        name="mlstm_mixer",
    )(x, pre_g.reshape(1, D), w_first, w_rest, conv_w, conv_b.reshape(1, -1), bias, head_g.reshape(1, MIX_WIDTH),
      wqk.astype(BF16), wv.astype(BF16))


def _mla_layer(x, mem, rope_operands, pre_g, w_in, q_a_g, w_uq, kv_a_g, w_ukv, mem_g, w_mem_kv, w_out, post_g):
    q, k, v, qmem, gate = _mla_in(x, rope_operands, pre_g, w_in, q_a_g, w_uq, kv_a_g, w_ukv)
    mix = _attention(q, k, v)
    kexp, vexp = _mem_kv(mem, mem_g, w_mem_kv)
    specs = lambda tm: [pl.BlockSpec((1, tm, MIX_WIDTH), lambda b, i: (b, i, 0))]
    return _layer_out(_mla_out_kernel, "mla_out", x, [mix], specs, qmem, gate, kexp, vexp, w_out, post_g)


def _mlstm_layer(x, mem, pre_g, w_in, gate_bias, conv_w, conv_b, w_q, w_k, w_v, head_g, skip,
                 mem_g, w_mem_kv, w_out, post_g):
    og, qmem, gate, uc, hn = _mlstm_mixer(x, pre_g, w_in, gate_bias, conv_w, conv_b, w_q, w_k, w_v, head_g)
    kexp, vexp = _mem_kv(mem, mem_g, w_mem_kv)
    row = lambda b, i: (b, i, 0)
    specs = lambda tm: [pl.BlockSpec((1, tm, MIX_WIDTH), row),
                        pl.BlockSpec((1, tm, MIX_WIDTH), row), pl.BlockSpec((1, tm, MIX_WIDTH), row),
                        _const_spec((1, MIX_WIDTH))]
    return _layer_out(_mlstm_out_kernel, "mlstm_out", x, [hn, og, uc, skip.reshape(1, -1)], specs,
                      qmem, gate, kexp, vexp, w_out, post_g)


def kernel(x, mem, positions, a_pre_g, a_w_in, a_q_a_g, a_w_uq, a_kv_a_g, a_w_ukv, a_mem_g, a_w_mem_kv, a_w_out, a_post_g, b_pre_g, b_w_in, b_gate_bias, b_conv_w, b_conv_b, b_w_q, b_w_k, b_w_v, b_head_g, b_skip, b_mem_g, b_w_mem_kv, b_w_out, b_post_g):
    depth = a_pre_g.shape[0] + b_pre_g.shape[0]
    rope_operands = _rope_operands(positions)
    for i in range(depth):
        j = i // 2
        if i % 2 == 0:
            x = _mla_layer(x, mem, rope_operands, a_pre_g[j], a_w_in[j], a_q_a_g[j], a_w_uq[j], a_kv_a_g[j],
                           a_w_ukv[j], a_mem_g[j], a_w_mem_kv[j], a_w_out[j], a_post_g[j])
        else:
            x = _mlstm_layer(x, mem, b_pre_g[j], b_w_in[j], b_gate_bias[j], b_conv_w[j], b_conv_b[j],
                             b_w_q[j], b_w_k[j], b_w_v[j], b_head_g[j], b_skip[j], b_mem_g[j],
                             b_w_mem_kv[j], b_w_out[j], b_post_g[j])
    return x
```

```python
import functools

import jax
import jax.numpy as jnp
import numpy as np
from jax import lax
from jax.experimental import pallas as pl
from jax.experimental.pallas import tpu as pltpu

EPS = 1e-6
ROPE_THETA = 10000.0
MEM_HEADS = 4
MEM_HEAD_DIM = 64
MEM_WIDTH = MEM_HEADS * MEM_HEAD_DIM
QK_NOPE_DIM = 128
QK_ROPE_DIM = 64
V_HEAD_DIM = 128
MLA_HEADS = 6
V_EXT_DIM = V_HEAD_DIM + 16
Q_LORA_RANK = 384
KV_LORA_RANK = 256
MLSTM_HEADS = 4
MLSTM_V_DIM = 192
MLSTM_QK_DIM = 96
CONV_WIDTH = 4
MIX_WIDTH = 768

LANES = 128
MXU_DIM = 256
VMEM_LIMIT_BYTES = 56 * 1024 * 1024

ROW_TILE = 512
IN_SUB_ROWS = 256
TAIL_ROW_TILE = 1024
TAIL_SUB_ROWS = 256
ATTN_TQ = 1024
ATTN_CW = 256
ATTN_TK = 256
ATTN_HEADS_PER_STEP = 2
ATTN_LOOKAHEAD = 16
MLSTM_CHUNK = 256

LOG2E = 1.4426950408889634
MLSTM_PAD_QK = LANES
MLSTM_PAD_V = MXU_DIM
MLSTM_HEAD_START = tuple((h * MLSTM_V_DIM // LANES) * LANES for h in range(MLSTM_HEADS))

F32 = jnp.float32
BF16 = jnp.bfloat16


def _dot(a, b):
    return jnp.dot(a, b, preferred_element_type=F32)


def _dot_nt(a, b):
    return lax.dot_general(a, b, (((1,), (1,)), ((), ())), preferred_element_type=F32)


def _rms(x, g, width=None):
    width = x.shape[-1] if width is None else width
    ms = jnp.sum(x * x, axis=-1, keepdims=True) * (1.0 / width)
    return x * lax.rsqrt(ms + EPS) * g


def _split3(x):
    b1 = x.astype(BF16)
    r1 = x - b1.astype(F32)
    b2 = r1.astype(BF16)
    b3 = (r1 - b2.astype(F32)).astype(BF16)
    return b1, b2, b3


def _params(*semantics):
    return pltpu.CompilerParams(dimension_semantics=semantics, vmem_limit_bytes=VMEM_LIMIT_BYTES)


def _const_spec(shape):
    zeros = (0,) * len(shape)
    return pl.BlockSpec(shape, lambda *_: zeros)


ROPE_HALF = QK_ROPE_DIM // 2
ROPE_PER_ROW = LANES // ROPE_HALF


def _rope_operands(positions):
    B, S = positions.shape
    rows = B * S // ROPE_PER_ROW
    inv_freq = ROPE_THETA ** (-jnp.arange(0, QK_ROPE_DIM, 2, dtype=F32) / QK_ROPE_DIM)
    pos4 = jnp.repeat(positions.reshape(rows, ROPE_PER_ROW), ROPE_HALF, axis=1)
    invf = jnp.tile(inv_freq, ROPE_PER_ROW).reshape(1, LANES)
    sel = np.zeros((2 * LANES, ROPE_PER_ROW * LANES), np.float32)
    for t in range(ROPE_PER_ROW):
        for f in range(ROPE_HALF):
            sel[t * ROPE_HALF + f, t * LANES + f] = 1.0
            sel[t * ROPE_HALF + f, t * LANES + ROPE_HALF + f] = 1.0
            sel[LANES + t * ROPE_HALF + f, t * LANES + 2 * ROPE_HALF + f] = -1.0
            sel[LANES + t * ROPE_HALF + f, t * LANES + 3 * ROPE_HALF + f] = 1.0
    return pos4, invf, jnp.asarray(sel, BF16)


def _rope_table(pos_ref, invf_ref, sel_ref, cs_sc):
    ang = pos_ref[...].astype(F32) * invf_ref[...]
    trig = jnp.concatenate([jnp.cos(ang), jnp.sin(ang)], axis=1)
    spread = sum(_dot(term, sel_ref[...]) for term in _split3(trig))
    rows = trig.shape[0]
    for t in range(ROPE_PER_ROW):
        cs_sc[pl.ds(t, rows, stride=ROPE_PER_ROW), :] = spread[:, t * LANES:(t + 1) * LANES]
    return cs_sc[...]


def _mem_kv_kernel(mem_ref, g_ref, w_ref, k_ref, v_ref):
    n_mem = mem_ref.shape[1]
    hn = _rms(mem_ref[0], g_ref[...]).astype(BF16)
    kv = _dot(hn, w_ref[...])
    k = kv[:, :MEM_WIDTH] * (MEM_HEAD_DIM ** -0.5 * LOG2E)
    v = kv[:, MEM_WIDTH:]
    col_head = lax.broadcasted_iota(jnp.int32, (n_mem, MEM_WIDTH), 1) // MEM_HEAD_DIM
    for h in range(MEM_HEADS):
        rows = pl.ds(h * n_mem, n_mem)
        k_ref[0, rows, :] = jnp.where(col_head == h, k, 0.0).astype(BF16)
        v_ref[0, rows, :] = jnp.where(col_head == h, v, 0.0).astype(BF16)


def _mem_kv(mem, mem_g, w_mem_kv):
    B, n_mem, D = mem.shape
    out = jax.ShapeDtypeStruct((B, MEM_HEADS * n_mem, MEM_WIDTH), BF16)
    spec = pl.BlockSpec((1, MEM_HEADS * n_mem, MEM_WIDTH), lambda b: (b, 0, 0))
    return pl.pallas_call(
        _mem_kv_kernel,
        grid=(B,),
        in_specs=[pl.BlockSpec((1, n_mem, D), lambda b: (b, 0, 0)), _const_spec((1, D)),
                  _const_spec((D, 2 * MEM_WIDTH))],
        out_specs=[spec, spec],
        out_shape=[out, out],
        compiler_params=_params("parallel"),
        name="mem_kv",
    )(mem, mem_g.reshape(1, D), w_mem_kv.astype(BF16))


A_CQ = (0, Q_LORA_RANK)
A_CKV = (A_CQ[1], A_CQ[1] + KV_LORA_RANK)
A_QMEM = (A_CKV[1], A_CKV[1] + MEM_WIDTH)
A_GATE = (A_QMEM[1], A_QMEM[1] + 1024)
A_KROPE = (A_GATE[1], A_GATE[1] + 2 * QK_ROPE_DIM)
Q_ROPE_OFF = MLA_HEADS * QK_NOPE_DIM


def _mla_in_kernel(x_ref, pos_ref, invf_ref, sel_ref, pre_g_ref, w_in_ref, qa_g_ref, w_uq_ref, kva_g_ref,
                   w_ukv_ref, q_ref, k_ref, v_ref, qmem_ref, gate_ref, cs_sc):
    q_scale = (QK_NOPE_DIM + QK_ROPE_DIM) ** -0.5 * LOG2E
    tm = x_ref.shape[1]
    sub = min(IN_SUB_ROWS, tm)
    blocks = [slice(i * sub, (i + 1) * sub) for i in range(tm // sub)]
    projected = [_dot(_rms(x_ref[0, r, :], pre_g_ref[...]).astype(BF16), w_in_ref[...]) for r in blocks]
    cs_tile = _rope_table(pos_ref, invf_ref, sel_ref, cs_sc)
    pad_row = lax.broadcasted_iota(jnp.int32, (V_EXT_DIM - V_HEAD_DIM, sub), 0)
    ones_row = jnp.where(pad_row == 0, 1.0, 0.0).astype(BF16)
    for r, p in zip(blocks, projected):
        cs = cs_tile[r, :]
        qmem_ref[0, r, :] = p[:, A_QMEM[0]:A_QMEM[1]].astype(BF16)
        gate = p[:, A_GATE[0]:A_GATE[1]]
        gate_ref[0, r, :] = (gate * jax.nn.sigmoid(gate)).astype(BF16)

        c_q = _rms(p[:, A_CQ[0]:A_CQ[1]], qa_g_ref[...]).astype(BF16)
        q = _dot(c_q, w_uq_ref[...])
        c_kv = _rms(p[:, A_CKV[0]:A_CKV[1]], kva_g_ref[...]).astype(BF16)
        kv = _dot(c_kv, w_ukv_ref[...])

        kr = p[:, A_KROPE[0]:A_KROPE[1]] * cs
        k_rot = (kr + pltpu.roll(kr, QK_ROPE_DIM, 1)).astype(BF16)
        for hd in range(MLA_HEADS):
            nope = slice(hd * QK_NOPE_DIM, (hd + 1) * QK_NOPE_DIM)
            rope = slice(Q_ROPE_OFF + hd * LANES, Q_ROPE_OFF + (hd + 1) * LANES)
            q_ref[0, hd, 0:LANES, r] = (q[:, nope] * q_scale).T.astype(BF16)
            q_ref[0, hd, LANES:2 * LANES, r] = (q[:, rope] * cs * q_scale).T.astype(BF16)
            k_ref[0, hd, r, 0:LANES] = kv[:, 2 * hd * LANES:(2 * hd + 1) * LANES].astype(BF16)
            k_ref[0, hd, r, LANES:2 * LANES] = k_rot
            v_ref[0, hd, 0:V_HEAD_DIM, r] = kv[:, (2 * hd + 1) * LANES:(2 * hd + 2) * LANES].T.astype(BF16)
            v_ref[0, hd, V_HEAD_DIM:V_EXT_DIM, r] = ones_row


def _rope_cols(w, start):
    half = QK_ROPE_DIM // 2
    return [w[:, start:start + QK_ROPE_DIM], w[:, start + half:start + QK_ROPE_DIM], w[:, start:start + half]]


def _mla_in(x, rope_operands, pre_g, w_in, q_a_g, w_uq, kv_a_g, w_ukv):
    B, S, D = x.shape
    tm = min(ROW_TILE, S)
    pos4, invf, sel = rope_operands
    pos_rows = tm // ROPE_PER_ROW
    tiles_per_seq = S // tm
    o_kr = Q_LORA_RANK + KV_LORA_RANK
    o_qm = o_kr + QK_ROPE_DIM
    w_in_p = jnp.concatenate([w_in[:, :o_kr], w_in[:, o_qm:]] + _rope_cols(w_in, o_kr), axis=1).astype(BF16)
    head_w = QK_NOPE_DIM + QK_ROPE_DIM
    uq_cols = [w_uq[:, h * head_w:h * head_w + QK_NOPE_DIM] for h in range(MLA_HEADS)]
    for h in range(MLA_HEADS):
        uq_cols += _rope_cols(w_uq, h * head_w + QK_NOPE_DIM)
    w_uq_p = jnp.concatenate(uq_cols, axis=1).astype(BF16)
    w_ukv_p = w_ukv.astype(BF16)

    row = lambda b, i: (b, i, 0)
    head_row = lambda b, i: (b, 0, i, 0)
    head_col = lambda b, i: (b, 0, 0, i)
    out_shape = [
        jax.ShapeDtypeStruct((B, MLA_HEADS, 2 * LANES, S), BF16),
        jax.ShapeDtypeStruct((B, MLA_HEADS, S, 2 * LANES), BF16),
        jax.ShapeDtypeStruct((B, MLA_HEADS, V_EXT_DIM, S), BF16),
        jax.ShapeDtypeStruct((B, S, MEM_WIDTH), BF16),
        jax.ShapeDtypeStruct((B, S, 1024), BF16),
    ]
    out_specs = [
        pl.BlockSpec((1, MLA_HEADS, 2 * LANES, tm), head_col),
        pl.BlockSpec((1, MLA_HEADS, tm, 2 * LANES), head_row),
        pl.BlockSpec((1, MLA_HEADS, V_EXT_DIM, tm), head_col),
        pl.BlockSpec((1, tm, MEM_WIDTH), row),
        pl.BlockSpec((1, tm, 1024), row),
    ]
    return pl.pallas_call(
        _mla_in_kernel,
        grid=(B, S // tm),
        in_specs=[pl.BlockSpec((1, tm, D), row),
                  pl.BlockSpec((pos_rows, LANES), lambda b, i: (b * tiles_per_seq + i, 0)),
                  _const_spec(invf.shape), _const_spec(sel.shape),
                  _const_spec((1, D)), _const_spec(w_in_p.shape),
                  _const_spec((1, Q_LORA_RANK)), _const_spec(w_uq_p.shape),
                  _const_spec((1, KV_LORA_RANK)), _const_spec(w_ukv_p.shape)],
        out_specs=out_specs,
        out_shape=out_shape,
        scratch_shapes=[pltpu.VMEM((tm, LANES), F32)],
        compiler_params=_params("parallel", "parallel"),
        name="mla_in",
    )(x, pos4, invf, sel, pre_g.reshape(1, D), w_in_p, q_a_g.reshape(1, -1), w_uq_p, kv_a_g.reshape(1, -1),
      w_ukv_p)


def _attn_kernel(qt_ref, k_ref, vt_ref, o_ref, *, tq, tk, cw):
    heads = k_ref.shape[1]
    S = k_ref.shape[2]
    dve = vt_ref.shape[2]
    dv = V_HEAD_DIM
    nc = tq // cw
    causal = (lax.broadcasted_iota(jnp.int32, (cw, cw), 0) <= lax.broadcasted_iota(jnp.int32, (cw, cw), 1))

    units = []
    for j in range(S // tq):
        for k0 in range(0, (j + 1) * tq, tk):
            for c in range(nc):
                q_start = j * tq + c * cw
                if k0 < q_start + cw:
                    units += [(j, h, c, k0, tk, k0 >= q_start) for h in range(heads)]

    def score(unit):
        j, h, c, k0, n, diagonal = unit
        s = _dot(k_ref[0, h, pl.ds(k0, n), :], qt_ref[0, h, :, pl.ds(j * tq + c * cw, cw)])
        if diagonal:
            s = jnp.where(causal[k0 - (j * tq + c * cw):k0 - (j * tq + c * cw) + tk, :], s, -jnp.inf)
        return s

    scores, carries = {}, {}
    for i in range(len(units) + ATTN_LOOKAHEAD):
        if i < len(units):
            scores[i] = score(units[i])
        if i < ATTN_LOOKAHEAD:
            continue
        j, h, c, k0, n, diagonal = units[i - ATTN_LOOKAHEAD]
        m, acc = carries.pop((j, h, c), (jnp.full((1, cw), -jnp.inf, F32), jnp.zeros((dve, cw), F32)))
        s = scores.pop(i - ATTN_LOOKAHEAD)
        m_new = jnp.maximum(m, jnp.max(s, axis=0, keepdims=True))
        p = jnp.exp2(s - m_new).astype(BF16)
        acc = jnp.exp2(m - m_new) * acc + _dot(vt_ref[0, h, :, pl.ds(k0, n)], p)
        if k0 + n == j * tq + (c + 1) * cw:
            out = acc[0:dv, :] / acc[dv:dv + 1, :]
            o_ref[0, pl.ds(j * tq + c * cw, cw), h * dv:(h + 1) * dv] = out.T.astype(o_ref.dtype)
        else:
            carries[j, h, c] = (m_new, acc)


def _attention(qt, k, vt):
    B, H, S, dqk = k.shape
    dve = vt.shape[2]
    tq = min(ATTN_TQ, S)
    tk = min(ATTN_TK, tq)
    cw = min(ATTN_CW, tq)
    hp = ATTN_HEADS_PER_STEP
    head = lambda b, h: (b, h, 0, 0)
    return pl.pallas_call(
        functools.partial(_attn_kernel, tq=tq, tk=tk, cw=cw),
        grid=(B, H // hp),
        in_specs=[pl.BlockSpec((1, hp, dqk, S), head), pl.BlockSpec((1, hp, S, dqk), head),
                  pl.BlockSpec((1, hp, dve, S), head)],
        out_specs=pl.BlockSpec((1, S, hp * V_HEAD_DIM), lambda b, h: (b, 0, h)),
        out_shape=jax.ShapeDtypeStruct((B, S, H * V_HEAD_DIM), BF16),
        compiler_params=_params("parallel", "parallel"),
        name="mla_attention",
    )(qt, k, vt)


def _memory_probs(s, n_mem):
    probs = []
    for h in range(MEM_HEADS):
        sh = s[:, h * n_mem:(h + 1) * n_mem]
        e = jnp.exp2(sh - jnp.max(sh, axis=-1, keepdims=True))
        probs.append((e / jnp.sum(e, axis=-1, keepdims=True)).astype(BF16))
    return jnp.concatenate(probs, axis=-1)


def _tail(gated_mix, x_ref, qmem_ref, gate_ref, kexp_ref, vexp_ref, w_out_ref, post_g_ref, o_ref):
    tm = x_ref.shape[1]
    sub = min(TAIL_SUB_ROWS, tm)
    blocks = [slice(i * sub, (i + 1) * sub) for i in range(tm // sub)]
    n_mem = kexp_ref.shape[1] // MEM_HEADS
    kexp = kexp_ref[0]
    scores = [_dot_nt(qmem_ref[0, r, :], kexp) for r in blocks]
    for r, s in zip(blocks, scores):
        mo = _dot(_memory_probs(s, n_mem), vexp_ref[0])
        y_mix = gated_mix(r, gate_ref[0, r, 0:MIX_WIDTH])
        y_mem = (mo * gate_ref[0, r, MIX_WIDTH:].astype(F32)).astype(BF16)
        y = _dot(y_mix, w_out_ref[0:MIX_WIDTH, :]) + _dot(y_mem, w_out_ref[MIX_WIDTH:, :])
        o_ref[0, r, :] = x_ref[0, r, :] + _rms(y, post_g_ref[...])


def _mla_out_kernel(x_ref, mix_ref, qmem_ref, gate_ref, kexp_ref, vexp_ref, w_out_ref, post_g_ref, o_ref):
    gated_mix = lambda r, gate: mix_ref[0, r, :] * gate
    _tail(gated_mix, x_ref, qmem_ref, gate_ref, kexp_ref, vexp_ref, w_out_ref, post_g_ref, o_ref)


def _mlstm_out_kernel(x_ref, hn_ref, og_ref, uc_ref, skip_ref, qmem_ref, gate_ref, kexp_ref, vexp_ref,
                      w_out_ref, post_g_ref, o_ref):
    def gated_mix(r, gate):
        mix = og_ref[0, r, :].astype(F32) * hn_ref[0, r, :].astype(F32) + skip_ref[...] * uc_ref[0, r, :].astype(F32)
        return (mix * gate.astype(F32)).astype(BF16)

    _tail(gated_mix, x_ref, qmem_ref, gate_ref, kexp_ref, vexp_ref, w_out_ref, post_g_ref, o_ref)


def _layer_out(kernel_fn, name, x, mixer_inputs, mixer_specs, qmem, gate, kexp, vexp, w_out, post_g):
    B, S, D = x.shape
    tm = min(TAIL_ROW_TILE, S)
    row = lambda b, i: (b, i, 0)
    per_batch = lambda b, i: (b, 0, 0)
    in_specs = ([pl.BlockSpec((1, tm, D), row)] + mixer_specs(tm) +
                [pl.BlockSpec((1, tm, MEM_WIDTH), row), pl.BlockSpec((1, tm, gate.shape[-1]), row),
                 pl.BlockSpec((1,) + kexp.shape[1:], per_batch), pl.BlockSpec((1,) + vexp.shape[1:], per_batch),
                 _const_spec(w_out.shape), _const_spec((1, D))])
    return pl.pallas_call(
        kernel_fn,
        grid=(B, S // tm),
        in_specs=in_specs,
        out_specs=pl.BlockSpec((1, tm, D), row),
        out_shape=jax.ShapeDtypeStruct((B, S, D), F32),
        compiler_params=_params("parallel", "parallel"),
        name=name,
    )(x, *mixer_inputs, qmem, gate, kexp, vexp, w_out.astype(BF16), post_g.reshape(1, D))


B_GATES = (MIX_WIDTH, MIX_WIDTH + LANES)
B_O = (0, MIX_WIDTH)
B_QMEM = (B_O[1], B_O[1] + MEM_WIDTH)
B_GATE = (B_QMEM[1], B_QMEM[1] + 1024)
CONV_HALO = 8
V_ONES_ROWS = 16


def _mlstm_chunk(carry, qts, ks, vts, gcol, grow, causal, between):
    H = len(qts)
    L = ks[0].shape[0]
    qk = [_dot(ks[h], qts[h]) for h in range(H)]
    cq = [_dot(carry[h][0].astype(BF16), qts[h]) for h in range(H)]
    between()
    out, normed = [], []
    for h in range(H):
        C, m = carry[h]
        c_col = gcol[:, h:h + 1]
        li_row = grow[h:h + 1, :]
        b_row = grow[MLSTM_HEADS + h:MLSTM_HEADS + h + 1, :]
        d = jnp.where(causal, c_col + b_row, -jnp.inf)
        inter = b_row + m
        m_t = jnp.maximum(inter, jnp.max(d, axis=0, keepdims=True))
        sqk = (qk[h] * jnp.exp(d - m_t)).astype(BF16)
        num = jnp.exp(inter - m_t) * cq[h] + _dot(vts[h], sqk)
        den = jnp.maximum(jnp.abs(num[MLSTM_V_DIM:MLSTM_V_DIM + 1, :]), jnp.exp(-m_t))
        hnum = num[0:MLSTM_V_DIM, :]
        inv_den = 1.0 / den
        ms = jnp.sum(hnum * hnum, axis=0, keepdims=True) * (inv_den * inv_den * (1.0 / MLSTM_V_DIM))
        normed.append(hnum * (inv_den * lax.rsqrt(ms + EPS)))

        b_last = b_row[:, L - 1:L]
        dec = b_last - b_row + li_row
        m_new = jnp.maximum(b_last + m, jnp.max(dec, axis=1, keepdims=True))
        vw = (vts[h].astype(F32) * jnp.exp(dec - m_new)).astype(BF16)
        out.append((jnp.exp(b_last + m - m_new) * C + _dot(vw, ks[h]), m_new))
    return tuple(out), jnp.concatenate(normed, axis=0).T


def _mlstm_mixer_kernel(x_ref, pre_g_ref, w_first_ref, w_rest_ref, conv_w_ref, conv_b_ref, bias_ref, hg_ref,
                        wqk_ref, wv_ref, og_ref, qmem_ref, gate_ref, uc_ref, hn_ref, tail_sc, c_sc, m_sc, *, chunk):
    tm = x_ref.shape[1]
    blocks = [slice(j * chunk, (j + 1) * chunk) for j in range(tm // chunk)]

    @pl.when(pl.program_id(1) == 0)
    def _():
        tail_sc[...] = jnp.zeros(tail_sc.shape, F32)
        c_sc[...] = jnp.zeros(c_sc.shape, F32)
        m_sc[...] = jnp.zeros(m_sc.shape, F32)

    hs = [_rms(x_ref[0, r, :], pre_g_ref[...]).astype(BF16) for r in blocks]
    firsts = [_dot(h, w_first_ref[...]) for h in hs]
    halos = [tail_sc[...]] + [f[chunk - CONV_HALO:chunk, 0:MIX_WIDTH] for f in firsts[:-1]]
    tail_sc[...] = firsts[-1][chunk - CONV_HALO:chunk, 0:MIX_WIDTH]

    k_scale = MLSTM_QK_DIM ** -0.5
    ones_rows = jnp.where(lax.broadcasted_iota(jnp.int32, (V_ONES_ROWS, chunk), 0) == 0, 1.0, 0.0)
    lane = lax.broadcasted_iota(jnp.int32, (chunk, LANES), 1)
    is_f = (lane >= MLSTM_HEADS) & (lane < 2 * MLSTM_HEADS)
    tril = (lax.broadcasted_iota(jnp.int32, (chunk, chunk), 1)
            <= lax.broadcasted_iota(jnp.int32, (chunk, chunk), 0)).astype(BF16)
    causal = (lax.broadcasted_iota(jnp.int32, (chunk, chunk), 0)
              <= lax.broadcasted_iota(jnp.int32, (chunk, chunk), 1))
    carry = tuple((c_sc[h], m_sc[h, 0:1, 0:1]) for h in range(MLSTM_HEADS))

    for r, h, first, halo in zip(blocks, hs, firsts, halos):
        u = first[:, 0:MIX_WIDTH]
        ext = jnp.concatenate([halo, u], axis=0)
        conv = conv_b_ref[...] + u * conv_w_ref[CONV_WIDTH - 1:CONV_WIDTH, :]
        for back in range(1, CONV_WIDTH):
            tap = CONV_WIDTH - 1 - back
            conv = conv + ext[CONV_HALO - back:CONV_HALO - back + chunk, :] * conv_w_ref[tap:tap + 1, :]
        uc_bf = (conv * jax.nn.sigmoid(conv)).astype(BF16)
        uc_ref[0, r, :] = uc_bf
        u_bf = u.astype(BF16)

        g = first[:, B_GATES[0]:B_GATES[1]] + bias_ref[...]
        log_f = jnp.minimum(g, 0.0) - jnp.log1p(jnp.exp(-jnp.abs(g)))
        gates = jnp.where(lane < MLSTM_HEADS, g, jnp.where(is_f, log_f, 0.0))
        gc = jnp.where(is_f, sum(_dot(tril, term) for term in _split3(gates)), gates)
        grow = gc.T[0:2 * MLSTM_HEADS, :]
        gcol = gc - pltpu.roll(gc, LANES - MLSTM_HEADS, 1)

        qts, ks, vts = [], [], []
        for hd in range(MLSTM_HEADS):
            cols = slice(MLSTM_HEAD_START[hd], MLSTM_HEAD_START[hd] + MXU_DIM)
            qk = _dot(uc_bf[:, cols], wqk_ref[hd])
            qts.append(qk[:, :MLSTM_PAD_QK].T.astype(BF16))
            ks.append((qk[:, MLSTM_PAD_QK:] * k_scale).astype(BF16))
            vt = _dot(u_bf[:, cols], wv_ref[hd]).T
            vts.append(jnp.concatenate([vt[0:MLSTM_V_DIM, :], ones_rows, vt[MLSTM_V_DIM + V_ONES_ROWS:, :]],
                                       axis=0).astype(BF16))

        def rest_projection(r=r, h=h):
            p = _dot(h, w_rest_ref[...])
            og_ref[0, r, :] = (jax.nn.sigmoid(p[:, B_O[0]:B_O[1]]) * hg_ref[...]).astype(BF16)
            qmem_ref[0, r, :] = p[:, B_QMEM[0]:B_QMEM[1]].astype(BF16)
            gate = p[:, B_GATE[0]:B_GATE[1]]
            gate_ref[0, r, :] = (gate * jax.nn.sigmoid(gate)).astype(BF16)

        carry, hn = _mlstm_chunk(carry, qts, ks, vts, gcol, grow, causal, rest_projection)
        hn_ref[0, r, :] = hn.astype(BF16)

    for hd in range(MLSTM_HEADS):
        c_sc[hd] = carry[hd][0]
        m_sc[hd] = jnp.broadcast_to(carry[hd][1], m_sc.shape[1:])


def _mlstm_mixer(x, pre_g, w_in, gate_bias, conv_w, conv_b, w_q, w_k, w_v, head_g):
    B, S, D = x.shape
    tm = min(ROW_TILE, S)
    chunk = min(MLSTM_CHUNK, S)
    H = MLSTM_HEADS
    n_if = 2 * H
    o_if = MIX_WIDTH
    o_o = o_if + n_if
    w_first = jnp.concatenate([w_in[:, :o_o], jnp.zeros((D, LANES - n_if), w_in.dtype)], axis=1).astype(BF16)
    w_rest = w_in[:, o_o:].astype(BF16)
    wqk = jnp.zeros((H, MXU_DIM, 2 * MLSTM_PAD_QK), F32)
    wv = jnp.zeros((H, MXU_DIM, MLSTM_PAD_V), F32)
    for h in range(H):
        off = h * MLSTM_V_DIM - MLSTM_HEAD_START[h]
        wqk = wqk.at[h, off:off + MLSTM_V_DIM, 0:MLSTM_QK_DIM].set(w_q[h])
        wqk = wqk.at[h, off:off + MLSTM_V_DIM, MLSTM_PAD_QK:MLSTM_PAD_QK + MLSTM_QK_DIM].set(w_k[h])
        wv = wv.at[h, off:off + MLSTM_V_DIM, 0:MLSTM_V_DIM].set(w_v[h])
    bias = jnp.pad(gate_bias, (0, LANES - n_if)).reshape(1, LANES)
    row = lambda b, i: (b, i, 0)
    widths = [MIX_WIDTH, MEM_WIDTH, 1024, MIX_WIDTH, MIX_WIDTH]
    return pl.pallas_call(
        functools.partial(_mlstm_mixer_kernel, chunk=chunk),
        grid=(B, S // tm),
        in_specs=[pl.BlockSpec((1, tm, D), row), _const_spec((1, D)), _const_spec(w_first.shape),
                  _const_spec(w_rest.shape), _const_spec((CONV_WIDTH, MIX_WIDTH)), _const_spec((1, MIX_WIDTH)),
                  _const_spec((1, LANES)), _const_spec((1, MIX_WIDTH)), _const_spec(wqk.shape),
                  _const_spec(wv.shape)],
        out_specs=[pl.BlockSpec((1, tm, w), row) for w in widths],
        out_shape=[jax.ShapeDtypeStruct((B, S, w), BF16) for w in widths],
        scratch_shapes=[pltpu.VMEM((CONV_HALO, MIX_WIDTH), F32),
                        pltpu.VMEM((H, MLSTM_PAD_V, MLSTM_PAD_QK), F32), pltpu.VMEM((H, 8, LANES), F32)],
        compiler_params=_params("parallel", "arbitrary"),
        name="mlstm_mixer",
    )(x, pre_g.reshape(1, D), w_first, w_rest, conv_w, conv_b.reshape(1, -1), bias, head_g.reshape(1, MIX_WIDTH),
      wqk.astype(BF16), wv.astype(BF16))


def _mla_layer(x, mem, rope_operands, pre_g, w_in, q_a_g, w_uq, kv_a_g, w_ukv, mem_g, w_mem_kv, w_out, post_g):
    q, k, v, qmem, gate = _mla_in(x, rope_operands, pre_g, w_in, q_a_g, w_uq, kv_a_g, w_ukv)
    mix = _attention(q, k, v)
    kexp, vexp = _mem_kv(mem, mem_g, w_mem_kv)
    specs = lambda tm: [pl.BlockSpec((1, tm, MIX_WIDTH), lambda b, i: (b, i, 0))]
    return _layer_out(_mla_out_kernel, "mla_out", x, [mix], specs, qmem, gate, kexp, vexp, w_out, post_g)


def _mlstm_layer(x, mem, pre_g, w_in, gate_bias, conv_w, conv_b, w_q, w_k, w_v, head_g, skip,
                 mem_g, w_mem_kv, w_out, post_g):
    og, qmem, gate, uc, hn = _mlstm_mixer(x, pre_g, w_in, gate_bias, conv_w, conv_b, w_q, w_k, w_v, head_g)
    kexp, vexp = _mem_kv(mem, mem_g, w_mem_kv)
    row = lambda b, i: (b, i, 0)
    specs = lambda tm: [pl.BlockSpec((1, tm, MIX_WIDTH), row),
                        pl.BlockSpec((1, tm, MIX_WIDTH), row), pl.BlockSpec((1, tm, MIX_WIDTH), row),
                        _const_spec((1, MIX_WIDTH))]
    return _layer_out(_mlstm_out_kernel, "mlstm_out", x, [hn, og, uc, skip.reshape(1, -1)], specs,
                      qmem, gate, kexp, vexp, w_out, post_g)


def kernel(x, mem, positions, a_pre_g, a_w_in, a_q_a_g, a_w_uq, a_kv_a_g, a_w_ukv, a_mem_g, a_w_mem_kv, a_w_out, a_post_g, b_pre_g, b_w_in, b_gate_bias, b_conv_w, b_conv_b, b_w_q, b_w_k, b_w_v, b_head_g, b_skip, b_mem_g, b_w_mem_kv, b_w_out, b_post_g):
    depth = a_pre_g.shape[0] + b_pre_g.shape[0]
    rope_operands = _rope_operands(positions)
    for i in range(depth):
        j = i // 2
        if i % 2 == 0:
            x = _mla_layer(x, mem, rope_operands, a_pre_g[j], a_w_in[j], a_q_a_g[j], a_w_uq[j], a_kv_a_g[j],
                           a_w_ukv[j], a_mem_g[j], a_w_mem_kv[j], a_w_out[j], a_post_g[j])
        else:
            x = _mlstm_layer(x, mem, b_pre_g[j], b_w_in[j], b_gate_bias[j], b_conv_w[j], b_conv_b[j],
                             b_w_q[j], b_w_k[j], b_w_v[j], b_head_g[j], b_skip[j], b_mem_g[j],
                             b_w_mem_kv[j], b_w_out[j], b_post_g[j])
    return x
```

```python
import functools

import jax
import jax.numpy as jnp
import numpy as np
from jax import lax
from jax.experimental import pallas as pl
from jax.experimental.pallas import tpu as pltpu

EPS = 1e-6
ROPE_THETA = 10000.0
MEM_HEADS = 4
MEM_HEAD_DIM = 64
MEM_WIDTH = MEM_HEADS * MEM_HEAD_DIM
QK_NOPE_DIM = 128
QK_ROPE_DIM = 64
V_HEAD_DIM = 128
MLA_HEADS = 6
V_EXT_DIM = V_HEAD_DIM + 16
Q_LORA_RANK = 384
KV_LORA_RANK = 256
MLSTM_HEADS = 4
MLSTM_V_DIM = 192
MLSTM_QK_DIM = 96
CONV_WIDTH = 4
MIX_WIDTH = 768

LANES = 128
MXU_DIM = 256
VMEM_LIMIT_BYTES = 56 * 1024 * 1024

ROW_TILE = 512
MIXER_ROW_TILE = 1024
IN_SUB_ROWS = 256
TAIL_ROW_TILE = 1024
TAIL_SUB_ROWS = 256
ATTN_TQ = 1024
ATTN_CW = 256
ATTN_TK = 256
ATTN_HEADS_PER_STEP = 2
ATTN_LOOKAHEAD = 16
MLSTM_CHUNK = 256

LOG2E = 1.4426950408889634
MLSTM_PAD_QK = LANES
MLSTM_PAD_V = MXU_DIM
MLSTM_HEAD_START = tuple((h * MLSTM_V_DIM // LANES) * LANES for h in range(MLSTM_HEADS))

F32 = jnp.float32
BF16 = jnp.bfloat16


def _dot(a, b):
    return jnp.dot(a, b, preferred_element_type=F32)


def _dot_nt(a, b):
    return lax.dot_general(a, b, (((1,), (1,)), ((), ())), preferred_element_type=F32)


def _rms(x, g, width=None):
    width = x.shape[-1] if width is None else width
    ms = jnp.sum(x * x, axis=-1, keepdims=True) * (1.0 / width)
    return x * lax.rsqrt(ms + EPS) * g


def _split3(x):
    b1 = x.astype(BF16)
    r1 = x - b1.astype(F32)
    b2 = r1.astype(BF16)
    b3 = (r1 - b2.astype(F32)).astype(BF16)
    return b1, b2, b3


def _params(*semantics):
    return pltpu.CompilerParams(dimension_semantics=semantics, vmem_limit_bytes=VMEM_LIMIT_BYTES)


def _const_spec(shape):
    zeros = (0,) * len(shape)
    return pl.BlockSpec(shape, lambda *_: zeros)


ROPE_HALF = QK_ROPE_DIM // 2
ROPE_PER_ROW = LANES // ROPE_HALF


def _rope_operands(positions):
    B, S = positions.shape
    rows = B * S // ROPE_PER_ROW
    inv_freq = ROPE_THETA ** (-jnp.arange(0, QK_ROPE_DIM, 2, dtype=F32) / QK_ROPE_DIM)
    pos4 = jnp.repeat(positions.reshape(rows, ROPE_PER_ROW), ROPE_HALF, axis=1)
    invf = jnp.tile(inv_freq, ROPE_PER_ROW).reshape(1, LANES)
    sel = np.zeros((2 * LANES, ROPE_PER_ROW * LANES), np.float32)
    for t in range(ROPE_PER_ROW):
        for f in range(ROPE_HALF):
            sel[t * ROPE_HALF + f, t * LANES + f] = 1.0
            sel[t * ROPE_HALF + f, t * LANES + ROPE_HALF + f] = 1.0
            sel[LANES + t * ROPE_HALF + f, t * LANES + 2 * ROPE_HALF + f] = -1.0
            sel[LANES + t * ROPE_HALF + f, t * LANES + 3 * ROPE_HALF + f] = 1.0
    return pos4, invf, jnp.asarray(sel, BF16)


def _rope_table(pos_ref, invf_ref, sel_ref, cs_sc):
    ang = pos_ref[...].astype(F32) * invf_ref[...]
    trig = jnp.concatenate([jnp.cos(ang), jnp.sin(ang)], axis=1)
    spread = sum(_dot(term, sel_ref[...]) for term in _split3(trig))
    rows = trig.shape[0]
    for t in range(ROPE_PER_ROW):
        cs_sc[pl.ds(t, rows, stride=ROPE_PER_ROW), :] = spread[:, t * LANES:(t + 1) * LANES]
    return cs_sc[...]


def _mem_kv_kernel(mem_ref, g_ref, w_ref, k_ref, v_ref):
    n_mem = mem_ref.shape[1]
    mem = mem_ref[0]
    col_head = lax.broadcasted_iota(jnp.int32, (n_mem, MEM_WIDTH), 1) // MEM_HEAD_DIM
    for layer in range(w_ref.shape[0]):
        hn = _rms(mem, g_ref[layer]).astype(BF16)
        kv = _dot(hn, w_ref[layer])
        k = kv[:, :MEM_WIDTH] * (MEM_HEAD_DIM ** -0.5 * LOG2E)
        v = kv[:, MEM_WIDTH:]
        for h in range(MEM_HEADS):
            rows = pl.ds(h * n_mem, n_mem)
            k_ref[layer, 0, rows, :] = jnp.where(col_head == h, k, 0.0).astype(BF16)
            v_ref[layer, 0, rows, :] = jnp.where(col_head == h, v, 0.0).astype(BF16)


def _mem_kv(mem, mem_gs, w_mem_kvs):
    B, n_mem, D = mem.shape
    layers = w_mem_kvs.shape[0]
    out = jax.ShapeDtypeStruct((layers, B, MEM_HEADS * n_mem, MEM_WIDTH), BF16)
    spec = pl.BlockSpec((layers, 1, MEM_HEADS * n_mem, MEM_WIDTH), lambda b: (0, b, 0, 0))
    return pl.pallas_call(
        _mem_kv_kernel,
        grid=(B,),
        in_specs=[pl.BlockSpec((1, n_mem, D), lambda b: (b, 0, 0)), _const_spec((layers, 1, D)),
                  _const_spec(w_mem_kvs.shape)],
        out_specs=[spec, spec],
        out_shape=[out, out],
        compiler_params=_params("parallel"),
        name="mem_kv",
    )(mem, mem_gs.reshape(layers, 1, D), w_mem_kvs.astype(BF16))


A_CQ = (0, Q_LORA_RANK)
A_CKV = (A_CQ[1], A_CQ[1] + KV_LORA_RANK)
A_QMEM = (A_CKV[1], A_CKV[1] + MEM_WIDTH)
A_GATE = (A_QMEM[1], A_QMEM[1] + 1024)
A_KROPE = (A_GATE[1], A_GATE[1] + 2 * QK_ROPE_DIM)
Q_ROPE_OFF = MLA_HEADS * QK_NOPE_DIM


def _mla_in_kernel(x_ref, pos_ref, invf_ref, sel_ref, pre_g_ref, w_in_ref, qa_g_ref, w_uq_ref, kva_g_ref,
                   w_ukv_ref, q_ref, k_ref, v_ref, qmem_ref, gate_ref, cs_sc):
    q_scale = (QK_NOPE_DIM + QK_ROPE_DIM) ** -0.5 * LOG2E
    tm = x_ref.shape[1]
    sub = min(IN_SUB_ROWS, tm)
    blocks = [slice(i * sub, (i + 1) * sub) for i in range(tm // sub)]
    projected = [_dot(_rms(x_ref[0, r, :], pre_g_ref[...]).astype(BF16), w_in_ref[...]) for r in blocks]
    cs_tile = _rope_table(pos_ref, invf_ref, sel_ref, cs_sc)
    pad_row = lax.broadcasted_iota(jnp.int32, (V_EXT_DIM - V_HEAD_DIM, sub), 0)
    ones_row = jnp.where(pad_row == 0, 1.0, 0.0).astype(BF16)
    for r, p in zip(blocks, projected):
        cs = cs_tile[r, :]
        qmem_ref[0, r, :] = p[:, A_QMEM[0]:A_QMEM[1]].astype(BF16)
        gate = p[:, A_GATE[0]:A_GATE[1]]
        gate_ref[0, r, :] = (gate * jax.nn.sigmoid(gate)).astype(BF16)

        c_q = _rms(p[:, A_CQ[0]:A_CQ[1]], qa_g_ref[...]).astype(BF16)
        q = _dot(c_q, w_uq_ref[...])
        c_kv = _rms(p[:, A_CKV[0]:A_CKV[1]], kva_g_ref[...]).astype(BF16)
        kv = _dot(c_kv, w_ukv_ref[...])

        kr = p[:, A_KROPE[0]:A_KROPE[1]] * cs
        k_rot = (kr + pltpu.roll(kr, QK_ROPE_DIM, 1)).astype(BF16)
        for hd in range(MLA_HEADS):
            nope = slice(hd * QK_NOPE_DIM, (hd + 1) * QK_NOPE_DIM)
            rope = slice(Q_ROPE_OFF + hd * LANES, Q_ROPE_OFF + (hd + 1) * LANES)
            q_ref[0, hd, 0:LANES, r] = (q[:, nope] * q_scale).T.astype(BF16)
            q_ref[0, hd, LANES:2 * LANES, r] = (q[:, rope] * cs * q_scale).T.astype(BF16)
            k_ref[0, hd, r, 0:LANES] = kv[:, 2 * hd * LANES:(2 * hd + 1) * LANES].astype(BF16)
            k_ref[0, hd, r, LANES:2 * LANES] = k_rot
            v_ref[0, hd, 0:V_HEAD_DIM, r] = kv[:, (2 * hd + 1) * LANES:(2 * hd + 2) * LANES].T.astype(BF16)
            v_ref[0, hd, V_HEAD_DIM:V_EXT_DIM, r] = ones_row


def _rope_cols(w, start):
    half = QK_ROPE_DIM // 2
    return [w[:, start:start + QK_ROPE_DIM], w[:, start + half:start + QK_ROPE_DIM], w[:, start:start + half]]


def _mla_in(x, rope_operands, pre_g, w_in, q_a_g, w_uq, kv_a_g, w_ukv):
    B, S, D = x.shape
    tm = min(ROW_TILE, S)
    pos4, invf, sel = rope_operands
    pos_rows = tm // ROPE_PER_ROW
    tiles_per_seq = S // tm
    o_kr = Q_LORA_RANK + KV_LORA_RANK
    o_qm = o_kr + QK_ROPE_DIM
    w_in_p = jnp.concatenate([w_in[:, :o_kr], w_in[:, o_qm:]] + _rope_cols(w_in, o_kr), axis=1).astype(BF16)
    head_w = QK_NOPE_DIM + QK_ROPE_DIM
    uq_cols = [w_uq[:, h * head_w:h * head_w + QK_NOPE_DIM] for h in range(MLA_HEADS)]
    for h in range(MLA_HEADS):
        uq_cols += _rope_cols(w_uq, h * head_w + QK_NOPE_DIM)
    w_uq_p = jnp.concatenate(uq_cols, axis=1).astype(BF16)
    w_ukv_p = w_ukv.astype(BF16)

    row = lambda b, i: (b, i, 0)
    head_row = lambda b, i: (b, 0, i, 0)
    head_col = lambda b, i: (b, 0, 0, i)
    out_shape = [
        jax.ShapeDtypeStruct((B, MLA_HEADS, 2 * LANES, S), BF16),
        jax.ShapeDtypeStruct((B, MLA_HEADS, S, 2 * LANES), BF16),
        jax.ShapeDtypeStruct((B, MLA_HEADS, V_EXT_DIM, S), BF16),
        jax.ShapeDtypeStruct((B, S, MEM_WIDTH), BF16),
        jax.ShapeDtypeStruct((B, S, 1024), BF16),
    ]
    out_specs = [
        pl.BlockSpec((1, MLA_HEADS, 2 * LANES, tm), head_col),
        pl.BlockSpec((1, MLA_HEADS, tm, 2 * LANES), head_row),
        pl.BlockSpec((1, MLA_HEADS, V_EXT_DIM, tm), head_col),
        pl.BlockSpec((1, tm, MEM_WIDTH), row),
        pl.BlockSpec((1, tm, 1024), row),
    ]
    return pl.pallas_call(
        _mla_in_kernel,
        grid=(B, S // tm),
        in_specs=[pl.BlockSpec((1, tm, D), row),
                  pl.BlockSpec((pos_rows, LANES), lambda b, i: (b * tiles_per_seq + i, 0)),
                  _const_spec(invf.shape), _const_spec(sel.shape),
                  _const_spec((1, D)), _const_spec(w_in_p.shape),
                  _const_spec((1, Q_LORA_RANK)), _const_spec(w_uq_p.shape),
                  _const_spec((1, KV_LORA_RANK)), _const_spec(w_ukv_p.shape)],
        out_specs=out_specs,
        out_shape=out_shape,
        scratch_shapes=[pltpu.VMEM((tm, LANES), F32)],
        compiler_params=_params("parallel", "parallel"),
        name="mla_in",
    )(x, pos4, invf, sel, pre_g.reshape(1, D), w_in_p, q_a_g.reshape(1, -1), w_uq_p, kv_a_g.reshape(1, -1),
      w_ukv_p)


def _attn_kernel(qt_ref, k_ref, vt_ref, o_ref, *, tq, tk, cw):
    heads = k_ref.shape[1]
    S = k_ref.shape[2]
    dve = vt_ref.shape[2]
    dv = V_HEAD_DIM
    nc = tq // cw
    causal = (lax.broadcasted_iota(jnp.int32, (cw, cw), 0) <= lax.broadcasted_iota(jnp.int32, (cw, cw), 1))

    units = []
    for j in range(S // tq):
        for k0 in range(0, (j + 1) * tq, tk):
            for c in range(nc):
                q_start = j * tq + c * cw
                if k0 < q_start + cw:
                    units += [(j, h, c, k0, tk, k0 >= q_start) for h in range(heads)]

    def score(unit):
        j, h, c, k0, n, diagonal = unit
        s = _dot(k_ref[0, h, pl.ds(k0, n), :], qt_ref[0, h, :, pl.ds(j * tq + c * cw, cw)])
        if diagonal:
            s = jnp.where(causal[k0 - (j * tq + c * cw):k0 - (j * tq + c * cw) + tk, :], s, -jnp.inf)
        return s

    scores, carries = {}, {}
    for i in range(len(units) + ATTN_LOOKAHEAD):
        if i < len(units):
            scores[i] = score(units[i])
        if i < ATTN_LOOKAHEAD:
            continue
        j, h, c, k0, n, diagonal = units[i - ATTN_LOOKAHEAD]
        m, acc = carries.pop((j, h, c), (jnp.full((1, cw), -jnp.inf, F32), jnp.zeros((dve, cw), F32)))
        s = scores.pop(i - ATTN_LOOKAHEAD)
        m_new = jnp.maximum(m, jnp.max(s, axis=0, keepdims=True))
        p = jnp.exp2(s - m_new).astype(BF16)
        acc = jnp.exp2(m - m_new) * acc + _dot(vt_ref[0, h, :, pl.ds(k0, n)], p)
        if k0 + n == j * tq + (c + 1) * cw:
            out = acc[0:dv, :] / acc[dv:dv + 1, :]
            o_ref[0, pl.ds(j * tq + c * cw, cw), h * dv:(h + 1) * dv] = out.T.astype(o_ref.dtype)
        else:
            carries[j, h, c] = (m_new, acc)


def _attention(qt, k, vt):
    B, H, S, dqk = k.shape
    dve = vt.shape[2]
    tq = min(ATTN_TQ, S)
    tk = min(ATTN_TK, tq)
    cw = min(ATTN_CW, tq)
    hp = ATTN_HEADS_PER_STEP
    head = lambda b, h: (b, h, 0, 0)
    return pl.pallas_call(
        functools.partial(_attn_kernel, tq=tq, tk=tk, cw=cw),
        grid=(B, H // hp),
        in_specs=[pl.BlockSpec((1, hp, dqk, S), head), pl.BlockSpec((1, hp, S, dqk), head),
                  pl.BlockSpec((1, hp, dve, S), head)],
        out_specs=pl.BlockSpec((1, S, hp * V_HEAD_DIM), lambda b, h: (b, 0, h)),
        out_shape=jax.ShapeDtypeStruct((B, S, H * V_HEAD_DIM), BF16),
        compiler_params=_params("parallel", "parallel"),
        name="mla_attention",
    )(qt, k, vt)


def _memory_probs(s, n_mem):
    probs = []
    for h in range(MEM_HEADS):
        sh = s[:, h * n_mem:(h + 1) * n_mem]
        e = jnp.exp2(sh - jnp.max(sh, axis=-1, keepdims=True))
        probs.append((e / jnp.sum(e, axis=-1, keepdims=True)).astype(BF16))
    return jnp.concatenate(probs, axis=-1)


def _tail(gated_mix, x_ref, qmem_ref, gate_ref, kexp_ref, vexp_ref, w_out_ref, post_g_ref, o_ref):
    tm = x_ref.shape[1]
    sub = min(TAIL_SUB_ROWS, tm)
    blocks = [slice(i * sub, (i + 1) * sub) for i in range(tm // sub)]
    n_mem = kexp_ref.shape[2] // MEM_HEADS
    kexp = kexp_ref[0, 0]
    scores = [_dot_nt(qmem_ref[0, r, :], kexp) for r in blocks]
    for r, s in zip(blocks, scores):
        mo = _dot(_memory_probs(s, n_mem), vexp_ref[0, 0])
        y_mix = gated_mix(r, gate_ref[0, r, 0:MIX_WIDTH])
        y_mem = (mo * gate_ref[0, r, MIX_WIDTH:].astype(F32)).astype(BF16)
        y = _dot(y_mix, w_out_ref[0:MIX_WIDTH, :]) + _dot(y_mem, w_out_ref[MIX_WIDTH:, :])
        o_ref[0, r, :] = x_ref[0, r, :] + _rms(y, post_g_ref[...])


def _mla_out_kernel(x_ref, mix_ref, qmem_ref, gate_ref, kexp_ref, vexp_ref, w_out_ref, post_g_ref, o_ref):
    gated_mix = lambda r, gate: mix_ref[0, r, :] * gate
    _tail(gated_mix, x_ref, qmem_ref, gate_ref, kexp_ref, vexp_ref, w_out_ref, post_g_ref, o_ref)


def _mlstm_out_kernel(x_ref, hn_ref, og_ref, uc_ref, skip_ref, qmem_ref, gate_ref, kexp_ref, vexp_ref,
                      w_out_ref, post_g_ref, o_ref):
    def gated_mix(r, gate):
        mix = og_ref[0, r, :].astype(F32) * hn_ref[0, r, :].astype(F32) + skip_ref[...] * uc_ref[0, r, :].astype(F32)
        return (mix * gate.astype(F32)).astype(BF16)

    _tail(gated_mix, x_ref, qmem_ref, gate_ref, kexp_ref, vexp_ref, w_out_ref, post_g_ref, o_ref)


def _layer_out(kernel_fn, name, x, mixer_inputs, mixer_specs, qmem, gate, mem_kv, layer, w_out, post_g):
    B, S, D = x.shape
    tm = min(TAIL_ROW_TILE, S)
    kexp, vexp = mem_kv
    row = lambda b, i: (b, i, 0)
    per_batch = lambda b, i: (layer, b, 0, 0)
    in_specs = ([pl.BlockSpec((1, tm, D), row)] + mixer_specs(tm) +
                [pl.BlockSpec((1, tm, MEM_WIDTH), row), pl.BlockSpec((1, tm, gate.shape[-1]), row),
                 pl.BlockSpec((1, 1) + kexp.shape[2:], per_batch), pl.BlockSpec((1, 1) + vexp.shape[2:], per_batch),
                 _const_spec(w_out.shape), _const_spec((1, D))])
    return pl.pallas_call(
        kernel_fn,
        grid=(B, S // tm),
        in_specs=in_specs,
        out_specs=pl.BlockSpec((1, tm, D), row),
        out_shape=jax.ShapeDtypeStruct((B, S, D), F32),
        compiler_params=_params("parallel", "parallel"),
        name=name,
    )(x, *mixer_inputs, qmem, gate, kexp, vexp, w_out.astype(BF16), post_g.reshape(1, D))


B_GATES = (MIX_WIDTH, MIX_WIDTH + LANES)
B_O = (0, MIX_WIDTH)
B_QMEM = (B_O[1], B_O[1] + MEM_WIDTH)
B_GATE = (B_QMEM[1], B_QMEM[1] + 1024)
CONV_HALO = 8
V_ONES_ROWS = 16


def _mlstm_chunk(carry, qts, ks, vts, gcol, grow, causal, between):
    H = len(qts)
    L = ks[0].shape[0]
    qk = [_dot(ks[h], qts[h]) for h in range(H)]
    cq = [_dot(carry[h][0].astype(BF16), qts[h]) for h in range(H)]
    between()
    out, normed = [], []
    for h in range(H):
        C, m = carry[h]
        c_col = gcol[:, h:h + 1]
        li_row = grow[h:h + 1, :]
        b_row = grow[MLSTM_HEADS + h:MLSTM_HEADS + h + 1, :]
        d = jnp.where(causal, c_col + b_row, -jnp.inf)
        inter = b_row + m
        m_t = jnp.maximum(inter, jnp.max(d, axis=0, keepdims=True))
        sqk = (qk[h] * jnp.exp(d - m_t)).astype(BF16)
        num = jnp.exp(inter - m_t) * cq[h] + _dot(vts[h], sqk)
        den = jnp.maximum(jnp.abs(num[MLSTM_V_DIM:MLSTM_V_DIM + 1, :]), jnp.exp(-m_t))
        hnum = num[0:MLSTM_V_DIM, :]
        inv_den = 1.0 / den
        ms = jnp.sum(hnum * hnum, axis=0, keepdims=True) * (inv_den * inv_den * (1.0 / MLSTM_V_DIM))
        normed.append(hnum * (inv_den * lax.rsqrt(ms + EPS)))

        b_last = b_row[:, L - 1:L]
        dec = b_last - b_row + li_row
        m_new = jnp.maximum(b_last + m, jnp.max(dec, axis=1, keepdims=True))
        vw = (vts[h].astype(F32) * jnp.exp(dec - m_new)).astype(BF16)
        out.append((jnp.exp(b_last + m - m_new) * C + _dot(vw, ks[h]), m_new))
    return tuple(out), jnp.concatenate(normed, axis=0).T


def _mlstm_mixer_kernel(x_ref, pre_g_ref, w_first_ref, w_rest_ref, conv_w_ref, conv_b_ref, bias_ref, hg_ref,
                        wqk_ref, wv_ref, og_ref, qmem_ref, gate_ref, uc_ref, hn_ref, tail_sc, c_sc, m_sc, *, chunk):
    tm = x_ref.shape[1]
    blocks = [slice(j * chunk, (j + 1) * chunk) for j in range(tm // chunk)]

    @pl.when(pl.program_id(1) == 0)
    def _():
        tail_sc[...] = jnp.zeros(tail_sc.shape, F32)
        c_sc[...] = jnp.zeros(c_sc.shape, F32)
        m_sc[...] = jnp.zeros(m_sc.shape, F32)

    hs = [_rms(x_ref[0, r, :], pre_g_ref[...]).astype(BF16) for r in blocks]
    firsts = [_dot(h, w_first_ref[...]) for h in hs]
    halos = [tail_sc[...]] + [f[chunk - CONV_HALO:chunk, 0:MIX_WIDTH] for f in firsts[:-1]]
    tail_sc[...] = firsts[-1][chunk - CONV_HALO:chunk, 0:MIX_WIDTH]

    k_scale = MLSTM_QK_DIM ** -0.5
    ones_rows = jnp.where(lax.broadcasted_iota(jnp.int32, (V_ONES_ROWS, chunk), 0) == 0, 1.0, 0.0)
    lane = lax.broadcasted_iota(jnp.int32, (chunk, LANES), 1)
    is_f = (lane >= MLSTM_HEADS) & (lane < 2 * MLSTM_HEADS)
    tril = (lax.broadcasted_iota(jnp.int32, (chunk, chunk), 1)
            <= lax.broadcasted_iota(jnp.int32, (chunk, chunk), 0)).astype(BF16)
    causal = (lax.broadcasted_iota(jnp.int32, (chunk, chunk), 0)
              <= lax.broadcasted_iota(jnp.int32, (chunk, chunk), 1))
    carry = tuple((c_sc[h], m_sc[h, 0:1, 0:1]) for h in range(MLSTM_HEADS))

    for r, h, first, halo in zip(blocks, hs, firsts, halos):
        u = first[:, 0:MIX_WIDTH]
        ext = jnp.concatenate([halo, u], axis=0)
        conv = conv_b_ref[...] + u * conv_w_ref[CONV_WIDTH - 1:CONV_WIDTH, :]
        for back in range(1, CONV_WIDTH):
            tap = CONV_WIDTH - 1 - back
            conv = conv + ext[CONV_HALO - back:CONV_HALO - back + chunk, :] * conv_w_ref[tap:tap + 1, :]
        uc_bf = (conv * jax.nn.sigmoid(conv)).astype(BF16)
        uc_ref[0, r, :] = uc_bf
        u_bf = u.astype(BF16)

        g = first[:, B_GATES[0]:B_GATES[1]] + bias_ref[...]
        log_f = jnp.minimum(g, 0.0) - jnp.log1p(jnp.exp(-jnp.abs(g)))
        gates = jnp.where(lane < MLSTM_HEADS, g, jnp.where(is_f, log_f, 0.0))
        gc = jnp.where(is_f, sum(_dot(tril, term) for term in _split3(gates)), gates)
        grow = gc.T[0:2 * MLSTM_HEADS, :]
        gcol = gc - pltpu.roll(gc, LANES - MLSTM_HEADS, 1)

        qts, ks, vts = [], [], []
        for hd in range(MLSTM_HEADS):
            cols = slice(MLSTM_HEAD_START[hd], MLSTM_HEAD_START[hd] + MXU_DIM)
            qk = _dot(uc_bf[:, cols], wqk_ref[hd])
            qts.append(qk[:, :MLSTM_PAD_QK].T.astype(BF16))
            ks.append((qk[:, MLSTM_PAD_QK:] * k_scale).astype(BF16))
            vt = _dot(u_bf[:, cols], wv_ref[hd]).T
            vts.append(jnp.concatenate([vt[0:MLSTM_V_DIM, :], ones_rows, vt[MLSTM_V_DIM + V_ONES_ROWS:, :]],
                                       axis=0).astype(BF16))

        def rest_projection(r=r, h=h):
            p = _dot(h, w_rest_ref[...])
            og_ref[0, r, :] = (jax.nn.sigmoid(p[:, B_O[0]:B_O[1]]) * hg_ref[...]).astype(BF16)
            qmem_ref[0, r, :] = p[:, B_QMEM[0]:B_QMEM[1]].astype(BF16)
            gate = p[:, B_GATE[0]:B_GATE[1]]
            gate_ref[0, r, :] = (gate * jax.nn.sigmoid(gate)).astype(BF16)

        carry, hn = _mlstm_chunk(carry, qts, ks, vts, gcol, grow, causal, rest_projection)
        hn_ref[0, r, :] = hn.astype(BF16)

    for hd in range(MLSTM_HEADS):
        c_sc[hd] = carry[hd][0]
        m_sc[hd] = jnp.broadcast_to(carry[hd][1], m_sc.shape[1:])


def _mlstm_mixer(x, pre_g, w_in, gate_bias, conv_w, conv_b, w_q, w_k, w_v, head_g):
    B, S, D = x.shape
    tm = min(MIXER_ROW_TILE, S)
    chunk = min(MLSTM_CHUNK, S)
    H = MLSTM_HEADS
    n_if = 2 * H
    o_if = MIX_WIDTH
    o_o = o_if + n_if
    w_first = jnp.concatenate([w_in[:, :o_o], jnp.zeros((D, LANES - n_if), w_in.dtype)], axis=1).astype(BF16)
    w_rest = w_in[:, o_o:].astype(BF16)
    def placed(w, width):
        blocks = []
        for h in range(H):
            off = h * MLSTM_V_DIM - MLSTM_HEAD_START[h]
            blocks.append(jnp.pad(w[h], ((off, MXU_DIM - MLSTM_V_DIM - off), (0, width - w.shape[-1]))))
        return jnp.stack(blocks)

    wqk = jnp.concatenate([placed(w_q, MLSTM_PAD_QK), placed(w_k, MLSTM_PAD_QK)], axis=-1)
    wv = placed(w_v, MLSTM_PAD_V)
    bias = jnp.pad(gate_bias, (0, LANES - n_if)).reshape(1, LANES)
    row = lambda b, i: (b, i, 0)
    widths = [MIX_WIDTH, MEM_WIDTH, 1024, MIX_WIDTH, MIX_WIDTH]
    return pl.pallas_call(
        functools.partial(_mlstm_mixer_kernel, chunk=chunk),
        grid=(B, S // tm),
        in_specs=[pl.BlockSpec((1, tm, D), row), _const_spec((1, D)), _const_spec(w_first.shape),
                  _const_spec(w_rest.shape), _const_spec((CONV_WIDTH, MIX_WIDTH)), _const_spec((1, MIX_WIDTH)),
                  _const_spec((1, LANES)), _const_spec((1, MIX_WIDTH)), _const_spec(wqk.shape),
                  _const_spec(wv.shape)],
        out_specs=[pl.BlockSpec((1, tm, w), row) for w in widths],
        out_shape=[jax.ShapeDtypeStruct((B, S, w), BF16) for w in widths],
        scratch_shapes=[pltpu.VMEM((CONV_HALO, MIX_WIDTH), F32),
                        pltpu.VMEM((H, MLSTM_PAD_V, MLSTM_PAD_QK), F32), pltpu.VMEM((H, 8, LANES), F32)],
        compiler_params=_params("parallel", "arbitrary"),
        name="mlstm_mixer",
    )(x, pre_g.reshape(1, D), w_first, w_rest, conv_w, conv_b.reshape(1, -1), bias, head_g.reshape(1, MIX_WIDTH),
      wqk.astype(BF16), wv.astype(BF16))


def _mla_layer(x, mem_kv, layer, rope_operands, pre_g, w_in, q_a_g, w_uq, kv_a_g, w_ukv, w_out, post_g):
    q, k, v, qmem, gate = _mla_in(x, rope_operands, pre_g, w_in, q_a_g, w_uq, kv_a_g, w_ukv)
    mix = _attention(q, k, v)
    specs = lambda tm: [pl.BlockSpec((1, tm, MIX_WIDTH), lambda b, i: (b, i, 0))]
    return _layer_out(_mla_out_kernel, "mla_out", x, [mix], specs, qmem, gate, mem_kv, layer, w_out, post_g)


def _mlstm_layer(x, mem_kv, layer, pre_g, w_in, gate_bias, conv_w, conv_b, w_q, w_k, w_v, head_g, skip,
                 w_out, post_g):
    og, qmem, gate, uc, hn = _mlstm_mixer(x, pre_g, w_in, gate_bias, conv_w, conv_b, w_q, w_k, w_v, head_g)
    row = lambda b, i: (b, i, 0)
    specs = lambda tm: [pl.BlockSpec((1, tm, MIX_WIDTH), row),
                        pl.BlockSpec((1, tm, MIX_WIDTH), row), pl.BlockSpec((1, tm, MIX_WIDTH), row),
                        _const_spec((1, MIX_WIDTH))]
    return _layer_out(_mlstm_out_kernel, "mlstm_out", x, [hn, og, uc, skip.reshape(1, -1)], specs,
                      qmem, gate, mem_kv, layer, w_out, post_g)


def kernel(x, mem, positions, a_pre_g, a_w_in, a_q_a_g, a_w_uq, a_kv_a_g, a_w_ukv, a_mem_g, a_w_mem_kv, a_w_out, a_post_g, b_pre_g, b_w_in, b_gate_bias, b_conv_w, b_conv_b, b_w_q, b_w_k, b_w_v, b_head_g, b_skip, b_mem_g, b_w_mem_kv, b_w_out, b_post_g):
    depth = a_pre_g.shape[0] + b_pre_g.shape[0]
    rope_operands = _rope_operands(positions)
    mem_gs = jnp.stack([(a_mem_g, b_mem_g)[i % 2][i // 2] for i in range(depth)])
    w_mem_kvs = jnp.stack([(a_w_mem_kv, b_w_mem_kv)[i % 2][i // 2] for i in range(depth)])
    mem_kv = _mem_kv(mem, mem_gs, w_mem_kvs)
    for i in range(depth):
        j = i // 2
        if i % 2 == 0:
            x = _mla_layer(x, mem_kv, i, rope_operands, a_pre_g[j], a_w_in[j], a_q_a_g[j], a_w_uq[j], a_kv_a_g[j],
                           a_w_ukv[j], a_w_out[j], a_post_g[j])
        else:
            x = _mlstm_layer(x, mem_kv, i, b_pre_g[j], b_w_in[j], b_gate_bias[j], b_conv_w[j], b_conv_b[j],
                             b_w_q[j], b_w_k[j], b_w_v[j], b_head_g[j], b_skip[j], b_w_out[j], b_post_g[j])
    return x
```

```python
import functools

import jax
import jax.numpy as jnp
import numpy as np
from jax import lax
from jax.experimental import pallas as pl
from jax.experimental.pallas import tpu as pltpu

EPS = 1e-6
ROPE_THETA = 10000.0
MEM_HEADS = 4
MEM_HEAD_DIM = 64
MEM_WIDTH = MEM_HEADS * MEM_HEAD_DIM
QK_NOPE_DIM = 128
QK_ROPE_DIM = 64
V_HEAD_DIM = 128
MLA_HEADS = 6
V_EXT_DIM = V_HEAD_DIM + 16
Q_LORA_RANK = 384
KV_LORA_RANK = 256
MLSTM_HEADS = 4
MLSTM_V_DIM = 192
MLSTM_QK_DIM = 96
CONV_WIDTH = 4
MIX_WIDTH = 768

LANES = 128
MXU_DIM = 256
VMEM_LIMIT_BYTES = 56 * 1024 * 1024

ROW_TILE = 512
MIXER_ROW_TILE = 1024
IN_SUB_ROWS = 256
TAIL_ROW_TILE = 1024
TAIL_SUB_ROWS = 256
ATTN_TQ = 1024
ATTN_CW = 256
ATTN_TK = 256
ATTN_HEADS_PER_STEP = 2
ATTN_LOOKAHEAD = 16
MLSTM_CHUNK = 256

LOG2E = 1.4426950408889634
MLSTM_PAD_QK = LANES
MLSTM_PAD_V = MXU_DIM
MLSTM_HEAD_START = tuple((h * MLSTM_V_DIM // LANES) * LANES for h in range(MLSTM_HEADS))

F32 = jnp.float32
BF16 = jnp.bfloat16


def _dot(a, b):
    return jnp.dot(a, b, preferred_element_type=F32)


def _dot_nt(a, b):
    return lax.dot_general(a, b, (((1,), (1,)), ((), ())), preferred_element_type=F32)


def _rms(x, g, width=None):
    width = x.shape[-1] if width is None else width
    ms = jnp.sum(x * x, axis=-1, keepdims=True) * (1.0 / width)
    return x * lax.rsqrt(ms + EPS) * g


def _split3(x):
    b1 = x.astype(BF16)
    r1 = x - b1.astype(F32)
    b2 = r1.astype(BF16)
    b3 = (r1 - b2.astype(F32)).astype(BF16)
    return b1, b2, b3


def _params(*semantics):
    return pltpu.CompilerParams(dimension_semantics=semantics, vmem_limit_bytes=VMEM_LIMIT_BYTES)


def _const_spec(shape):
    zeros = (0,) * len(shape)
    return pl.BlockSpec(shape, lambda *_: zeros)


def _rope_table(pos_ref, invf_ref):
    ang = invf_ref[...] * pos_ref[0].astype(F32)
    cos_t = jnp.cos(ang)
    sin_t = jnp.sin(ang)
    return jnp.concatenate([cos_t, cos_t, -sin_t, sin_t], axis=0).T


def _mem_kv_kernel(mem_ref, g_ref, w_ref, k_ref, v_ref):
    n_mem = mem_ref.shape[1]
    mem = mem_ref[0]
    col_head = lax.broadcasted_iota(jnp.int32, (n_mem, MEM_WIDTH), 1) // MEM_HEAD_DIM
    for layer in range(w_ref.shape[0]):
        hn = _rms(mem, g_ref[layer]).astype(BF16)
        kv = _dot(hn, w_ref[layer])
        k = kv[:, :MEM_WIDTH] * (MEM_HEAD_DIM ** -0.5 * LOG2E)
        v = kv[:, MEM_WIDTH:]
        for h in range(MEM_HEADS):
            rows = pl.ds(h * n_mem, n_mem)
            k_ref[layer, 0, rows, :] = jnp.where(col_head == h, k, 0.0).astype(BF16)
            v_ref[layer, 0, rows, :] = jnp.where(col_head == h, v, 0.0).astype(BF16)


def _mem_kv(mem, mem_gs, w_mem_kvs):
    B, n_mem, D = mem.shape
    layers = w_mem_kvs.shape[0]
    out = jax.ShapeDtypeStruct((layers, B, MEM_HEADS * n_mem, MEM_WIDTH), BF16)
    spec = pl.BlockSpec((layers, 1, MEM_HEADS * n_mem, MEM_WIDTH), lambda b: (0, b, 0, 0))
    return pl.pallas_call(
        _mem_kv_kernel,
        grid=(B,),
        in_specs=[pl.BlockSpec((1, n_mem, D), lambda b: (b, 0, 0)), _const_spec((layers, 1, D)),
                  _const_spec(w_mem_kvs.shape)],
        out_specs=[spec, spec],
        out_shape=[out, out],
        compiler_params=_params("parallel"),
        name="mem_kv",
    )(mem, mem_gs.reshape(layers, 1, D), w_mem_kvs.astype(BF16))


A_CQ = (0, Q_LORA_RANK)
A_CKV = (A_CQ[1], A_CQ[1] + KV_LORA_RANK)
A_QMEM = (A_CKV[1], A_CKV[1] + MEM_WIDTH)
A_GATE = (A_QMEM[1], A_QMEM[1] + 1024)
A_KROPE = (A_GATE[1], A_GATE[1] + 2 * QK_ROPE_DIM)
Q_ROPE_OFF = MLA_HEADS * QK_NOPE_DIM


def _mla_in_kernel(x_ref, pos_ref, invf_ref, pre_g_ref, w_in_ref, qa_g_ref, w_uq_ref, kva_g_ref,
                   w_ukv_ref, q_ref, k_ref, v_ref, qmem_ref, gate_ref):
    q_scale = (QK_NOPE_DIM + QK_ROPE_DIM) ** -0.5 * LOG2E
    tm = x_ref.shape[1]
    sub = min(IN_SUB_ROWS, tm)
    blocks = [slice(i * sub, (i + 1) * sub) for i in range(tm // sub)]
    projected = [_dot(_rms(x_ref[0, r, :], pre_g_ref[...]).astype(BF16), w_in_ref[...]) for r in blocks]
    cs_tile = _rope_table(pos_ref, invf_ref)
    pad_row = lax.broadcasted_iota(jnp.int32, (V_EXT_DIM - V_HEAD_DIM, sub), 0)
    ones_row = jnp.where(pad_row == 0, 1.0, 0.0).astype(BF16)
    for r, p in zip(blocks, projected):
        cs = cs_tile[r, :]
        qmem_ref[0, r, :] = p[:, A_QMEM[0]:A_QMEM[1]].astype(BF16)
        gate = p[:, A_GATE[0]:A_GATE[1]]
        gate_ref[0, r, :] = (gate * jax.nn.sigmoid(gate)).astype(BF16)

        c_q = _rms(p[:, A_CQ[0]:A_CQ[1]], qa_g_ref[...]).astype(BF16)
        q = _dot(c_q, w_uq_ref[...])
        c_kv = _rms(p[:, A_CKV[0]:A_CKV[1]], kva_g_ref[...]).astype(BF16)
        kv = _dot(c_kv, w_ukv_ref[...])

        kr = p[:, A_KROPE[0]:A_KROPE[1]] * cs
        k_rot = (kr + pltpu.roll(kr, QK_ROPE_DIM, 1)).astype(BF16)
        for hd in range(MLA_HEADS):
            nope = slice(hd * QK_NOPE_DIM, (hd + 1) * QK_NOPE_DIM)
            rope = slice(Q_ROPE_OFF + hd * LANES, Q_ROPE_OFF + (hd + 1) * LANES)
            q_ref[0, hd, 0:LANES, r] = (q[:, nope] * q_scale).T.astype(BF16)
            q_ref[0, hd, LANES:2 * LANES, r] = (q[:, rope] * cs * q_scale).T.astype(BF16)
            k_ref[0, hd, r, 0:LANES] = kv[:, 2 * hd * LANES:(2 * hd + 1) * LANES].astype(BF16)
            k_ref[0, hd, r, LANES:2 * LANES] = k_rot
            v_ref[0, hd, 0:V_HEAD_DIM, r] = kv[:, (2 * hd + 1) * LANES:(2 * hd + 2) * LANES].T.astype(BF16)
            v_ref[0, hd, V_HEAD_DIM:V_EXT_DIM, r] = ones_row


def _rope_cols(w, start):
    half = QK_ROPE_DIM // 2
    return [w[:, start:start + QK_ROPE_DIM], w[:, start + half:start + QK_ROPE_DIM], w[:, start:start + half]]


def _mla_in(x, positions, pre_g, w_in, q_a_g, w_uq, kv_a_g, w_ukv):
    B, S, D = x.shape
    tm = min(ROW_TILE, S)
    inv_freq = ROPE_THETA ** (-jnp.arange(0, QK_ROPE_DIM, 2, dtype=F32) / QK_ROPE_DIM)
    invf = inv_freq.reshape(QK_ROPE_DIM // 2, 1)
    o_kr = Q_LORA_RANK + KV_LORA_RANK
    o_qm = o_kr + QK_ROPE_DIM
    w_in_p = jnp.concatenate([w_in[:, :o_kr], w_in[:, o_qm:]] + _rope_cols(w_in, o_kr), axis=1).astype(BF16)
    head_w = QK_NOPE_DIM + QK_ROPE_DIM
    uq_cols = [w_uq[:, h * head_w:h * head_w + QK_NOPE_DIM] for h in range(MLA_HEADS)]
    for h in range(MLA_HEADS):
        uq_cols += _rope_cols(w_uq, h * head_w + QK_NOPE_DIM)
    w_uq_p = jnp.concatenate(uq_cols, axis=1).astype(BF16)
    w_ukv_p = w_ukv.astype(BF16)

    row = lambda b, i: (b, i, 0)
    head_row = lambda b, i: (b, 0, i, 0)
    head_col = lambda b, i: (b, 0, 0, i)
    out_shape = [
        jax.ShapeDtypeStruct((B, MLA_HEADS, 2 * LANES, S), BF16),
        jax.ShapeDtypeStruct((B, MLA_HEADS, S, 2 * LANES), BF16),
        jax.ShapeDtypeStruct((B, MLA_HEADS, V_EXT_DIM, S), BF16),
        jax.ShapeDtypeStruct((B, S, MEM_WIDTH), BF16),
        jax.ShapeDtypeStruct((B, S, 1024), BF16),
    ]
    out_specs = [
        pl.BlockSpec((1, MLA_HEADS, 2 * LANES, tm), head_col),
        pl.BlockSpec((1, MLA_HEADS, tm, 2 * LANES), head_row),
        pl.BlockSpec((1, MLA_HEADS, V_EXT_DIM, tm), head_col),
        pl.BlockSpec((1, tm, MEM_WIDTH), row),
        pl.BlockSpec((1, tm, 1024), row),
    ]
    return pl.pallas_call(
        _mla_in_kernel,
        grid=(B, S // tm),
        in_specs=[pl.BlockSpec((1, tm, D), row),
                  pl.BlockSpec((1, 1, tm), lambda b, i: (b, 0, i)), _const_spec(invf.shape),
                  _const_spec((1, D)), _const_spec(w_in_p.shape),
                  _const_spec((1, Q_LORA_RANK)), _const_spec(w_uq_p.shape),
                  _const_spec((1, KV_LORA_RANK)), _const_spec(w_ukv_p.shape)],
        out_specs=out_specs,
        out_shape=out_shape,
        compiler_params=_params("parallel", "parallel"),
        name="mla_in",
    )(x, positions.reshape(B, 1, S), invf, pre_g.reshape(1, D), w_in_p, q_a_g.reshape(1, -1), w_uq_p, kv_a_g.reshape(1, -1),
      w_ukv_p)


def _attn_kernel(qt_ref, k_ref, vt_ref, o_ref, *, tq, tk, cw):
    heads = k_ref.shape[1]
    S = k_ref.shape[2]
    dve = vt_ref.shape[2]
    dv = V_HEAD_DIM
    nc = tq // cw
    causal = (lax.broadcasted_iota(jnp.int32, (cw, cw), 0) <= lax.broadcasted_iota(jnp.int32, (cw, cw), 1))

    units = []
    for j in range(S // tq):
        for k0 in range(0, (j + 1) * tq, tk):
            for c in range(nc):
                q_start = j * tq + c * cw
                if k0 < q_start + cw:
                    units += [(j, h, c, k0, tk, k0 >= q_start) for h in range(heads)]

    def score(unit):
        j, h, c, k0, n, diagonal = unit
        s = _dot(k_ref[0, h, pl.ds(k0, n), :], qt_ref[0, h, :, pl.ds(j * tq + c * cw, cw)])
        if diagonal:
            s = jnp.where(causal[k0 - (j * tq + c * cw):k0 - (j * tq + c * cw) + tk, :], s, -jnp.inf)
        return s

    scores, carries = {}, {}
    for i in range(len(units) + ATTN_LOOKAHEAD):
        if i < len(units):
            scores[i] = score(units[i])
        if i < ATTN_LOOKAHEAD:
            continue
        j, h, c, k0, n, diagonal = units[i - ATTN_LOOKAHEAD]
        m, acc = carries.pop((j, h, c), (jnp.full((1, cw), -jnp.inf, F32), jnp.zeros((dve, cw), F32)))
        s = scores.pop(i - ATTN_LOOKAHEAD)
        m_new = jnp.maximum(m, jnp.max(s, axis=0, keepdims=True))
        p = jnp.exp2(s - m_new).astype(BF16)
        acc = jnp.exp2(m - m_new) * acc + _dot(vt_ref[0, h, :, pl.ds(k0, n)], p)
        if k0 + n == j * tq + (c + 1) * cw:
            out = acc[0:dv, :] / acc[dv:dv + 1, :]
            o_ref[0, pl.ds(j * tq + c * cw, cw), h * dv:(h + 1) * dv] = out.T.astype(o_ref.dtype)
        else:
            carries[j, h, c] = (m_new, acc)


def _attention(qt, k, vt):
    B, H, S, dqk = k.shape
    dve = vt.shape[2]
    tq = min(ATTN_TQ, S)
    tk = min(ATTN_TK, tq)
    cw = min(ATTN_CW, tq)
    hp = ATTN_HEADS_PER_STEP
    head = lambda b, h: (b, h, 0, 0)
    return pl.pallas_call(
        functools.partial(_attn_kernel, tq=tq, tk=tk, cw=cw),
        grid=(B, H // hp),
        in_specs=[pl.BlockSpec((1, hp, dqk, S), head), pl.BlockSpec((1, hp, S, dqk), head),
                  pl.BlockSpec((1, hp, dve, S), head)],
        out_specs=pl.BlockSpec((1, S, hp * V_HEAD_DIM), lambda b, h: (b, 0, h)),
        out_shape=jax.ShapeDtypeStruct((B, S, H * V_HEAD_DIM), BF16),
        compiler_params=_params("parallel", "parallel"),
        name="mla_attention",
    )(qt, k, vt)


def _memory_probs(s, n_mem):
    probs = []
    for h in range(MEM_HEADS):
        sh = s[:, h * n_mem:(h + 1) * n_mem]
        e = jnp.exp2(sh - jnp.max(sh, axis=-1, keepdims=True))
        probs.append((e / jnp.sum(e, axis=-1, keepdims=True)).astype(BF16))
    return jnp.concatenate(probs, axis=-1)


def _tail(gated_mix, x_ref, qmem_ref, gate_ref, kexp_ref, vexp_ref, w_out_ref, post_g_ref, o_ref):
    tm = x_ref.shape[1]
    sub = min(TAIL_SUB_ROWS, tm)
    blocks = [slice(i * sub, (i + 1) * sub) for i in range(tm // sub)]
    n_mem = kexp_ref.shape[2] // MEM_HEADS
    kexp = kexp_ref[0, 0]
    scores = [_dot_nt(qmem_ref[0, r, :], kexp) for r in blocks]
    for r, s in zip(blocks, scores):
        mo = _dot(_memory_probs(s, n_mem), vexp_ref[0, 0])
        y_mix = gated_mix(r, gate_ref[0, r, 0:MIX_WIDTH])
        y_mem = (mo * gate_ref[0, r, MIX_WIDTH:].astype(F32)).astype(BF16)
        y = _dot(y_mix, w_out_ref[0:MIX_WIDTH, :]) + _dot(y_mem, w_out_ref[MIX_WIDTH:, :])
        o_ref[0, r, :] = x_ref[0, r, :] + _rms(y, post_g_ref[...])


def _mla_out_kernel(x_ref, mix_ref, qmem_ref, gate_ref, kexp_ref, vexp_ref, w_out_ref, post_g_ref, o_ref):
    gated_mix = lambda r, gate: mix_ref[0, r, :] * gate
    _tail(gated_mix, x_ref, qmem_ref, gate_ref, kexp_ref, vexp_ref, w_out_ref, post_g_ref, o_ref)


def _mlstm_out_kernel(x_ref, hn_ref, og_ref, uc_ref, skip_ref, qmem_ref, gate_ref, kexp_ref, vexp_ref,
                      w_out_ref, post_g_ref, o_ref):
    def gated_mix(r, gate):
        mix = og_ref[0, r, :].astype(F32) * hn_ref[0, r, :].astype(F32) + skip_ref[...] * uc_ref[0, r, :].astype(F32)
        return (mix * gate.astype(F32)).astype(BF16)

    _tail(gated_mix, x_ref, qmem_ref, gate_ref, kexp_ref, vexp_ref, w_out_ref, post_g_ref, o_ref)


def _layer_out(kernel_fn, name, x, mixer_inputs, mixer_specs, qmem, gate, mem_kv, layer, w_out, post_g):
    B, S, D = x.shape
    tm = min(TAIL_ROW_TILE, S)
    kexp, vexp = mem_kv
    row = lambda b, i: (b, i, 0)
    per_batch = lambda b, i: (layer, b, 0, 0)
    in_specs = ([pl.BlockSpec((1, tm, D), row)] + mixer_specs(tm) +
                [pl.BlockSpec((1, tm, MEM_WIDTH), row), pl.BlockSpec((1, tm, gate.shape[-1]), row),
                 pl.BlockSpec((1, 1) + kexp.shape[2:], per_batch), pl.BlockSpec((1, 1) + vexp.shape[2:], per_batch),
                 _const_spec(w_out.shape), _const_spec((1, D))])
    return pl.pallas_call(
        kernel_fn,
        grid=(B, S // tm),
        in_specs=in_specs,
        out_specs=pl.BlockSpec((1, tm, D), row),
        out_shape=jax.ShapeDtypeStruct((B, S, D), F32),
        compiler_params=_params("parallel", "parallel"),
        name=name,
    )(x, *mixer_inputs, qmem, gate, kexp, vexp, w_out.astype(BF16), post_g.reshape(1, D))


B_GATES = (MIX_WIDTH, MIX_WIDTH + LANES)
B_O = (0, MIX_WIDTH)
B_QMEM = (B_O[1], B_O[1] + MEM_WIDTH)
B_GATE = (B_QMEM[1], B_QMEM[1] + 1024)
CONV_HALO = 8
V_ONES_ROWS = 16


def _mlstm_chunk(carry, qts, ks, vts, gcol, grow, causal, between):
    H = len(qts)
    L = ks[0].shape[0]
    qk = [_dot(ks[h], qts[h]) for h in range(H)]
    cq = [_dot(carry[h][0].astype(BF16), qts[h]) for h in range(H)]
    between()
    out, normed = [], []
    for h in range(H):
        C, m = carry[h]
        c_col = gcol[:, h:h + 1]
        li_row = grow[h:h + 1, :]
        b_row = grow[MLSTM_HEADS + h:MLSTM_HEADS + h + 1, :]
        d = jnp.where(causal, c_col + b_row, -jnp.inf)
        inter = b_row + m
        m_t = jnp.maximum(inter, jnp.max(d, axis=0, keepdims=True))
        sqk = (qk[h] * jnp.exp(d - m_t)).astype(BF16)
        num = jnp.exp(inter - m_t) * cq[h] + _dot(vts[h], sqk)
        den = jnp.maximum(jnp.abs(num[MLSTM_V_DIM:MLSTM_V_DIM + 1, :]), jnp.exp(-m_t))
        hnum = num[0:MLSTM_V_DIM, :]
        inv_den = 1.0 / den
        ms = jnp.sum(hnum * hnum, axis=0, keepdims=True) * (inv_den * inv_den * (1.0 / MLSTM_V_DIM))
        normed.append(hnum * (inv_den * lax.rsqrt(ms + EPS)))

        b_last = b_row[:, L - 1:L]
        dec = b_last - b_row + li_row
        m_new = jnp.maximum(b_last + m, jnp.max(dec, axis=1, keepdims=True))
        vw = (vts[h].astype(F32) * jnp.exp(dec - m_new)).astype(BF16)
        out.append((jnp.exp(b_last + m - m_new) * C + _dot(vw, ks[h]), m_new))
    return tuple(out), jnp.concatenate(normed, axis=0).T


def _mlstm_mixer_kernel(x_ref, pre_g_ref, w_first_ref, w_rest_ref, conv_w_ref, conv_b_ref, bias_ref, hg_ref,
                        wqk_ref, wv_ref, og_ref, qmem_ref, gate_ref, uc_ref, hn_ref, tail_sc, c_sc, m_sc, *, chunk):
    tm = x_ref.shape[1]
    blocks = [slice(j * chunk, (j + 1) * chunk) for j in range(tm // chunk)]

    @pl.when(pl.program_id(1) == 0)
    def _():
        tail_sc[...] = jnp.zeros(tail_sc.shape, F32)
        c_sc[...] = jnp.zeros(c_sc.shape, F32)
        m_sc[...] = jnp.zeros(m_sc.shape, F32)

    hs = [_rms(x_ref[0, r, :], pre_g_ref[...]).astype(BF16) for r in blocks]
    firsts = [_dot(h, w_first_ref[...]) for h in hs]
    halos = [tail_sc[...]] + [f[chunk - CONV_HALO:chunk, 0:MIX_WIDTH] for f in firsts[:-1]]
    tail_sc[...] = firsts[-1][chunk - CONV_HALO:chunk, 0:MIX_WIDTH]

    k_scale = MLSTM_QK_DIM ** -0.5
    ones_rows = jnp.where(lax.broadcasted_iota(jnp.int32, (V_ONES_ROWS, chunk), 0) == 0, 1.0, 0.0)
    lane = lax.broadcasted_iota(jnp.int32, (chunk, LANES), 1)
    is_f = (lane >= MLSTM_HEADS) & (lane < 2 * MLSTM_HEADS)
    tril = (lax.broadcasted_iota(jnp.int32, (chunk, chunk), 1)
            <= lax.broadcasted_iota(jnp.int32, (chunk, chunk), 0)).astype(BF16)
    causal = (lax.broadcasted_iota(jnp.int32, (chunk, chunk), 0)
              <= lax.broadcasted_iota(jnp.int32, (chunk, chunk), 1))
    carry = tuple((c_sc[h], m_sc[h, 0:1, 0:1]) for h in range(MLSTM_HEADS))

    for r, h, first, halo in zip(blocks, hs, firsts, halos):
        u = first[:, 0:MIX_WIDTH]
        ext = jnp.concatenate([halo, u], axis=0)
        conv = conv_b_ref[...] + u * conv_w_ref[CONV_WIDTH - 1:CONV_WIDTH, :]
        for back in range(1, CONV_WIDTH):
            tap = CONV_WIDTH - 1 - back
            conv = conv + ext[CONV_HALO - back:CONV_HALO - back + chunk, :] * conv_w_ref[tap:tap + 1, :]
        uc_bf = (conv * jax.nn.sigmoid(conv)).astype(BF16)
        uc_ref[0, r, :] = uc_bf
        u_bf = u.astype(BF16)

        g = first[:, B_GATES[0]:B_GATES[1]] + bias_ref[...]
        log_f = jnp.minimum(g, 0.0) - jnp.log1p(jnp.exp(-jnp.abs(g)))
        gates = jnp.where(lane < MLSTM_HEADS, g, jnp.where(is_f, log_f, 0.0))
        gc = jnp.where(is_f, sum(_dot(tril, term) for term in _split3(gates)), gates)
        grow = gc.T[0:2 * MLSTM_HEADS, :]
        gcol = gc - pltpu.roll(gc, LANES - MLSTM_HEADS, 1)

        qts, ks, vts = [], [], []
        for hd in range(MLSTM_HEADS):
            cols = slice(MLSTM_HEAD_START[hd], MLSTM_HEAD_START[hd] + MXU_DIM)
            qk = _dot(uc_bf[:, cols], wqk_ref[hd])
            qts.append(qk[:, :MLSTM_PAD_QK].T.astype(BF16))
            ks.append((qk[:, MLSTM_PAD_QK:] * k_scale).astype(BF16))
            vt = _dot(u_bf[:, cols], wv_ref[hd]).T
            vts.append(jnp.concatenate([vt[0:MLSTM_V_DIM, :], ones_rows, vt[MLSTM_V_DIM + V_ONES_ROWS:, :]],
                                       axis=0).astype(BF16))

        def rest_projection(r=r, h=h):
            p = _dot(h, w_rest_ref[...])
            og_ref[0, r, :] = (jax.nn.sigmoid(p[:, B_O[0]:B_O[1]]) * hg_ref[...]).astype(BF16)
            qmem_ref[0, r, :] = p[:, B_QMEM[0]:B_QMEM[1]].astype(BF16)
            gate = p[:, B_GATE[0]:B_GATE[1]]
            gate_ref[0, r, :] = (gate * jax.nn.sigmoid(gate)).astype(BF16)

        carry, hn = _mlstm_chunk(carry, qts, ks, vts, gcol, grow, causal, rest_projection)
        hn_ref[0, r, :] = hn.astype(BF16)

    for hd in range(MLSTM_HEADS):
        c_sc[hd] = carry[hd][0]
        m_sc[hd] = jnp.broadcast_to(carry[hd][1], m_sc.shape[1:])


def _mlstm_mixer(x, pre_g, w_in, gate_bias, conv_w, conv_b, w_q, w_k, w_v, head_g):
    B, S, D = x.shape
    tm = min(MIXER_ROW_TILE, S)
    chunk = min(MLSTM_CHUNK, S)
    H = MLSTM_HEADS
    n_if = 2 * H
    o_if = MIX_WIDTH
    o_o = o_if + n_if
    w_first = jnp.concatenate([w_in[:, :o_o], jnp.zeros((D, LANES - n_if), w_in.dtype)], axis=1).astype(BF16)
    w_rest = w_in[:, o_o:].astype(BF16)
    def placed(w, width):
        blocks = []
        for h in range(H):
            off = h * MLSTM_V_DIM - MLSTM_HEAD_START[h]
            blocks.append(jnp.pad(w[h], ((off, MXU_DIM - MLSTM_V_DIM - off), (0, width - w.shape[-1]))))
        return jnp.stack(blocks)

    wqk = jnp.concatenate([placed(w_q, MLSTM_PAD_QK), placed(w_k, MLSTM_PAD_QK)], axis=-1)
    wv = placed(w_v, MLSTM_PAD_V)
    bias = jnp.pad(gate_bias, (0, LANES - n_if)).reshape(1, LANES)
    row = lambda b, i: (b, i, 0)
    widths = [MIX_WIDTH, MEM_WIDTH, 1024, MIX_WIDTH, MIX_WIDTH]
    return pl.pallas_call(
        functools.partial(_mlstm_mixer_kernel, chunk=chunk),
        grid=(B, S // tm),
        in_specs=[pl.BlockSpec((1, tm, D), row), _const_spec((1, D)), _const_spec(w_first.shape),
                  _const_spec(w_rest.shape), _const_spec((CONV_WIDTH, MIX_WIDTH)), _const_spec((1, MIX_WIDTH)),
                  _const_spec((1, LANES)), _const_spec((1, MIX_WIDTH)), _const_spec(wqk.shape),
                  _const_spec(wv.shape)],
        out_specs=[pl.BlockSpec((1, tm, w), row) for w in widths],
        out_shape=[jax.ShapeDtypeStruct((B, S, w), BF16) for w in widths],
        scratch_shapes=[pltpu.VMEM((CONV_HALO, MIX_WIDTH), F32),
                        pltpu.VMEM((H, MLSTM_PAD_V, MLSTM_PAD_QK), F32), pltpu.VMEM((H, 8, LANES), F32)],
        compiler_params=_params("parallel", "arbitrary"),
        name="mlstm_mixer",
    )(x, pre_g.reshape(1, D), w_first, w_rest, conv_w, conv_b.reshape(1, -1), bias, head_g.reshape(1, MIX_WIDTH),
      wqk.astype(BF16), wv.astype(BF16))


def _mla_layer(x, mem_kv, layer, positions, pre_g, w_in, q_a_g, w_uq, kv_a_g, w_ukv, w_out, post_g):
    q, k, v, qmem, gate = _mla_in(x, positions, pre_g, w_in, q_a_g, w_uq, kv_a_g, w_ukv)
    mix = _attention(q, k, v)
    specs = lambda tm: [pl.BlockSpec((1, tm, MIX_WIDTH), lambda b, i: (b, i, 0))]
    return _layer_out(_mla_out_kernel, "mla_out", x, [mix], specs, qmem, gate, mem_kv, layer, w_out, post_g)


def _mlstm_layer(x, mem_kv, layer, pre_g, w_in, gate_bias, conv_w, conv_b, w_q, w_k, w_v, head_g, skip,
                 w_out, post_g):
    og, qmem, gate, uc, hn = _mlstm_mixer(x, pre_g, w_in, gate_bias, conv_w, conv_b, w_q, w_k, w_v, head_g)
    row = lambda b, i: (b, i, 0)
    specs = lambda tm: [pl.BlockSpec((1, tm, MIX_WIDTH), row),
                        pl.BlockSpec((1, tm, MIX_WIDTH), row), pl.BlockSpec((1, tm, MIX_WIDTH), row),
                        _const_spec((1, MIX_WIDTH))]
    return _layer_out(_mlstm_out_kernel, "mlstm_out", x, [hn, og, uc, skip.reshape(1, -1)], specs,
                      qmem, gate, mem_kv, layer, w_out, post_g)


def kernel(x, mem, positions, a_pre_g, a_w_in, a_q_a_g, a_w_uq, a_kv_a_g, a_w_ukv, a_mem_g, a_w_mem_kv, a_w_out, a_post_g, b_pre_g, b_w_in, b_gate_bias, b_conv_w, b_conv_b, b_w_q, b_w_k, b_w_v, b_head_g, b_skip, b_mem_g, b_w_mem_kv, b_w_out, b_post_g):
    depth = a_pre_g.shape[0] + b_pre_g.shape[0]
    mem_gs = jnp.stack([(a_mem_g, b_mem_g)[i % 2][i // 2] for i in range(depth)])
    w_mem_kvs = jnp.stack([(a_w_mem_kv, b_w_mem_kv)[i % 2][i // 2] for i in range(depth)])
    mem_kv = _mem_kv(mem, mem_gs, w_mem_kvs)
    for i in range(depth):
        j = i // 2
        if i % 2 == 0:
            x = _mla_layer(x, mem_kv, i, positions, a_pre_g[j], a_w_in[j], a_q_a_g[j], a_w_uq[j], a_kv_a_g[j],
                           a_w_ukv[j], a_w_out[j], a_post_g[j])
        else:
            x = _mlstm_layer(x, mem_kv, i, b_pre_g[j], b_w_in[j], b_gate_bias[j], b_conv_w[j], b_conv_b[j],
                             b_w_q[j], b_w_k[j], b_w_v[j], b_head_g[j], b_skip[j], b_w_out[j], b_post_g[j])
    return x
```

```python
import functools

import jax
import jax.numpy as jnp
from jax import lax
from jax.experimental import pallas as pl
from jax.experimental.pallas import tpu as pltpu

EPS = 1e-6
ROPE_THETA = 10000.0
MEM_HEADS = 4
MEM_HEAD_DIM = 64
MEM_WIDTH = MEM_HEADS * MEM_HEAD_DIM
QK_NOPE_DIM = 128
QK_ROPE_DIM = 64
V_HEAD_DIM = 128
MLA_HEADS = 6
V_EXT_DIM = V_HEAD_DIM + 16
Q_LORA_RANK = 384
KV_LORA_RANK = 256
MLSTM_HEADS = 4
MLSTM_V_DIM = 192
MLSTM_QK_DIM = 96
CONV_WIDTH = 4
MIX_WIDTH = 768

LANES = 128
MXU_DIM = 256
VMEM_LIMIT_BYTES = 56 * 1024 * 1024

MLA_IN_ROW_TILE = 1024
MIXER_ROW_TILE = 1024
IN_SUB_ROWS = 256
TAIL_ROW_TILE = 1024
TAIL_SUB_ROWS = 256
ATTN_TQ = 1024
ATTN_CW = 256
ATTN_TK = 256
ATTN_HEADS_PER_STEP = 2
ATTN_LOOKAHEAD = 16
MLSTM_CHUNK = 256

LOG2E = 1.4426950408889634
MLSTM_PAD_QK = LANES
MLSTM_PAD_V = MXU_DIM
MLSTM_HEAD_START = tuple((h * MLSTM_V_DIM // LANES) * LANES for h in range(MLSTM_HEADS))

F32 = jnp.float32
BF16 = jnp.bfloat16


def _dot(a, b):
    return jnp.dot(a, b, preferred_element_type=F32)


def _dot_nt(a, b):
    return lax.dot_general(a, b, (((1,), (1,)), ((), ())), preferred_element_type=F32)


def _rms(x, g, width=None):
    width = x.shape[-1] if width is None else width
    ms = jnp.sum(x * x, axis=-1, keepdims=True) * (1.0 / width)
    return x * lax.rsqrt(ms + EPS) * g


def _split3(x):
    b1 = x.astype(BF16)
    r1 = x - b1.astype(F32)
    b2 = r1.astype(BF16)
    b3 = (r1 - b2.astype(F32)).astype(BF16)
    return b1, b2, b3


def _params(*semantics):
    return pltpu.CompilerParams(dimension_semantics=semantics, vmem_limit_bytes=VMEM_LIMIT_BYTES)


def _const_spec(shape):
    zeros = (0,) * len(shape)
    return pl.BlockSpec(shape, lambda *_: zeros)


def _rope_table(pos_ref, invf_ref):
    ang = invf_ref[...] * pos_ref[0].astype(F32)
    cos_t = jnp.cos(ang)
    sin_t = jnp.sin(ang)
    return jnp.concatenate([cos_t, cos_t, -sin_t, sin_t], axis=0).T


def _mem_kv_kernel(mem_ref, g_ref, w_ref, k_ref, v_ref):
    n_mem = mem_ref.shape[1]
    mem = mem_ref[0]
    col_head = lax.broadcasted_iota(jnp.int32, (n_mem, MEM_WIDTH), 1) // MEM_HEAD_DIM
    for layer in range(w_ref.shape[0]):
        hn = _rms(mem, g_ref[layer]).astype(BF16)
        kv = _dot(hn, w_ref[layer])
        k = kv[:, :MEM_WIDTH] * (MEM_HEAD_DIM ** -0.5 * LOG2E)
        v = kv[:, MEM_WIDTH:]
        for h in range(MEM_HEADS):
            rows = pl.ds(h * n_mem, n_mem)
            k_ref[layer, 0, rows, :] = jnp.where(col_head == h, k, 0.0).astype(BF16)
            v_ref[layer, 0, rows, :] = jnp.where(col_head == h, v, 0.0).astype(BF16)


def _mem_kv(mem, mem_gs, w_mem_kvs):
    B, n_mem, D = mem.shape
    layers = w_mem_kvs.shape[0]
    out = jax.ShapeDtypeStruct((layers, B, MEM_HEADS * n_mem, MEM_WIDTH), BF16)
    spec = pl.BlockSpec((layers, 1, MEM_HEADS * n_mem, MEM_WIDTH), lambda b: (0, b, 0, 0))
    return pl.pallas_call(
        _mem_kv_kernel,
        grid=(B,),
        in_specs=[pl.BlockSpec((1, n_mem, D), lambda b: (b, 0, 0)), _const_spec((layers, 1, D)),
                  _const_spec(w_mem_kvs.shape)],
        out_specs=[spec, spec],
        out_shape=[out, out],
        compiler_params=_params("parallel"),
        name="mem_kv",
    )(mem, mem_gs.reshape(layers, 1, D), w_mem_kvs.astype(BF16))


A_CQ = (0, Q_LORA_RANK)
A_CKV = (A_CQ[1], A_CQ[1] + KV_LORA_RANK)
A_QMEM = (A_CKV[1], A_CKV[1] + MEM_WIDTH)
A_GATE = (A_QMEM[1], A_QMEM[1] + 1024)
A_KROPE = (A_GATE[1], A_GATE[1] + 2 * QK_ROPE_DIM)
Q_ROPE_OFF = MLA_HEADS * QK_NOPE_DIM


def _mla_in_kernel(x_ref, pos_ref, invf_ref, pre_g_ref, w_in_ref, qa_g_ref, w_uq_ref, kva_g_ref,
                   w_ukv_ref, q_ref, k_ref, v_ref, qmem_ref, gate_ref):
    q_scale = (QK_NOPE_DIM + QK_ROPE_DIM) ** -0.5 * LOG2E
    tm = x_ref.shape[1]
    sub = min(IN_SUB_ROWS, tm)
    blocks = [slice(i * sub, (i + 1) * sub) for i in range(tm // sub)]
    projected = [_dot(_rms(x_ref[0, r, :], pre_g_ref[...]).astype(BF16), w_in_ref[...]) for r in blocks]
    cs_tile = _rope_table(pos_ref, invf_ref)
    pad_row = lax.broadcasted_iota(jnp.int32, (V_EXT_DIM - V_HEAD_DIM, sub), 0)
    ones_row = jnp.where(pad_row == 0, 1.0, 0.0).astype(BF16)
    for r, p in zip(blocks, projected):
        cs = cs_tile[r, :]
        qmem_ref[0, r, :] = p[:, A_QMEM[0]:A_QMEM[1]].astype(BF16)
        gate = p[:, A_GATE[0]:A_GATE[1]]
        gate_ref[0, r, :] = (gate * jax.nn.sigmoid(gate)).astype(BF16)

        c_q = _rms(p[:, A_CQ[0]:A_CQ[1]], qa_g_ref[...]).astype(BF16)
        q = _dot(c_q, w_uq_ref[...])
        c_kv = _rms(p[:, A_CKV[0]:A_CKV[1]], kva_g_ref[...]).astype(BF16)
        kv = _dot(c_kv, w_ukv_ref[...])

        kr = p[:, A_KROPE[0]:A_KROPE[1]] * cs
        k_rot = (kr + pltpu.roll(kr, QK_ROPE_DIM, 1)).astype(BF16)
        for hd in range(MLA_HEADS):
            nope = slice(hd * QK_NOPE_DIM, (hd + 1) * QK_NOPE_DIM)
            rope = slice(Q_ROPE_OFF + hd * LANES, Q_ROPE_OFF + (hd + 1) * LANES)
            q_ref[0, hd, 0:LANES, r] = (q[:, nope] * q_scale).T.astype(BF16)
            q_ref[0, hd, LANES:2 * LANES, r] = (q[:, rope] * cs * q_scale).T.astype(BF16)
            k_ref[0, hd, r, 0:LANES] = kv[:, 2 * hd * LANES:(2 * hd + 1) * LANES].astype(BF16)
            k_ref[0, hd, r, LANES:2 * LANES] = k_rot
            v_ref[0, hd, 0:V_HEAD_DIM, r] = kv[:, (2 * hd + 1) * LANES:(2 * hd + 2) * LANES].T.astype(BF16)
            v_ref[0, hd, V_HEAD_DIM:V_EXT_DIM, r] = ones_row


def _rope_cols(w, start):
    half = QK_ROPE_DIM // 2
    return [w[:, start:start + QK_ROPE_DIM], w[:, start + half:start + QK_ROPE_DIM], w[:, start:start + half]]


def _mla_in(x, positions, pre_g, w_in, q_a_g, w_uq, kv_a_g, w_ukv):
    B, S, D = x.shape
    tm = min(MLA_IN_ROW_TILE, S)
    inv_freq = ROPE_THETA ** (-jnp.arange(0, QK_ROPE_DIM, 2, dtype=F32) / QK_ROPE_DIM)
    invf = inv_freq.reshape(QK_ROPE_DIM // 2, 1)
    o_kr = Q_LORA_RANK + KV_LORA_RANK
    o_qm = o_kr + QK_ROPE_DIM
    w_in_p = jnp.concatenate([w_in[:, :o_kr], w_in[:, o_qm:]] + _rope_cols(w_in, o_kr), axis=1).astype(BF16)
    head_w = QK_NOPE_DIM + QK_ROPE_DIM
    uq_cols = [w_uq[:, h * head_w:h * head_w + QK_NOPE_DIM] for h in range(MLA_HEADS)]
    for h in range(MLA_HEADS):
        uq_cols += _rope_cols(w_uq, h * head_w + QK_NOPE_DIM)
    w_uq_p = jnp.concatenate(uq_cols, axis=1).astype(BF16)
    w_ukv_p = w_ukv.astype(BF16)

    row = lambda b, i: (b, i, 0)
    head_row = lambda b, i: (b, 0, i, 0)
    head_col = lambda b, i: (b, 0, 0, i)
    out_shape = [
        jax.ShapeDtypeStruct((B, MLA_HEADS, 2 * LANES, S), BF16),
        jax.ShapeDtypeStruct((B, MLA_HEADS, S, 2 * LANES), BF16),
        jax.ShapeDtypeStruct((B, MLA_HEADS, V_EXT_DIM, S), BF16),
        jax.ShapeDtypeStruct((B, S, MEM_WIDTH), BF16),
        jax.ShapeDtypeStruct((B, S, 1024), BF16),
    ]
    out_specs = [
        pl.BlockSpec((1, MLA_HEADS, 2 * LANES, tm), head_col),
        pl.BlockSpec((1, MLA_HEADS, tm, 2 * LANES), head_row),
        pl.BlockSpec((1, MLA_HEADS, V_EXT_DIM, tm), head_col),
        pl.BlockSpec((1, tm, MEM_WIDTH), row),
        pl.BlockSpec((1, tm, 1024), row),
    ]
    return pl.pallas_call(
        _mla_in_kernel,
        grid=(B, S // tm),
        in_specs=[pl.BlockSpec((1, tm, D), row),
                  pl.BlockSpec((1, 1, tm), lambda b, i: (b, 0, i)), _const_spec(invf.shape),
                  _const_spec((1, D)), _const_spec(w_in_p.shape),
                  _const_spec((1, Q_LORA_RANK)), _const_spec(w_uq_p.shape),
                  _const_spec((1, KV_LORA_RANK)), _const_spec(w_ukv_p.shape)],
        out_specs=out_specs,
        out_shape=out_shape,
        compiler_params=_params("parallel", "parallel"),
        name="mla_in",
    )(x, positions.reshape(B, 1, S), invf, pre_g.reshape(1, D), w_in_p, q_a_g.reshape(1, -1), w_uq_p, kv_a_g.reshape(1, -1),
      w_ukv_p)


def _attn_kernel(qt_ref, k_ref, vt_ref, o_ref, *, tq, tk, cw):
    heads = k_ref.shape[1]
    S = k_ref.shape[2]
    dve = vt_ref.shape[2]
    dv = V_HEAD_DIM
    nc = tq // cw
    causal = (lax.broadcasted_iota(jnp.int32, (cw, cw), 0) <= lax.broadcasted_iota(jnp.int32, (cw, cw), 1))

    units = []
    for j in range(S // tq):
        for k0 in range(0, (j + 1) * tq, tk):
            for c in range(nc):
                q_start = j * tq + c * cw
                if k0 < q_start + cw:
                    units += [(j, h, c, k0, tk, k0 >= q_start) for h in range(heads)]

    def score(unit):
        j, h, c, k0, n, diagonal = unit
        s = _dot(k_ref[0, h, pl.ds(k0, n), :], qt_ref[0, h, :, pl.ds(j * tq + c * cw, cw)])
        if diagonal:
            s = jnp.where(causal[k0 - (j * tq + c * cw):k0 - (j * tq + c * cw) + tk, :], s, -jnp.inf)
        return s

    scores, carries = {}, {}
    for i in range(len(units) + ATTN_LOOKAHEAD):
        if i < len(units):
            scores[i] = score(units[i])
        if i < ATTN_LOOKAHEAD:
            continue
        j, h, c, k0, n, diagonal = units[i - ATTN_LOOKAHEAD]
        m, acc = carries.pop((j, h, c), (jnp.full((1, cw), -jnp.inf, F32), jnp.zeros((dve, cw), F32)))
        s = scores.pop(i - ATTN_LOOKAHEAD)
        m_new = jnp.maximum(m, jnp.max(s, axis=0, keepdims=True))
        p = jnp.exp2(s - m_new).astype(BF16)
        acc = jnp.exp2(m - m_new) * acc + _dot(vt_ref[0, h, :, pl.ds(k0, n)], p)
        if k0 + n == j * tq + (c + 1) * cw:
            out = acc[0:dv, :] / acc[dv:dv + 1, :]
            o_ref[0, pl.ds(j * tq + c * cw, cw), h * dv:(h + 1) * dv] = out.T.astype(o_ref.dtype)
        else:
            carries[j, h, c] = (m_new, acc)


def _attention(qt, k, vt):
    B, H, S, dqk = k.shape
    dve = vt.shape[2]
    tq = min(ATTN_TQ, S)
    tk = min(ATTN_TK, tq)
    cw = min(ATTN_CW, tq)
    hp = ATTN_HEADS_PER_STEP
    head = lambda b, h: (b, h, 0, 0)
    return pl.pallas_call(
        functools.partial(_attn_kernel, tq=tq, tk=tk, cw=cw),
        grid=(B, H // hp),
        in_specs=[pl.BlockSpec((1, hp, dqk, S), head), pl.BlockSpec((1, hp, S, dqk), head),
                  pl.BlockSpec((1, hp, dve, S), head)],
        out_specs=pl.BlockSpec((1, S, hp * V_HEAD_DIM), lambda b, h: (b, 0, h)),
        out_shape=jax.ShapeDtypeStruct((B, S, H * V_HEAD_DIM), BF16),
        compiler_params=_params("parallel", "parallel"),
        name="mla_attention",
    )(qt, k, vt)


def _memory_probs(s, n_mem):
    probs = []
    for h in range(MEM_HEADS):
        sh = s[:, h * n_mem:(h + 1) * n_mem]
        e = jnp.exp2(sh - jnp.max(sh, axis=-1, keepdims=True))
        probs.append((e / jnp.sum(e, axis=-1, keepdims=True)).astype(BF16))
    return jnp.concatenate(probs, axis=-1)


def _tail(gated_mix, x_ref, qmem_ref, gate_ref, kexp_ref, vexp_ref, w_out_ref, post_g_ref, o_ref):
    tm = x_ref.shape[1]
    sub = min(TAIL_SUB_ROWS, tm)
    blocks = [slice(i * sub, (i + 1) * sub) for i in range(tm // sub)]
    n_mem = kexp_ref.shape[2] // MEM_HEADS
    kexp = kexp_ref[0, 0]
    scores = [_dot_nt(qmem_ref[0, r, :], kexp) for r in blocks]
    for r, s in zip(blocks, scores):
        mo = _dot(_memory_probs(s, n_mem), vexp_ref[0, 0])
        y_mix = gated_mix(r, gate_ref[0, r, 0:MIX_WIDTH])
        y_mem = (mo * gate_ref[0, r, MIX_WIDTH:].astype(F32)).astype(BF16)
        y = _dot(y_mix, w_out_ref[0:MIX_WIDTH, :]) + _dot(y_mem, w_out_ref[MIX_WIDTH:, :])
        o_ref[0, r, :] = x_ref[0, r, :] + _rms(y, post_g_ref[...])


def _mla_out_kernel(x_ref, mix_ref, qmem_ref, gate_ref, kexp_ref, vexp_ref, w_out_ref, post_g_ref, o_ref):
    gated_mix = lambda r, gate: mix_ref[0, r, :] * gate
    _tail(gated_mix, x_ref, qmem_ref, gate_ref, kexp_ref, vexp_ref, w_out_ref, post_g_ref, o_ref)


def _mlstm_out_kernel(x_ref, hn_ref, og_ref, uc_ref, skip_ref, qmem_ref, gate_ref, kexp_ref, vexp_ref,
                      w_out_ref, post_g_ref, o_ref):
    def gated_mix(r, gate):
        mix = og_ref[0, r, :].astype(F32) * hn_ref[0, r, :].astype(F32) + skip_ref[...] * uc_ref[0, r, :].astype(F32)
        return (mix * gate.astype(F32)).astype(BF16)

    _tail(gated_mix, x_ref, qmem_ref, gate_ref, kexp_ref, vexp_ref, w_out_ref, post_g_ref, o_ref)


def _layer_out(kernel_fn, name, x, mixer_inputs, mixer_specs, qmem, gate, mem_kv, layer, w_out, post_g):
    B, S, D = x.shape
    tm = min(TAIL_ROW_TILE, S)
    kexp, vexp = mem_kv
    row = lambda b, i: (b, i, 0)
    per_batch = lambda b, i: (layer, b, 0, 0)
    in_specs = ([pl.BlockSpec((1, tm, D), row)] + mixer_specs(tm) +
                [pl.BlockSpec((1, tm, MEM_WIDTH), row), pl.BlockSpec((1, tm, gate.shape[-1]), row),
                 pl.BlockSpec((1, 1) + kexp.shape[2:], per_batch), pl.BlockSpec((1, 1) + vexp.shape[2:], per_batch),
                 _const_spec(w_out.shape), _const_spec((1, D))])
    return pl.pallas_call(
        kernel_fn,
        grid=(B, S // tm),
        in_specs=in_specs,
        out_specs=pl.BlockSpec((1, tm, D), row),
        out_shape=jax.ShapeDtypeStruct((B, S, D), F32),
        compiler_params=_params("parallel", "parallel"),
        name=name,
    )(x, *mixer_inputs, qmem, gate, kexp, vexp, w_out.astype(BF16), post_g.reshape(1, D))


B_GATES = (MIX_WIDTH, MIX_WIDTH + LANES)
B_O = (0, MIX_WIDTH)
B_QMEM = (B_O[1], B_O[1] + MEM_WIDTH)
B_GATE = (B_QMEM[1], B_QMEM[1] + 1024)
CONV_HALO = 8
V_ONES_ROWS = 16


def _mlstm_chunk(carry, qts, ks, vts, gcol, grow, causal, between):
    H = len(qts)
    L = ks[0].shape[0]
    qk = [_dot(ks[h], qts[h]) for h in range(H)]
    cq = [_dot(carry[h][0].astype(BF16), qts[h]) for h in range(H)]
    between()
    out, normed = [], []
    for h in range(H):
        C, m = carry[h]
        c_col = gcol[:, h:h + 1]
        li_row = grow[h:h + 1, :]
        b_row = grow[MLSTM_HEADS + h:MLSTM_HEADS + h + 1, :]
        d = jnp.where(causal, c_col + b_row, -jnp.inf)
        inter = b_row + m
        m_t = jnp.maximum(inter, jnp.max(d, axis=0, keepdims=True))
        sqk = (qk[h] * jnp.exp(d - m_t)).astype(BF16)
        num = jnp.exp(inter - m_t) * cq[h] + _dot(vts[h], sqk)
        den = jnp.maximum(jnp.abs(num[MLSTM_V_DIM:MLSTM_V_DIM + 1, :]), jnp.exp(-m_t))
        hnum = num[0:MLSTM_V_DIM, :]
        inv_den = 1.0 / den
        ms = jnp.sum(hnum * hnum, axis=0, keepdims=True) * (inv_den * inv_den * (1.0 / MLSTM_V_DIM))
        normed.append(hnum * (inv_den * lax.rsqrt(ms + EPS)))

        b_last = b_row[:, L - 1:L]
        dec = b_last - b_row + li_row
        m_new = jnp.maximum(b_last + m, jnp.max(dec, axis=1, keepdims=True))
        vw = (vts[h].astype(F32) * jnp.exp(dec - m_new)).astype(BF16)
        out.append((jnp.exp(b_last + m - m_new) * C + _dot(vw, ks[h]), m_new))
    return tuple(out), jnp.concatenate(normed, axis=0).T


def _mlstm_mixer_kernel(x_ref, pre_g_ref, w_first_ref, w_rest_ref, conv_w_ref, conv_b_ref, bias_ref, hg_ref,
                        wqk_ref, wv_ref, og_ref, qmem_ref, gate_ref, uc_ref, hn_ref, tail_sc, c_sc, m_sc, *, chunk):
    tm = x_ref.shape[1]
    blocks = [slice(j * chunk, (j + 1) * chunk) for j in range(tm // chunk)]

    @pl.when(pl.program_id(1) == 0)
    def _():
        tail_sc[...] = jnp.zeros(tail_sc.shape, F32)
        c_sc[...] = jnp.zeros(c_sc.shape, F32)
        m_sc[...] = jnp.zeros(m_sc.shape, F32)

    hs = [_rms(x_ref[0, r, :], pre_g_ref[...]).astype(BF16) for r in blocks]
    firsts = [_dot(h, w_first_ref[...]) for h in hs]
    halos = [tail_sc[...]] + [f[chunk - CONV_HALO:chunk, 0:MIX_WIDTH] for f in firsts[:-1]]
    tail_sc[...] = firsts[-1][chunk - CONV_HALO:chunk, 0:MIX_WIDTH]

    k_scale = MLSTM_QK_DIM ** -0.5
    ones_rows = jnp.where(lax.broadcasted_iota(jnp.int32, (V_ONES_ROWS, chunk), 0) == 0, 1.0, 0.0)
    lane = lax.broadcasted_iota(jnp.int32, (chunk, LANES), 1)
    is_f = (lane >= MLSTM_HEADS) & (lane < 2 * MLSTM_HEADS)
    tril = (lax.broadcasted_iota(jnp.int32, (chunk, chunk), 1)
            <= lax.broadcasted_iota(jnp.int32, (chunk, chunk), 0)).astype(BF16)
    causal = (lax.broadcasted_iota(jnp.int32, (chunk, chunk), 0)
              <= lax.broadcasted_iota(jnp.int32, (chunk, chunk), 1))
    carry = tuple((c_sc[h], m_sc[h, 0:1, 0:1]) for h in range(MLSTM_HEADS))

    for r, h, first, halo in zip(blocks, hs, firsts, halos):
        u = first[:, 0:MIX_WIDTH]
        ext = jnp.concatenate([halo, u], axis=0)
        conv = conv_b_ref[...] + u * conv_w_ref[CONV_WIDTH - 1:CONV_WIDTH, :]
        for back in range(1, CONV_WIDTH):
            tap = CONV_WIDTH - 1 - back
            conv = conv + ext[CONV_HALO - back:CONV_HALO - back + chunk, :] * conv_w_ref[tap:tap + 1, :]
        uc_bf = (conv * jax.nn.sigmoid(conv)).astype(BF16)
        uc_ref[0, r, :] = uc_bf
        u_bf = u.astype(BF16)

        g = first[:, B_GATES[0]:B_GATES[1]] + bias_ref[...]
        log_f = jnp.minimum(g, 0.0) - jnp.log1p(jnp.exp(-jnp.abs(g)))
        gates = jnp.where(lane < MLSTM_HEADS, g, jnp.where(is_f, log_f, 0.0))
        gc = jnp.where(is_f, sum(_dot(tril, term) for term in _split3(gates)), gates)
        grow = gc.T[0:2 * MLSTM_HEADS, :]
        gcol = gc - pltpu.roll(gc, LANES - MLSTM_HEADS, 1)

        qts, ks, vts = [], [], []
        for hd in range(MLSTM_HEADS):
            cols = slice(MLSTM_HEAD_START[hd], MLSTM_HEAD_START[hd] + MXU_DIM)
            qk = _dot(uc_bf[:, cols], wqk_ref[hd])
            qts.append(qk[:, :MLSTM_PAD_QK].T.astype(BF16))
            ks.append((qk[:, MLSTM_PAD_QK:] * k_scale).astype(BF16))
            vt = _dot(u_bf[:, cols], wv_ref[hd]).T
            vts.append(jnp.concatenate([vt[0:MLSTM_V_DIM, :], ones_rows, vt[MLSTM_V_DIM + V_ONES_ROWS:, :]],
                                       axis=0).astype(BF16))

        def rest_projection(r=r, h=h):
            p = _dot(h, w_rest_ref[...])
            og_ref[0, r, :] = (jax.nn.sigmoid(p[:, B_O[0]:B_O[1]]) * hg_ref[...]).astype(BF16)
            qmem_ref[0, r, :] = p[:, B_QMEM[0]:B_QMEM[1]].astype(BF16)
            gate = p[:, B_GATE[0]:B_GATE[1]]
            gate_ref[0, r, :] = (gate * jax.nn.sigmoid(gate)).astype(BF16)

        carry, hn = _mlstm_chunk(carry, qts, ks, vts, gcol, grow, causal, rest_projection)
        hn_ref[0, r, :] = hn.astype(BF16)

    for hd in range(MLSTM_HEADS):
        c_sc[hd] = carry[hd][0]
        m_sc[hd] = jnp.broadcast_to(carry[hd][1], m_sc.shape[1:])


def _mlstm_mixer(x, pre_g, w_in, gate_bias, conv_w, conv_b, w_q, w_k, w_v, head_g):
    B, S, D = x.shape
    tm = min(MIXER_ROW_TILE, S)
    chunk = min(MLSTM_CHUNK, S)
    H = MLSTM_HEADS
    n_if = 2 * H
    o_if = MIX_WIDTH
    o_o = o_if + n_if
    w_first = jnp.concatenate([w_in[:, :o_o], jnp.zeros((D, LANES - n_if), w_in.dtype)], axis=1).astype(BF16)
    w_rest = w_in[:, o_o:].astype(BF16)
    def placed(w, width):
        blocks = []
        for h in range(H):
            off = h * MLSTM_V_DIM - MLSTM_HEAD_START[h]
            blocks.append(jnp.pad(w[h], ((off, MXU_DIM - MLSTM_V_DIM - off), (0, width - w.shape[-1]))))
        return jnp.stack(blocks)

    wqk = jnp.concatenate([placed(w_q, MLSTM_PAD_QK), placed(w_k, MLSTM_PAD_QK)], axis=-1)
    wv = placed(w_v, MLSTM_PAD_V)
    bias = jnp.pad(gate_bias, (0, LANES - n_if)).reshape(1, LANES)
    row = lambda b, i: (b, i, 0)
    widths = [MIX_WIDTH, MEM_WIDTH, 1024, MIX_WIDTH, MIX_WIDTH]
    return pl.pallas_call(
        functools.partial(_mlstm_mixer_kernel, chunk=chunk),
        grid=(B, S // tm),
        in_specs=[pl.BlockSpec((1, tm, D), row), _const_spec((1, D)), _const_spec(w_first.shape),
                  _const_spec(w_rest.shape), _const_spec((CONV_WIDTH, MIX_WIDTH)), _const_spec((1, MIX_WIDTH)),
                  _const_spec((1, LANES)), _const_spec((1, MIX_WIDTH)), _const_spec(wqk.shape),
                  _const_spec(wv.shape)],
        out_specs=[pl.BlockSpec((1, tm, w), row) for w in widths],
        out_shape=[jax.ShapeDtypeStruct((B, S, w), BF16) for w in widths],
        scratch_shapes=[pltpu.VMEM((CONV_HALO, MIX_WIDTH), F32),
                        pltpu.VMEM((H, MLSTM_PAD_V, MLSTM_PAD_QK), F32), pltpu.VMEM((H, 8, LANES), F32)],
        compiler_params=_params("parallel", "arbitrary"),
        name="mlstm_mixer",
    )(x, pre_g.reshape(1, D), w_first, w_rest, conv_w, conv_b.reshape(1, -1), bias, head_g.reshape(1, MIX_WIDTH),
      wqk.astype(BF16), wv.astype(BF16))


def _mla_layer(x, mem_kv, layer, positions, pre_g, w_in, q_a_g, w_uq, kv_a_g, w_ukv, w_out, post_g):
    q, k, v, qmem, gate = _mla_in(x, positions, pre_g, w_in, q_a_g, w_uq, kv_a_g, w_ukv)
    mix = _attention(q, k, v)
    specs = lambda tm: [pl.BlockSpec((1, tm, MIX_WIDTH), lambda b, i: (b, i, 0))]
    return _layer_out(_mla_out_kernel, "mla_out", x, [mix], specs, qmem, gate, mem_kv, layer, w_out, post_g)


def _mlstm_layer(x, mem_kv, layer, pre_g, w_in, gate_bias, conv_w, conv_b, w_q, w_k, w_v, head_g, skip,
                 w_out, post_g):
    og, qmem, gate, uc, hn = _mlstm_mixer(x, pre_g, w_in, gate_bias, conv_w, conv_b, w_q, w_k, w_v, head_g)
    row = lambda b, i: (b, i, 0)
    specs = lambda tm: [pl.BlockSpec((1, tm, MIX_WIDTH), row),
                        pl.BlockSpec((1, tm, MIX_WIDTH), row), pl.BlockSpec((1, tm, MIX_WIDTH), row),
                        _const_spec((1, MIX_WIDTH))]
    return _layer_out(_mlstm_out_kernel, "mlstm_out", x, [hn, og, uc, skip.reshape(1, -1)], specs,
                      qmem, gate, mem_kv, layer, w_out, post_g)


def kernel(x, mem, positions, a_pre_g, a_w_in, a_q_a_g, a_w_uq, a_kv_a_g, a_w_ukv, a_mem_g, a_w_mem_kv, a_w_out, a_post_g, b_pre_g, b_w_in, b_gate_bias, b_conv_w, b_conv_b, b_w_q, b_w_k, b_w_v, b_head_g, b_skip, b_mem_g, b_w_mem_kv, b_w_out, b_post_g):
    depth = a_pre_g.shape[0] + b_pre_g.shape[0]
    mem_gs = jnp.stack([(a_mem_g, b_mem_g)[i % 2][i // 2] for i in range(depth)])
    w_mem_kvs = jnp.stack([(a_w_mem_kv, b_w_mem_kv)[i % 2][i // 2] for i in range(depth)])
    mem_kv = _mem_kv(mem, mem_gs, w_mem_kvs)
    for i in range(depth):
        j = i // 2
        if i % 2 == 0:
            x = _mla_layer(x, mem_kv, i, positions, a_pre_g[j], a_w_in[j], a_q_a_g[j], a_w_uq[j], a_kv_a_g[j],
                           a_w_ukv[j], a_w_out[j], a_post_g[j])
        else:
            x = _mlstm_layer(x, mem_kv, i, b_pre_g[j], b_w_in[j], b_gate_bias[j], b_conv_w[j], b_conv_b[j],
                             b_w_q[j], b_w_k[j], b_w_v[j], b_head_g[j], b_skip[j], b_w_out[j], b_post_g[j])
    return x
```

```python
import functools

import jax
import jax.numpy as jnp
from jax import lax
from jax.experimental import pallas as pl
from jax.experimental.pallas import tpu as pltpu

EPS = 1e-6
ROPE_THETA = 10000.0
MEM_HEADS = 4
MEM_HEAD_DIM = 64
MEM_WIDTH = MEM_HEADS * MEM_HEAD_DIM
QK_NOPE_DIM = 128
QK_ROPE_DIM = 64
V_HEAD_DIM = 128
MLA_HEADS = 6
V_EXT_DIM = V_HEAD_DIM + 16
Q_LORA_RANK = 384
KV_LORA_RANK = 256
MLSTM_HEADS = 4
MLSTM_V_DIM = 192
MLSTM_QK_DIM = 96
CONV_WIDTH = 4
MIX_WIDTH = 768

LANES = 128
MXU_DIM = 256
VMEM_LIMIT_BYTES = 56 * 1024 * 1024

MLA_IN_ROW_TILE = 1024
MIXER_ROW_TILE = 1024
IN_SUB_ROWS = 256
TAIL_ROW_TILE = 1024
TAIL_SUB_ROWS = 256
ATTN_TQ = 1024
ATTN_CW = 256
ATTN_TK = 256
ATTN_HEADS_PER_STEP = 2
ATTN_LOOKAHEAD = 10
MLSTM_CHUNK = 256

LOG2E = 1.4426950408889634
MLSTM_PAD_QK = LANES
MLSTM_PAD_V = MXU_DIM
MLSTM_HEAD_START = tuple((h * MLSTM_V_DIM // LANES) * LANES for h in range(MLSTM_HEADS))

F32 = jnp.float32
BF16 = jnp.bfloat16


def _dot(a, b):
    return jnp.dot(a, b, preferred_element_type=F32)


def _dot_nt(a, b):
    return lax.dot_general(a, b, (((1,), (1,)), ((), ())), preferred_element_type=F32)


def _rms(x, g, width=None):
    width = x.shape[-1] if width is None else width
    ms = jnp.sum(x * x, axis=-1, keepdims=True) * (1.0 / width)
    return x * lax.rsqrt(ms + EPS) * g


def _split3(x):
    b1 = x.astype(BF16)
    r1 = x - b1.astype(F32)
    b2 = r1.astype(BF16)
    b3 = (r1 - b2.astype(F32)).astype(BF16)
    return b1, b2, b3


def _params(*semantics):
    return pltpu.CompilerParams(dimension_semantics=semantics, vmem_limit_bytes=VMEM_LIMIT_BYTES)


def _const_spec(shape):
    zeros = (0,) * len(shape)
    return pl.BlockSpec(shape, lambda *_: zeros)


def _rope_table(pos_ref, invf_ref):
    ang = invf_ref[...] * pos_ref[0].astype(F32)
    cos_t = jnp.cos(ang)
    sin_t = jnp.sin(ang)
    return jnp.concatenate([cos_t, cos_t, -sin_t, sin_t], axis=0).T


def _mem_kv_kernel(mem_ref, g_ref, w_ref, k_ref, v_ref):
    n_mem = mem_ref.shape[1]
    mem = mem_ref[0]
    col_head = lax.broadcasted_iota(jnp.int32, (n_mem, MEM_WIDTH), 1) // MEM_HEAD_DIM
    for layer in range(w_ref.shape[0]):
        hn = _rms(mem, g_ref[layer]).astype(BF16)
        kv = _dot(hn, w_ref[layer])
        k = kv[:, :MEM_WIDTH] * (MEM_HEAD_DIM ** -0.5 * LOG2E)
        v = kv[:, MEM_WIDTH:]
        for h in range(MEM_HEADS):
            rows = pl.ds(h * n_mem, n_mem)
            k_ref[layer, 0, rows, :] = jnp.where(col_head == h, k, 0.0).astype(BF16)
            v_ref[layer, 0, rows, :] = jnp.where(col_head == h, v, 0.0).astype(BF16)


def _mem_kv(mem, mem_gs, w_mem_kvs):
    B, n_mem, D = mem.shape
    layers = w_mem_kvs.shape[0]
    out = jax.ShapeDtypeStruct((layers, B, MEM_HEADS * n_mem, MEM_WIDTH), BF16)
    spec = pl.BlockSpec((layers, 1, MEM_HEADS * n_mem, MEM_WIDTH), lambda b: (0, b, 0, 0))
    return pl.pallas_call(
        _mem_kv_kernel,
        grid=(B,),
        in_specs=[pl.BlockSpec((1, n_mem, D), lambda b: (b, 0, 0)), _const_spec((layers, 1, D)),
                  _const_spec(w_mem_kvs.shape)],
        out_specs=[spec, spec],
        out_shape=[out, out],
        compiler_params=_params("parallel"),
        name="mem_kv",
    )(mem, mem_gs.reshape(layers, 1, D), w_mem_kvs.astype(BF16))


A_CQ = (0, Q_LORA_RANK)
A_CKV = (A_CQ[1], A_CQ[1] + KV_LORA_RANK)
A_QMEM = (A_CKV[1], A_CKV[1] + MEM_WIDTH)
A_GATE = (A_QMEM[1], A_QMEM[1] + 1024)
A_KROPE = (A_GATE[1], A_GATE[1] + 2 * QK_ROPE_DIM)
Q_ROPE_OFF = MLA_HEADS * QK_NOPE_DIM


def _mla_in_kernel(x_ref, pos_ref, invf_ref, pre_g_ref, w_in_ref, qa_g_ref, w_uq_ref, kva_g_ref,
                   w_ukv_ref, q_ref, k_ref, v_ref, qmem_ref, gate_ref):
    q_scale = (QK_NOPE_DIM + QK_ROPE_DIM) ** -0.5 * LOG2E
    tm = x_ref.shape[1]
    sub = min(IN_SUB_ROWS, tm)
    blocks = [slice(i * sub, (i + 1) * sub) for i in range(tm // sub)]
    projected = [_dot(_rms(x_ref[0, r, :], pre_g_ref[...]).astype(BF16), w_in_ref[...]) for r in blocks]
    cs_tile = _rope_table(pos_ref, invf_ref)
    pad_row = lax.broadcasted_iota(jnp.int32, (V_EXT_DIM - V_HEAD_DIM, sub), 0)
    ones_row = jnp.where(pad_row == 0, 1.0, 0.0).astype(BF16)
    for r, p in zip(blocks, projected):
        cs = cs_tile[r, :]
        qmem_ref[0, r, :] = p[:, A_QMEM[0]:A_QMEM[1]].astype(BF16)
        gate = p[:, A_GATE[0]:A_GATE[1]]
        gate_ref[0, r, :] = (gate * jax.nn.sigmoid(gate)).astype(BF16)

        c_q = _rms(p[:, A_CQ[0]:A_CQ[1]], qa_g_ref[...]).astype(BF16)
        q = _dot(c_q, w_uq_ref[...])
        c_kv = _rms(p[:, A_CKV[0]:A_CKV[1]], kva_g_ref[...]).astype(BF16)
        kv = _dot(c_kv, w_ukv_ref[...])

        kr = p[:, A_KROPE[0]:A_KROPE[1]] * cs
        k_rot = (kr + pltpu.roll(kr, QK_ROPE_DIM, 1)).astype(BF16)
        for hd in range(MLA_HEADS):
            nope = slice(hd * QK_NOPE_DIM, (hd + 1) * QK_NOPE_DIM)
            rope = slice(Q_ROPE_OFF + hd * LANES, Q_ROPE_OFF + (hd + 1) * LANES)
            q_ref[0, hd, 0:LANES, r] = (q[:, nope] * q_scale).T.astype(BF16)
            q_ref[0, hd, LANES:2 * LANES, r] = (q[:, rope] * cs * q_scale).T.astype(BF16)
            k_ref[0, hd, r, 0:LANES] = kv[:, 2 * hd * LANES:(2 * hd + 1) * LANES].astype(BF16)
            k_ref[0, hd, r, LANES:2 * LANES] = k_rot
            v_ref[0, hd, 0:V_HEAD_DIM, r] = kv[:, (2 * hd + 1) * LANES:(2 * hd + 2) * LANES].T.astype(BF16)
            v_ref[0, hd, V_HEAD_DIM:V_EXT_DIM, r] = ones_row


def _rope_cols(w, start):
    half = QK_ROPE_DIM // 2
    return [w[:, start:start + QK_ROPE_DIM], w[:, start + half:start + QK_ROPE_DIM], w[:, start:start + half]]


def _mla_in(x, positions, pre_g, w_in, q_a_g, w_uq, kv_a_g, w_ukv):
    B, S, D = x.shape
    tm = min(MLA_IN_ROW_TILE, S)
    inv_freq = ROPE_THETA ** (-jnp.arange(0, QK_ROPE_DIM, 2, dtype=F32) / QK_ROPE_DIM)
    invf = inv_freq.reshape(QK_ROPE_DIM // 2, 1)
    o_kr = Q_LORA_RANK + KV_LORA_RANK
    o_qm = o_kr + QK_ROPE_DIM
    w_in_p = jnp.concatenate([w_in[:, :o_kr], w_in[:, o_qm:]] + _rope_cols(w_in, o_kr), axis=1).astype(BF16)
    head_w = QK_NOPE_DIM + QK_ROPE_DIM
    uq_cols = [w_uq[:, h * head_w:h * head_w + QK_NOPE_DIM] for h in range(MLA_HEADS)]
    for h in range(MLA_HEADS):
        uq_cols += _rope_cols(w_uq, h * head_w + QK_NOPE_DIM)
    w_uq_p = jnp.concatenate(uq_cols, axis=1).astype(BF16)
    w_ukv_p = w_ukv.astype(BF16)

    row = lambda b, i: (b, i, 0)
    head_row = lambda b, i: (b, 0, i, 0)
    head_col = lambda b, i: (b, 0, 0, i)
    out_shape = [
        jax.ShapeDtypeStruct((B, MLA_HEADS, 2 * LANES, S), BF16),
        jax.ShapeDtypeStruct((B, MLA_HEADS, S, 2 * LANES), BF16),
        jax.ShapeDtypeStruct((B, MLA_HEADS, V_EXT_DIM, S), BF16),
        jax.ShapeDtypeStruct((B, S, MEM_WIDTH), BF16),
        jax.ShapeDtypeStruct((B, S, 1024), BF16),
    ]
    out_specs = [
        pl.BlockSpec((1, MLA_HEADS, 2 * LANES, tm), head_col),
        pl.BlockSpec((1, MLA_HEADS, tm, 2 * LANES), head_row),
        pl.BlockSpec((1, MLA_HEADS, V_EXT_DIM, tm), head_col),
        pl.BlockSpec((1, tm, MEM_WIDTH), row),
        pl.BlockSpec((1, tm, 1024), row),
    ]
    return pl.pallas_call(
        _mla_in_kernel,
        grid=(B, S // tm),
        in_specs=[pl.BlockSpec((1, tm, D), row),
                  pl.BlockSpec((1, 1, tm), lambda b, i: (b, 0, i)), _const_spec(invf.shape),
                  _const_spec((1, D)), _const_spec(w_in_p.shape),
                  _const_spec((1, Q_LORA_RANK)), _const_spec(w_uq_p.shape),
                  _const_spec((1, KV_LORA_RANK)), _const_spec(w_ukv_p.shape)],
        out_specs=out_specs,
        out_shape=out_shape,
        compiler_params=_params("parallel", "parallel"),
        name="mla_in",
    )(x, positions.reshape(B, 1, S), invf, pre_g.reshape(1, D), w_in_p, q_a_g.reshape(1, -1), w_uq_p, kv_a_g.reshape(1, -1),
      w_ukv_p)


def _attn_kernel(qt_ref, k_ref, vt_ref, o_ref, *, tq, tk, cw):
    heads = k_ref.shape[1]
    S = k_ref.shape[2]
    dve = vt_ref.shape[2]
    dv = V_HEAD_DIM
    nc = tq // cw
    causal = (lax.broadcasted_iota(jnp.int32, (cw, cw), 0) <= lax.broadcasted_iota(jnp.int32, (cw, cw), 1))

    units = []
    for j in range(S // tq):
        for k0 in range(0, (j + 1) * tq, tk):
            for c in range(nc):
                q_start = j * tq + c * cw
                if k0 < q_start + cw:
                    units += [(j, h, c, k0, tk, k0 >= q_start) for h in range(heads)]

    def score(unit):
        j, h, c, k0, n, diagonal = unit
        s = _dot(k_ref[0, h, pl.ds(k0, n), :], qt_ref[0, h, :, pl.ds(j * tq + c * cw, cw)])
        if diagonal:
            s = jnp.where(causal[k0 - (j * tq + c * cw):k0 - (j * tq + c * cw) + tk, :], s, -jnp.inf)
        return s

    scores, carries = {}, {}
    for i in range(len(units) + ATTN_LOOKAHEAD):
        if i < len(units):
            scores[i] = score(units[i])
        if i < ATTN_LOOKAHEAD:
            continue
        j, h, c, k0, n, diagonal = units[i - ATTN_LOOKAHEAD]
        m, acc = carries.pop((j, h, c), (jnp.full((1, cw), -jnp.inf, F32), jnp.zeros((dve, cw), F32)))
        s = scores.pop(i - ATTN_LOOKAHEAD)
        m_new = jnp.maximum(m, jnp.max(s, axis=0, keepdims=True))
        p = jnp.exp2(s - m_new).astype(BF16)
        acc = jnp.exp2(m - m_new) * acc + _dot(vt_ref[0, h, :, pl.ds(k0, n)], p)
        if k0 + n == j * tq + (c + 1) * cw:
            out = acc[0:dv, :] / acc[dv:dv + 1, :]
            o_ref[0, pl.ds(j * tq + c * cw, cw), h * dv:(h + 1) * dv] = out.T.astype(o_ref.dtype)
        else:
            carries[j, h, c] = (m_new, acc)


def _attention(qt, k, vt):
    B, H, S, dqk = k.shape
    dve = vt.shape[2]
    tq = min(ATTN_TQ, S)
    tk = min(ATTN_TK, tq)
    cw = min(ATTN_CW, tq)
    hp = ATTN_HEADS_PER_STEP
    head = lambda b, h: (b, h, 0, 0)
    return pl.pallas_call(
        functools.partial(_attn_kernel, tq=tq, tk=tk, cw=cw),
        grid=(B, H // hp),
        in_specs=[pl.BlockSpec((1, hp, dqk, S), head), pl.BlockSpec((1, hp, S, dqk), head),
                  pl.BlockSpec((1, hp, dve, S), head)],
        out_specs=pl.BlockSpec((1, S, hp * V_HEAD_DIM), lambda b, h: (b, 0, h)),
        out_shape=jax.ShapeDtypeStruct((B, S, H * V_HEAD_DIM), BF16),
        compiler_params=_params("parallel", "parallel"),
        name="mla_attention",
    )(qt, k, vt)


def _memory_probs(s, n_mem):
    probs = []
    for h in range(MEM_HEADS):
        sh = s[:, h * n_mem:(h + 1) * n_mem]
        e = jnp.exp2(sh - jnp.max(sh, axis=-1, keepdims=True))
        probs.append((e / jnp.sum(e, axis=-1, keepdims=True)).astype(BF16))
    return jnp.concatenate(probs, axis=-1)


def _tail(gated_mix, x_ref, qmem_ref, gate_ref, kexp_ref, vexp_ref, w_out_ref, post_g_ref, o_ref):
    tm = x_ref.shape[1]
    sub = min(TAIL_SUB_ROWS, tm)
    blocks = [slice(i * sub, (i + 1) * sub) for i in range(tm // sub)]
    n_mem = kexp_ref.shape[2] // MEM_HEADS
    kexp = kexp_ref[0, 0]
    scores = [_dot_nt(qmem_ref[0, r, :], kexp) for r in blocks]
    for r, s in zip(blocks, scores):
        mo = _dot(_memory_probs(s, n_mem), vexp_ref[0, 0])
        y_mix = gated_mix(r, gate_ref[0, r, 0:MIX_WIDTH])
        y_mem = (mo * gate_ref[0, r, MIX_WIDTH:].astype(F32)).astype(BF16)
        y = _dot(y_mix, w_out_ref[0:MIX_WIDTH, :]) + _dot(y_mem, w_out_ref[MIX_WIDTH:, :])
        o_ref[0, r, :] = x_ref[0, r, :] + _rms(y, post_g_ref[...])


def _mla_out_kernel(x_ref, mix_ref, qmem_ref, gate_ref, kexp_ref, vexp_ref, w_out_ref, post_g_ref, o_ref):
    gated_mix = lambda r, gate: mix_ref[0, r, :] * gate
    _tail(gated_mix, x_ref, qmem_ref, gate_ref, kexp_ref, vexp_ref, w_out_ref, post_g_ref, o_ref)


def _mlstm_out_kernel(x_ref, hn_ref, og_ref, uc_ref, skip_ref, qmem_ref, gate_ref, kexp_ref, vexp_ref,
                      w_out_ref, post_g_ref, o_ref):
    def gated_mix(r, gate):
        mix = og_ref[0, r, :].astype(F32) * hn_ref[0, r, :].astype(F32) + skip_ref[...] * uc_ref[0, r, :].astype(F32)
        return (mix * gate.astype(F32)).astype(BF16)

    _tail(gated_mix, x_ref, qmem_ref, gate_ref, kexp_ref, vexp_ref, w_out_ref, post_g_ref, o_ref)


def _layer_out(kernel_fn, name, x, mixer_inputs, mixer_specs, qmem, gate, mem_kv, layer, w_out, post_g):
    B, S, D = x.shape
    tm = min(TAIL_ROW_TILE, S)
    kexp, vexp = mem_kv
    row = lambda b, i: (b, i, 0)
    per_batch = lambda b, i: (layer, b, 0, 0)
    in_specs = ([pl.BlockSpec((1, tm, D), row)] + mixer_specs(tm) +
                [pl.BlockSpec((1, tm, MEM_WIDTH), row), pl.BlockSpec((1, tm, gate.shape[-1]), row),
                 pl.BlockSpec((1, 1) + kexp.shape[2:], per_batch), pl.BlockSpec((1, 1) + vexp.shape[2:], per_batch),
                 _const_spec(w_out.shape), _const_spec((1, D))])
    return pl.pallas_call(
        kernel_fn,
        grid=(B, S // tm),
        in_specs=in_specs,
        out_specs=pl.BlockSpec((1, tm, D), row),
        out_shape=jax.ShapeDtypeStruct((B, S, D), F32),
        compiler_params=_params("parallel", "parallel"),
        name=name,
    )(x, *mixer_inputs, qmem, gate, kexp, vexp, w_out.astype(BF16), post_g.reshape(1, D))


B_GATES = (MIX_WIDTH, MIX_WIDTH + LANES)
B_O = (0, MIX_WIDTH)
B_QMEM = (B_O[1], B_O[1] + MEM_WIDTH)
B_GATE = (B_QMEM[1], B_QMEM[1] + 1024)
CONV_HALO = 8
V_ONES_ROWS = 16


def _mlstm_chunk(carry, qts, ks, vts, gcol, grow, causal, between):
    H = len(qts)
    L = ks[0].shape[0]
    qk = [_dot(ks[h], qts[h]) for h in range(H)]
    cq = [_dot(carry[h][0].astype(BF16), qts[h]) for h in range(H)]
    between()
    out, normed = [], []
    for h in range(H):
        C, m = carry[h]
        c_col = gcol[:, h:h + 1]
        li_row = grow[h:h + 1, :]
        b_row = grow[MLSTM_HEADS + h:MLSTM_HEADS + h + 1, :]
        d = jnp.where(causal, c_col + b_row, -jnp.inf)
        inter = b_row + m
        m_t = jnp.maximum(inter, jnp.max(d, axis=0, keepdims=True))
        sqk = (qk[h] * jnp.exp(d - m_t)).astype(BF16)
        num = jnp.exp(inter - m_t) * cq[h] + _dot(vts[h], sqk)
        den = jnp.maximum(jnp.abs(num[MLSTM_V_DIM:MLSTM_V_DIM + 1, :]), jnp.exp(-m_t))
        hnum = num[0:MLSTM_V_DIM, :]
        inv_den = 1.0 / den
        ms = jnp.sum(hnum * hnum, axis=0, keepdims=True) * (inv_den * inv_den * (1.0 / MLSTM_V_DIM))
        normed.append(hnum * (inv_den * lax.rsqrt(ms + EPS)))

        b_last = b_row[:, L - 1:L]
        dec = b_last - b_row + li_row
        m_new = jnp.maximum(b_last + m, jnp.max(dec, axis=1, keepdims=True))
        vw = (vts[h].astype(F32) * jnp.exp(dec - m_new)).astype(BF16)
        out.append((jnp.exp(b_last + m - m_new) * C + _dot(vw, ks[h]), m_new))
    return tuple(out), jnp.concatenate(normed, axis=0).T


def _mlstm_mixer_kernel(x_ref, pre_g_ref, w_first_ref, w_rest_ref, conv_w_ref, conv_b_ref, bias_ref, hg_ref,
                        wqk_ref, wv_ref, og_ref, qmem_ref, gate_ref, uc_ref, hn_ref, tail_sc, c_sc, m_sc, *, chunk):
    tm = x_ref.shape[1]
    blocks = [slice(j * chunk, (j + 1) * chunk) for j in range(tm // chunk)]

    @pl.when(pl.program_id(1) == 0)
    def _():
        tail_sc[...] = jnp.zeros(tail_sc.shape, F32)
        c_sc[...] = jnp.zeros(c_sc.shape, F32)
        m_sc[...] = jnp.zeros(m_sc.shape, F32)

    hs = [_rms(x_ref[0, r, :], pre_g_ref[...]).astype(BF16) for r in blocks]
    firsts = [_dot(h, w_first_ref[...]) for h in hs]
    halos = [tail_sc[...]] + [f[chunk - CONV_HALO:chunk, 0:MIX_WIDTH] for f in firsts[:-1]]
    tail_sc[...] = firsts[-1][chunk - CONV_HALO:chunk, 0:MIX_WIDTH]

    k_scale = MLSTM_QK_DIM ** -0.5
    ones_rows = jnp.where(lax.broadcasted_iota(jnp.int32, (V_ONES_ROWS, chunk), 0) == 0, 1.0, 0.0)
    lane = lax.broadcasted_iota(jnp.int32, (chunk, LANES), 1)
    is_f = (lane >= MLSTM_HEADS) & (lane < 2 * MLSTM_HEADS)
    tril = (lax.broadcasted_iota(jnp.int32, (chunk, chunk), 1)
            <= lax.broadcasted_iota(jnp.int32, (chunk, chunk), 0)).astype(BF16)
    causal = (lax.broadcasted_iota(jnp.int32, (chunk, chunk), 0)
              <= lax.broadcasted_iota(jnp.int32, (chunk, chunk), 1))
    carry = tuple((c_sc[h], m_sc[h, 0:1, 0:1]) for h in range(MLSTM_HEADS))

    for r, h, first, halo in zip(blocks, hs, firsts, halos):
        u = first[:, 0:MIX_WIDTH]
        ext = jnp.concatenate([halo, u], axis=0)
        conv = conv_b_ref[...] + u * conv_w_ref[CONV_WIDTH - 1:CONV_WIDTH, :]
        for back in range(1, CONV_WIDTH):
            tap = CONV_WIDTH - 1 - back
            conv = conv + ext[CONV_HALO - back:CONV_HALO - back + chunk, :] * conv_w_ref[tap:tap + 1, :]
        uc_bf = (conv * jax.nn.sigmoid(conv)).astype(BF16)
        uc_ref[0, r, :] = uc_bf
        u_bf = u.astype(BF16)

        g = first[:, B_GATES[0]:B_GATES[1]] + bias_ref[...]
        log_f = jnp.minimum(g, 0.0) - jnp.log1p(jnp.exp(-jnp.abs(g)))
        gates = jnp.where(lane < MLSTM_HEADS, g, jnp.where(is_f, log_f, 0.0))
        gc = jnp.where(is_f, sum(_dot(tril, term) for term in _split3(gates)), gates)
        grow = gc.T[0:2 * MLSTM_HEADS, :]
        gcol = gc - pltpu.roll(gc, LANES - MLSTM_HEADS, 1)

        qts, ks, vts = [], [], []
        for hd in range(MLSTM_HEADS):
            cols = slice(MLSTM_HEAD_START[hd], MLSTM_HEAD_START[hd] + MXU_DIM)
            qk = _dot(uc_bf[:, cols], wqk_ref[hd])
            qts.append(qk[:, :MLSTM_PAD_QK].T.astype(BF16))
            ks.append((qk[:, MLSTM_PAD_QK:] * k_scale).astype(BF16))
            vt = _dot(u_bf[:, cols], wv_ref[hd]).T
            vts.append(jnp.concatenate([vt[0:MLSTM_V_DIM, :], ones_rows, vt[MLSTM_V_DIM + V_ONES_ROWS:, :]],
                                       axis=0).astype(BF16))

        def rest_projection(r=r, h=h):
            p = _dot(h, w_rest_ref[...])
            og_ref[0, r, :] = (jax.nn.sigmoid(p[:, B_O[0]:B_O[1]]) * hg_ref[...]).astype(BF16)
            qmem_ref[0, r, :] = p[:, B_QMEM[0]:B_QMEM[1]].astype(BF16)
            gate = p[:, B_GATE[0]:B_GATE[1]]
            gate_ref[0, r, :] = (gate * jax.nn.sigmoid(gate)).astype(BF16)

        carry, hn = _mlstm_chunk(carry, qts, ks, vts, gcol, grow, causal, rest_projection)
        hn_ref[0, r, :] = hn.astype(BF16)

    for hd in range(MLSTM_HEADS):
        c_sc[hd] = carry[hd][0]
        m_sc[hd] = jnp.broadcast_to(carry[hd][1], m_sc.shape[1:])


def _mlstm_mixer(x, pre_g, w_in, gate_bias, conv_w, conv_b, w_q, w_k, w_v, head_g):
    B, S, D = x.shape
    tm = min(MIXER_ROW_TILE, S)
    chunk = min(MLSTM_CHUNK, S)
    H = MLSTM_HEADS
    n_if = 2 * H
    o_if = MIX_WIDTH
    o_o = o_if + n_if
    w_first = jnp.concatenate([w_in[:, :o_o], jnp.zeros((D, LANES - n_if), w_in.dtype)], axis=1).astype(BF16)
    w_rest = w_in[:, o_o:].astype(BF16)
    def placed(w, width):
        blocks = []
        for h in range(H):
            off = h * MLSTM_V_DIM - MLSTM_HEAD_START[h]
            blocks.append(jnp.pad(w[h], ((off, MXU_DIM - MLSTM_V_DIM - off), (0, width - w.shape[-1]))))
        return jnp.stack(blocks)

    wqk = jnp.concatenate([placed(w_q, MLSTM_PAD_QK), placed(w_k, MLSTM_PAD_QK)], axis=-1)
    wv = placed(w_v, MLSTM_PAD_V)
    bias = jnp.pad(gate_bias, (0, LANES - n_if)).reshape(1, LANES)
    row = lambda b, i: (b, i, 0)
    widths = [MIX_WIDTH, MEM_WIDTH, 1024, MIX_WIDTH, MIX_WIDTH]
    return pl.pallas_call(
        functools.partial(_mlstm_mixer_kernel, chunk=chunk),
        grid=(B, S // tm),
        in_specs=[pl.BlockSpec((1, tm, D), row), _const_spec((1, D)), _const_spec(w_first.shape),
                  _const_spec(w_rest.shape), _const_spec((CONV_WIDTH, MIX_WIDTH)), _const_spec((1, MIX_WIDTH)),
                  _const_spec((1, LANES)), _const_spec((1, MIX_WIDTH)), _const_spec(wqk.shape),
                  _const_spec(wv.shape)],
        out_specs=[pl.BlockSpec((1, tm, w), row) for w in widths],
        out_shape=[jax.ShapeDtypeStruct((B, S, w), BF16) for w in widths],
        scratch_shapes=[pltpu.VMEM((CONV_HALO, MIX_WIDTH), F32),
                        pltpu.VMEM((H, MLSTM_PAD_V, MLSTM_PAD_QK), F32), pltpu.VMEM((H, 8, LANES), F32)],
        compiler_params=_params("parallel", "arbitrary"),
        name="mlstm_mixer",
    )(x, pre_g.reshape(1, D), w_first, w_rest, conv_w, conv_b.reshape(1, -1), bias, head_g.reshape(1, MIX_WIDTH),
      wqk.astype(BF16), wv.astype(BF16))


def _mla_layer(x, mem_kv, layer, positions, pre_g, w_in, q_a_g, w_uq, kv_a_g, w_ukv, w_out, post_g):
    q, k, v, qmem, gate = _mla_in(x, positions, pre_g, w_in, q_a_g, w_uq, kv_a_g, w_ukv)
    mix = _attention(q, k, v)
    specs = lambda tm: [pl.BlockSpec((1, tm, MIX_WIDTH), lambda b, i: (b, i, 0))]
    return _layer_out(_mla_out_kernel, "mla_out", x, [mix], specs, qmem, gate, mem_kv, layer, w_out, post_g)


def _mlstm_layer(x, mem_kv, layer, pre_g, w_in, gate_bias, conv_w, conv_b, w_q, w_k, w_v, head_g, skip,
                 w_out, post_g):
    og, qmem, gate, uc, hn = _mlstm_mixer(x, pre_g, w_in, gate_bias, conv_w, conv_b, w_q, w_k, w_v, head_g)
    row = lambda b, i: (b, i, 0)
    specs = lambda tm: [pl.BlockSpec((1, tm, MIX_WIDTH), row),
                        pl.BlockSpec((1, tm, MIX_WIDTH), row), pl.BlockSpec((1, tm, MIX_WIDTH), row),
                        _const_spec((1, MIX_WIDTH))]
    return _layer_out(_mlstm_out_kernel, "mlstm_out", x, [hn, og, uc, skip.reshape(1, -1)], specs,
                      qmem, gate, mem_kv, layer, w_out, post_g)


def kernel(x, mem, positions, a_pre_g, a_w_in, a_q_a_g, a_w_uq, a_kv_a_g, a_w_ukv, a_mem_g, a_w_mem_kv, a_w_out, a_post_g, b_pre_g, b_w_in, b_gate_bias, b_conv_w, b_conv_b, b_w_q, b_w_k, b_w_v, b_head_g, b_skip, b_mem_g, b_w_mem_kv, b_w_out, b_post_g):
    depth = a_pre_g.shape[0] + b_pre_g.shape[0]
    mem_gs = jnp.stack([(a_mem_g, b_mem_g)[i % 2][i // 2] for i in range(depth)])
    w_mem_kvs = jnp.stack([(a_w_mem_kv, b_w_mem_kv)[i % 2][i // 2] for i in range(depth)])
    mem_kv = _mem_kv(mem, mem_gs, w_mem_kvs)
    for i in range(depth):
        j = i // 2
        if i % 2 == 0:
            x = _mla_layer(x, mem_kv, i, positions, a_pre_g[j], a_w_in[j], a_q_a_g[j], a_w_uq[j], a_kv_a_g[j],
                           a_w_ukv[j], a_w_out[j], a_post_g[j])
        else:
            x = _mlstm_layer(x, mem_kv, i, b_pre_g[j], b_w_in[j], b_gate_bias[j], b_conv_w[j], b_conv_b[j],
                             b_w_q[j], b_w_k[j], b_w_v[j], b_head_g[j], b_skip[j], b_w_out[j], b_post_g[j])
    return x
```

```python
import functools

import jax
import jax.numpy as jnp
from jax import lax
from jax.experimental import pallas as pl
from jax.experimental.pallas import tpu as pltpu

EPS = 1e-6
ROPE_THETA = 10000.0
MEM_HEADS = 4
MEM_HEAD_DIM = 64
MEM_WIDTH = MEM_HEADS * MEM_HEAD_DIM
QK_NOPE_DIM = 128
QK_ROPE_DIM = 64
V_HEAD_DIM = 128
MLA_HEADS = 6
Q_LORA_RANK = 384
KV_LORA_RANK = 256
MLSTM_HEADS = 4
MLSTM_V_DIM = 192
MLSTM_QK_DIM = 96
CONV_WIDTH = 4
MIX_WIDTH = 768

LANES = 128
MXU_DIM = 256
VMEM_LIMIT_BYTES = 56 * 1024 * 1024

MLA_IN_ROW_TILE = 1024
MIXER_ROW_TILE = 1024
IN_SUB_ROWS = 256
TAIL_ROW_TILE = 1024
TAIL_SUB_ROWS = 256
ATTN_TQ = 1024
ATTN_CW = 256
ATTN_TK = 256
ATTN_HEADS_PER_STEP = 2
ATTN_LOOKAHEAD = 10
MLSTM_CHUNK = 256

LOG2E = 1.4426950408889634
MLSTM_PAD_QK = LANES
MLSTM_PAD_V = MXU_DIM
MLSTM_HEAD_START = tuple((h * MLSTM_V_DIM // LANES) * LANES for h in range(MLSTM_HEADS))

F32 = jnp.float32
BF16 = jnp.bfloat16


def _dot(a, b):
    return jnp.dot(a, b, preferred_element_type=F32)


def _dot_nt(a, b):
    return lax.dot_general(a, b, (((1,), (1,)), ((), ())), preferred_element_type=F32)


def _rms(x, g, width=None):
    width = x.shape[-1] if width is None else width
    ms = jnp.sum(x * x, axis=-1, keepdims=True) * (1.0 / width)
    return x * lax.rsqrt(ms + EPS) * g


def _split3(x):
    b1 = x.astype(BF16)
    r1 = x - b1.astype(F32)
    b2 = r1.astype(BF16)
    b3 = (r1 - b2.astype(F32)).astype(BF16)
    return b1, b2, b3


def _params(*semantics):
    return pltpu.CompilerParams(dimension_semantics=semantics, vmem_limit_bytes=VMEM_LIMIT_BYTES)


def _const_spec(shape):
    zeros = (0,) * len(shape)
    return pl.BlockSpec(shape, lambda *_: zeros)


def _rope_table(pos_ref, invf_ref):
    ang = invf_ref[...] * pos_ref[0].astype(F32)
    cos_t = jnp.cos(ang)
    sin_t = jnp.sin(ang)
    return jnp.concatenate([cos_t, cos_t, -sin_t, sin_t], axis=0).T


def _mem_kv_kernel(mem_ref, g_ref, w_ref, k_ref, v_ref):
    n_mem = mem_ref.shape[1]
    mem = mem_ref[0]
    col_head = lax.broadcasted_iota(jnp.int32, (n_mem, MEM_WIDTH), 1) // MEM_HEAD_DIM
    for layer in range(w_ref.shape[0]):
        hn = _rms(mem, g_ref[layer]).astype(BF16)
        kv = _dot(hn, w_ref[layer])
        k = kv[:, :MEM_WIDTH] * (MEM_HEAD_DIM ** -0.5 * LOG2E)
        v = kv[:, MEM_WIDTH:]
        for h in range(MEM_HEADS):
            rows = pl.ds(h * n_mem, n_mem)
            k_ref[layer, 0, rows, :] = jnp.where(col_head == h, k, 0.0).astype(BF16)
            v_ref[layer, 0, rows, :] = jnp.where(col_head == h, v, 0.0).astype(BF16)


def _mem_kv(mem, mem_gs, w_mem_kvs):
    B, n_mem, D = mem.shape
    layers = w_mem_kvs.shape[0]
    out = jax.ShapeDtypeStruct((layers, B, MEM_HEADS * n_mem, MEM_WIDTH), BF16)
    spec = pl.BlockSpec((layers, 1, MEM_HEADS * n_mem, MEM_WIDTH), lambda b: (0, b, 0, 0))
    return pl.pallas_call(
        _mem_kv_kernel,
        grid=(B,),
        in_specs=[pl.BlockSpec((1, n_mem, D), lambda b: (b, 0, 0)), _const_spec((layers, 1, D)),
                  _const_spec(w_mem_kvs.shape)],
        out_specs=[spec, spec],
        out_shape=[out, out],
        compiler_params=_params("parallel"),
        name="mem_kv",
    )(mem, mem_gs.reshape(layers, 1, D), w_mem_kvs.astype(BF16))


A_CQ = (0, Q_LORA_RANK)
A_CKV = (A_CQ[1], A_CQ[1] + KV_LORA_RANK)
A_QMEM = (A_CKV[1], A_CKV[1] + MEM_WIDTH)
A_GATE = (A_QMEM[1], A_QMEM[1] + 1024)
A_KROPE = (A_GATE[1], A_GATE[1] + 2 * QK_ROPE_DIM)
Q_ROPE_OFF = MLA_HEADS * QK_NOPE_DIM


def _mla_in_kernel(x_ref, pos_ref, invf_ref, pre_g_ref, w_in_ref, qa_g_ref, w_uq_ref, kva_g_ref,
                   w_ukv_ref, q_ref, k_ref, v_ref, qmem_ref, gate_ref):
    q_scale = (QK_NOPE_DIM + QK_ROPE_DIM) ** -0.5 * LOG2E
    tm = x_ref.shape[1]
    sub = min(IN_SUB_ROWS, tm)
    blocks = [slice(i * sub, (i + 1) * sub) for i in range(tm // sub)]
    projected = [_dot(_rms(x_ref[0, r, :], pre_g_ref[...]).astype(BF16), w_in_ref[...]) for r in blocks]
    cs_tile = _rope_table(pos_ref, invf_ref)
    for r, p in zip(blocks, projected):
        cs = cs_tile[r, :]
        qmem_ref[0, r, :] = p[:, A_QMEM[0]:A_QMEM[1]].astype(BF16)
        gate = p[:, A_GATE[0]:A_GATE[1]]
        gate_ref[0, r, :] = (gate * jax.nn.sigmoid(gate)).astype(BF16)

        c_q = _rms(p[:, A_CQ[0]:A_CQ[1]], qa_g_ref[...]).astype(BF16)
        q = _dot(c_q, w_uq_ref[...])
        c_kv = _rms(p[:, A_CKV[0]:A_CKV[1]], kva_g_ref[...]).astype(BF16)
        kv = _dot(c_kv, w_ukv_ref[...])

        kr = p[:, A_KROPE[0]:A_KROPE[1]] * cs
        k_rot = (kr + pltpu.roll(kr, QK_ROPE_DIM, 1)).astype(BF16)
        for hd in range(MLA_HEADS):
            nope = slice(hd * QK_NOPE_DIM, (hd + 1) * QK_NOPE_DIM)
            rope = slice(Q_ROPE_OFF + hd * LANES, Q_ROPE_OFF + (hd + 1) * LANES)
            q_ref[0, hd, 0:LANES, r] = (q[:, nope] * q_scale).T.astype(BF16)
            q_ref[0, hd, LANES:2 * LANES, r] = (q[:, rope] * cs * q_scale).T.astype(BF16)
            k_ref[0, hd, r, 0:LANES] = kv[:, 2 * hd * LANES:(2 * hd + 1) * LANES].astype(BF16)
            k_ref[0, hd, r, LANES:2 * LANES] = k_rot
            v_ref[0, hd, :, r] = kv[:, (2 * hd + 1) * LANES:(2 * hd + 2) * LANES].T.astype(BF16)


def _rope_cols(w, start):
    half = QK_ROPE_DIM // 2
    return [w[:, start:start + QK_ROPE_DIM], w[:, start + half:start + QK_ROPE_DIM], w[:, start:start + half]]


def _mla_in(x, positions, pre_g, w_in, q_a_g, w_uq, kv_a_g, w_ukv):
    B, S, D = x.shape
    tm = min(MLA_IN_ROW_TILE, S)
    inv_freq = ROPE_THETA ** (-jnp.arange(0, QK_ROPE_DIM, 2, dtype=F32) / QK_ROPE_DIM)
    invf = inv_freq.reshape(QK_ROPE_DIM // 2, 1)
    o_kr = Q_LORA_RANK + KV_LORA_RANK
    o_qm = o_kr + QK_ROPE_DIM
    w_in_p = jnp.concatenate([w_in[:, :o_kr], w_in[:, o_qm:]] + _rope_cols(w_in, o_kr), axis=1).astype(BF16)
    head_w = QK_NOPE_DIM + QK_ROPE_DIM
    uq_cols = [w_uq[:, h * head_w:h * head_w + QK_NOPE_DIM] for h in range(MLA_HEADS)]
    for h in range(MLA_HEADS):
        uq_cols += _rope_cols(w_uq, h * head_w + QK_NOPE_DIM)
    w_uq_p = jnp.concatenate(uq_cols, axis=1).astype(BF16)
    w_ukv_p = w_ukv.astype(BF16)

    row = lambda b, i: (b, i, 0)
    head_row = lambda b, i: (b, 0, i, 0)
    head_col = lambda b, i: (b, 0, 0, i)
    out_shape = [
        jax.ShapeDtypeStruct((B, MLA_HEADS, 2 * LANES, S), BF16),
        jax.ShapeDtypeStruct((B, MLA_HEADS, S, 2 * LANES), BF16),
        jax.ShapeDtypeStruct((B, MLA_HEADS, V_HEAD_DIM, S), BF16),
        jax.ShapeDtypeStruct((B, S, MEM_WIDTH), BF16),
        jax.ShapeDtypeStruct((B, S, 1024), BF16),
    ]
    out_specs = [
        pl.BlockSpec((1, MLA_HEADS, 2 * LANES, tm), head_col),
        pl.BlockSpec((1, MLA_HEADS, tm, 2 * LANES), head_row),
        pl.BlockSpec((1, MLA_HEADS, V_HEAD_DIM, tm), head_col),
        pl.BlockSpec((1, tm, MEM_WIDTH), row),
        pl.BlockSpec((1, tm, 1024), row),
    ]
    return pl.pallas_call(
        _mla_in_kernel,
        grid=(B, S // tm),
        in_specs=[pl.BlockSpec((1, tm, D), row),
                  pl.BlockSpec((1, 1, tm), lambda b, i: (b, 0, i)), _const_spec(invf.shape),
                  _const_spec((1, D)), _const_spec(w_in_p.shape),
                  _const_spec((1, Q_LORA_RANK)), _const_spec(w_uq_p.shape),
                  _const_spec((1, KV_LORA_RANK)), _const_spec(w_ukv_p.shape)],
        out_specs=out_specs,
        out_shape=out_shape,
        compiler_params=_params("parallel", "parallel"),
        name="mla_in",
    )(x, positions.reshape(B, 1, S), invf, pre_g.reshape(1, D), w_in_p, q_a_g.reshape(1, -1), w_uq_p, kv_a_g.reshape(1, -1),
      w_ukv_p)


def _attn_kernel(qt_ref, k_ref, vt_ref, o_ref, *, tq, tk, cw):
    heads = k_ref.shape[1]
    S = k_ref.shape[2]
    dv = vt_ref.shape[2]
    nc = tq // cw
    causal = (lax.broadcasted_iota(jnp.int32, (cw, cw), 0) <= lax.broadcasted_iota(jnp.int32, (cw, cw), 1))

    units = []
    for j in range(S // tq):
        for k0 in range(0, (j + 1) * tq, tk):
            for c in range(nc):
                q_start = j * tq + c * cw
                if k0 < q_start + cw:
                    units += [(j, h, c, k0, tk, k0 >= q_start) for h in range(heads)]

    def score(unit):
        j, h, c, k0, n, diagonal = unit
        s = _dot(k_ref[0, h, pl.ds(k0, n), :], qt_ref[0, h, :, pl.ds(j * tq + c * cw, cw)])
        if diagonal:
            s = jnp.where(causal[k0 - (j * tq + c * cw):k0 - (j * tq + c * cw) + tk, :], s, -jnp.inf)
        return s

    scores, carries = {}, {}
    for i in range(len(units) + ATTN_LOOKAHEAD):
        if i < len(units):
            scores[i] = score(units[i])
        if i < ATTN_LOOKAHEAD:
            continue
        j, h, c, k0, n, diagonal = units[i - ATTN_LOOKAHEAD]
        m, l, acc = carries.pop((j, h, c), (jnp.full((1, cw), -jnp.inf, F32), jnp.zeros((1, cw), F32),
                                            jnp.zeros((dv, cw), F32)))
        s = scores.pop(i - ATTN_LOOKAHEAD)
        m_new = jnp.maximum(m, jnp.max(s, axis=0, keepdims=True))
        p = jnp.exp2(s - m_new)
        alpha = jnp.exp2(m - m_new)
        l = alpha * l + jnp.sum(p, axis=0, keepdims=True)
        acc = alpha * acc + _dot(vt_ref[0, h, :, pl.ds(k0, n)], p.astype(BF16))
        if k0 + n == j * tq + (c + 1) * cw:
            o_ref[0, pl.ds(j * tq + c * cw, cw), h * dv:(h + 1) * dv] = (acc / l).T.astype(o_ref.dtype)
        else:
            carries[j, h, c] = (m_new, l, acc)


def _attention(qt, k, vt):
    B, H, S, dqk = k.shape
    dv = vt.shape[2]
    tq = min(ATTN_TQ, S)
    tk = min(ATTN_TK, tq)
    cw = min(ATTN_CW, tq)
    hp = ATTN_HEADS_PER_STEP
    head = lambda b, h: (b, h, 0, 0)
    return pl.pallas_call(
        functools.partial(_attn_kernel, tq=tq, tk=tk, cw=cw),
        grid=(B, H // hp),
        in_specs=[pl.BlockSpec((1, hp, dqk, S), head), pl.BlockSpec((1, hp, S, dqk), head),
                  pl.BlockSpec((1, hp, dv, S), head)],
        out_specs=pl.BlockSpec((1, S, hp * V_HEAD_DIM), lambda b, h: (b, 0, h)),
        out_shape=jax.ShapeDtypeStruct((B, S, H * V_HEAD_DIM), BF16),
        compiler_params=_params("parallel", "parallel"),
        name="mla_attention",
    )(qt, k, vt)


def _memory_probs(s, n_mem):
    probs = []
    for h in range(MEM_HEADS):
        sh = s[:, h * n_mem:(h + 1) * n_mem]
        e = jnp.exp2(sh - jnp.max(sh, axis=-1, keepdims=True))
        probs.append((e / jnp.sum(e, axis=-1, keepdims=True)).astype(BF16))
    return jnp.concatenate(probs, axis=-1)


def _tail(gated_mix, x_ref, qmem_ref, gate_ref, kexp_ref, vexp_ref, w_out_ref, post_g_ref, o_ref):
    tm = x_ref.shape[1]
    sub = min(TAIL_SUB_ROWS, tm)
    blocks = [slice(i * sub, (i + 1) * sub) for i in range(tm // sub)]
    n_mem = kexp_ref.shape[2] // MEM_HEADS
    kexp = kexp_ref[0, 0]
    scores = [_dot_nt(qmem_ref[0, r, :], kexp) for r in blocks]
    for r, s in zip(blocks, scores):
        mo = _dot(_memory_probs(s, n_mem), vexp_ref[0, 0])
        y_mix = gated_mix(r, gate_ref[0, r, 0:MIX_WIDTH])
        y_mem = (mo * gate_ref[0, r, MIX_WIDTH:].astype(F32)).astype(BF16)
        y = _dot(y_mix, w_out_ref[0:MIX_WIDTH, :]) + _dot(y_mem, w_out_ref[MIX_WIDTH:, :])
        o_ref[0, r, :] = x_ref[0, r, :] + _rms(y, post_g_ref[...])


def _mla_out_kernel(x_ref, mix_ref, qmem_ref, gate_ref, kexp_ref, vexp_ref, w_out_ref, post_g_ref, o_ref):
    gated_mix = lambda r, gate: mix_ref[0, r, :] * gate
    _tail(gated_mix, x_ref, qmem_ref, gate_ref, kexp_ref, vexp_ref, w_out_ref, post_g_ref, o_ref)


def _mlstm_out_kernel(x_ref, hn_ref, og_ref, uc_ref, skip_ref, qmem_ref, gate_ref, kexp_ref, vexp_ref,
                      w_out_ref, post_g_ref, o_ref):
    def gated_mix(r, gate):
        mix = og_ref[0, r, :].astype(F32) * hn_ref[0, r, :].astype(F32) + skip_ref[...] * uc_ref[0, r, :].astype(F32)
        return (mix * gate.astype(F32)).astype(BF16)

    _tail(gated_mix, x_ref, qmem_ref, gate_ref, kexp_ref, vexp_ref, w_out_ref, post_g_ref, o_ref)


def _layer_out(kernel_fn, name, x, mixer_inputs, mixer_specs, qmem, gate, mem_kv, layer, w_out, post_g):
    B, S, D = x.shape
    tm = min(TAIL_ROW_TILE, S)
    kexp, vexp = mem_kv
    row = lambda b, i: (b, i, 0)
    per_batch = lambda b, i: (layer, b, 0, 0)
    in_specs = ([pl.BlockSpec((1, tm, D), row)] + mixer_specs(tm) +
                [pl.BlockSpec((1, tm, MEM_WIDTH), row), pl.BlockSpec((1, tm, gate.shape[-1]), row),
                 pl.BlockSpec((1, 1) + kexp.shape[2:], per_batch), pl.BlockSpec((1, 1) + vexp.shape[2:], per_batch),
                 _const_spec(w_out.shape), _const_spec((1, D))])
    return pl.pallas_call(
        kernel_fn,
        grid=(B, S // tm),
        in_specs=in_specs,
        out_specs=pl.BlockSpec((1, tm, D), row),
        out_shape=jax.ShapeDtypeStruct((B, S, D), F32),
        compiler_params=_params("parallel", "parallel"),
        name=name,
    )(x, *mixer_inputs, qmem, gate, kexp, vexp, w_out.astype(BF16), post_g.reshape(1, D))


B_GATES = (MIX_WIDTH, MIX_WIDTH + LANES)
B_O = (0, MIX_WIDTH)
B_QMEM = (B_O[1], B_O[1] + MEM_WIDTH)
B_GATE = (B_QMEM[1], B_QMEM[1] + 1024)
CONV_HALO = 8
V_ONES_ROWS = 16


def _mlstm_chunk(carry, qts, ks, vts, gcol, grow, causal, between):
    H = len(qts)
    L = ks[0].shape[0]
    qk = [_dot(ks[h], qts[h]) for h in range(H)]
    cq = [_dot(carry[h][0].astype(BF16), qts[h]) for h in range(H)]
    between()
    out, normed = [], []
    for h in range(H):
        C, m = carry[h]
        c_col = gcol[:, h:h + 1]
        li_row = grow[h:h + 1, :]
        b_row = grow[MLSTM_HEADS + h:MLSTM_HEADS + h + 1, :]
        d = jnp.where(causal, c_col + b_row, -jnp.inf)
        inter = b_row + m
        m_t = jnp.maximum(inter, jnp.max(d, axis=0, keepdims=True))
        sqk = (qk[h] * jnp.exp(d - m_t)).astype(BF16)
        num = jnp.exp(inter - m_t) * cq[h] + _dot(vts[h], sqk)
        den = jnp.maximum(jnp.abs(num[MLSTM_V_DIM:MLSTM_V_DIM + 1, :]), jnp.exp(-m_t))
        hnum = num[0:MLSTM_V_DIM, :]
        inv_den = 1.0 / den
        ms = jnp.sum(hnum * hnum, axis=0, keepdims=True) * (inv_den * inv_den * (1.0 / MLSTM_V_DIM))
        normed.append(hnum * (inv_den * lax.rsqrt(ms + EPS)))

        b_last = b_row[:, L - 1:L]
        dec = b_last - b_row + li_row
        m_new = jnp.maximum(b_last + m, jnp.max(dec, axis=1, keepdims=True))
        vw = (vts[h].astype(F32) * jnp.exp(dec - m_new)).astype(BF16)
        out.append((jnp.exp(b_last + m - m_new) * C + _dot(vw, ks[h]), m_new))
    return tuple(out), jnp.concatenate(normed, axis=0).T


def _mlstm_mixer_kernel(x_ref, pre_g_ref, w_first_ref, w_rest_ref, conv_w_ref, conv_b_ref, bias_ref, hg_ref,
                        wqk_ref, wv_ref, og_ref, qmem_ref, gate_ref, uc_ref, hn_ref, tail_sc, c_sc, m_sc, *, chunk):
    tm = x_ref.shape[1]
    blocks = [slice(j * chunk, (j + 1) * chunk) for j in range(tm // chunk)]

    @pl.when(pl.program_id(1) == 0)
    def _():
        tail_sc[...] = jnp.zeros(tail_sc.shape, F32)
        c_sc[...] = jnp.zeros(c_sc.shape, F32)
        m_sc[...] = jnp.zeros(m_sc.shape, F32)

    hs = [_rms(x_ref[0, r, :], pre_g_ref[...]).astype(BF16) for r in blocks]
    firsts = [_dot(h, w_first_ref[...]) for h in hs]
    halos = [tail_sc[...]] + [f[chunk - CONV_HALO:chunk, 0:MIX_WIDTH] for f in firsts[:-1]]
    tail_sc[...] = firsts[-1][chunk - CONV_HALO:chunk, 0:MIX_WIDTH]

    k_scale = MLSTM_QK_DIM ** -0.5
    ones_rows = jnp.where(lax.broadcasted_iota(jnp.int32, (V_ONES_ROWS, chunk), 0) == 0, 1.0, 0.0)
    lane = lax.broadcasted_iota(jnp.int32, (chunk, LANES), 1)
    is_f = (lane >= MLSTM_HEADS) & (lane < 2 * MLSTM_HEADS)
    tril = (lax.broadcasted_iota(jnp.int32, (chunk, chunk), 1)
            <= lax.broadcasted_iota(jnp.int32, (chunk, chunk), 0)).astype(BF16)
    causal = (lax.broadcasted_iota(jnp.int32, (chunk, chunk), 0)
              <= lax.broadcasted_iota(jnp.int32, (chunk, chunk), 1))
    carry = tuple((c_sc[h], m_sc[h, 0:1, 0:1]) for h in range(MLSTM_HEADS))

    for r, h, first, halo in zip(blocks, hs, firsts, halos):
        u = first[:, 0:MIX_WIDTH]
        ext = jnp.concatenate([halo, u], axis=0)
        conv = conv_b_ref[...] + u * conv_w_ref[CONV_WIDTH - 1:CONV_WIDTH, :]
        for back in range(1, CONV_WIDTH):
            tap = CONV_WIDTH - 1 - back
            conv = conv + ext[CONV_HALO - back:CONV_HALO - back + chunk, :] * conv_w_ref[tap:tap + 1, :]
        uc_bf = (conv * jax.nn.sigmoid(conv)).astype(BF16)
        uc_ref[0, r, :] = uc_bf
        u_bf = u.astype(BF16)

        g = first[:, B_GATES[0]:B_GATES[1]] + bias_ref[...]
        log_f = jnp.minimum(g, 0.0) - jnp.log1p(jnp.exp(-jnp.abs(g)))
        gates = jnp.where(lane < MLSTM_HEADS, g, jnp.where(is_f, log_f, 0.0))
        gc = jnp.where(is_f, sum(_dot(tril, term) for term in _split3(gates)), gates)
        grow = gc.T[0:2 * MLSTM_HEADS, :]
        gcol = gc - pltpu.roll(gc, LANES - MLSTM_HEADS, 1)

        qts, ks, vts = [], [], []
        for hd in range(MLSTM_HEADS):
            cols = slice(MLSTM_HEAD_START[hd], MLSTM_HEAD_START[hd] + MXU_DIM)
            qk = _dot(uc_bf[:, cols], wqk_ref[hd])
            qts.append(qk[:, :MLSTM_PAD_QK].T.astype(BF16))
            ks.append((qk[:, MLSTM_PAD_QK:] * k_scale).astype(BF16))
            vt = _dot(u_bf[:, cols], wv_ref[hd]).T
            vts.append(jnp.concatenate([vt[0:MLSTM_V_DIM, :], ones_rows, vt[MLSTM_V_DIM + V_ONES_ROWS:, :]],
                                       axis=0).astype(BF16))

        def rest_projection(r=r, h=h):
            p = _dot(h, w_rest_ref[...])
            og_ref[0, r, :] = (jax.nn.sigmoid(p[:, B_O[0]:B_O[1]]) * hg_ref[...]).astype(BF16)
            qmem_ref[0, r, :] = p[:, B_QMEM[0]:B_QMEM[1]].astype(BF16)
            gate = p[:, B_GATE[0]:B_GATE[1]]
            gate_ref[0, r, :] = (gate * jax.nn.sigmoid(gate)).astype(BF16)

        carry, hn = _mlstm_chunk(carry, qts, ks, vts, gcol, grow, causal, rest_projection)
        hn_ref[0, r, :] = hn.astype(BF16)

    for hd in range(MLSTM_HEADS):
        c_sc[hd] = carry[hd][0]
        m_sc[hd] = jnp.broadcast_to(carry[hd][1], m_sc.shape[1:])


def _mlstm_mixer(x, pre_g, w_in, gate_bias, conv_w, conv_b, w_q, w_k, w_v, head_g):
    B, S, D = x.shape
    tm = min(MIXER_ROW_TILE, S)
    chunk = min(MLSTM_CHUNK, S)
    H = MLSTM_HEADS
    n_if = 2 * H
    o_if = MIX_WIDTH
    o_o = o_if + n_if
    w_first = jnp.concatenate([w_in[:, :o_o], jnp.zeros((D, LANES - n_if), w_in.dtype)], axis=1).astype(BF16)
    w_rest = w_in[:, o_o:].astype(BF16)
    def placed(w, width):
        blocks = []
        for h in range(H):
            off = h * MLSTM_V_DIM - MLSTM_HEAD_START[h]
            blocks.append(jnp.pad(w[h], ((off, MXU_DIM - MLSTM_V_DIM - off), (0, width - w.shape[-1]))))
        return jnp.stack(blocks)

    wqk = jnp.concatenate([placed(w_q, MLSTM_PAD_QK), placed(w_k, MLSTM_PAD_QK)], axis=-1)
    wv = placed(w_v, MLSTM_PAD_V)
    bias = jnp.pad(gate_bias, (0, LANES - n_if)).reshape(1, LANES)
    row = lambda b, i: (b, i, 0)
    widths = [MIX_WIDTH, MEM_WIDTH, 1024, MIX_WIDTH, MIX_WIDTH]
    return pl.pallas_call(
        functools.partial(_mlstm_mixer_kernel, chunk=chunk),
        grid=(B, S // tm),
        in_specs=[pl.BlockSpec((1, tm, D), row), _const_spec((1, D)), _const_spec(w_first.shape),
                  _const_spec(w_rest.shape), _const_spec((CONV_WIDTH, MIX_WIDTH)), _const_spec((1, MIX_WIDTH)),
                  _const_spec((1, LANES)), _const_spec((1, MIX_WIDTH)), _const_spec(wqk.shape),
                  _const_spec(wv.shape)],
        out_specs=[pl.BlockSpec((1, tm, w), row) for w in widths],
        out_shape=[jax.ShapeDtypeStruct((B, S, w), BF16) for w in widths],
        scratch_shapes=[pltpu.VMEM((CONV_HALO, MIX_WIDTH), F32),
                        pltpu.VMEM((H, MLSTM_PAD_V, MLSTM_PAD_QK), F32), pltpu.VMEM((H, 8, LANES), F32)],
        compiler_params=_params("parallel", "arbitrary"),
        name="mlstm_mixer",
    )(x, pre_g.reshape(1, D), w_first, w_rest, conv_w, conv_b.reshape(1, -1), bias, head_g.reshape(1, MIX_WIDTH),
      wqk.astype(BF16), wv.astype(BF16))


def _mla_layer(x, mem_kv, layer, positions, pre_g, w_in, q_a_g, w_uq, kv_a_g, w_ukv, w_out, post_g):
    q, k, v, qmem, gate = _mla_in(x, positions, pre_g, w_in, q_a_g, w_uq, kv_a_g, w_ukv)
    mix = _attention(q, k, v)
    specs = lambda tm: [pl.BlockSpec((1, tm, MIX_WIDTH), lambda b, i: (b, i, 0))]
    return _layer_out(_mla_out_kernel, "mla_out", x, [mix], specs, qmem, gate, mem_kv, layer, w_out, post_g)


def _mlstm_layer(x, mem_kv, layer, pre_g, w_in, gate_bias, conv_w, conv_b, w_q, w_k, w_v, head_g, skip,
                 w_out, post_g):
    og, qmem, gate, uc, hn = _mlstm_mixer(x, pre_g, w_in, gate_bias, conv_w, conv_b, w_q, w_k, w_v, head_g)
    row = lambda b, i: (b, i, 0)
    specs = lambda tm: [pl.BlockSpec((1, tm, MIX_WIDTH), row),
                        pl.BlockSpec((1, tm, MIX_WIDTH), row), pl.BlockSpec((1, tm, MIX_WIDTH), row),
                        _const_spec((1, MIX_WIDTH))]
    return _layer_out(_mlstm_out_kernel, "mlstm_out", x, [hn, og, uc, skip.reshape(1, -1)], specs,
                      qmem, gate, mem_kv, layer, w_out, post_g)


def kernel(x, mem, positions, a_pre_g, a_w_in, a_q_a_g, a_w_uq, a_kv_a_g, a_w_ukv, a_mem_g, a_w_mem_kv, a_w_out, a_post_g, b_pre_g, b_w_in, b_gate_bias, b_conv_w, b_conv_b, b_w_q, b_w_k, b_w_v, b_head_g, b_skip, b_mem_g, b_w_mem_kv, b_w_out, b_post_g):
    depth = a_pre_g.shape[0] + b_pre_g.shape[0]
    mem_gs = jnp.stack([(a_mem_g, b_mem_g)[i % 2][i // 2] for i in range(depth)])
    w_mem_kvs = jnp.stack([(a_w_mem_kv, b_w_mem_kv)[i % 2][i // 2] for i in range(depth)])
    mem_kv = _mem_kv(mem, mem_gs, w_mem_kvs)
    for i in range(depth):
        j = i // 2
        if i % 2 == 0:
            x = _mla_layer(x, mem_kv, i, positions, a_pre_g[j], a_w_in[j], a_q_a_g[j], a_w_uq[j], a_kv_a_g[j],
                           a_w_ukv[j], a_w_out[j], a_post_g[j])
        else:
            x = _mlstm_layer(x, mem_kv, i, b_pre_g[j], b_w_in[j], b_gate_bias[j], b_conv_w[j], b_conv_b[j],
                             b_w_q[j], b_w_k[j], b_w_v[j], b_head_g[j], b_skip[j], b_w_out[j], b_post_g[j])
    return x
```

```python
import functools

import jax
import jax.numpy as jnp
from jax import lax
from jax.experimental import pallas as pl
from jax.experimental.pallas import tpu as pltpu

EPS = 1e-6
ROPE_THETA = 10000.0
MEM_HEADS = 4
MEM_HEAD_DIM = 64
MEM_WIDTH = MEM_HEADS * MEM_HEAD_DIM
QK_NOPE_DIM = 128
QK_ROPE_DIM = 64
V_HEAD_DIM = 128
MLA_HEADS = 6
V_EXT_DIM = V_HEAD_DIM + 16
Q_LORA_RANK = 384
KV_LORA_RANK = 256
MLSTM_HEADS = 4
MLSTM_V_DIM = 192
MLSTM_QK_DIM = 96
CONV_WIDTH = 4
MIX_WIDTH = 768

LANES = 128
MXU_DIM = 256
VMEM_LIMIT_BYTES = 56 * 1024 * 1024

MLA_IN_ROW_TILE = 1024
MIXER_ROW_TILE = 1024
IN_SUB_ROWS = 256
TAIL_ROW_TILE = 1024
TAIL_SUB_ROWS = 256
ATTN_TQ = 1024
ATTN_CW = 256
ATTN_TK = 256
ATTN_HEADS_PER_STEP = 2
ATTN_LOOKAHEAD = 10
MLSTM_CHUNK = 256

LOG2E = 1.4426950408889634
MLSTM_PAD_QK = LANES
MLSTM_PAD_V = MXU_DIM
MLSTM_HEAD_START = tuple((h * MLSTM_V_DIM // LANES) * LANES for h in range(MLSTM_HEADS))

F32 = jnp.float32
BF16 = jnp.bfloat16


def _dot(a, b):
    return jnp.dot(a, b, preferred_element_type=F32)


def _dot_nt(a, b):
    return lax.dot_general(a, b, (((1,), (1,)), ((), ())), preferred_element_type=F32)


def _rms(x, g, width=None):
    width = x.shape[-1] if width is None else width
    ms = jnp.sum(x * x, axis=-1, keepdims=True) * (1.0 / width)
    return x * lax.rsqrt(ms + EPS) * g


def _split3(x):
    b1 = x.astype(BF16)
    r1 = x - b1.astype(F32)
    b2 = r1.astype(BF16)
    b3 = (r1 - b2.astype(F32)).astype(BF16)
    return b1, b2, b3


def _params(*semantics):
    return pltpu.CompilerParams(dimension_semantics=semantics, vmem_limit_bytes=VMEM_LIMIT_BYTES)


def _const_spec(shape):
    zeros = (0,) * len(shape)
    return pl.BlockSpec(shape, lambda *_: zeros)


def _rope_table(pos_ref, invf_ref):
    ang = invf_ref[...] * pos_ref[0].astype(F32)
    cos_t = jnp.cos(ang)
    sin_t = jnp.sin(ang)
    return jnp.concatenate([cos_t, cos_t, -sin_t, sin_t], axis=0).T


def _mem_kv_kernel(mem_ref, g_ref, w_ref, k_ref, v_ref):
    n_mem = mem_ref.shape[1]
    mem = mem_ref[0]
    col_head = lax.broadcasted_iota(jnp.int32, (n_mem, MEM_WIDTH), 1) // MEM_HEAD_DIM
    for layer in range(w_ref.shape[0]):
        hn = _rms(mem, g_ref[layer]).astype(BF16)
        kv = _dot(hn, w_ref[layer])
        k = kv[:, :MEM_WIDTH] * (MEM_HEAD_DIM ** -0.5 * LOG2E)
        v = kv[:, MEM_WIDTH:]
        for h in range(MEM_HEADS):
            rows = pl.ds(h * n_mem, n_mem)
            k_ref[layer, 0, rows, :] = jnp.where(col_head == h, k, 0.0).astype(BF16)
            v_ref[layer, 0, rows, :] = jnp.where(col_head == h, v, 0.0).astype(BF16)


def _mem_kv(mem, mem_gs, w_mem_kvs):
    B, n_mem, D = mem.shape
    layers = w_mem_kvs.shape[0]
    out = jax.ShapeDtypeStruct((layers, B, MEM_HEADS * n_mem, MEM_WIDTH), BF16)
    spec = pl.BlockSpec((layers, 1, MEM_HEADS * n_mem, MEM_WIDTH), lambda b: (0, b, 0, 0))
    return pl.pallas_call(
        _mem_kv_kernel,
        grid=(B,),
        in_specs=[pl.BlockSpec((1, n_mem, D), lambda b: (b, 0, 0)), _const_spec((layers, 1, D)),
                  _const_spec(w_mem_kvs.shape)],
        out_specs=[spec, spec],
        out_shape=[out, out],
        compiler_params=_params("parallel"),
        name="mem_kv",
    )(mem, mem_gs.reshape(layers, 1, D), w_mem_kvs.astype(BF16))


A_CQ = (0, Q_LORA_RANK)
A_CKV = (A_CQ[1], A_CQ[1] + KV_LORA_RANK)
A_QMEM = (A_CKV[1], A_CKV[1] + MEM_WIDTH)
A_GATE = (A_QMEM[1], A_QMEM[1] + 1024)
A_KROPE = (A_GATE[1], A_GATE[1] + 2 * QK_ROPE_DIM)
Q_ROPE_OFF = MLA_HEADS * QK_NOPE_DIM


def _mla_in_kernel(x_ref, pos_ref, invf_ref, pre_g_ref, w_in_ref, qa_g_ref, w_uq_ref, kva_g_ref,
                   w_ukv_ref, q_ref, k_ref, v_ref, qmem_ref, gate_ref):
    q_scale = (QK_NOPE_DIM + QK_ROPE_DIM) ** -0.5 * LOG2E
    tm = x_ref.shape[1]
    sub = min(IN_SUB_ROWS, tm)
    blocks = [slice(i * sub, (i + 1) * sub) for i in range(tm // sub)]
    projected = [_dot(_rms(x_ref[0, r, :], pre_g_ref[...]).astype(BF16), w_in_ref[...]) for r in blocks]
    cs_tile = _rope_table(pos_ref, invf_ref)
    pad_row = lax.broadcasted_iota(jnp.int32, (V_EXT_DIM - V_HEAD_DIM, sub), 0)
    ones_row = jnp.where(pad_row == 0, 1.0, 0.0).astype(BF16)
    for r, p in zip(blocks, projected):
        cs = cs_tile[r, :]
        qmem_ref[0, r, :] = p[:, A_QMEM[0]:A_QMEM[1]].astype(BF16)
        gate = p[:, A_GATE[0]:A_GATE[1]]
        gate_ref[0, r, :] = (gate * jax.nn.sigmoid(gate)).astype(BF16)

        c_q = _rms(p[:, A_CQ[0]:A_CQ[1]], qa_g_ref[...]).astype(BF16)
        q = _dot(c_q, w_uq_ref[...])
        c_kv = _rms(p[:, A_CKV[0]:A_CKV[1]], kva_g_ref[...]).astype(BF16)
        kv = _dot(c_kv, w_ukv_ref[...])

        kr = p[:, A_KROPE[0]:A_KROPE[1]] * cs
        k_rot = (kr + pltpu.roll(kr, QK_ROPE_DIM, 1)).astype(BF16)
        for hd in range(MLA_HEADS):
            nope = slice(hd * QK_NOPE_DIM, (hd + 1) * QK_NOPE_DIM)
            rope = slice(Q_ROPE_OFF + hd * LANES, Q_ROPE_OFF + (hd + 1) * LANES)
            q_ref[0, hd, 0:LANES, r] = (q[:, nope] * q_scale).T.astype(BF16)
            q_ref[0, hd, LANES:2 * LANES, r] = (q[:, rope] * cs * q_scale).T.astype(BF16)
            k_ref[0, hd, r, 0:LANES] = kv[:, 2 * hd * LANES:(2 * hd + 1) * LANES].astype(BF16)
            k_ref[0, hd, r, LANES:2 * LANES] = k_rot
            v_ref[0, hd, 0:V_HEAD_DIM, r] = kv[:, (2 * hd + 1) * LANES:(2 * hd + 2) * LANES].T.astype(BF16)
            v_ref[0, hd, V_HEAD_DIM:V_EXT_DIM, r] = ones_row


def _rope_cols(w, start):
    half = QK_ROPE_DIM // 2
    return [w[:, start:start + QK_ROPE_DIM], w[:, start + half:start + QK_ROPE_DIM], w[:, start:start + half]]


def _mla_in(x, positions, pre_g, w_in, q_a_g, w_uq, kv_a_g, w_ukv):
    B, S, D = x.shape
    tm = min(MLA_IN_ROW_TILE, S)
    inv_freq = ROPE_THETA ** (-jnp.arange(0, QK_ROPE_DIM, 2, dtype=F32) / QK_ROPE_DIM)
    invf = inv_freq.reshape(QK_ROPE_DIM // 2, 1)
    o_kr = Q_LORA_RANK + KV_LORA_RANK
    o_qm = o_kr + QK_ROPE_DIM
    w_in_p = jnp.concatenate([w_in[:, :o_kr], w_in[:, o_qm:]] + _rope_cols(w_in, o_kr), axis=1).astype(BF16)
    head_w = QK_NOPE_DIM + QK_ROPE_DIM
    uq_cols = [w_uq[:, h * head_w:h * head_w + QK_NOPE_DIM] for h in range(MLA_HEADS)]
    for h in range(MLA_HEADS):
        uq_cols += _rope_cols(w_uq, h * head_w + QK_NOPE_DIM)
    w_uq_p = jnp.concatenate(uq_cols, axis=1).astype(BF16)
    w_ukv_p = w_ukv.astype(BF16)

    row = lambda b, i: (b, i, 0)
    head_row = lambda b, i: (b, 0, i, 0)
    head_col = lambda b, i: (b, 0, 0, i)
    out_shape = [
        jax.ShapeDtypeStruct((B, MLA_HEADS, 2 * LANES, S), BF16),
        jax.ShapeDtypeStruct((B, MLA_HEADS, S, 2 * LANES), BF16),
        jax.ShapeDtypeStruct((B, MLA_HEADS, V_EXT_DIM, S), BF16),
        jax.ShapeDtypeStruct((B, S, MEM_WIDTH), BF16),
        jax.ShapeDtypeStruct((B, S, 1024), BF16),
    ]
    out_specs = [
        pl.BlockSpec((1, MLA_HEADS, 2 * LANES, tm), head_col),
        pl.BlockSpec((1, MLA_HEADS, tm, 2 * LANES), head_row),
        pl.BlockSpec((1, MLA_HEADS, V_EXT_DIM, tm), head_col),
        pl.BlockSpec((1, tm, MEM_WIDTH), row),
        pl.BlockSpec((1, tm, 1024), row),
    ]
    return pl.pallas_call(
        _mla_in_kernel,
        grid=(B, S // tm),
        in_specs=[pl.BlockSpec((1, tm, D), row),
                  pl.BlockSpec((1, 1, tm), lambda b, i: (b, 0, i)), _const_spec(invf.shape),
                  _const_spec((1, D)), _const_spec(w_in_p.shape),
                  _const_spec((1, Q_LORA_RANK)), _const_spec(w_uq_p.shape),
                  _const_spec((1, KV_LORA_RANK)), _const_spec(w_ukv_p.shape)],
        out_specs=out_specs,
        out_shape=out_shape,
        compiler_params=_params("parallel", "parallel"),
        name="mla_in",
    )(x, positions.reshape(B, 1, S), invf, pre_g.reshape(1, D), w_in_p, q_a_g.reshape(1, -1), w_uq_p, kv_a_g.reshape(1, -1),
      w_ukv_p)


def _attn_kernel(qt_ref, k_ref, vt_ref, o_ref, *, tq, tk, cw):
    heads = k_ref.shape[1]
    S = k_ref.shape[2]
    dve = vt_ref.shape[2]
    dv = V_HEAD_DIM
    nc = tq // cw
    causal = (lax.broadcasted_iota(jnp.int32, (cw, cw), 0) <= lax.broadcasted_iota(jnp.int32, (cw, cw), 1))

    units = []
    for j in range(S // tq):
        for k0 in range(0, (j + 1) * tq, tk):
            for c in range(nc):
                q_start = j * tq + c * cw
                if k0 < q_start + cw:
                    units += [(j, h, c, k0, tk, k0 >= q_start) for h in range(heads)]

    def score(unit):
        j, h, c, k0, n, diagonal = unit
        s = _dot(k_ref[0, h, pl.ds(k0, n), :], qt_ref[0, h, :, pl.ds(j * tq + c * cw, cw)])
        if diagonal:
            s = jnp.where(causal[k0 - (j * tq + c * cw):k0 - (j * tq + c * cw) + tk, :], s, -jnp.inf)
        return s

    scores, carries = {}, {}
    for i in range(len(units) + ATTN_LOOKAHEAD):
        if i < len(units):
            scores[i] = score(units[i])
        if i < ATTN_LOOKAHEAD:
            continue
        j, h, c, k0, n, diagonal = units[i - ATTN_LOOKAHEAD]
        m, acc = carries.pop((j, h, c), (jnp.full((1, cw), -jnp.inf, F32), jnp.zeros((dve, cw), F32)))
        s = scores.pop(i - ATTN_LOOKAHEAD)
        m_new = jnp.maximum(m, jnp.max(s, axis=0, keepdims=True))
        p = jnp.exp2(s - m_new).astype(BF16)
        acc = jnp.exp2(m - m_new) * acc + _dot(vt_ref[0, h, :, pl.ds(k0, n)], p)
        if k0 + n == j * tq + (c + 1) * cw:
            out = acc[0:dv, :] / acc[dv:dv + 1, :]
            o_ref[0, pl.ds(j * tq + c * cw, cw), h * dv:(h + 1) * dv] = out.T.astype(o_ref.dtype)
        else:
            carries[j, h, c] = (m_new, acc)


def _attention(qt, k, vt):
    B, H, S, dqk = k.shape
    dve = vt.shape[2]
    tq = min(ATTN_TQ, S)
    tk = min(ATTN_TK, tq)
    cw = min(ATTN_CW, tq)
    hp = ATTN_HEADS_PER_STEP
    head = lambda b, h: (b, h, 0, 0)
    return pl.pallas_call(
        functools.partial(_attn_kernel, tq=tq, tk=tk, cw=cw),
        grid=(B, H // hp),
        in_specs=[pl.BlockSpec((1, hp, dqk, S), head), pl.BlockSpec((1, hp, S, dqk), head),
                  pl.BlockSpec((1, hp, dve, S), head)],
        out_specs=pl.BlockSpec((1, S, hp * V_HEAD_DIM), lambda b, h: (b, 0, h)),
        out_shape=jax.ShapeDtypeStruct((B, S, H * V_HEAD_DIM), BF16),
        compiler_params=_params("parallel", "parallel"),
        name="mla_attention",
    )(qt, k, vt)


def _memory_probs(s, n_mem):
    probs = []
    for h in range(MEM_HEADS):
        sh = s[:, h * n_mem:(h + 1) * n_mem]
        e = jnp.exp2(sh - jnp.max(sh, axis=-1, keepdims=True))
        probs.append((e / jnp.sum(e, axis=-1, keepdims=True)).astype(BF16))
    return jnp.concatenate(probs, axis=-1)


def _tail(gated_mix, x_ref, qmem_ref, gate_ref, kexp_ref, vexp_ref, w_out_ref, post_g_ref, o_ref):
    tm = x_ref.shape[1]
    sub = min(TAIL_SUB_ROWS, tm)
    blocks = [slice(i * sub, (i + 1) * sub) for i in range(tm // sub)]
    n_mem = kexp_ref.shape[2] // MEM_HEADS
    kexp = kexp_ref[0, 0]
    scores = [_dot_nt(qmem_ref[0, r, :], kexp) for r in blocks]
    for r, s in zip(blocks, scores):
        mo = _dot(_memory_probs(s, n_mem), vexp_ref[0, 0])
        y_mix = gated_mix(r, gate_ref[0, r, 0:MIX_WIDTH])
        y_mem = (mo * gate_ref[0, r, MIX_WIDTH:].astype(F32)).astype(BF16)
        y = _dot(jnp.concatenate([y_mix, y_mem], axis=-1), w_out_ref[...])
        o_ref[0, r, :] = x_ref[0, r, :] + _rms(y, post_g_ref[...])


def _mla_out_kernel(x_ref, mix_ref, qmem_ref, gate_ref, kexp_ref, vexp_ref, w_out_ref, post_g_ref, o_ref):
    gated_mix = lambda r, gate: mix_ref[0, r, :] * gate
    _tail(gated_mix, x_ref, qmem_ref, gate_ref, kexp_ref, vexp_ref, w_out_ref, post_g_ref, o_ref)


def _mlstm_out_kernel(x_ref, hn_ref, og_ref, uc_ref, skip_ref, qmem_ref, gate_ref, kexp_ref, vexp_ref,
                      w_out_ref, post_g_ref, o_ref):
    def gated_mix(r, gate):
        mix = og_ref[0, r, :].astype(F32) * hn_ref[0, r, :].astype(F32) + skip_ref[...] * uc_ref[0, r, :].astype(F32)
        return (mix * gate.astype(F32)).astype(BF16)

    _tail(gated_mix, x_ref, qmem_ref, gate_ref, kexp_ref, vexp_ref, w_out_ref, post_g_ref, o_ref)


def _layer_out(kernel_fn, name, x, mixer_inputs, mixer_specs, qmem, gate, mem_kv, layer, w_out, post_g):
    B, S, D = x.shape
    tm = min(TAIL_ROW_TILE, S)
    kexp, vexp = mem_kv
    row = lambda b, i: (b, i, 0)
    per_batch = lambda b, i: (layer, b, 0, 0)
    in_specs = ([pl.BlockSpec((1, tm, D), row)] + mixer_specs(tm) +
                [pl.BlockSpec((1, tm, MEM_WIDTH), row), pl.BlockSpec((1, tm, gate.shape[-1]), row),
                 pl.BlockSpec((1, 1) + kexp.shape[2:], per_batch), pl.BlockSpec((1, 1) + vexp.shape[2:], per_batch),
                 _const_spec(w_out.shape), _const_spec((1, D))])
    return pl.pallas_call(
        kernel_fn,
        grid=(B, S // tm),
        in_specs=in_specs,
        out_specs=pl.BlockSpec((1, tm, D), row),
        out_shape=jax.ShapeDtypeStruct((B, S, D), F32),
        compiler_params=_params("parallel", "parallel"),
        name=name,
    )(x, *mixer_inputs, qmem, gate, kexp, vexp, w_out.astype(BF16), post_g.reshape(1, D))


B_GATES = (MIX_WIDTH, MIX_WIDTH + LANES)
B_O = (0, MIX_WIDTH)
B_QMEM = (B_O[1], B_O[1] + MEM_WIDTH)
B_GATE = (B_QMEM[1], B_QMEM[1] + 1024)
CONV_HALO = 8
V_ONES_ROWS = 16


def _mlstm_chunk(carry, qts, ks, vts, gcol, grow, causal, between):
    H = len(qts)
    L = ks[0].shape[0]
    qk = [_dot(ks[h], qts[h]) for h in range(H)]
    cq = [_dot(carry[h][0].astype(BF16), qts[h]) for h in range(H)]
    between()
    out, normed = [], []
    for h in range(H):
        C, m = carry[h]
        c_col = gcol[:, h:h + 1]
        li_row = grow[h:h + 1, :]
        b_row = grow[MLSTM_HEADS + h:MLSTM_HEADS + h + 1, :]
        d = jnp.where(causal, c_col + b_row, -jnp.inf)
        inter = b_row + m
        m_t = jnp.maximum(inter, jnp.max(d, axis=0, keepdims=True))
        sqk = (qk[h] * jnp.exp(d - m_t)).astype(BF16)
        num = jnp.exp(inter - m_t) * cq[h] + _dot(vts[h], sqk)
        den = jnp.maximum(jnp.abs(num[MLSTM_V_DIM:MLSTM_V_DIM + 1, :]), jnp.exp(-m_t))
        hnum = num[0:MLSTM_V_DIM, :]
        inv_den = 1.0 / den
        ms = jnp.sum(hnum * hnum, axis=0, keepdims=True) * (inv_den * inv_den * (1.0 / MLSTM_V_DIM))
        normed.append(hnum * (inv_den * lax.rsqrt(ms + EPS)))

        b_last = b_row[:, L - 1:L]
        dec = b_last - b_row + li_row
        m_new = jnp.maximum(b_last + m, jnp.max(dec, axis=1, keepdims=True))
        vw = (vts[h].astype(F32) * jnp.exp(dec - m_new)).astype(BF16)
        out.append((jnp.exp(b_last + m - m_new) * C + _dot(vw, ks[h]), m_new))
    return tuple(out), jnp.concatenate(normed, axis=0).T


def _mlstm_mixer_kernel(x_ref, pre_g_ref, w_first_ref, w_rest_ref, conv_w_ref, conv_b_ref, bias_ref, hg_ref,
                        wqk_ref, wv_ref, og_ref, qmem_ref, gate_ref, uc_ref, hn_ref, tail_sc, c_sc, m_sc, *, chunk):
    tm = x_ref.shape[1]
    blocks = [slice(j * chunk, (j + 1) * chunk) for j in range(tm // chunk)]

    @pl.when(pl.program_id(1) == 0)
    def _():
        tail_sc[...] = jnp.zeros(tail_sc.shape, F32)
        c_sc[...] = jnp.zeros(c_sc.shape, F32)
        m_sc[...] = jnp.zeros(m_sc.shape, F32)

    hs = [_rms(x_ref[0, r, :], pre_g_ref[...]).astype(BF16) for r in blocks]
    firsts = [_dot(h, w_first_ref[...]) for h in hs]
    halos = [tail_sc[...]] + [f[chunk - CONV_HALO:chunk, 0:MIX_WIDTH] for f in firsts[:-1]]
    tail_sc[...] = firsts[-1][chunk - CONV_HALO:chunk, 0:MIX_WIDTH]

    k_scale = MLSTM_QK_DIM ** -0.5
    ones_rows = jnp.where(lax.broadcasted_iota(jnp.int32, (V_ONES_ROWS, chunk), 0) == 0, 1.0, 0.0)
    lane = lax.broadcasted_iota(jnp.int32, (chunk, LANES), 1)
    is_f = (lane >= MLSTM_HEADS) & (lane < 2 * MLSTM_HEADS)
    tril = (lax.broadcasted_iota(jnp.int32, (chunk, chunk), 1)
            <= lax.broadcasted_iota(jnp.int32, (chunk, chunk), 0)).astype(BF16)
    causal = (lax.broadcasted_iota(jnp.int32, (chunk, chunk), 0)
              <= lax.broadcasted_iota(jnp.int32, (chunk, chunk), 1))
    carry = tuple((c_sc[h], m_sc[h, 0:1, 0:1]) for h in range(MLSTM_HEADS))

    for r, h, first, halo in zip(blocks, hs, firsts, halos):
        u = first[:, 0:MIX_WIDTH]
        ext = jnp.concatenate([halo, u], axis=0)
        conv = conv_b_ref[...] + u * conv_w_ref[CONV_WIDTH - 1:CONV_WIDTH, :]
        for back in range(1, CONV_WIDTH):
            tap = CONV_WIDTH - 1 - back
            conv = conv + ext[CONV_HALO - back:CONV_HALO - back + chunk, :] * conv_w_ref[tap:tap + 1, :]
        uc_bf = (conv * jax.nn.sigmoid(conv)).astype(BF16)
        uc_ref[0, r, :] = uc_bf
        u_bf = u.astype(BF16)

        g = first[:, B_GATES[0]:B_GATES[1]] + bias_ref[...]
        log_f = jnp.minimum(g, 0.0) - jnp.log1p(jnp.exp(-jnp.abs(g)))
        gates = jnp.where(lane < MLSTM_HEADS, g, jnp.where(is_f, log_f, 0.0))
        gc = jnp.where(is_f, sum(_dot(tril, term) for term in _split3(gates)), gates)
        grow = gc.T[0:2 * MLSTM_HEADS, :]
        gcol = gc - pltpu.roll(gc, LANES - MLSTM_HEADS, 1)

        qts, ks, vts = [], [], []
        for hd in range(MLSTM_HEADS):
            cols = slice(MLSTM_HEAD_START[hd], MLSTM_HEAD_START[hd] + MXU_DIM)
            qk = _dot(uc_bf[:, cols], wqk_ref[hd])
            qts.append(qk[:, :MLSTM_PAD_QK].T.astype(BF16))
            ks.append((qk[:, MLSTM_PAD_QK:] * k_scale).astype(BF16))
            vt = _dot(u_bf[:, cols], wv_ref[hd]).T
            vts.append(jnp.concatenate([vt[0:MLSTM_V_DIM, :], ones_rows, vt[MLSTM_V_DIM + V_ONES_ROWS:, :]],
                                       axis=0).astype(BF16))

        def rest_projection(r=r, h=h):
            p = _dot(h, w_rest_ref[...])
            og_ref[0, r, :] = (jax.nn.sigmoid(p[:, B_O[0]:B_O[1]]) * hg_ref[...]).astype(BF16)
            qmem_ref[0, r, :] = p[:, B_QMEM[0]:B_QMEM[1]].astype(BF16)
            gate = p[:, B_GATE[0]:B_GATE[1]]
            gate_ref[0, r, :] = (gate * jax.nn.sigmoid(gate)).astype(BF16)

        carry, hn = _mlstm_chunk(carry, qts, ks, vts, gcol, grow, causal, rest_projection)
        hn_ref[0, r, :] = hn.astype(BF16)

    for hd in range(MLSTM_HEADS):
        c_sc[hd] = carry[hd][0]
        m_sc[hd] = jnp.broadcast_to(carry[hd][1], m_sc.shape[1:])


def _mlstm_mixer(x, pre_g, w_in, gate_bias, conv_w, conv_b, w_q, w_k, w_v, head_g):
    B, S, D = x.shape
    tm = min(MIXER_ROW_TILE, S)
    chunk = min(MLSTM_CHUNK, S)
    H = MLSTM_HEADS
    n_if = 2 * H
    o_if = MIX_WIDTH
    o_o = o_if + n_if
    w_first = jnp.concatenate([w_in[:, :o_o], jnp.zeros((D, LANES - n_if), w_in.dtype)], axis=1).astype(BF16)
    w_rest = w_in[:, o_o:].astype(BF16)
    def placed(w, width):
        blocks = []
        for h in range(H):
            off = h * MLSTM_V_DIM - MLSTM_HEAD_START[h]
            blocks.append(jnp.pad(w[h], ((off, MXU_DIM - MLSTM_V_DIM - off), (0, width - w.shape[-1]))))
        return jnp.stack(blocks)

    wqk = jnp.concatenate([placed(w_q, MLSTM_PAD_QK), placed(w_k, MLSTM_PAD_QK)], axis=-1)
    wv = placed(w_v, MLSTM_PAD_V)
    bias = jnp.pad(gate_bias, (0, LANES - n_if)).reshape(1, LANES)
    row = lambda b, i: (b, i, 0)
    widths = [MIX_WIDTH, MEM_WIDTH, 1024, MIX_WIDTH, MIX_WIDTH]
    return pl.pallas_call(
        functools.partial(_mlstm_mixer_kernel, chunk=chunk),
        grid=(B, S // tm),
        in_specs=[pl.BlockSpec((1, tm, D), row), _const_spec((1, D)), _const_spec(w_first.shape),
                  _const_spec(w_rest.shape), _const_spec((CONV_WIDTH, MIX_WIDTH)), _const_spec((1, MIX_WIDTH)),
                  _const_spec((1, LANES)), _const_spec((1, MIX_WIDTH)), _const_spec(wqk.shape),
                  _const_spec(wv.shape)],
        out_specs=[pl.BlockSpec((1, tm, w), row) for w in widths],
        out_shape=[jax.ShapeDtypeStruct((B, S, w), BF16) for w in widths],
        scratch_shapes=[pltpu.VMEM((CONV_HALO, MIX_WIDTH), F32),
                        pltpu.VMEM((H, MLSTM_PAD_V, MLSTM_PAD_QK), F32), pltpu.VMEM((H, 8, LANES), F32)],
        compiler_params=_params("parallel", "arbitrary"),
        name="mlstm_mixer",
    )(x, pre_g.reshape(1, D), w_first, w_rest, conv_w, conv_b.reshape(1, -1), bias, head_g.reshape(1, MIX_WIDTH),
      wqk.astype(BF16), wv.astype(BF16))


def _mla_layer(x, mem_kv, layer, positions, pre_g, w_in, q_a_g, w_uq, kv_a_g, w_ukv, w_out, post_g):
    q, k, v, qmem, gate = _mla_in(x, positions, pre_g, w_in, q_a_g, w_uq, kv_a_g, w_ukv)
    mix = _attention(q, k, v)
    specs = lambda tm: [pl.BlockSpec((1, tm, MIX_WIDTH), lambda b, i: (b, i, 0))]
    return _layer_out(_mla_out_kernel, "mla_out", x, [mix], specs, qmem, gate, mem_kv, layer, w_out, post_g)


def _mlstm_layer(x, mem_kv, layer, pre_g, w_in, gate_bias, conv_w, conv_b, w_q, w_k, w_v, head_g, skip,
                 w_out, post_g):
    og, qmem, gate, uc, hn = _mlstm_mixer(x, pre_g, w_in, gate_bias, conv_w, conv_b, w_q, w_k, w_v, head_g)
    row = lambda b, i: (b, i, 0)
    specs = lambda tm: [pl.BlockSpec((1, tm, MIX_WIDTH), row),
                        pl.BlockSpec((1, tm, MIX_WIDTH), row), pl.BlockSpec((1, tm, MIX_WIDTH), row),
                        _const_spec((1, MIX_WIDTH))]
    return _layer_out(_mlstm_out_kernel, "mlstm_out", x, [hn, og, uc, skip.reshape(1, -1)], specs,
                      qmem, gate, mem_kv, layer, w_out, post_g)


def kernel(x, mem, positions, a_pre_g, a_w_in, a_q_a_g, a_w_uq, a_kv_a_g, a_w_ukv, a_mem_g, a_w_mem_kv, a_w_out, a_post_g, b_pre_g, b_w_in, b_gate_bias, b_conv_w, b_conv_b, b_w_q, b_w_k, b_w_v, b_head_g, b_skip, b_mem_g, b_w_mem_kv, b_w_out, b_post_g):
    depth = a_pre_g.shape[0] + b_pre_g.shape[0]
    mem_gs = jnp.stack([(a_mem_g, b_mem_g)[i % 2][i // 2] for i in range(depth)])
    w_mem_kvs = jnp.stack([(a_w_mem_kv, b_w_mem_kv)[i % 2][i // 2] for i in range(depth)])
    mem_kv = _mem_kv(mem, mem_gs, w_mem_kvs)
    for i in range(depth):
        j = i // 2
        if i % 2 == 0:
            x = _mla_layer(x, mem_kv, i, positions, a_pre_g[j], a_w_in[j], a_q_a_g[j], a_w_uq[j], a_kv_a_g[j],
                           a_w_ukv[j], a_w_out[j], a_post_g[j])
        else:
            x = _mlstm_layer(x, mem_kv, i, b_pre_g[j], b_w_in[j], b_gate_bias[j], b_conv_w[j], b_conv_b[j],
                             b_w_q[j], b_w_k[j], b_w_v[j], b_head_g[j], b_skip[j], b_w_out[j], b_post_g[j])
    return x
```

```python
import functools

import jax
import jax.numpy as jnp
from jax import lax
from jax.experimental import pallas as pl
from jax.experimental.pallas import tpu as pltpu

EPS = 1e-6
ROPE_THETA = 10000.0
MEM_HEADS = 4
MEM_HEAD_DIM = 64
MEM_WIDTH = MEM_HEADS * MEM_HEAD_DIM
QK_NOPE_DIM = 128
QK_ROPE_DIM = 64
V_HEAD_DIM = 128
MLA_HEADS = 6
V_EXT_DIM = V_HEAD_DIM + 16
Q_LORA_RANK = 384
KV_LORA_RANK = 256
MLSTM_HEADS = 4
MLSTM_V_DIM = 192
MLSTM_QK_DIM = 96
CONV_WIDTH = 4
MIX_WIDTH = 768

LANES = 128
MXU_DIM = 256
VMEM_LIMIT_BYTES = 56 * 1024 * 1024

MLA_IN_ROW_TILE = 1024
MIXER_ROW_TILE = 1024
IN_SUB_ROWS = 512
TAIL_ROW_TILE = 1024
TAIL_SUB_ROWS = 256
ATTN_TQ = 1024
ATTN_CW = 256
ATTN_TK = 256
ATTN_HEADS_PER_STEP = 2
ATTN_LOOKAHEAD = 10
MLSTM_CHUNK = 256

LOG2E = 1.4426950408889634
MLSTM_PAD_QK = LANES
MLSTM_PAD_V = MXU_DIM
MLSTM_HEAD_START = tuple((h * MLSTM_V_DIM // LANES) * LANES for h in range(MLSTM_HEADS))

F32 = jnp.float32
BF16 = jnp.bfloat16


def _dot(a, b):
    return jnp.dot(a, b, preferred_element_type=F32)


def _dot_nt(a, b):
    return lax.dot_general(a, b, (((1,), (1,)), ((), ())), preferred_element_type=F32)


def _rms(x, g, width=None):
    width = x.shape[-1] if width is None else width
    ms = jnp.sum(x * x, axis=-1, keepdims=True) * (1.0 / width)
    return x * lax.rsqrt(ms + EPS) * g


def _split3(x):
    b1 = x.astype(BF16)
    r1 = x - b1.astype(F32)
    b2 = r1.astype(BF16)
    b3 = (r1 - b2.astype(F32)).astype(BF16)
    return b1, b2, b3


def _params(*semantics):
    return pltpu.CompilerParams(dimension_semantics=semantics, vmem_limit_bytes=VMEM_LIMIT_BYTES)


def _const_spec(shape):
    zeros = (0,) * len(shape)
    return pl.BlockSpec(shape, lambda *_: zeros)


def _rope_table(pos_ref, invf_ref):
    ang = invf_ref[...] * pos_ref[0].astype(F32)
    cos_t = jnp.cos(ang)
    sin_t = jnp.sin(ang)
    return jnp.concatenate([cos_t, cos_t, -sin_t, sin_t], axis=0).T


def _mem_kv_kernel(mem_ref, g_ref, w_ref, k_ref, v_ref):
    n_mem = mem_ref.shape[1]
    mem = mem_ref[0]
    col_head = lax.broadcasted_iota(jnp.int32, (n_mem, MEM_WIDTH), 1) // MEM_HEAD_DIM
    for layer in range(w_ref.shape[0]):
        hn = _rms(mem, g_ref[layer]).astype(BF16)
        kv = _dot(hn, w_ref[layer])
        k = kv[:, :MEM_WIDTH] * (MEM_HEAD_DIM ** -0.5 * LOG2E)
        v = kv[:, MEM_WIDTH:]
        for h in range(MEM_HEADS):
            rows = pl.ds(h * n_mem, n_mem)
            k_ref[layer, 0, rows, :] = jnp.where(col_head == h, k, 0.0).astype(BF16)
            v_ref[layer, 0, rows, :] = jnp.where(col_head == h, v, 0.0).astype(BF16)


def _mem_kv(mem, mem_gs, w_mem_kvs):
    B, n_mem, D = mem.shape
    layers = w_mem_kvs.shape[0]
    out = jax.ShapeDtypeStruct((layers, B, MEM_HEADS * n_mem, MEM_WIDTH), BF16)
    spec = pl.BlockSpec((layers, 1, MEM_HEADS * n_mem, MEM_WIDTH), lambda b: (0, b, 0, 0))
    return pl.pallas_call(
        _mem_kv_kernel,
        grid=(B,),
        in_specs=[pl.BlockSpec((1, n_mem, D), lambda b: (b, 0, 0)), _const_spec((layers, 1, D)),
                  _const_spec(w_mem_kvs.shape)],
        out_specs=[spec, spec],
        out_shape=[out, out],
        compiler_params=_params("parallel"),
        name="mem_kv",
    )(mem, mem_gs.reshape(layers, 1, D), w_mem_kvs.astype(BF16))


A_CQ = (0, Q_LORA_RANK)
A_CKV = (A_CQ[1], A_CQ[1] + KV_LORA_RANK)
A_QMEM = (A_CKV[1], A_CKV[1] + MEM_WIDTH)
A_GATE = (A_QMEM[1], A_QMEM[1] + 1024)
A_KROPE = (A_GATE[1], A_GATE[1] + 2 * QK_ROPE_DIM)
Q_ROPE_OFF = MLA_HEADS * QK_NOPE_DIM


def _mla_in_kernel(x_ref, pos_ref, invf_ref, pre_g_ref, w_in_ref, qa_g_ref, w_uq_ref, kva_g_ref,
                   w_ukv_ref, q_ref, k_ref, v_ref, qmem_ref, gate_ref):
    q_scale = (QK_NOPE_DIM + QK_ROPE_DIM) ** -0.5 * LOG2E
    tm = x_ref.shape[1]
    sub = min(IN_SUB_ROWS, tm)
    blocks = [slice(i * sub, (i + 1) * sub) for i in range(tm // sub)]
    projected = [_dot(_rms(x_ref[0, r, :], pre_g_ref[...]).astype(BF16), w_in_ref[...]) for r in blocks]
    cs_tile = _rope_table(pos_ref, invf_ref)
    pad_row = lax.broadcasted_iota(jnp.int32, (V_EXT_DIM - V_HEAD_DIM, sub), 0)
    ones_row = jnp.where(pad_row == 0, 1.0, 0.0).astype(BF16)
    for r, p in zip(blocks, projected):
        cs = cs_tile[r, :]
        qmem_ref[0, r, :] = p[:, A_QMEM[0]:A_QMEM[1]].astype(BF16)
        gate = p[:, A_GATE[0]:A_GATE[1]]
        gate_ref[0, r, :] = (gate * jax.nn.sigmoid(gate)).astype(BF16)

        c_q = _rms(p[:, A_CQ[0]:A_CQ[1]], qa_g_ref[...]).astype(BF16)
        q = _dot(c_q, w_uq_ref[...])
        c_kv = _rms(p[:, A_CKV[0]:A_CKV[1]], kva_g_ref[...]).astype(BF16)
        kv = _dot(c_kv, w_ukv_ref[...])

        kr = p[:, A_KROPE[0]:A_KROPE[1]] * cs
        k_rot = (kr + pltpu.roll(kr, QK_ROPE_DIM, 1)).astype(BF16)
        for hd in range(MLA_HEADS):
            nope = slice(hd * QK_NOPE_DIM, (hd + 1) * QK_NOPE_DIM)
            rope = slice(Q_ROPE_OFF + hd * LANES, Q_ROPE_OFF + (hd + 1) * LANES)
            q_ref[0, hd, 0:LANES, r] = (q[:, nope] * q_scale).T.astype(BF16)
            q_ref[0, hd, LANES:2 * LANES, r] = (q[:, rope] * cs * q_scale).T.astype(BF16)
            k_ref[0, hd, r, 0:LANES] = kv[:, 2 * hd * LANES:(2 * hd + 1) * LANES].astype(BF16)
            k_ref[0, hd, r, LANES:2 * LANES] = k_rot
            v_ref[0, hd, 0:V_HEAD_DIM, r] = kv[:, (2 * hd + 1) * LANES:(2 * hd + 2) * LANES].T.astype(BF16)
            v_ref[0, hd, V_HEAD_DIM:V_EXT_DIM, r] = ones_row


def _rope_cols(w, start):
    half = QK_ROPE_DIM // 2
    return [w[:, start:start + QK_ROPE_DIM], w[:, start + half:start + QK_ROPE_DIM], w[:, start:start + half]]


def _mla_in(x, positions, pre_g, w_in, q_a_g, w_uq, kv_a_g, w_ukv):
    B, S, D = x.shape
    tm = min(MLA_IN_ROW_TILE, S)
    inv_freq = ROPE_THETA ** (-jnp.arange(0, QK_ROPE_DIM, 2, dtype=F32) / QK_ROPE_DIM)
    invf = inv_freq.reshape(QK_ROPE_DIM // 2, 1)
    o_kr = Q_LORA_RANK + KV_LORA_RANK
    o_qm = o_kr + QK_ROPE_DIM
    w_in_p = jnp.concatenate([w_in[:, :o_kr], w_in[:, o_qm:]] + _rope_cols(w_in, o_kr), axis=1).astype(BF16)
    head_w = QK_NOPE_DIM + QK_ROPE_DIM
    uq_cols = [w_uq[:, h * head_w:h * head_w + QK_NOPE_DIM] for h in range(MLA_HEADS)]
    for h in range(MLA_HEADS):
        uq_cols += _rope_cols(w_uq, h * head_w + QK_NOPE_DIM)
    w_uq_p = jnp.concatenate(uq_cols, axis=1).astype(BF16)
    w_ukv_p = w_ukv.astype(BF16)

    row = lambda b, i: (b, i, 0)
    head_row = lambda b, i: (b, 0, i, 0)
    head_col = lambda b, i: (b, 0, 0, i)
    out_shape = [
        jax.ShapeDtypeStruct((B, MLA_HEADS, 2 * LANES, S), BF16),
        jax.ShapeDtypeStruct((B, MLA_HEADS, S, 2 * LANES), BF16),
        jax.ShapeDtypeStruct((B, MLA_HEADS, V_EXT_DIM, S), BF16),
        jax.ShapeDtypeStruct((B, S, MEM_WIDTH), BF16),
        jax.ShapeDtypeStruct((B, S, 1024), BF16),
    ]
    out_specs = [
        pl.BlockSpec((1, MLA_HEADS, 2 * LANES, tm), head_col),
        pl.BlockSpec((1, MLA_HEADS, tm, 2 * LANES), head_row),
        pl.BlockSpec((1, MLA_HEADS, V_EXT_DIM, tm), head_col),
        pl.BlockSpec((1, tm, MEM_WIDTH), row),
        pl.BlockSpec((1, tm, 1024), row),
    ]
    return pl.pallas_call(
        _mla_in_kernel,
        grid=(B, S // tm),
        in_specs=[pl.BlockSpec((1, tm, D), row),
                  pl.BlockSpec((1, 1, tm), lambda b, i: (b, 0, i)), _const_spec(invf.shape),
                  _const_spec((1, D)), _const_spec(w_in_p.shape),
                  _const_spec((1, Q_LORA_RANK)), _const_spec(w_uq_p.shape),
                  _const_spec((1, KV_LORA_RANK)), _const_spec(w_ukv_p.shape)],
        out_specs=out_specs,
        out_shape=out_shape,
        compiler_params=_params("parallel", "parallel"),
        name="mla_in",
    )(x, positions.reshape(B, 1, S), invf, pre_g.reshape(1, D), w_in_p, q_a_g.reshape(1, -1), w_uq_p, kv_a_g.reshape(1, -1),
      w_ukv_p)


def _attn_kernel(qt_ref, k_ref, vt_ref, o_ref, *, tq, tk, cw):
    heads = k_ref.shape[1]
    S = k_ref.shape[2]
    dve = vt_ref.shape[2]
    dv = V_HEAD_DIM
    nc = tq // cw
    causal = (lax.broadcasted_iota(jnp.int32, (cw, cw), 0) <= lax.broadcasted_iota(jnp.int32, (cw, cw), 1))

    units = []
    for j in range(S // tq):
        for k0 in range(0, (j + 1) * tq, tk):
            for c in range(nc):
                q_start = j * tq + c * cw
                if k0 < q_start + cw:
                    units += [(j, h, c, k0, tk, k0 >= q_start) for h in range(heads)]

    def score(unit):
        j, h, c, k0, n, diagonal = unit
        s = _dot(k_ref[0, h, pl.ds(k0, n), :], qt_ref[0, h, :, pl.ds(j * tq + c * cw, cw)])
        if diagonal:
            s = jnp.where(causal[k0 - (j * tq + c * cw):k0 - (j * tq + c * cw) + tk, :], s, -jnp.inf)
        return s

    scores, carries = {}, {}
    for i in range(len(units) + ATTN_LOOKAHEAD):
        if i < len(units):
            scores[i] = score(units[i])
        if i < ATTN_LOOKAHEAD:
            continue
        j, h, c, k0, n, diagonal = units[i - ATTN_LOOKAHEAD]
        m, acc = carries.pop((j, h, c), (jnp.full((1, cw), -jnp.inf, F32), jnp.zeros((dve, cw), F32)))
        s = scores.pop(i - ATTN_LOOKAHEAD)
        m_new = jnp.maximum(m, jnp.max(s, axis=0, keepdims=True))
        p = jnp.exp2(s - m_new).astype(BF16)
        acc = jnp.exp2(m - m_new) * acc + _dot(vt_ref[0, h, :, pl.ds(k0, n)], p)
        if k0 + n == j * tq + (c + 1) * cw:
            out = acc[0:dv, :] / acc[dv:dv + 1, :]
            o_ref[0, pl.ds(j * tq + c * cw, cw), h * dv:(h + 1) * dv] = out.T.astype(o_ref.dtype)
        else:
            carries[j, h, c] = (m_new, acc)


def _attention(qt, k, vt):
    B, H, S, dqk = k.shape
    dve = vt.shape[2]
    tq = min(ATTN_TQ, S)
    tk = min(ATTN_TK, tq)
    cw = min(ATTN_CW, tq)
    hp = ATTN_HEADS_PER_STEP
    head = lambda b, h: (b, h, 0, 0)
    return pl.pallas_call(
        functools.partial(_attn_kernel, tq=tq, tk=tk, cw=cw),
        grid=(B, H // hp),
        in_specs=[pl.BlockSpec((1, hp, dqk, S), head), pl.BlockSpec((1, hp, S, dqk), head),
                  pl.BlockSpec((1, hp, dve, S), head)],
        out_specs=pl.BlockSpec((1, S, hp * V_HEAD_DIM), lambda b, h: (b, 0, h)),
        out_shape=jax.ShapeDtypeStruct((B, S, H * V_HEAD_DIM), BF16),
        compiler_params=_params("parallel", "parallel"),
        name="mla_attention",
    )(qt, k, vt)


def _memory_probs(s, n_mem):
    probs = []
    for h in range(MEM_HEADS):
        sh = s[:, h * n_mem:(h + 1) * n_mem]
        e = jnp.exp2(sh - jnp.max(sh, axis=-1, keepdims=True))
        probs.append((e / jnp.sum(e, axis=-1, keepdims=True)).astype(BF16))
    return jnp.concatenate(probs, axis=-1)


def _tail(gated_mix, x_ref, qmem_ref, gate_ref, kexp_ref, vexp_ref, w_out_ref, post_g_ref, o_ref):
    tm = x_ref.shape[1]
    sub = min(TAIL_SUB_ROWS, tm)
    blocks = [slice(i * sub, (i + 1) * sub) for i in range(tm // sub)]
    n_mem = kexp_ref.shape[2] // MEM_HEADS
    kexp = kexp_ref[0, 0]
    scores = [_dot_nt(qmem_ref[0, r, :], kexp) for r in blocks]
    for r, s in zip(blocks, scores):
        mo = _dot(_memory_probs(s, n_mem), vexp_ref[0, 0])
        y_mix = gated_mix(r, gate_ref[0, r, 0:MIX_WIDTH])
        y_mem = (mo * gate_ref[0, r, MIX_WIDTH:].astype(F32)).astype(BF16)
        y = _dot(jnp.concatenate([y_mix, y_mem], axis=-1), w_out_ref[...])
        o_ref[0, r, :] = x_ref[0, r, :] + _rms(y, post_g_ref[...])


def _mla_out_kernel(x_ref, mix_ref, qmem_ref, gate_ref, kexp_ref, vexp_ref, w_out_ref, post_g_ref, o_ref):
    gated_mix = lambda r, gate: mix_ref[0, r, :] * gate
    _tail(gated_mix, x_ref, qmem_ref, gate_ref, kexp_ref, vexp_ref, w_out_ref, post_g_ref, o_ref)


def _mlstm_out_kernel(x_ref, hn_ref, og_ref, uc_ref, skip_ref, qmem_ref, gate_ref, kexp_ref, vexp_ref,
                      w_out_ref, post_g_ref, o_ref):
    def gated_mix(r, gate):
        mix = og_ref[0, r, :].astype(F32) * hn_ref[0, r, :].astype(F32) + skip_ref[...] * uc_ref[0, r, :].astype(F32)
        return (mix * gate.astype(F32)).astype(BF16)

    _tail(gated_mix, x_ref, qmem_ref, gate_ref, kexp_ref, vexp_ref, w_out_ref, post_g_ref, o_ref)


def _layer_out(kernel_fn, name, x, mixer_inputs, mixer_specs, qmem, gate, mem_kv, layer, w_out, post_g):
    B, S, D = x.shape
    tm = min(TAIL_ROW_TILE, S)
    kexp, vexp = mem_kv
    row = lambda b, i: (b, i, 0)
    per_batch = lambda b, i: (layer, b, 0, 0)
    in_specs = ([pl.BlockSpec((1, tm, D), row)] + mixer_specs(tm) +
                [pl.BlockSpec((1, tm, MEM_WIDTH), row), pl.BlockSpec((1, tm, gate.shape[-1]), row),
                 pl.BlockSpec((1, 1) + kexp.shape[2:], per_batch), pl.BlockSpec((1, 1) + vexp.shape[2:], per_batch),
                 _const_spec(w_out.shape), _const_spec((1, D))])
    return pl.pallas_call(
        kernel_fn,
        grid=(B, S // tm),
        in_specs=in_specs,
        out_specs=pl.BlockSpec((1, tm, D), row),
        out_shape=jax.ShapeDtypeStruct((B, S, D), F32),
        compiler_params=_params("parallel", "parallel"),
        name=name,
    )(x, *mixer_inputs, qmem, gate, kexp, vexp, w_out.astype(BF16), post_g.reshape(1, D))


B_GATES = (MIX_WIDTH, MIX_WIDTH + LANES)
B_O = (0, MIX_WIDTH)
B_QMEM = (B_O[1], B_O[1] + MEM_WIDTH)
B_GATE = (B_QMEM[1], B_QMEM[1] + 1024)
CONV_HALO = 8
V_ONES_ROWS = 16


def _mlstm_chunk(carry, qts, ks, vts, gcol, grow, causal, between):
    H = len(qts)
    L = ks[0].shape[0]
    qk = [_dot(ks[h], qts[h]) for h in range(H)]
    cq = [_dot(carry[h][0].astype(BF16), qts[h]) for h in range(H)]
    between()
    out, normed = [], []
    for h in range(H):
        C, m = carry[h]
        c_col = gcol[:, h:h + 1]
        li_row = grow[h:h + 1, :]
        b_row = grow[MLSTM_HEADS + h:MLSTM_HEADS + h + 1, :]
        d = jnp.where(causal, c_col + b_row, -jnp.inf)
        inter = b_row + m
        m_t = jnp.maximum(inter, jnp.max(d, axis=0, keepdims=True))
        sqk = (qk[h] * jnp.exp(d - m_t)).astype(BF16)
        num = jnp.exp(inter - m_t) * cq[h] + _dot(vts[h], sqk)
        den = jnp.maximum(jnp.abs(num[MLSTM_V_DIM:MLSTM_V_DIM + 1, :]), jnp.exp(-m_t))
        hnum = num[0:MLSTM_V_DIM, :]
        inv_den = 1.0 / den
        ms = jnp.sum(hnum * hnum, axis=0, keepdims=True) * (inv_den * inv_den * (1.0 / MLSTM_V_DIM))
        normed.append(hnum * (inv_den * lax.rsqrt(ms + EPS)))

        b_last = b_row[:, L - 1:L]
        dec = b_last - b_row + li_row
        m_new = jnp.maximum(b_last + m, jnp.max(dec, axis=1, keepdims=True))
        vw = (vts[h].astype(F32) * jnp.exp(dec - m_new)).astype(BF16)
        out.append((jnp.exp(b_last + m - m_new) * C + _dot(vw, ks[h]), m_new))
    return tuple(out), jnp.concatenate(normed, axis=0).T


def _mlstm_mixer_kernel(x_ref, pre_g_ref, w_first_ref, w_rest_ref, conv_w_ref, conv_b_ref, bias_ref, hg_ref,
                        wqk_ref, wv_ref, og_ref, qmem_ref, gate_ref, uc_ref, hn_ref, tail_sc, c_sc, m_sc, *, chunk):
    tm = x_ref.shape[1]
    blocks = [slice(j * chunk, (j + 1) * chunk) for j in range(tm // chunk)]

    @pl.when(pl.program_id(1) == 0)
    def _():
        tail_sc[...] = jnp.zeros(tail_sc.shape, F32)
        c_sc[...] = jnp.zeros(c_sc.shape, F32)
        m_sc[...] = jnp.zeros(m_sc.shape, F32)

    hs = [_rms(x_ref[0, r, :], pre_g_ref[...]).astype(BF16) for r in blocks]
    firsts = [_dot(h, w_first_ref[...]) for h in hs]
    halos = [tail_sc[...]] + [f[chunk - CONV_HALO:chunk, 0:MIX_WIDTH] for f in firsts[:-1]]
    tail_sc[...] = firsts[-1][chunk - CONV_HALO:chunk, 0:MIX_WIDTH]

    k_scale = MLSTM_QK_DIM ** -0.5
    ones_rows = jnp.where(lax.broadcasted_iota(jnp.int32, (V_ONES_ROWS, chunk), 0) == 0, 1.0, 0.0)
    lane = lax.broadcasted_iota(jnp.int32, (chunk, LANES), 1)
    is_f = (lane >= MLSTM_HEADS) & (lane < 2 * MLSTM_HEADS)
    tril = (lax.broadcasted_iota(jnp.int32, (chunk, chunk), 1)
            <= lax.broadcasted_iota(jnp.int32, (chunk, chunk), 0)).astype(BF16)
    causal = (lax.broadcasted_iota(jnp.int32, (chunk, chunk), 0)
              <= lax.broadcasted_iota(jnp.int32, (chunk, chunk), 1))
    carry = tuple((c_sc[h], m_sc[h, 0:1, 0:1]) for h in range(MLSTM_HEADS))

    for r, h, first, halo in zip(blocks, hs, firsts, halos):
        u = first[:, 0:MIX_WIDTH]
        ext = jnp.concatenate([halo, u], axis=0)
        conv = conv_b_ref[...] + u * conv_w_ref[CONV_WIDTH - 1:CONV_WIDTH, :]
        for back in range(1, CONV_WIDTH):
            tap = CONV_WIDTH - 1 - back
            conv = conv + ext[CONV_HALO - back:CONV_HALO - back + chunk, :] * conv_w_ref[tap:tap + 1, :]
        uc_bf = (conv * jax.nn.sigmoid(conv)).astype(BF16)
        uc_ref[0, r, :] = uc_bf
        u_bf = u.astype(BF16)

        g = first[:, B_GATES[0]:B_GATES[1]] + bias_ref[...]
        log_f = jnp.minimum(g, 0.0) - jnp.log1p(jnp.exp(-jnp.abs(g)))
        gates = jnp.where(lane < MLSTM_HEADS, g, jnp.where(is_f, log_f, 0.0))
        gc = jnp.where(is_f, sum(_dot(tril, term) for term in _split3(gates)), gates)
        grow = gc.T[0:2 * MLSTM_HEADS, :]
        gcol = gc - pltpu.roll(gc, LANES - MLSTM_HEADS, 1)

        qts, ks, vts = [], [], []
        for hd in range(MLSTM_HEADS):
            cols = slice(MLSTM_HEAD_START[hd], MLSTM_HEAD_START[hd] + MXU_DIM)
            qk = _dot(uc_bf[:, cols], wqk_ref[hd])
            qts.append(qk[:, :MLSTM_PAD_QK].T.astype(BF16))
            ks.append((qk[:, MLSTM_PAD_QK:] * k_scale).astype(BF16))
            vt = _dot(u_bf[:, cols], wv_ref[hd]).T
            vts.append(jnp.concatenate([vt[0:MLSTM_V_DIM, :], ones_rows, vt[MLSTM_V_DIM + V_ONES_ROWS:, :]],
                                       axis=0).astype(BF16))

        def rest_projection(r=r, h=h):
            p = _dot(h, w_rest_ref[...])
            og_ref[0, r, :] = (jax.nn.sigmoid(p[:, B_O[0]:B_O[1]]) * hg_ref[...]).astype(BF16)
            qmem_ref[0, r, :] = p[:, B_QMEM[0]:B_QMEM[1]].astype(BF16)
            gate = p[:, B_GATE[0]:B_GATE[1]]
            gate_ref[0, r, :] = (gate * jax.nn.sigmoid(gate)).astype(BF16)

        carry, hn = _mlstm_chunk(carry, qts, ks, vts, gcol, grow, causal, rest_projection)
        hn_ref[0, r, :] = hn.astype(BF16)

    for hd in range(MLSTM_HEADS):
        c_sc[hd] = carry[hd][0]
        m_sc[hd] = jnp.broadcast_to(carry[hd][1], m_sc.shape[1:])


def _mlstm_mixer(x, pre_g, w_in, gate_bias, conv_w, conv_b, w_q, w_k, w_v, head_g):
    B, S, D = x.shape
    tm = min(MIXER_ROW_TILE, S)
    chunk = min(MLSTM_CHUNK, S)
    H = MLSTM_HEADS
    n_if = 2 * H
    o_if = MIX_WIDTH
    o_o = o_if + n_if
    w_first = jnp.concatenate([w_in[:, :o_o], jnp.zeros((D, LANES - n_if), w_in.dtype)], axis=1).astype(BF16)
    w_rest = w_in[:, o_o:].astype(BF16)
    def placed(w, width):
        blocks = []
        for h in range(H):
            off = h * MLSTM_V_DIM - MLSTM_HEAD_START[h]
            blocks.append(jnp.pad(w[h], ((off, MXU_DIM - MLSTM_V_DIM - off), (0, width - w.shape[-1]))))
        return jnp.stack(blocks)

    wqk = jnp.concatenate([placed(w_q, MLSTM_PAD_QK), placed(w_k, MLSTM_PAD_QK)], axis=-1)
    wv = placed(w_v, MLSTM_PAD_V)
    bias = jnp.pad(gate_bias, (0, LANES - n_if)).reshape(1, LANES)
    row = lambda b, i: (b, i, 0)
    widths = [MIX_WIDTH, MEM_WIDTH, 1024, MIX_WIDTH, MIX_WIDTH]
    return pl.pallas_call(
        functools.partial(_mlstm_mixer_kernel, chunk=chunk),
        grid=(B, S // tm),
        in_specs=[pl.BlockSpec((1, tm, D), row), _const_spec((1, D)), _const_spec(w_first.shape),
                  _const_spec(w_rest.shape), _const_spec((CONV_WIDTH, MIX_WIDTH)), _const_spec((1, MIX_WIDTH)),
                  _const_spec((1, LANES)), _const_spec((1, MIX_WIDTH)), _const_spec(wqk.shape),
                  _const_spec(wv.shape)],
        out_specs=[pl.BlockSpec((1, tm, w), row) for w in widths],
        out_shape=[jax.ShapeDtypeStruct((B, S, w), BF16) for w in widths],
        scratch_shapes=[pltpu.VMEM((CONV_HALO, MIX_WIDTH), F32),
                        pltpu.VMEM((H, MLSTM_PAD_V, MLSTM_PAD_QK), F32), pltpu.VMEM((H, 8, LANES), F32)],
        compiler_params=_params("parallel", "arbitrary"),
        name="mlstm_mixer",
    )(x, pre_g.reshape(1, D), w_first, w_rest, conv_w, conv_b.reshape(1, -1), bias, head_g.reshape(1, MIX_WIDTH),
      wqk.astype(BF16), wv.astype(BF16))


def _mla_layer(x, mem_kv, layer, positions, pre_g, w_in, q_a_g, w_uq, kv_a_g, w_ukv, w_out, post_g):
    q, k, v, qmem, gate = _mla_in(x, positions, pre_g, w_in, q_a_g, w_uq, kv_a_g, w_ukv)
    mix = _attention(q, k, v)
    specs = lambda tm: [pl.BlockSpec((1, tm, MIX_WIDTH), lambda b, i: (b, i, 0))]
    return _layer_out(_mla_out_kernel, "mla_out", x, [mix], specs, qmem, gate, mem_kv, layer, w_out, post_g)


def _mlstm_layer(x, mem_kv, layer, pre_g, w_in, gate_bias, conv_w, conv_b, w_q, w_k, w_v, head_g, skip,
                 w_out, post_g):
    og, qmem, gate, uc, hn = _mlstm_mixer(x, pre_g, w_in, gate_bias, conv_w, conv_b, w_q, w_k, w_v, head_g)
    row = lambda b, i: (b, i, 0)
    specs = lambda tm: [pl.BlockSpec((1, tm, MIX_WIDTH), row),
                        pl.BlockSpec((1, tm, MIX_WIDTH), row), pl.BlockSpec((1, tm, MIX_WIDTH), row),
                        _const_spec((1, MIX_WIDTH))]
    return _layer_out(_mlstm_out_kernel, "mlstm_out", x, [hn, og, uc, skip.reshape(1, -1)], specs,
                      qmem, gate, mem_kv, layer, w_out, post_g)


def kernel(x, mem, positions, a_pre_g, a_w_in, a_q_a_g, a_w_uq, a_kv_a_g, a_w_ukv, a_mem_g, a_w_mem_kv, a_w_out, a_post_g, b_pre_g, b_w_in, b_gate_bias, b_conv_w, b_conv_b, b_w_q, b_w_k, b_w_v, b_head_g, b_skip, b_mem_g, b_w_mem_kv, b_w_out, b_post_g):
    depth = a_pre_g.shape[0] + b_pre_g.shape[0]
    mem_gs = jnp.stack([(a_mem_g, b_mem_g)[i % 2][i // 2] for i in range(depth)])
    w_mem_kvs = jnp.stack([(a_w_mem_kv, b_w_mem_kv)[i % 2][i // 2] for i in range(depth)])
    mem_kv = _mem_kv(mem, mem_gs, w_mem_kvs)
    for i in range(depth):
        j = i // 2
        if i % 2 == 0:
            x = _mla_layer(x, mem_kv, i, positions, a_pre_g[j], a_w_in[j], a_q_a_g[j], a_w_uq[j], a_kv_a_g[j],
                           a_w_ukv[j], a_w_out[j], a_post_g[j])
        else:
            x = _mlstm_layer(x, mem_kv, i, b_pre_g[j], b_w_in[j], b_gate_bias[j], b_conv_w[j], b_conv_b[j],
                             b_w_q[j], b_w_k[j], b_w_v[j], b_head_g[j], b_skip[j], b_w_out[j], b_post_g[j])
    return x
```

```python
import functools

import jax
import jax.numpy as jnp
from jax import lax
from jax.experimental import pallas as pl
from jax.experimental.pallas import tpu as pltpu

EPS = 1e-6
ROPE_THETA = 10000.0
MEM_HEADS = 4
MEM_HEAD_DIM = 64
MEM_WIDTH = MEM_HEADS * MEM_HEAD_DIM
QK_NOPE_DIM = 128
QK_ROPE_DIM = 64
V_HEAD_DIM = 128
MLA_HEADS = 6
V_EXT_DIM = V_HEAD_DIM + 16
Q_LORA_RANK = 384
KV_LORA_RANK = 256
MLSTM_HEADS = 4
MLSTM_V_DIM = 192
MLSTM_QK_DIM = 96
CONV_WIDTH = 4
MIX_WIDTH = 768

LANES = 128
MXU_DIM = 256
VMEM_LIMIT_BYTES = 56 * 1024 * 1024

MLA_IN_ROW_TILE = 1024
MIXER_ROW_TILE = 1024
IN_SUB_ROWS = 256
TAIL_ROW_TILE = 1024
TAIL_SUB_ROWS = 256
ATTN_TQ = 1024
ATTN_CW = 256
ATTN_TK = 256
ATTN_HEADS_PER_STEP = 2
ATTN_LOOKAHEAD = 10
MLSTM_CHUNK = 256

LOG2E = 1.4426950408889634
MLSTM_PAD_QK = LANES
MLSTM_PAD_V = MXU_DIM
MLSTM_HEAD_START = tuple((h * MLSTM_V_DIM // LANES) * LANES for h in range(MLSTM_HEADS))

F32 = jnp.float32
BF16 = jnp.bfloat16


def _dot(a, b):
    return jnp.dot(a, b, preferred_element_type=F32)


def _dot_nt(a, b):
    return lax.dot_general(a, b, (((1,), (1,)), ((), ())), preferred_element_type=F32)


def _rms(x, g, width=None):
    width = x.shape[-1] if width is None else width
    ms = jnp.sum(x * x, axis=-1, keepdims=True) * (1.0 / width)
    return x * lax.rsqrt(ms + EPS) * g


def _split3(x):
    b1 = x.astype(BF16)
    r1 = x - b1.astype(F32)
    b2 = r1.astype(BF16)
    b3 = (r1 - b2.astype(F32)).astype(BF16)
    return b1, b2, b3


def _params(*semantics):
    return pltpu.CompilerParams(dimension_semantics=semantics, vmem_limit_bytes=VMEM_LIMIT_BYTES)


def _const_spec(shape):
    zeros = (0,) * len(shape)
    return pl.BlockSpec(shape, lambda *_: zeros)


def _rope_table(pos_ref, invf_ref):
    ang = invf_ref[...] * pos_ref[0].astype(F32)
    cos_t = jnp.cos(ang)
    sin_t = jnp.sin(ang)
    return jnp.concatenate([cos_t, cos_t, -sin_t, sin_t], axis=0).T


def _mem_kv_kernel(mem_ref, g_ref, w_ref, k_ref, v_ref):
    n_mem = mem_ref.shape[1]
    mem = mem_ref[0]
    col_head = lax.broadcasted_iota(jnp.int32, (n_mem, MEM_WIDTH), 1) // MEM_HEAD_DIM
    for layer in range(w_ref.shape[0]):
        hn = _rms(mem, g_ref[layer]).astype(BF16)
        kv = _dot(hn, w_ref[layer])
        k = kv[:, :MEM_WIDTH] * (MEM_HEAD_DIM ** -0.5 * LOG2E)
        v = kv[:, MEM_WIDTH:]
        for h in range(MEM_HEADS):
            rows = pl.ds(h * n_mem, n_mem)
            k_ref[layer, 0, rows, :] = jnp.where(col_head == h, k, 0.0).astype(BF16)
            v_ref[layer, 0, rows, :] = jnp.where(col_head == h, v, 0.0).astype(BF16)


def _mem_kv(mem, mem_gs, w_mem_kvs):
    B, n_mem, D = mem.shape
    layers = w_mem_kvs.shape[0]
    out = jax.ShapeDtypeStruct((layers, B, MEM_HEADS * n_mem, MEM_WIDTH), BF16)
    spec = pl.BlockSpec((layers, 1, MEM_HEADS * n_mem, MEM_WIDTH), lambda b: (0, b, 0, 0))
    return pl.pallas_call(
        _mem_kv_kernel,
        grid=(B,),
        in_specs=[pl.BlockSpec((1, n_mem, D), lambda b: (b, 0, 0)), _const_spec((layers, 1, D)),
                  _const_spec(w_mem_kvs.shape)],
        out_specs=[spec, spec],
        out_shape=[out, out],
        compiler_params=_params("parallel"),
        name="mem_kv",
    )(mem, mem_gs.reshape(layers, 1, D), w_mem_kvs.astype(BF16))


A_CQ = (0, Q_LORA_RANK)
A_CKV = (A_CQ[1], A_CQ[1] + KV_LORA_RANK)
A_QMEM = (A_CKV[1], A_CKV[1] + MEM_WIDTH)
A_GATE = (A_QMEM[1], A_QMEM[1] + 1024)
A_KROPE = (A_GATE[1], A_GATE[1] + 2 * QK_ROPE_DIM)
Q_ROPE_OFF = MLA_HEADS * QK_NOPE_DIM


def _mla_in_kernel(x_ref, pos_ref, invf_ref, pre_g_ref, w_in_ref, qa_g_ref, w_uq_ref, kva_g_ref,
                   w_ukv_ref, q_ref, k_ref, v_ref, qmem_ref, gate_ref):
    q_scale = (QK_NOPE_DIM + QK_ROPE_DIM) ** -0.5 * LOG2E
    tm = x_ref.shape[1]
    sub = min(IN_SUB_ROWS, tm)
    blocks = [slice(i * sub, (i + 1) * sub) for i in range(tm // sub)]
    projected = [_dot(_rms(x_ref[0, r, :], pre_g_ref[...]).astype(BF16), w_in_ref[...]) for r in blocks]
    cs_tile = _rope_table(pos_ref, invf_ref)
    pad_row = lax.broadcasted_iota(jnp.int32, (V_EXT_DIM - V_HEAD_DIM, sub), 0)
    ones_row = jnp.where(pad_row == 0, 1.0, 0.0).astype(BF16)
    for r, p in zip(blocks, projected):
        cs = cs_tile[r, :]
        qmem_ref[0, r, :] = p[:, A_QMEM[0]:A_QMEM[1]].astype(BF16)
        gate = p[:, A_GATE[0]:A_GATE[1]]
        gate_ref[0, r, :] = (gate * jax.nn.sigmoid(gate)).astype(BF16)

        c_q = _rms(p[:, A_CQ[0]:A_CQ[1]], qa_g_ref[...]).astype(BF16)
        q = _dot(c_q, w_uq_ref[...])
        c_kv = _rms(p[:, A_CKV[0]:A_CKV[1]], kva_g_ref[...]).astype(BF16)
        kv = _dot(c_kv, w_ukv_ref[...])

        kr = p[:, A_KROPE[0]:A_KROPE[1]] * cs
        k_rot = (kr + pltpu.roll(kr, QK_ROPE_DIM, 1)).astype(BF16)
        for hd in range(MLA_HEADS):
            nope = slice(hd * QK_NOPE_DIM, (hd + 1) * QK_NOPE_DIM)
            rope = slice(Q_ROPE_OFF + hd * LANES, Q_ROPE_OFF + (hd + 1) * LANES)
            q_ref[0, hd, 0:LANES, r] = (q[:, nope] * q_scale).T.astype(BF16)
            q_ref[0, hd, LANES:2 * LANES, r] = (q[:, rope] * cs * q_scale).T.astype(BF16)
            k_ref[0, hd, r, 0:LANES] = kv[:, 2 * hd * LANES:(2 * hd + 1) * LANES].astype(BF16)
            k_ref[0, hd, r, LANES:2 * LANES] = k_rot
            v_ref[0, hd, 0:V_HEAD_DIM, r] = kv[:, (2 * hd + 1) * LANES:(2 * hd + 2) * LANES].T.astype(BF16)
            v_ref[0, hd, V_HEAD_DIM:V_EXT_DIM, r] = ones_row


def _rope_cols(w, start):
    half = QK_ROPE_DIM // 2
    return [w[:, start:start + QK_ROPE_DIM], w[:, start + half:start + QK_ROPE_DIM], w[:, start:start + half]]


def _mla_in(x, positions, pre_g, w_in, q_a_g, w_uq, kv_a_g, w_ukv):
    B, S, D = x.shape
    tm = min(MLA_IN_ROW_TILE, S)
    inv_freq = ROPE_THETA ** (-jnp.arange(0, QK_ROPE_DIM, 2, dtype=F32) / QK_ROPE_DIM)
    invf = inv_freq.reshape(QK_ROPE_DIM // 2, 1)
    o_kr = Q_LORA_RANK + KV_LORA_RANK
    o_qm = o_kr + QK_ROPE_DIM
    w_in_p = jnp.concatenate([w_in[:, :o_kr], w_in[:, o_qm:]] + _rope_cols(w_in, o_kr), axis=1).astype(BF16)
    head_w = QK_NOPE_DIM + QK_ROPE_DIM
    uq_cols = [w_uq[:, h * head_w:h * head_w + QK_NOPE_DIM] for h in range(MLA_HEADS)]
    for h in range(MLA_HEADS):
        uq_cols += _rope_cols(w_uq, h * head_w + QK_NOPE_DIM)
    w_uq_p = jnp.concatenate(uq_cols, axis=1).astype(BF16)
    w_ukv_p = w_ukv.astype(BF16)

    row = lambda b, i: (b, i, 0)
    head_row = lambda b, i: (b, 0, i, 0)
    head_col = lambda b, i: (b, 0, 0, i)
    out_shape = [
        jax.ShapeDtypeStruct((B, MLA_HEADS, 2 * LANES, S), BF16),
        jax.ShapeDtypeStruct((B, MLA_HEADS, S, 2 * LANES), BF16),
        jax.ShapeDtypeStruct((B, MLA_HEADS, V_EXT_DIM, S), BF16),
        jax.ShapeDtypeStruct((B, S, MEM_WIDTH), BF16),
        jax.ShapeDtypeStruct((B, S, 1024), BF16),
    ]
    out_specs = [
        pl.BlockSpec((1, MLA_HEADS, 2 * LANES, tm), head_col),
        pl.BlockSpec((1, MLA_HEADS, tm, 2 * LANES), head_row),
        pl.BlockSpec((1, MLA_HEADS, V_EXT_DIM, tm), head_col),
        pl.BlockSpec((1, tm, MEM_WIDTH), row),
        pl.BlockSpec((1, tm, 1024), row),
    ]
    return pl.pallas_call(
        _mla_in_kernel,
        grid=(B, S // tm),
        in_specs=[pl.BlockSpec((1, tm, D), row),
                  pl.BlockSpec((1, 1, tm), lambda b, i: (b, 0, i)), _const_spec(invf.shape),
                  _const_spec((1, D)), _const_spec(w_in_p.shape),
                  _const_spec((1, Q_LORA_RANK)), _const_spec(w_uq_p.shape),
                  _const_spec((1, KV_LORA_RANK)), _const_spec(w_ukv_p.shape)],
        out_specs=out_specs,
        out_shape=out_shape,
        compiler_params=_params("parallel", "parallel"),
        name="mla_in",
    )(x, positions.reshape(B, 1, S), invf, pre_g.reshape(1, D), w_in_p, q_a_g.reshape(1, -1), w_uq_p, kv_a_g.reshape(1, -1),
      w_ukv_p)


def _attn_kernel(qt_ref, k_ref, vt_ref, o_ref, *, tq, tk, cw):
    heads = k_ref.shape[1]
    S = k_ref.shape[2]
    dve = vt_ref.shape[2]
    dv = V_HEAD_DIM
    nc = tq // cw
    causal = (lax.broadcasted_iota(jnp.int32, (cw, cw), 0) <= lax.broadcasted_iota(jnp.int32, (cw, cw), 1))

    units = []
    for j in range(S // tq):
        for k0 in range(0, (j + 1) * tq, tk):
            for c in range(nc):
                q_start = j * tq + c * cw
                if k0 < q_start + cw:
                    units += [(j, h, c, k0, tk, k0 >= q_start) for h in range(heads)]

    def score(unit):
        j, h, c, k0, n, diagonal = unit
        s = _dot(k_ref[0, h, pl.ds(k0, n), :], qt_ref[0, h, :, pl.ds(j * tq + c * cw, cw)])
        if diagonal:
            s = jnp.where(causal[k0 - (j * tq + c * cw):k0 - (j * tq + c * cw) + tk, :], s, -jnp.inf)
        return s

    scores, carries = {}, {}
    for i in range(len(units) + ATTN_LOOKAHEAD):
        if i < len(units):
            scores[i] = score(units[i])
        if i < ATTN_LOOKAHEAD:
            continue
        j, h, c, k0, n, diagonal = units[i - ATTN_LOOKAHEAD]
        m, acc = carries.pop((j, h, c), (jnp.full((1, cw), -jnp.inf, F32), jnp.zeros((dve, cw), F32)))
        s = scores.pop(i - ATTN_LOOKAHEAD)
        m_new = jnp.maximum(m, jnp.max(s, axis=0, keepdims=True))
        p = jnp.exp2(s - m_new).astype(BF16)
        acc = jnp.exp2(m - m_new) * acc + _dot(vt_ref[0, h, :, pl.ds(k0, n)], p)
        if k0 + n == j * tq + (c + 1) * cw:
            out = acc[0:dv, :] / acc[dv:dv + 1, :]
            o_ref[0, pl.ds(j * tq + c * cw, cw), h * dv:(h + 1) * dv] = out.T.astype(o_ref.dtype)
        else:
            carries[j, h, c] = (m_new, acc)


def _attention(qt, k, vt):
    B, H, S, dqk = k.shape
    dve = vt.shape[2]
    tq = min(ATTN_TQ, S)
    tk = min(ATTN_TK, tq)
    cw = min(ATTN_CW, tq)
    hp = ATTN_HEADS_PER_STEP
    head = lambda b, h: (b, h, 0, 0)
    return pl.pallas_call(
        functools.partial(_attn_kernel, tq=tq, tk=tk, cw=cw),
        grid=(B, H // hp),
        in_specs=[pl.BlockSpec((1, hp, dqk, S), head), pl.BlockSpec((1, hp, S, dqk), head),
                  pl.BlockSpec((1, hp, dve, S), head)],
        out_specs=pl.BlockSpec((1, S, hp * V_HEAD_DIM), lambda b, h: (b, 0, h)),
        out_shape=jax.ShapeDtypeStruct((B, S, H * V_HEAD_DIM), BF16),
        compiler_params=_params("parallel", "parallel"),
        name="mla_attention",
    )(qt, k, vt)


def _memory_probs(s, n_mem):
    probs = []
    for h in range(MEM_HEADS):
        sh = s[:, h * n_mem:(h + 1) * n_mem]
        e = jnp.exp2(sh - jnp.max(sh, axis=-1, keepdims=True))
        probs.append((e * (1.0 / jnp.sum(e, axis=-1, keepdims=True))).astype(BF16))
    return jnp.concatenate(probs, axis=-1)


def _tail(gated_mix, x_ref, qmem_ref, gate_ref, kexp_ref, vexp_ref, w_out_ref, post_g_ref, o_ref):
    tm = x_ref.shape[1]
    sub = min(TAIL_SUB_ROWS, tm)
    blocks = [slice(i * sub, (i + 1) * sub) for i in range(tm // sub)]
    n_mem = kexp_ref.shape[2] // MEM_HEADS
    kexp = kexp_ref[0, 0]
    scores = [_dot_nt(qmem_ref[0, r, :], kexp) for r in blocks]
    for r, s in zip(blocks, scores):
        mo = _dot(_memory_probs(s, n_mem), vexp_ref[0, 0])
        y_mix = gated_mix(r, gate_ref[0, r, 0:MIX_WIDTH])
        y_mem = (mo * gate_ref[0, r, MIX_WIDTH:].astype(F32)).astype(BF16)
        y = _dot(jnp.concatenate([y_mix, y_mem], axis=-1), w_out_ref[...])
        o_ref[0, r, :] = x_ref[0, r, :] + _rms(y, post_g_ref[...])


def _mla_out_kernel(x_ref, mix_ref, qmem_ref, gate_ref, kexp_ref, vexp_ref, w_out_ref, post_g_ref, o_ref):
    gated_mix = lambda r, gate: mix_ref[0, r, :] * gate
    _tail(gated_mix, x_ref, qmem_ref, gate_ref, kexp_ref, vexp_ref, w_out_ref, post_g_ref, o_ref)


def _mlstm_out_kernel(x_ref, hn_ref, og_ref, uc_ref, skip_ref, qmem_ref, gate_ref, kexp_ref, vexp_ref,
                      w_out_ref, post_g_ref, o_ref):
    def gated_mix(r, gate):
        mix = og_ref[0, r, :].astype(F32) * hn_ref[0, r, :].astype(F32) + skip_ref[...] * uc_ref[0, r, :].astype(F32)
        return (mix * gate.astype(F32)).astype(BF16)

    _tail(gated_mix, x_ref, qmem_ref, gate_ref, kexp_ref, vexp_ref, w_out_ref, post_g_ref, o_ref)


def _layer_out(kernel_fn, name, x, mixer_inputs, mixer_specs, qmem, gate, mem_kv, layer, w_out, post_g):
    B, S, D = x.shape
    tm = min(TAIL_ROW_TILE, S)
    kexp, vexp = mem_kv
    row = lambda b, i: (b, i, 0)
    per_batch = lambda b, i: (layer, b, 0, 0)
    in_specs = ([pl.BlockSpec((1, tm, D), row)] + mixer_specs(tm) +
                [pl.BlockSpec((1, tm, MEM_WIDTH), row), pl.BlockSpec((1, tm, gate.shape[-1]), row),
                 pl.BlockSpec((1, 1) + kexp.shape[2:], per_batch), pl.BlockSpec((1, 1) + vexp.shape[2:], per_batch),
                 _const_spec(w_out.shape), _const_spec((1, D))])
    return pl.pallas_call(
        kernel_fn,
        grid=(B, S // tm),
        in_specs=in_specs,
        out_specs=pl.BlockSpec((1, tm, D), row),
        out_shape=jax.ShapeDtypeStruct((B, S, D), F32),
        compiler_params=_params("parallel", "parallel"),
        name=name,
    )(x, *mixer_inputs, qmem, gate, kexp, vexp, w_out.astype(BF16), post_g.reshape(1, D))


B_GATES = (MIX_WIDTH, MIX_WIDTH + LANES)
B_O = (0, MIX_WIDTH)
B_QMEM = (B_O[1], B_O[1] + MEM_WIDTH)
B_GATE = (B_QMEM[1], B_QMEM[1] + 1024)
CONV_HALO = 8
V_ONES_ROWS = 16


def _mlstm_chunk(carry, qts, ks, vts, gcol, grow, causal, between):
    H = len(qts)
    L = ks[0].shape[0]
    qk = [_dot(ks[h], qts[h]) for h in range(H)]
    cq = [_dot(carry[h][0].astype(BF16), qts[h]) for h in range(H)]
    between()
    out, normed = [], []
    for h in range(H):
        C, m = carry[h]
        c_col = gcol[:, h:h + 1]
        li_row = grow[h:h + 1, :]
        b_row = grow[MLSTM_HEADS + h:MLSTM_HEADS + h + 1, :]
        d = jnp.where(causal, c_col + b_row, -jnp.inf)
        inter = b_row + m
        m_t = jnp.maximum(inter, jnp.max(d, axis=0, keepdims=True))
        sqk = (qk[h] * jnp.exp(d - m_t)).astype(BF16)
        num = jnp.exp(inter - m_t) * cq[h] + _dot(vts[h], sqk)
        den = jnp.maximum(jnp.abs(num[MLSTM_V_DIM:MLSTM_V_DIM + 1, :]), jnp.exp(-m_t))
        hnum = num[0:MLSTM_V_DIM, :]
        inv_den = 1.0 / den
        ms = jnp.sum(hnum * hnum, axis=0, keepdims=True) * (inv_den * inv_den * (1.0 / MLSTM_V_DIM))
        normed.append(hnum * (inv_den * lax.rsqrt(ms + EPS)))

        b_last = b_row[:, L - 1:L]
        dec = b_last - b_row + li_row
        m_new = jnp.maximum(b_last + m, jnp.max(dec, axis=1, keepdims=True))
        vw = (vts[h].astype(F32) * jnp.exp(dec - m_new)).astype(BF16)
        out.append((jnp.exp(b_last + m - m_new) * C + _dot(vw, ks[h]), m_new))
    return tuple(out), jnp.concatenate(normed, axis=0).T


def _mlstm_mixer_kernel(x_ref, pre_g_ref, w_first_ref, w_rest_ref, conv_w_ref, conv_b_ref, bias_ref, hg_ref,
                        wqk_ref, wv_ref, og_ref, qmem_ref, gate_ref, uc_ref, hn_ref, tail_sc, c_sc, m_sc, *, chunk):
    tm = x_ref.shape[1]
    blocks = [slice(j * chunk, (j + 1) * chunk) for j in range(tm // chunk)]

    @pl.when(pl.program_id(1) == 0)
    def _():
        tail_sc[...] = jnp.zeros(tail_sc.shape, F32)
        c_sc[...] = jnp.zeros(c_sc.shape, F32)
        m_sc[...] = jnp.zeros(m_sc.shape, F32)

    hs = [_rms(x_ref[0, r, :], pre_g_ref[...]).astype(BF16) for r in blocks]
    firsts = [_dot(h, w_first_ref[...]) for h in hs]
    halos = [tail_sc[...]] + [f[chunk - CONV_HALO:chunk, 0:MIX_WIDTH] for f in firsts[:-1]]
    tail_sc[...] = firsts[-1][chunk - CONV_HALO:chunk, 0:MIX_WIDTH]

    k_scale = MLSTM_QK_DIM ** -0.5
    ones_rows = jnp.where(lax.broadcasted_iota(jnp.int32, (V_ONES_ROWS, chunk), 0) == 0, 1.0, 0.0)
    lane = lax.broadcasted_iota(jnp.int32, (chunk, LANES), 1)
    is_f = (lane >= MLSTM_HEADS) & (lane < 2 * MLSTM_HEADS)
    tril = (lax.broadcasted_iota(jnp.int32, (chunk, chunk), 1)
            <= lax.broadcasted_iota(jnp.int32, (chunk, chunk), 0)).astype(BF16)
    causal = (lax.broadcasted_iota(jnp.int32, (chunk, chunk), 0)
              <= lax.broadcasted_iota(jnp.int32, (chunk, chunk), 1))
    carry = tuple((c_sc[h], m_sc[h, 0:1, 0:1]) for h in range(MLSTM_HEADS))

    for r, h, first, halo in zip(blocks, hs, firsts, halos):
        u = first[:, 0:MIX_WIDTH]
        ext = jnp.concatenate([halo, u], axis=0)
        conv = conv_b_ref[...] + u * conv_w_ref[CONV_WIDTH - 1:CONV_WIDTH, :]
        for back in range(1, CONV_WIDTH):
            tap = CONV_WIDTH - 1 - back
            conv = conv + ext[CONV_HALO - back:CONV_HALO - back + chunk, :] * conv_w_ref[tap:tap + 1, :]
        uc_bf = (conv * jax.nn.sigmoid(conv)).astype(BF16)
        uc_ref[0, r, :] = uc_bf
        u_bf = u.astype(BF16)

        g = first[:, B_GATES[0]:B_GATES[1]] + bias_ref[...]
        log_f = jnp.minimum(g, 0.0) - jnp.log1p(jnp.exp(-jnp.abs(g)))
        gates = jnp.where(lane < MLSTM_HEADS, g, jnp.where(is_f, log_f, 0.0))
        gc = jnp.where(is_f, sum(_dot(tril, term) for term in _split3(gates)), gates)
        grow = gc.T[0:2 * MLSTM_HEADS, :]
        gcol = gc - pltpu.roll(gc, LANES - MLSTM_HEADS, 1)

        qts, ks, vts = [], [], []
        for hd in range(MLSTM_HEADS):
            cols = slice(MLSTM_HEAD_START[hd], MLSTM_HEAD_START[hd] + MXU_DIM)
            qk = _dot(uc_bf[:, cols], wqk_ref[hd])
            qts.append(qk[:, :MLSTM_PAD_QK].T.astype(BF16))
            ks.append((qk[:, MLSTM_PAD_QK:] * k_scale).astype(BF16))
            vt = _dot(u_bf[:, cols], wv_ref[hd]).T
            vts.append(jnp.concatenate([vt[0:MLSTM_V_DIM, :], ones_rows, vt[MLSTM_V_DIM + V_ONES_ROWS:, :]],
                                       axis=0).astype(BF16))

        def rest_projection(r=r, h=h):
            p = _dot(h, w_rest_ref[...])
            og_ref[0, r, :] = (jax.nn.sigmoid(p[:, B_O[0]:B_O[1]]) * hg_ref[...]).astype(BF16)
            qmem_ref[0, r, :] = p[:, B_QMEM[0]:B_QMEM[1]].astype(BF16)
            gate = p[:, B_GATE[0]:B_GATE[1]]
            gate_ref[0, r, :] = (gate * jax.nn.sigmoid(gate)).astype(BF16)

        carry, hn = _mlstm_chunk(carry, qts, ks, vts, gcol, grow, causal, rest_projection)
        hn_ref[0, r, :] = hn.astype(BF16)

    for hd in range(MLSTM_HEADS):
        c_sc[hd] = carry[hd][0]
        m_sc[hd] = jnp.broadcast_to(carry[hd][1], m_sc.shape[1:])


def _mlstm_mixer(x, pre_g, w_in, gate_bias, conv_w, conv_b, w_q, w_k, w_v, head_g):
    B, S, D = x.shape
    tm = min(MIXER_ROW_TILE, S)
    chunk = min(MLSTM_CHUNK, S)
    H = MLSTM_HEADS
    n_if = 2 * H
    o_if = MIX_WIDTH
    o_o = o_if + n_if
    w_first = jnp.concatenate([w_in[:, :o_o], jnp.zeros((D, LANES - n_if), w_in.dtype)], axis=1).astype(BF16)
    w_rest = w_in[:, o_o:].astype(BF16)
    def placed(w, width):
        blocks = []
        for h in range(H):
            off = h * MLSTM_V_DIM - MLSTM_HEAD_START[h]
            blocks.append(jnp.pad(w[h], ((off, MXU_DIM - MLSTM_V_DIM - off), (0, width - w.shape[-1]))))
        return jnp.stack(blocks)

    wqk = jnp.concatenate([placed(w_q, MLSTM_PAD_QK), placed(w_k, MLSTM_PAD_QK)], axis=-1)
    wv = placed(w_v, MLSTM_PAD_V)
    bias = jnp.pad(gate_bias, (0, LANES - n_if)).reshape(1, LANES)
    row = lambda b, i: (b, i, 0)
    widths = [MIX_WIDTH, MEM_WIDTH, 1024, MIX_WIDTH, MIX_WIDTH]
    return pl.pallas_call(
        functools.partial(_mlstm_mixer_kernel, chunk=chunk),
        grid=(B, S // tm),
        in_specs=[pl.BlockSpec((1, tm, D), row), _const_spec((1, D)), _const_spec(w_first.shape),
                  _const_spec(w_rest.shape), _const_spec((CONV_WIDTH, MIX_WIDTH)), _const_spec((1, MIX_WIDTH)),
                  _const_spec((1, LANES)), _const_spec((1, MIX_WIDTH)), _const_spec(wqk.shape),
                  _const_spec(wv.shape)],
        out_specs=[pl.BlockSpec((1, tm, w), row) for w in widths],
        out_shape=[jax.ShapeDtypeStruct((B, S, w), BF16) for w in widths],
        scratch_shapes=[pltpu.VMEM((CONV_HALO, MIX_WIDTH), F32),
                        pltpu.VMEM((H, MLSTM_PAD_V, MLSTM_PAD_QK), F32), pltpu.VMEM((H, 8, LANES), F32)],
        compiler_params=_params("parallel", "arbitrary"),
        name="mlstm_mixer",
    )(x, pre_g.reshape(1, D), w_first, w_rest, conv_w, conv_b.reshape(1, -1), bias, head_g.reshape(1, MIX_WIDTH),
      wqk.astype(BF16), wv.astype(BF16))


def _mla_layer(x, mem_kv, layer, positions, pre_g, w_in, q_a_g, w_uq, kv_a_g, w_ukv, w_out, post_g):
    q, k, v, qmem, gate = _mla_in(x, positions, pre_g, w_in, q_a_g, w_uq, kv_a_g, w_ukv)
    mix = _attention(q, k, v)
    specs = lambda tm: [pl.BlockSpec((1, tm, MIX_WIDTH), lambda b, i: (b, i, 0))]
    return _layer_out(_mla_out_kernel, "mla_out", x, [mix], specs, qmem, gate, mem_kv, layer, w_out, post_g)


def _mlstm_layer(x, mem_kv, layer, pre_g, w_in, gate_bias, conv_w, conv_b, w_q, w_k, w_v, head_g, skip,
                 w_out, post_g):
    og, qmem, gate, uc, hn = _mlstm_mixer(x, pre_g, w_in, gate_bias, conv_w, conv_b, w_q, w_k, w_v, head_g)
    row = lambda b, i: (b, i, 0)
    specs = lambda tm: [pl.BlockSpec((1, tm, MIX_WIDTH), row),
                        pl.BlockSpec((1, tm, MIX_WIDTH), row), pl.BlockSpec((1, tm, MIX_WIDTH), row),
                        _const_spec((1, MIX_WIDTH))]
    return _layer_out(_mlstm_out_kernel, "mlstm_out", x, [hn, og, uc, skip.reshape(1, -1)], specs,
                      qmem, gate, mem_kv, layer, w_out, post_g)


def kernel(x, mem, positions, a_pre_g, a_w_in, a_q_a_g, a_w_uq, a_kv_a_g, a_w_ukv, a_mem_g, a_w_mem_kv, a_w_out, a_post_g, b_pre_g, b_w_in, b_gate_bias, b_conv_w, b_conv_b, b_w_q, b_w_k, b_w_v, b_head_g, b_skip, b_mem_g, b_w_mem_kv, b_w_out, b_post_g):
    depth = a_pre_g.shape[0] + b_pre_g.shape[0]
    mem_gs = jnp.stack([(a_mem_g, b_mem_g)[i % 2][i // 2] for i in range(depth)])
    w_mem_kvs = jnp.stack([(a_w_mem_kv, b_w_mem_kv)[i % 2][i // 2] for i in range(depth)])
    mem_kv = _mem_kv(mem, mem_gs, w_mem_kvs)
    for i in range(depth):
        j = i // 2
        if i % 2 == 0:
            x = _mla_layer(x, mem_kv, i, positions, a_pre_g[j], a_w_in[j], a_q_a_g[j], a_w_uq[j], a_kv_a_g[j],
                           a_w_ukv[j], a_w_out[j], a_post_g[j])
        else:
            x = _mlstm_layer(x, mem_kv, i, b_pre_g[j], b_w_in[j], b_gate_bias[j], b_conv_w[j], b_conv_b[j],
                             b_w_q[j], b_w_k[j], b_w_v[j], b_head_g[j], b_skip[j], b_w_out[j], b_post_g[j])
    return x
```
